```python
import math
import jax
import jax.numpy as jnp
from jax import lax
import numpy as np

D_MODEL = 2048
BATCH = 2
SEQ = 4096
DEPTH = 4

GRID_W = 64
CTX_LEN = 256
N_MIXERS = 3
NORM_EPS = 1e-6

NA_HEADS = 16
NA_HEAD_DIM = D_MODEL // NA_HEADS
NA_ROWS = 8
NA_COLS = 16
NA_QBLK = 16
NA_KBLK = NA_QBLK + NA_COLS

S5_GROUP_CH = 16
S5_GROUPS = D_MODEL // S5_GROUP_CH
S5_STATE = 64

RW_HEAD_DIM = 64
RW_HEADS = D_MODEL // RW_HEAD_DIM
RW_DECAY_LORA = 96
RW_AAA_LORA = 96
RW_GATE_LORA = 256
RW_GN_EPS = 64e-5

MOE_GROUPS = 4
MOE_EXPERTS_PER_GROUP = 8
MOE_EXPERTS = MOE_GROUPS * MOE_EXPERTS_PER_GROUP
MOE_TOP_K = 2
MOE_HIDDEN = D_MODEL // 4
MOE_BLOCK = 128

kernel_name = 'hybrid_na_s5_rwkv7_hmoe_dit'


def layers_with_mixer(m):
    return (DEPTH - m + N_MIXERS - 1) // N_MIXERS


def rmsnorm(x, g):
    xf = x.astype(jnp.float32)
    xf = xf * lax.rsqrt(jnp.mean(xf * xf, axis=-1, keepdims=True) + NORM_EPS)
    return xf.astype(x.dtype) * g


def na_mixer(h, hc, w_qkv, w_o, rpb, need_ctx):
    b_, l_, d = h.shape
    rows = l_ // GRID_W
    kr = min(NA_ROWS, rows)
    scale = NA_HEAD_DIM ** -0.5

    def qkv_of(u):
        t = (u @ w_qkv).reshape(u.shape[0], u.shape[1], 3, NA_HEADS, NA_HEAD_DIM)
        return t[:, :, 0] * scale, t[:, :, 1], t[:, :, 2]

    q, k, v = qkv_of(h)
    qc, kc, vc = qkv_of(hc)
    qg = q.reshape(b_, rows, GRID_W, NA_HEADS, NA_HEAD_DIM)
    kg = k.reshape(b_, rows, GRID_W, NA_HEADS, NA_HEAD_DIM)
    vg = v.reshape(b_, rows, GRID_W, NA_HEADS, NA_HEAD_DIM)

    n_cb = GRID_W // NA_QBLK
    row_start = jnp.clip(jnp.arange(rows) - kr // 2, 0, rows - kr)
    q_cols = jnp.arange(GRID_W).reshape(n_cb, NA_QBLK)
    cb_start = jnp.clip(jnp.arange(n_cb) * NA_QBLK - NA_COLS // 2, 0, GRID_W - NA_KBLK)
    key_cols = cb_start[:, None] + jnp.arange(NA_KBLK)
    win_start = jnp.clip(q_cols - NA_COLS // 2, 0, GRID_W - NA_COLS)
    kcol = key_cols[:, None, :]
    col_valid = (kcol >= win_start[..., None]) & (kcol < win_start[..., None] + NA_COLS)
    dc_idx = jnp.clip(kcol - q_cols[..., None], 1 - NA_COLS, NA_COLS - 1) + NA_COLS - 1
    rpb_c = rpb.astype(jnp.float32)[:, :, dc_idx]
    n_loc = kr * NA_KBLK

    def row_attend(r):
        rs = row_start[r]
        q_r = lax.dynamic_index_in_dim(qg, r, 1, keepdims=False).reshape(
            b_, n_cb, NA_QBLK, NA_HEADS, NA_HEAD_DIM)
        k_blk = lax.dynamic_slice_in_dim(kg, rs, kr, 1)[:, :, key_cols]
        v_blk = lax.dynamic_slice_in_dim(vg, rs, kr, 1)[:, :, key_cols]
        dr_idx = rs + jnp.arange(kr) - r + NA_ROWS - 1
        bias = jnp.take(rpb_c, dr_idx, axis=1).transpose(0, 2, 3, 1, 4)
        s_loc = jnp.einsum('bnqhd,brnkhd->bhnqrk', q_r, k_blk).astype(jnp.float32) + bias
        s_loc = jnp.where(col_valid[:, :, None, :], s_loc, -jnp.inf)
        s_ctx = jnp.einsum('bnqhd,bchd->bhnqc', q_r, kc).astype(jnp.float32)
        s = jnp.concatenate([s_loc.reshape(b_, NA_HEADS, n_cb, NA_QBLK, n_loc), s_ctx], axis=-1)
        p = jax.nn.softmax(s, axis=-1).astype(v.dtype)
        p_loc = p[..., :n_loc].reshape(b_, NA_HEADS, n_cb, NA_QBLK, kr, NA_KBLK)
        o = (jnp.einsum('bhnqrk,brnkhd->bnqhd', p_loc, v_blk)
             + jnp.einsum('bhnqc,bchd->bnqhd', p[..., n_loc:], vc))
        return o.reshape(b_, GRID_W, d)

    o = lax.map(row_attend, jnp.arange(rows))
    y = o.transpose(1, 0, 2, 3).reshape(b_, l_, d) @ w_o
    yc = None
    if need_ctx:
        sc = jnp.einsum('bqhd,bkhd->bhqk', qc, kc).astype(jnp.float32)
        pc = jax.nn.softmax(sc, axis=-1).astype(vc.dtype)
        yc = jnp.einsum('bhqk,bkhd->bqhd', pc, vc).reshape(b_, -1, d) @ w_o
    return y, yc


def _lin_combine(e1, e2):
    a1, b1 = e1
    a2, b2 = e2
    return a1 * a2, a2 * b1 + b2


def s5_scan(u, lam_bar, b_bar, c_mat, h0, reverse):
    bu = jnp.einsum('blgc,gpc->blgp', u, b_bar)
    a = jnp.broadcast_to(lam_bar, (1,) + bu.shape[1:])
    a_cum, hs = lax.associative_scan(_lin_combine, (a, bu), axis=1, reverse=reverse)
    hs = hs + a_cum * h0[:, None]
    y = jnp.einsum('blgp,gcp->blgc', hs, c_mat).real
    h_fin = hs[:, 0] if reverse else hs[:, -1]
    return y, h_fin


def s5_mixer(h, hc, lam_re, lam_im, log_dt, b_re, b_im, c_re, c_im, d_skip, w_glu, need_ctx):
    f32 = jnp.float32
    lam = lax.complex(lam_re.astype(f32), lam_im.astype(f32))
    lam_bar = jnp.exp(lam * jnp.exp(log_dt.astype(f32))[..., None])
    b_bar = ((lam_bar - 1.0) / lam)[..., None] * lax.complex(b_re.astype(f32), b_im.astype(f32))
    c_mat = lax.complex(c_re.astype(f32), c_im.astype(f32))
    dsk = d_skip.astype(f32).reshape(S5_GROUPS, S5_GROUP_CH)

    def grp(u):
        return u.astype(f32).reshape(u.shape[0], u.shape[1], S5_GROUPS, S5_GROUP_CH)

    ul, uc = grp(h), grp(hc)
    ulc, ucc = ul.astype(jnp.complex64), uc.astype(jnp.complex64)
    ys_l, ys_c = [], []
    for dn, rev in enumerate((False, True)):
        h0 = jnp.zeros((h.shape[0], S5_GROUPS, S5_STATE), jnp.complex64)
        yc_d, hc_fin = s5_scan(ucc, lam_bar[dn], b_bar[dn], c_mat[dn], h0, rev)
        yl_d, _ = s5_scan(ulc, lam_bar[dn], b_bar[dn], c_mat[dn], hc_fin, rev)
        ys_l.append(yl_d)
        ys_c.append(yc_d)

    def out(ys, u):
        y = (ys[0] + ys[1] + u * dsk).reshape(u.shape[0], u.shape[1], D_MODEL)
        y = jax.nn.gelu(y).astype(h.dtype)
        z1, z2 = jnp.split(y @ w_glu, 2, axis=-1)
        return z1 * jax.nn.sigmoid(z2)

    return out(ys_l, ul), (out(ys_c, uc) if need_ctx else None)


def token_shift(u):
    zero = jnp.zeros_like(u[:, :1])
    prev = jnp.concatenate([zero, u[:, :-1]], axis=1)
    nxt = jnp.concatenate([u[:, 1:], zero], axis=1)
    return 0.5 * (prev + nxt) - u


def rwkv_prep(u, mu, w_rkv, w0, w1, w2, a0, a1, a2, k_k, k_a, g1, g2):
    f32 = jnp.float32
    b_, l_, _ = u.shape
    hs = (b_, l_, RW_HEADS, RW_HEAD_DIM)
    xx = token_shift(u)
    x_rkv = u[:, :, None] + xx[:, :, None] * mu[:3]
    rkv = jnp.einsum('blmd,mde->blme', x_rkv, w_rkv).astype(f32)
    r, k, v = rkv[:, :, 0].reshape(hs), rkv[:, :, 1].reshape(hs), rkv[:, :, 2].reshape(hs)
    xw = u + xx * mu[3]
    xa = u + xx * mu[4]
    xg = u + xx * mu[5]
    gate = jax.nn.sigmoid(xg @ g1) @ g2
    kk = k * k_k.astype(f32).reshape(RW_HEADS, RW_HEAD_DIM)
    kk = kk / jnp.maximum(jnp.sqrt(jnp.sum(kk * kk, axis=-1, keepdims=True)), 1e-12)
    dirs = []
    for dn in range(2):
        wl = -jax.nn.softplus(-(w0[dn] + jnp.tanh(xw @ w1[dn]) @ w2[dn])) - 0.5
        decay = jnp.exp(-jnp.exp(wl.astype(f32))).reshape(hs)
        a = jax.nn.sigmoid(a0[dn] + (xa @ a1[dn]) @ a2[dn]).astype(f32).reshape(hs)
        kd = k * (1.0 + (a - 1.0) * k_a.astype(f32).reshape(RW_HEADS, RW_HEAD_DIM))
        dirs.append((decay, a, kd))
    return r, v, kk, gate, dirs


def rwkv_scan(r, decay, kd, v, kk, a, s0, reverse):
    xs = tuple(jnp.moveaxis(t, 1, 0) for t in (r, decay, kd, v, -kk, kk * a))

    def step(S, inp):
        r_t, w_t, k_t, v_t, nkk_t, akk_t = inp
        sa = jnp.einsum('bhvk,bhk->bhv', S, nkk_t)
        S = (S * w_t[:, :, None, :] + sa[..., None] * akk_t[:, :, None, :]
             + v_t[..., None] * k_t[:, :, None, :])
        return S, jnp.einsum('bhvk,bhk->bhv', S, r_t)

    s_fin, o = lax.scan(step, s0, xs, reverse=reverse)
    return jnp.moveaxis(o, 0, 1), s_fin


def rwkv_out(r, v, kds, o, gate, r_k, ln_g, ln_b, w_o, dtype):
    f32 = jnp.float32
    b_, l_ = o.shape[:2]
    mean = jnp.mean(o, axis=-1, keepdims=True)
    var = jnp.mean(jnp.square(o - mean), axis=-1, keepdims=True)
    on = ((o - mean) * lax.rsqrt(var + RW_GN_EPS)).reshape(b_, l_, D_MODEL)
    on = on * ln_g.astype(f32) + ln_b.astype(f32)
    rk = r_k.astype(f32)
    bonus = (jnp.sum(r * kds[0] * rk, axis=-1, keepdims=True)
             + jnp.sum(r * kds[1] * rk, axis=-1, keepdims=True)) * v
    y = (on + bonus.reshape(b_, l_, D_MODEL)).astype(dtype)
    return (y * gate) @ w_o


def rwkv_mixer(h, hc, mu, w_rkv, w0, w1, w2, a0, a1, a2, k_k, k_a, r_k, ln_g, ln_b,
               g1, g2, w_o, need_ctx):
    r_l, v_l, kk_l, gate_l, d_l = rwkv_prep(h, mu, w_rkv, w0, w1, w2, a0, a1, a2, k_k, k_a, g1, g2)
    r_c, v_c, kk_c, gate_c, d_c = rwkv_prep(hc, mu, w_rkv, w0, w1, w2, a0, a1, a2, k_k, k_a, g1, g2)
    os_l, os_c = [], []
    for dn, rev in enumerate((False, True)):
        s0 = jnp.zeros((h.shape[0], RW_HEADS, RW_HEAD_DIM, RW_HEAD_DIM), jnp.float32)
        dec_c, a_c, kd_c = d_c[dn]
        oc, s_ctx = rwkv_scan(r_c, dec_c, kd_c, v_c, kk_c, a_c, s0, rev)
        dec_l, a_l, kd_l = d_l[dn]
        ol, _ = rwkv_scan(r_l, dec_l, kd_l, v_l, kk_l, a_l, s_ctx, rev)
        os_l.append(ol)
        os_c.append(oc)
    y = rwkv_out(r_l, v_l, (d_l[0][2], d_l[1][2]), os_l[0] + os_l[1], gate_l,
                 r_k, ln_g, ln_b, w_o, h.dtype)
    yc = None
    if need_ctx:
        yc = rwkv_out(r_c, v_c, (d_c[0][2], d_c[1][2]), os_c[0] + os_c[1], gate_c,
                      r_k, ln_g, ln_b, w_o, hc.dtype)
    return y, yc


def expert_dispatch(tok, expert, gate, w_gu, w_down):
    n_tok, d = tok.shape
    n_exp = w_gu.shape[0]
    flat_e = expert.reshape(-1)
    n_asg = flat_e.shape[0]
    order = jnp.argsort(flat_e)
    e_sorted = flat_e[order]
    tok_sorted = order // MOE_TOP_K
    w_sorted = gate.reshape(-1)[order]
    counts = jnp.zeros((n_exp,), jnp.int32).at[flat_e].add(1)
    padded = (counts + MOE_BLOCK - 1) // MOE_BLOCK * MOE_BLOCK
    pad_end = jnp.cumsum(padded)
    pad_start = pad_end - padded
    start = jnp.cumsum(counts) - counts
    slot = pad_start[e_sorted] + jnp.arange(n_asg, dtype=jnp.int32) - start[e_sorted]
    n_blocks = -(-n_asg // MOE_BLOCK) + n_exp
    slot_tok = jnp.full((n_blocks * MOE_BLOCK,), n_tok, jnp.int32).at[slot].set(tok_sorted.astype(jnp.int32))
    x_pad = jnp.concatenate([tok, jnp.zeros((1, d), tok.dtype)], axis=0)
    xb = x_pad[slot_tok].reshape(n_blocks, MOE_BLOCK, d)
    blk_e = jnp.minimum(jnp.searchsorted(pad_end, jnp.arange(n_blocks, dtype=jnp.int32) * MOE_BLOCK,
                                         side='right'), n_exp - 1)

    def expert_block(args):
        xe, e = args
        g, u = jnp.split(xe @ w_gu[e], 2, axis=-1)
        return (jax.nn.silu(g) * u) @ w_down[e]

    yb = lax.map(expert_block, (xb, blk_e)).reshape(-1, d)
    y_asg = yb[slot] * w_sorted[:, None].astype(yb.dtype)
    return jnp.zeros_like(tok).at[tok_sorted].add(y_asg)


def hier_moe(tok, wg, bg, we, be, w_gu, w_down):
    n_tok = tok.shape[0]
    lg = (tok @ wg).astype(jnp.float32) + bg.astype(jnp.float32)
    p_grp, grp = lax.top_k(jax.nn.softmax(lg, axis=-1), 1)
    le = ((tok @ we).astype(jnp.float32) + be.astype(jnp.float32)).reshape(
        n_tok, MOE_GROUPS, MOE_EXPERTS_PER_GROUP)
    le_sel = le[jnp.arange(n_tok), grp[:, 0]]
    top_l, top_i = lax.top_k(le_sel, MOE_TOP_K)
    gate = p_grp * jax.nn.softmax(top_l, axis=-1)
    expert = grp * MOE_EXPERTS_PER_GROUP + top_i
    return expert_dispatch(tok, expert, gate, w_gu, w_down)


def setup_inputs(seed: int = 0) -> dict:
    key = jax.random.key(seed)
    keys = iter(jax.random.split(key, 64))
    f32 = jnp.float32
    d = D_MODEL

    def nrm(shape, scale):
        return jax.random.normal(next(keys), shape, f32) * scale

    def unif(shape, lo, hi):
        return jax.random.uniform(next(keys), shape, f32, lo, hi)

    n_a, n_b, n_c = layers_with_mixer(0), layers_with_mixer(1), layers_with_mixer(2)
    lam_shape = (n_b, 2, S5_GROUPS, S5_STATE)
    return {
        'x': nrm((BATCH, SEQ, d), 1.0),
        'c': nrm((BATCH, d), 1.0),
        'ctx': nrm((BATCH, CTX_LEN, d), 1.0),
        'c_ctx': nrm((d,), 1.0),
        'mod_w': nrm((DEPTH, d, 6 * d), 0.5 * d ** -0.5),
        'mod_b': nrm((DEPTH, 6 * d), 0.01),
        'norm_g': 1.0 + nrm((DEPTH, 2, d), 0.02),
        'final_g': 1.0 + nrm((d,), 0.02),
        'na_w_qkv': nrm((n_a, d, 3 * d), d ** -0.5),
        'na_w_o': nrm((n_a, d, d), d ** -0.5),
        'na_rpb': nrm((n_a, NA_HEADS, 2 * NA_ROWS - 1, 2 * NA_COLS - 1), 0.1),
        's5_lam_re': -0.5 + nrm(lam_shape, 0.01),
        's5_lam_im': jnp.broadcast_to(jnp.pi * jnp.arange(S5_STATE, dtype=f32), lam_shape) + nrm(lam_shape, 0.01),
        's5_log_dt': unif((n_b, 2, S5_GROUPS), math.log(0.001), math.log(0.1)),
        's5_b_re': nrm((n_b, 2, S5_GROUPS, S5_STATE, S5_GROUP_CH), (0.5 / S5_GROUP_CH) ** 0.5),
        's5_b_im': nrm((n_b, 2, S5_GROUPS, S5_STATE, S5_GROUP_CH), (0.5 / S5_GROUP_CH) ** 0.5),
        's5_c_re': nrm((n_b, 2, S5_GROUPS, S5_GROUP_CH, S5_STATE), (0.5 / S5_STATE) ** 0.5),
        's5_c_im': nrm((n_b, 2, S5_GROUPS, S5_GROUP_CH, S5_STATE), (0.5 / S5_STATE) ** 0.5),
        's5_d': nrm((n_b, d), 1.0),
        's5_w_glu': nrm((n_b, d, 2 * d), d ** -0.5),
        'rw_mu': unif((n_c, 6, d), 0.0, 1.0),
        'rw_w_rkv': nrm((n_c, 3, d, d), d ** -0.5),
        'rw_w0': unif((n_c, 2, d), -3.0, 1.0),
        'rw_w1': nrm((n_c, 2, d, RW_DECAY_LORA), d ** -0.5),
        'rw_w2': nrm((n_c, 2, RW_DECAY_LORA, d), 0.1 * RW_DECAY_LORA ** -0.5),
        'rw_a0': nrm((n_c, 2, d), 0.1),
        'rw_a1': nrm((n_c, 2, d, RW_AAA_LORA), d ** -0.5),
        'rw_a2': nrm((n_c, 2, RW_AAA_LORA, d), 0.5 * RW_AAA_LORA ** -0.5),
        'rw_k_k': 0.85 + nrm((n_c, d), 0.05),
        'rw_k_a': 1.0 + nrm((n_c, d), 0.05),
        'rw_r_k': nrm((n_c, RW_HEADS, RW_HEAD_DIM), 0.1),
        'rw_ln_g': 1.0 + nrm((n_c, d), 0.02),
        'rw_ln_b': nrm((n_c, d), 0.01),
        'rw_g1': nrm((n_c, d, RW_GATE_LORA), d ** -0.5),
        'rw_g2': nrm((n_c, RW_GATE_LORA, d), RW_GATE_LORA ** -0.5),
        'rw_w_o': nrm((n_c, d, d), d ** -0.5),
        'moe_wg': nrm((DEPTH, d, MOE_GROUPS), d ** -0.5),
        'moe_bg': nrm((DEPTH, MOE_GROUPS), 0.01),
        'moe_we': nrm((DEPTH, d, MOE_EXPERTS), d ** -0.5),
        'moe_be': nrm((DEPTH, MOE_EXPERTS), 0.01),
        'moe_w_gu': nrm((DEPTH, MOE_EXPERTS, d, 2 * MOE_HIDDEN), d ** -0.5),
        'moe_w_down': nrm((DEPTH, MOE_EXPERTS, MOE_HIDDEN, d), MOE_HIDDEN ** -0.5),
    }


def reference(x, c, ctx, c_ctx, mod_w, mod_b, norm_g, final_g,
              na_w_qkv, na_w_o, na_rpb,
              s5_lam_re, s5_lam_im, s5_log_dt, s5_b_re, s5_b_im, s5_c_re, s5_c_im, s5_d, s5_w_glu,
              rw_mu, rw_w_rkv, rw_w0, rw_w1, rw_w2, rw_a0, rw_a1, rw_a2, rw_k_k, rw_k_a, rw_r_k,
              rw_ln_g, rw_ln_b, rw_g1, rw_g2, rw_w_o,
              moe_wg, moe_bg, moe_we, moe_be, moe_w_gu, moe_w_down):
    bsz, seq, d = x.shape
    n_lat = bsz * seq
    xc = ctx
    silu_c = jax.nn.silu(c)
    silu_cc = jax.nn.silu(c_ctx)
    for i in range(DEPTH):
        last = i == DEPTH - 1
        sh1, sc1, g1, sh2, sc2, g2 = jnp.split((silu_c @ mod_w[i] + mod_b[i])[:, None, :], 6, axis=-1)
        csh1, csc1, cg1, csh2, csc2, cg2 = jnp.split(silu_cc @ mod_w[i] + mod_b[i], 6, axis=-1)
        h = rmsnorm(x, norm_g[i, 0]) * (1.0 + sc1) + sh1
        hc = rmsnorm(xc, norm_g[i, 0]) * (1.0 + csc1) + csh1
        m, j = i % N_MIXERS, i // N_MIXERS
        if m == 0:
            y, yc = na_mixer(h, hc, na_w_qkv[j], na_w_o[j], na_rpb[j], not last)
        elif m == 1:
            y, yc = s5_mixer(h, hc, s5_lam_re[j], s5_lam_im[j], s5_log_dt[j], s5_b_re[j], s5_b_im[j],
                             s5_c_re[j], s5_c_im[j], s5_d[j], s5_w_glu[j], not last)
        else:
            y, yc = rwkv_mixer(h, hc, rw_mu[j], rw_w_rkv[j], rw_w0[j], rw_w1[j], rw_w2[j],
                               rw_a0[j], rw_a1[j], rw_a2[j], rw_k_k[j], rw_k_a[j], rw_r_k[j],
                               rw_ln_g[j], rw_ln_b[j], rw_g1[j], rw_g2[j], rw_w_o[j], not last)
        x = x + g1 * y
        h = rmsnorm(x, norm_g[i, 1]) * (1.0 + sc2) + sh2
        if last:
            f = hier_moe(h.reshape(n_lat, d), moe_wg[i], moe_bg[i], moe_we[i], moe_be[i],
                         moe_w_gu[i], moe_w_down[i])
            x = x + g2 * f.reshape(bsz, seq, d)
        else:
            xc = xc + cg1 * yc
            hc = rmsnorm(xc, norm_g[i, 1]) * (1.0 + csc2) + csh2
            toks = jnp.concatenate([h.reshape(n_lat, d), hc.reshape(-1, d)], axis=0)
            f = hier_moe(toks, moe_wg[i], moe_bg[i], moe_we[i], moe_be[i], moe_w_gu[i], moe_w_down[i])
            x = x + g2 * f[:n_lat].reshape(bsz, seq, d)
            xc = xc + cg2 * f[n_lat:].reshape(xc.shape)
    return rmsnorm(x, final_g)
```

```python
import functools
import math

import jax
import jax.numpy as jnp
from jax import lax
from jax.experimental import pallas as pl
from jax.experimental.pallas import tpu as pltpu

F32 = jnp.float32
BF16 = jnp.bfloat16

GRID_W = 64
N_MIXERS = 3
NORM_EPS = 1e-6
RW_GN_EPS = 64e-5
MOE_TOP_K = 2
MOE_BLOCK = 128
S5_CHUNK = 16
NEG_BIG = -1e30
VMEM_LIMIT_BYTES = 48 * 1024 * 1024
VMEM_LIMIT_BIG_BYTES = 56 * 1024 * 1024


def _cparams(sem, limit=VMEM_LIMIT_BYTES):
    return pltpu.CompilerParams(dimension_semantics=sem, vmem_limit_bytes=limit)


def _row_tile(n_lat_per_batch, n_ctx_total, cap=512):
    t = cap
    while t > 8 and (n_lat_per_batch % t or (n_ctx_total and n_ctx_total % t)):
        t //= 2
    return t


def _seg_fn(tiles_per_batch, n_batch):
    def seg(i):
        return jnp.minimum(i // tiles_per_batch, n_batch)
    return seg


def _bdot(a, b):
    return jnp.dot(a.astype(BF16), b.astype(BF16), preferred_element_type=F32)


def _split3(x):
    hi = x.astype(BF16)
    r1 = x - hi.astype(F32)
    mid = r1.astype(BF16)
    lo = (r1 - mid.astype(F32)).astype(BF16)
    return hi, mid, lo


def _prenorm_body(x_ref, g_ref, sc_ref, sh_ref, *o_refs):
    x = x_ref[...]
    ms = jnp.mean(x * x, axis=-1, keepdims=True)
    h = (x * lax.rsqrt(ms + NORM_EPS)) * g_ref[...]
    h = h * sc_ref[0] + sh_ref[0]
    for o_ref in o_refs:
        o_ref[...] = h.astype(o_ref.dtype)


def prenorm(x, g, scale1p, shift, tm, tiles_per_batch, n_batch, out_dtypes):
    m, d = x.shape
    seg = _seg_fn(tiles_per_batch, n_batch)
    row = pl.BlockSpec((tm, d), lambda i: (i, 0))
    tab = pl.BlockSpec((1, 1, d), lambda i: (seg(i), 0, 0))
    outs = pl.pallas_call(
        _prenorm_body,
        grid=(m // tm,),
        in_specs=[row, pl.BlockSpec((1, d), lambda i: (0, 0)), tab, tab],
        out_specs=[row for _ in out_dtypes],
        out_shape=[jax.ShapeDtypeStruct((m, d), dt) for dt in out_dtypes],
        compiler_params=_cparams(("parallel",)),
    )(x, g.reshape(1, d), scale1p, shift)
    return outs


def _softplus(z):
    return jnp.maximum(z, 0.0) + jnp.log(1.0 + jnp.exp(-jnp.abs(z)))


def _apply_act(y, act):
    if act is None:
        return y
    if act == "tanh":
        return jnp.tanh(y)
    if act == "sigmoid":
        return jax.nn.sigmoid(y)
    if act == "decay":
        return jnp.exp(-jnp.exp(-_softplus(-y) - 0.5))
    raise ValueError(act)


def _mm_body(*refs, mode, act, has_bias, has_scale):
    it = iter(refs)
    x_ref = next(it)
    w_ref = next(it)
    w2_ref = next(it) if mode == "glu" else None
    b_ref = next(it) if has_bias else None
    s_ref = next(it) if has_scale else None
    r_ref = next(it) if mode in ("resid", "glu") else None
    g_ref = next(it) if mode in ("resid", "glu") else None
    o_ref = next(it)
    x = x_ref[...].astype(BF16)
    acc = jnp.dot(x, w_ref[...].astype(BF16), preferred_element_type=F32)
    if mode == "glu":
        acc2 = jnp.dot(x, w2_ref[...].astype(BF16), preferred_element_type=F32)
        acc = acc * jax.nn.sigmoid(acc2)
    if has_bias:
        acc = acc + b_ref[...]
    acc = _apply_act(acc, act)
    if has_scale:
        acc = acc * s_ref[...]
    if mode in ("resid", "glu"):
        acc = r_ref[...] + g_ref[0] * acc
    o_ref[...] = acc.astype(o_ref.dtype)


def matmul(x, w, *, tm, tn=512, out_dtype=F32, mode="plain", act=None, bias=None, colscale=None,
           resid=None, gate=None, tiles_per_batch=None, n_batch=None):
    m, k = x.shape
    n = w.shape[1] // 2 if mode == "glu" else w.shape[1]
    tn = min(tn, n)
    assert m % tm == 0 and n % tn == 0, (m, tm, n, tn)
    nj = n // tn
    in_specs = [pl.BlockSpec((tm, k), lambda i, j: (i, 0)), pl.BlockSpec((k, tn), lambda i, j: (0, j))]
    args = [x, w]
    if mode == "glu":
        in_specs.append(pl.BlockSpec((k, tn), lambda i, j: (0, j + nj)))
        args.append(w)
    col = pl.BlockSpec((1, tn), lambda i, j: (0, j))
    if bias is not None:
        in_specs.append(col)
        args.append(bias.reshape(1, n).astype(F32))
    if colscale is not None:
        in_specs.append(col)
        args.append(colscale.reshape(1, n).astype(F32))
    if mode in ("resid", "glu"):
        seg = _seg_fn(tiles_per_batch, n_batch)
        in_specs.append(pl.BlockSpec((tm, tn), lambda i, j: (i, j)))
        in_specs.append(pl.BlockSpec((1, 1, tn), lambda i, j: (seg(i), 0, j)))
        args += [resid, gate]
    return pl.pallas_call(
        functools.partial(_mm_body, mode=mode, act=act, has_bias=bias is not None,
                          has_scale=colscale is not None),
        grid=(m // tm, nj),
        in_specs=in_specs,
        out_specs=pl.BlockSpec((tm, tn), lambda i, j: (i, j)),
        out_shape=jax.ShapeDtypeStruct((m, n), out_dtype),
        compiler_params=_cparams(("parallel", "parallel")),
    )(*args)


def _mm3_body(x_ref, w_ref, b_ref, o_ref):
    xh, xm, xl = _split3(x_ref[...])
    wh, wm, wl = _split3(w_ref[...])
    dot = functools.partial(jnp.dot, preferred_element_type=F32)
    acc = dot(xh, wh) + (dot(xh, wm) + dot(xm, wh)) + (dot(xh, wl) + dot(xm, wm) + dot(xl, wh))
    o_ref[...] = acc + b_ref[...]


def matmul_f32(x, w, bias, tm):
    m, k = x.shape
    n = w.shape[1]
    return pl.pallas_call(
        _mm3_body,
        grid=(m // tm,),
        in_specs=[pl.BlockSpec((tm, k), lambda i: (i, 0)), pl.BlockSpec((k, n), lambda i: (0, 0)),
                  pl.BlockSpec((1, n), lambda i: (0, 0))],
        out_specs=pl.BlockSpec((tm, n), lambda i: (i, 0)),
        out_shape=jax.ShapeDtypeStruct((m, n), F32),
        compiler_params=_cparams(("parallel",)),
    )(x, w, bias.reshape(1, n))


def _mod_body(s_ref, w_ref, b_ref, o_ref):
    s = s_ref[...]
    s = s * jax.nn.sigmoid(s)
    o_ref[0] = _bdot(s, w_ref[0]) + b_ref[0]


def modulation(cvecs, mod_w, mod_b, tn=768):
    depth, d, n = mod_w.shape
    rows = cvecs.shape[0]
    while n % tn:
        tn //= 2
    return pl.pallas_call(
        _mod_body,
        grid=(depth, n // tn),
        in_specs=[pl.BlockSpec((rows, d), lambda l, j: (0, 0)),
                  pl.BlockSpec((1, d, tn), lambda l, j: (l, 0, j)),
                  pl.BlockSpec((1, 1, tn), lambda l, j: (l, 0, j))],
        out_specs=pl.BlockSpec((1, rows, tn), lambda l, j: (l, 0, j)),
        out_shape=jax.ShapeDtypeStruct((depth, rows, n), F32),
        compiler_params=_cparams(("parallel", "parallel")),
    )(cvecs, mod_w, mod_b.reshape(depth, 1, n))


def _na_bias_table(rpb, grid_w):
    n_heads = rpb.shape[0]
    kr = (rpb.shape[1] + 1) // 2
    kcw = (rpb.shape[2] + 1) // 2
    q = jnp.arange(grid_w)[:, None]
    kc = jnp.arange(grid_w)[None, :]
    win = jnp.clip(q - kcw // 2, 0, grid_w - kcw)
    valid = (kc >= win) & (kc < win + kcw)
    dc = jnp.clip(kc - q, 1 - kcw, kcw - 1) + kcw - 1
    d = jnp.arange(kr)[:, None]
    j = jnp.arange(kr)[None, :]
    dr = j - d + kr - 1
    tab = rpb.astype(F32)[:, dr][:, :, :, dc]
    tab = jnp.where(valid[None, None, None], tab, NEG_BIG)
    return tab.transpose(0, 1, 3, 2, 4).reshape(n_heads, kr, grid_w, kr * grid_w)


def _na_body(q_ref, k_ref, v_ref, kc_ref, vc_ref, bias_ref, o_ref, *, hpb, dh, grid_w, kr, rows):
    r = pl.program_id(2)
    rs = jnp.clip(r - kr // 2, 0, rows - kr)
    start = pl.multiple_of(rs * grid_w, grid_w)
    nt = (((1,), (1,)), ((), ()))
    for h in range(hpb):
        sl = slice(h * dh, (h + 1) * dh)
        q = q_ref[:, sl]
        k = k_ref[pl.ds(start, kr * grid_w), sl]
        v = v_ref[pl.ds(start, kr * grid_w), sl]
        s = lax.dot_general(q, k, nt, preferred_element_type=F32) + bias_ref[h, 0]
        sc = lax.dot_general(q, kc_ref[:, sl], nt, preferred_element_type=F32)
        m = jnp.maximum(jnp.max(s, axis=-1, keepdims=True), jnp.max(sc, axis=-1, keepdims=True))
        p = jnp.exp(s - m)
        pc = jnp.exp(sc - m)
        den = jnp.sum(p, axis=-1, keepdims=True) + jnp.sum(pc, axis=-1, keepdims=True)
        o = (jnp.dot(p.astype(BF16), v, preferred_element_type=F32)
             + jnp.dot(pc.astype(BF16), vc_ref[:, sl], preferred_element_type=F32))
        o_ref[:, sl] = (o / den).astype(o_ref.dtype)


def _ctx_attn_body(q_ref, k_ref, v_ref, o_ref, *, hpb, dh):
    nt = (((1,), (1,)), ((), ()))
    for h in range(hpb):
        sl = slice(h * dh, (h + 1) * dh)
        s = lax.dot_general(q_ref[:, sl], k_ref[:, sl], nt, preferred_element_type=F32)
        m = jnp.max(s, axis=-1, keepdims=True)
        p = jnp.exp(s - m)
        den = jnp.sum(p, axis=-1, keepdims=True)
        o = jnp.dot(p.astype(BF16), v_ref[:, sl], preferred_element_type=F32)
        o_ref[:, sl] = (o / den).astype(o_ref.dtype)


def na_attention(qkv, bias_tab, n_batch, seq, ctx_len, need_ctx):
    d = qkv.shape[1] // 3
    n_heads, kr = bias_tab.shape[0], bias_tab.shape[1]
    dh = d // n_heads
    hpb = max(1, min(n_heads, 256 // dh))
    bw = hpb * dh
    n_d = d // bw
    rows = seq // GRID_W
    n_lat = n_batch * seq
    ctx_blk0 = n_lat // ctx_len

    def d_of(r):
        return r - jnp.clip(r - kr // 2, 0, rows - kr)

    o_lat = pl.pallas_call(
        functools.partial(_na_body, hpb=hpb, dh=dh, grid_w=GRID_W, kr=kr, rows=rows),
        grid=(n_batch, n_d, rows),
        in_specs=[
            pl.BlockSpec((GRID_W, bw), lambda b, g, r: (b * rows + r, g)),
            pl.BlockSpec((seq, bw), lambda b, g, r: (b, n_d + g)),
            pl.BlockSpec((seq, bw), lambda b, g, r: (b, 2 * n_d + g)),
            pl.BlockSpec((ctx_len, bw), lambda b, g, r: (ctx_blk0 + b, n_d + g)),
            pl.BlockSpec((ctx_len, bw), lambda b, g, r: (ctx_blk0 + b, 2 * n_d + g)),
            pl.BlockSpec((hpb, 1, GRID_W, kr * GRID_W), lambda b, g, r: (g, d_of(r), 0, 0)),
        ],
        out_specs=pl.BlockSpec((GRID_W, bw), lambda b, g, r: (b * rows + r, g)),
        out_shape=jax.ShapeDtypeStruct((n_lat, d), BF16),
        compiler_params=_cparams(("parallel", "parallel", "arbitrary")),
    )(qkv, qkv, qkv, qkv, qkv, bias_tab)
    if not need_ctx:
        return o_lat
    o_ctx = pl.pallas_call(
        functools.partial(_ctx_attn_body, hpb=hpb, dh=dh),
        grid=(n_batch, n_d),
        in_specs=[
            pl.BlockSpec((ctx_len, bw), lambda b, g: (ctx_blk0 + b, g)),
            pl.BlockSpec((ctx_len, bw), lambda b, g: (ctx_blk0 + b, n_d + g)),
            pl.BlockSpec((ctx_len, bw), lambda b, g: (ctx_blk0 + b, 2 * n_d + g)),
        ],
        out_specs=pl.BlockSpec((ctx_len, bw), lambda b, g: (b, g)),
        out_shape=jax.ShapeDtypeStruct((n_batch * ctx_len, d), BF16),
        compiler_params=_cparams(("parallel", "parallel")),
    )(qkv, qkv, qkv)
    return jnp.concatenate([o_lat, o_ctx], axis=0)


def _expert_body(be_ref, x_ref, wgu_ref, wd_ref, o_ref, wgu_s, wd_s, *, hidden):
    j = pl.program_id(0)
    prev = be_ref[jnp.maximum(j - 1, 0)]
    changed = jnp.logical_or(j == 0, be_ref[j] != prev)

    @pl.when(changed)
    def _():
        wgu_s[...] = wgu_ref[0].astype(BF16)
        wd_s[...] = wd_ref[0].astype(BF16)

    gu = jnp.dot(x_ref[...], wgu_s[...], preferred_element_type=F32)
    g = gu[:, :hidden]
    u = gu[:, hidden:]
    a = (g * jax.nn.sigmoid(g) * u).astype(BF16)
    o_ref[...] = jnp.dot(a, wd_s[...], preferred_element_type=F32)


def expert_blocks(xb, blk_e, w_gu, w_down):
    n_rows, d = xb.shape
    hidden = w_down.shape[1]
    n_blocks = n_rows // MOE_BLOCK
    return pl.pallas_call(
        functools.partial(_expert_body, hidden=hidden),
        grid_spec=pltpu.PrefetchScalarGridSpec(
            num_scalar_prefetch=1,
            grid=(n_blocks,),
            in_specs=[pl.BlockSpec((MOE_BLOCK, d), lambda j, be: (j, 0)),
                      pl.BlockSpec((1, d, 2 * hidden), lambda j, be: (be[j], 0, 0)),
                      pl.BlockSpec((1, hidden, d), lambda j, be: (be[j], 0, 0))],
            out_specs=pl.BlockSpec((MOE_BLOCK, d), lambda j, be: (j, 0)),
            scratch_shapes=[pltpu.VMEM((d, 2 * hidden), BF16), pltpu.VMEM((hidden, d), BF16)]),
        out_shape=jax.ShapeDtypeStruct((n_rows, d), F32),
        compiler_params=_cparams(("arbitrary",), VMEM_LIMIT_BIG_BYTES),
    )(blk_e, xb, w_gu, w_down)


def hier_moe(h_f32, h_bf16, wg, bg, we, be, w_gu, w_down, tm):
    n_tok, d = h_f32.shape
    n_grp = wg.shape[1]
    n_exp = we.shape[1]
    epg = n_exp // n_grp
    n_logit = n_grp + n_exp
    n_pad = -(-n_logit // 128) * 128
    wcat = jnp.pad(jnp.concatenate([wg, we], axis=1), ((0, 0), (0, n_pad - n_logit)))
    bcat = jnp.pad(jnp.concatenate([bg, be], axis=0), (0, n_pad - n_logit))
    logits = matmul_f32(h_f32, wcat, bcat, tm)
    lg = logits[:, :n_grp]
    le = logits[:, n_grp:n_logit].reshape(n_tok, n_grp, epg)
    p_grp, grp = lax.top_k(jax.nn.softmax(lg, axis=-1), 1)
    le_sel = jnp.take_along_axis(le, grp[:, :, None], axis=1)[:, 0]
    top_l, top_i = lax.top_k(le_sel, MOE_TOP_K)
    gate = p_grp * jax.nn.softmax(top_l, axis=-1)
    expert = grp * epg + top_i

    flat_e = expert.reshape(-1).astype(jnp.int32)
    n_asg = flat_e.shape[0]
    order = jnp.argsort(flat_e)
    e_sorted = flat_e[order]
    tok_sorted = (order // MOE_TOP_K).astype(jnp.int32)
    counts = jnp.zeros((n_exp,), jnp.int32).at[flat_e].add(1)
    padded = (counts + MOE_BLOCK - 1) // MOE_BLOCK * MOE_BLOCK
    pad_end = jnp.cumsum(padded)
    pad_start = pad_end - padded
    start = jnp.cumsum(counts) - counts
    slot = pad_start[e_sorted] + jnp.arange(n_asg, dtype=jnp.int32) - start[e_sorted]
    n_blocks = -(-n_asg // MOE_BLOCK) + n_exp
    slot_tok = jnp.full((n_blocks * MOE_BLOCK,), n_tok, jnp.int32).at[slot].set(tok_sorted)
    blk_e = jnp.minimum(jnp.searchsorted(pad_end, jnp.arange(n_blocks, dtype=jnp.int32) * MOE_BLOCK,
                                         side="right"), n_exp - 1).astype(jnp.int32)
    x_pad = jnp.concatenate([h_bf16, jnp.zeros((1, d), h_bf16.dtype)], axis=0)
    xb = x_pad[slot_tok]
    yb = expert_blocks(xb, blk_e, w_gu, w_down)
    slot_of_asg = jnp.zeros((n_asg,), jnp.int32).at[order].set(slot).reshape(n_tok, MOE_TOP_K)
    f = jnp.zeros((n_tok, d), F32)
    for kk in range(MOE_TOP_K):
        f = f + yb[slot_of_asg[:, kk]] * gate[:, kk:kk + 1]
    return f


def _s5_weights(lam_re, lam_im, log_dt, b_re, b_im, c_re, c_im, t_chunk):
    n_grp, n_state = lam_re.shape[1], lam_re.shape[2]
    cg = b_re.shape[-1]
    lam = lax.complex(lam_re.astype(F32), lam_im.astype(F32))
    dt = jnp.exp(log_dt.astype(F32))[..., None]
    lam_bar = jnp.exp(lam * dt)
    b_bar = ((lam_bar - 1.0) / lam)[..., None] * lax.complex(b_re.astype(F32), b_im.astype(F32))
    c_mat = lax.complex(c_re.astype(F32), c_im.astype(F32))
    ks = jnp.arange(t_chunk + 1, dtype=F32)
    pw = jnp.exp((lam * dt)[..., None] * ks)
    hi = lax.Precision.HIGHEST
    kern = jnp.einsum("dgcp,dgpk,dgpi->dgkci", c_mat, pw[..., :t_chunk], b_bar, precision=hi).real
    t_idx = jnp.arange(t_chunk)
    lag = t_idx[None, :] - t_idx[:, None]
    k_f = kern[0][:, jnp.clip(lag, 0, t_chunk - 1)]
    k_r = kern[1][:, jnp.clip(-lag, 0, t_chunk - 1)]
    toep = (jnp.where((lag >= 0)[None, :, :, None, None], k_f, 0.0)
            + jnp.where((lag <= 0)[None, :, :, None, None], k_r, 0.0))
    toep = toep.transpose(0, 1, 4, 2, 3).reshape(n_grp, t_chunk * cg, t_chunk * cg)
    pf = pw[0][..., :t_chunk][..., ::-1]
    pr = pw[1][..., :t_chunk]
    bend_f = jnp.einsum("gps,gpc->gscp", pf, b_bar[0]).reshape(n_grp, t_chunk * cg, n_state)
    bend_r = jnp.einsum("gps,gpc->gscp", pr, b_bar[1]).reshape(n_grp, t_chunk * cg, n_state)
    bend = jnp.stack([bend_f.real, bend_f.imag, bend_r.real, bend_r.imag], axis=1)
    of = jnp.einsum("gcp,gpt->gptc", c_mat[0], pw[0][..., 1:]).reshape(n_grp, n_state, t_chunk * cg)
    orv = jnp.einsum("gcp,gpt->gptc", c_mat[1], pw[1][..., 1:][..., ::-1]).reshape(n_grp, n_state, t_chunk * cg)
    cout = jnp.stack([of.real, -of.imag, orv.real, -orv.imag], axis=1)

    def pair_diag(a, axis_r, axis_c):
        g2 = a.reshape(n_grp // 2, 2, *a.shape[1:])
        z = jnp.zeros_like(g2[:, 0])
        top = jnp.concatenate([g2[:, 0], z], axis=axis_c)
        bot = jnp.concatenate([z, g2[:, 1]], axis=axis_c)
        return jnp.concatenate([top, bot], axis=axis_r)

    bend2 = pair_diag(bend, 2, 3)
    cout2 = pair_diag(cout, 2, 3)
    lam_t = pw[..., t_chunk]
    lam_t4 = jnp.stack([lam_t[0].real, lam_t[0].imag, lam_t[1].real, lam_t[1].imag]).reshape(4, n_grp * n_state)
    return toep, bend2, cout2, lam_t4


def _s5_a_body(u_ref, toep_ref, bend_ref, y_ref, e0, e1, e2, e3, *, w1):
    u = u_ref[...]
    for g in range(2):
        y_ref[:, g * w1:(g + 1) * w1] = jnp.dot(u[:, g * w1:(g + 1) * w1], toep_ref[g].astype(BF16),
                                                 preferred_element_type=F32)
    for c, e_ref in enumerate((e0, e1, e2, e3)):
        e_ref[...] = jnp.dot(u, bend_ref[0, c].astype(BF16), preferred_element_type=F32)


def _s5_b_body(e0, e1, e2, e3, lam_ref, xf_re, xf_im, xr_re, xr_im, *, n_batch, n_chunk, n_ctx_chunk):
    lam = lam_ref[...]
    lfr, lfi, lrr, lri = lam[0:1], lam[1:2], lam[2:3], lam[3:4]
    width = e0.shape[1]
    for b in range(n_batch):
        base = b * n_chunk

        def fwd(n, carry):
            sr, si = carry
            row = base + n
            xf_re[pl.ds(row, 1), :] = sr
            xf_im[pl.ds(row, 1), :] = si
            er = e0[pl.ds(row, 1), :]
            ei = e1[pl.ds(row, 1), :]
            return lfr * sr - lfi * si + er, lfr * si + lfi * sr + ei

        def rev(n, carry):
            sr, si = carry
            row = base + jnp.where(n < n_ctx_chunk, n_ctx_chunk - 1 - n, n_chunk + n_ctx_chunk - 1 - n)
            xr_re[pl.ds(row, 1), :] = sr
            xr_im[pl.ds(row, 1), :] = si
            er = e2[pl.ds(row, 1), :]
            ei = e3[pl.ds(row, 1), :]
            return lrr * sr - lri * si + er, lrr * si + lri * sr + ei

        z = jnp.zeros((1, width), F32)
        lax.fori_loop(0, n_chunk, fwd, (z, z))
        lax.fori_loop(0, n_chunk, rev, (z, z))


def _gelu(y):
    return 0.5 * y * (1.0 + jnp.tanh(0.7978845608028654 * (y + 0.044715 * (y * y * y))))


def _s5_c_body(y_ref, u_ref, x0, x1, x2, x3, cout_ref, dsk_ref, o_ref):
    acc = y_ref[...] + u_ref[...].astype(F32) * dsk_ref[...]
    for c, x_ref in enumerate((x0, x1, x2, x3)):
        acc = acc + jnp.dot(x_ref[...].astype(BF16), cout_ref[0, c].astype(BF16), preferred_element_type=F32)
    o_ref[...] = _gelu(acc).astype(o_ref.dtype)


def s5_core(h_f32, weights, d_skip, n_batch, seq, ctx_len):
    toep, bend2, cout2, lam_t4 = weights
    t = S5_CHUNK
    m, d = h_f32.shape
    n_grp = toep.shape[0]
    cg = d // n_grp
    w1 = t * cg
    n_state2 = bend2.shape[-1]
    n_lat = n_batch * seq
    ncc, ncl = ctx_len // t, seq // t
    n_chunk = ncc + ncl
    rows = n_batch * n_chunk
    hb = h_f32.astype(BF16)
    lat = hb[:n_lat].reshape(n_batch, ncl, t, n_grp, cg)
    ctx = hb[n_lat:].reshape(n_batch, ncc, t, n_grp, cg)
    u2 = jnp.concatenate([ctx, lat], axis=1).transpose(0, 1, 3, 2, 4).reshape(rows, n_grp * w1)
    n_pair = n_grp // 2
    ublk = pl.BlockSpec((rows, 2 * w1), lambda p: (0, p))
    eblk = pl.BlockSpec((rows, n_state2), lambda p: (0, p))
    e_shape = jax.ShapeDtypeStruct((rows, n_pair * n_state2), F32)
    y_intra, e0, e1, e2, e3 = pl.pallas_call(
        functools.partial(_s5_a_body, w1=w1),
        grid=(n_pair,),
        in_specs=[ublk,
                  pl.BlockSpec((2, w1, w1), lambda p: (p, 0, 0)),
                  pl.BlockSpec((1, 4, 2 * w1, n_state2), lambda p: (p, 0, 0, 0))],
        out_specs=[ublk, eblk, eblk, eblk, eblk],
        out_shape=[jax.ShapeDtypeStruct((rows, n_grp * w1), F32), e_shape, e_shape, e_shape, e_shape],
        compiler_params=_cparams(("parallel",)),
    )(u2, toep, bend2)

    lanes = n_pair * n_state2
    lb = min(lanes, 512)
    sblk = pl.BlockSpec((rows, lb), lambda p: (0, p))
    xs = pl.pallas_call(
        functools.partial(_s5_b_body, n_batch=n_batch, n_chunk=n_chunk, n_ctx_chunk=ncc),
        grid=(lanes // lb,),
        in_specs=[sblk, sblk, sblk, sblk, pl.BlockSpec((4, lb), lambda p: (0, p))],
        out_specs=[sblk, sblk, sblk, sblk],
        out_shape=[e_shape, e_shape, e_shape, e_shape],
        compiler_params=_cparams(("parallel",)),
    )(e0, e1, e2, e3, lam_t4)

    dsk = jnp.tile(d_skip.astype(F32).reshape(n_grp, 1, cg), (1, t, 1)).reshape(1, n_grp * w1)
    y2 = pl.pallas_call(
        _s5_c_body,
        grid=(n_pair,),
        in_specs=[ublk, ublk, eblk, eblk, eblk, eblk,
                  pl.BlockSpec((1, 4, n_state2, 2 * w1), lambda p: (p, 0, 0, 0)),
                  pl.BlockSpec((1, 2 * w1), lambda p: (0, p))],
        out_specs=ublk,
        out_shape=jax.ShapeDtypeStruct((rows, n_grp * w1), BF16),
        compiler_params=_cparams(("parallel",)),
    )(y_intra, u2, *xs, cout2, dsk)
    y5 = y2.reshape(n_batch, n_chunk, n_grp, t, cg).transpose(0, 1, 3, 2, 4)
    y_ctx = y5[:, :ncc].reshape(n_batch * ctx_len, d)
    y_lat = y5[:, ncc:].reshape(n_lat, d)
    return jnp.concatenate([y_lat, y_ctx], axis=0)


def _shift_body(h_ref, hp_ref, hn_ref, mu_ref, *o_refs, tm, tiles_lat, tiles_ctx, n_lat_tiles):
    i = pl.program_id(0)
    h = h_ref[...]
    first = jnp.where(i < n_lat_tiles, i % tiles_lat == 0, (i - n_lat_tiles) % tiles_ctx == 0)
    last = jnp.where(i < n_lat_tiles, i % tiles_lat == tiles_lat - 1,
                     (i - n_lat_tiles) % tiles_ctx == tiles_ctx - 1)
    rid = lax.broadcasted_iota(jnp.int32, h.shape, 0)
    prev_edge = jnp.where(first, 0.0, hp_ref[7:8, :])
    next_edge = jnp.where(last, 0.0, hn_ref[0:1, :])
    prev = jnp.where(rid == 0, prev_edge, pltpu.roll(h, 1, 0))
    nxt = jnp.where(rid == tm - 1, next_edge, pltpu.roll(h, tm - 1, 0))
    xx = 0.5 * (prev + nxt) - h
    for m, o_ref in enumerate(o_refs):
        o_ref[...] = (h + xx * mu_ref[m:m + 1, :]).astype(o_ref.dtype)


def token_shift_mix(h, mu, tm, seq, ctx_len, n_batch):
    m, d = h.shape
    n_mix = mu.shape[0]
    n_lat_tiles = n_batch * seq // tm
    nb8 = m // 8
    per = tm // 8
    row = pl.BlockSpec((tm, d), lambda i: (i, 0))
    return pl.pallas_call(
        functools.partial(_shift_body, tm=tm, tiles_lat=seq // tm, tiles_ctx=max(ctx_len // tm, 1),
                          n_lat_tiles=n_lat_tiles),
        grid=(m // tm,),
        in_specs=[row,
                  pl.BlockSpec((8, d), lambda i: (jnp.maximum(i * per - 1, 0), 0)),
                  pl.BlockSpec((8, d), lambda i: (jnp.minimum((i + 1) * per, nb8 - 1), 0)),
                  pl.BlockSpec((n_mix, d), lambda i: (0, 0))],
        out_specs=[row] * n_mix,
        out_shape=[jax.ShapeDtypeStruct((m, d), BF16)] * n_mix,
        compiler_params=_cparams(("parallel",)),
    )(h, h, h, mu.astype(F32))


def _rwkv_scan_body(r_ref, k_ref, v_ref, w_ref, a_ref, kk_ref, ka_ref, o_ref, s_ref, *, tb, n):
    @pl.when(pl.program_id(0) == 0)
    def _():
        s_ref[...] = jnp.zeros_like(s_ref)

    k_k = kk_ref[...]
    k_a = ka_ref[...]

    def step(t, carry):
        r = r_ref[t]
        k = k_ref[t]
        v = v_ref[t]
        w = w_ref[t]
        a = a_ref[t]
        kk = k * k_k
        nrm = jnp.sqrt(jnp.sum(kk * kk, axis=0, keepdims=True))
        kk = kk / jnp.maximum(nrm, 1e-12)
        nkk = -kk
        akk = kk * a
        kd = k * (1.0 + (a - 1.0) * k_a)
        wr = w * r
        sa = jnp.zeros_like(v)
        so = jnp.zeros_like(v)
        for j in range(n):
            s_j = s_ref[j]
            sa = sa + s_j * nkk[j:j + 1, :]
            so = so + s_j * wr[j:j + 1, :]
        c_a = jnp.sum(akk * r, axis=0, keepdims=True)
        c_k = jnp.sum(kd * r, axis=0, keepdims=True)
        o_ref[t] = so + sa * c_a + v * c_k
        for j in range(n):
            s_ref[j] = s_ref[j] * w[j:j + 1, :] + sa * akk[j:j + 1, :] + v * kd[j:j + 1, :]
        return carry

    lax.fori_loop(0, tb, step, 0)


def rwkv_scan(r, k, v, w, a, k_k, k_a, tb=32):
    steps, n, ch = r.shape
    while steps % tb:
        tb //= 2
    blk = pl.BlockSpec((tb, n, ch), lambda i: (i, 0, 0))
    tab = pl.BlockSpec((n, ch), lambda i: (0, 0))
    return pl.pallas_call(
        functools.partial(_rwkv_scan_body, tb=tb, n=n),
        grid=(steps // tb,),
        in_specs=[blk, blk, blk, blk, blk, tab, tab],
        out_specs=blk,
        out_shape=jax.ShapeDtypeStruct((steps, n, ch), F32),
        scratch_shapes=[pltpu.VMEM((n, n, ch), F32)],
        compiler_params=_cparams(("arbitrary",)),
    )(r, k, v, w, a, k_k, k_a)


def _rwkv_post_body(o_ref, r_ref, k_ref, v_ref, a0_ref, a1_ref, g_ref, ka_ref, rk_ref, lg_ref, lb_ref, y_ref):
    o = o_ref[...]
    mean = jnp.mean(o, axis=1, keepdims=True)
    var = jnp.mean(jnp.square(o - mean), axis=1, keepdims=True)
    on = (o - mean) * lax.rsqrt(var + RW_GN_EPS) * lg_ref[...] + lb_ref[...]
    r = r_ref[...]
    k = k_ref[...]
    k_a = ka_ref[...]
    rk = rk_ref[...]
    kd0 = k * (1.0 + (a0_ref[...] - 1.0) * k_a)
    kd1 = k * (1.0 + (a1_ref[...] - 1.0) * k_a)
    bonus = (jnp.sum(r * kd0 * rk, axis=1, keepdims=True)
             + jnp.sum(r * kd1 * rk, axis=1, keepdims=True)) * v_ref[...]
    y = (on + bonus)
    y_ref[...] = (y * g_ref[...]).astype(y_ref.dtype)


def rwkv_post(o, r, k, v, a0, a1, gate, k_a, r_k, ln_g, ln_b, tb=32):
    steps, n, ch = o.shape
    while steps % tb:
        tb //= 2
    blk = pl.BlockSpec((tb, n, ch), lambda i: (i, 0, 0))
    tab = pl.BlockSpec((1, n, ch), lambda i: (0, 0, 0))
    return pl.pallas_call(
        _rwkv_post_body,
        grid=(steps // tb,),
        in_specs=[blk] * 7 + [tab] * 4,
        out_specs=blk,
        out_shape=jax.ShapeDtypeStruct((steps, n, ch), BF16),
        compiler_params=_cparams(("parallel",)),
    )(o, r, k, v, a0, a1, gate, k_a[None], r_k[None], ln_g[None], ln_b[None])


def _pad_cols(w, mult=128):
    n = w.shape[-1]
    p = -(-n // mult) * mult - n
    return jnp.pad(w, [(0, 0)] * (w.ndim - 1) + [(0, p)]) if p else w


def _pad_rows(w, mult=128):
    n = w.shape[-2]
    p = -(-n // mult) * mult - n
    return jnp.pad(w, [(0, 0)] * (w.ndim - 2) + [(0, p), (0, 0)]) if p else w


def rwkv_mixer_tokens(h, mu, w_rkv, w0, w1, w2, a0, a1, a2, k_k, k_a, r_k, ln_g, ln_b, g1, g2,
                      tm, n_batch, seq, ctx_len):
    m, d = h.shape
    n_heads, n = r_k.shape
    n_lat = n_batch * seq
    ts = _row_tile(seq, ctx_len, cap=256)
    x_r, x_k, x_v, x_w, x_a, x_g = token_shift_mix(h, mu, ts, seq, ctx_len, n_batch)
    mm = functools.partial(matmul, tm=tm)
    r = mm(x_r, w_rkv[0])
    k = mm(x_k, w_rkv[1])
    v = mm(x_v, w_rkv[2])
    gate = mm(mm(x_g, _pad_cols(g1), act="sigmoid", out_dtype=BF16), _pad_rows(g2))
    dec, aa = [], []
    for dn in range(2):
        lw = mm(x_w, _pad_cols(w1[dn]), act="tanh", out_dtype=BF16)
        dec.append(mm(lw, _pad_rows(w2[dn]), bias=w0[dn], act="decay"))
        la = mm(x_a, _pad_cols(a1[dn]), out_dtype=BF16)
        aa.append(mm(la, _pad_rows(a2[dn]), bias=a0[dn], act="sigmoid"))

    def to_seq(x):
        lat = x[:n_lat].reshape(n_batch, seq, n_heads, n)
        ctx = x[n_lat:].reshape(n_batch, ctx_len, n_heads, n)
        return jnp.concatenate([ctx, lat], axis=1).transpose(1, 3, 0, 2).reshape(ctx_len + seq, n, n_batch * n_heads)

    def rev_time(x):
        return jnp.concatenate([x[:ctx_len][::-1], x[ctx_len:][::-1]], axis=0)

    r_t, k_t, v_t, g_t = to_seq(r), to_seq(k), to_seq(v), to_seq(gate)
    d0_t, d1_t, a0_t, a1_t = to_seq(dec[0]), to_seq(dec[1]), to_seq(aa[0]), to_seq(aa[1])

    def both(x0, x1):
        return jnp.concatenate([x0, rev_time(x1)], axis=-1)

    def table(vec):
        return jnp.tile(vec.astype(F32).reshape(n_heads, n).T[:, None, :], (1, n_batch, 1)).reshape(n, n_batch * n_heads)

    kk_tab, ka_tab = table(k_k), table(k_a)
    o = rwkv_scan(both(r_t, r_t), both(k_t, k_t), both(v_t, v_t), both(d0_t, d1_t), both(a0_t, a1_t),
                  jnp.concatenate([kk_tab, kk_tab], axis=-1), jnp.concatenate([ka_tab, ka_tab], axis=-1))
    bh = n_batch * n_heads
    o_sum = o[..., :bh] + rev_time(o[..., bh:])
    y_t = rwkv_post(o_sum, r_t, k_t, v_t, a0_t, a1_t, g_t, ka_tab, table(r_k.reshape(-1)),
                    table(ln_g), table(ln_b))
    y4 = y_t.reshape(ctx_len + seq, n, n_batch, n_heads).transpose(2, 0, 3, 1)
    y_ctx = y4[:, :ctx_len].reshape(n_batch * ctx_len, d)
    y_lat = y4[:, ctx_len:].reshape(n_lat, d)
    return jnp.concatenate([y_lat, y_ctx], axis=0)


def kernel(x, c, ctx, c_ctx, mod_w, mod_b, norm_g, final_g, na_w_qkv, na_w_o, na_rpb, s5_lam_re, s5_lam_im, s5_log_dt, s5_b_re, s5_b_im, s5_c_re, s5_c_im, s5_d, s5_w_glu, rw_mu, rw_w_rkv, rw_w0, rw_w1, rw_w2, rw_a0, rw_a1, rw_a2, rw_k_k, rw_k_a, rw_r_k, rw_ln_g, rw_ln_b, rw_g1, rw_g2, rw_w_o, moe_wg, moe_bg, moe_we, moe_be, moe_w_gu, moe_w_down):
    n_batch, seq, d = x.shape
    ctx_len = ctx.shape[1]
    depth = mod_w.shape[0]
    n_lat = n_batch * seq
    n_ctx = n_batch * ctx_len
    tm = _row_tile(seq, n_ctx)
    tpb = seq // tm
    xs = jnp.concatenate([x.reshape(n_lat, d), ctx.reshape(n_ctx, d)], axis=0).astype(F32)

    cvecs = jnp.concatenate([c.astype(F32), c_ctx.astype(F32)[None]], axis=0)
    rows_pad = -(-(n_batch + 1) // 8) * 8
    cvecs = jnp.pad(cvecs, ((0, rows_pad - n_batch - 1), (0, 0)))
    mods = modulation(cvecs, mod_w, mod_b)[:, :n_batch + 1].reshape(depth, n_batch + 1, 6, 1, d)
    mods = mods.transpose(0, 2, 1, 3, 4)

    for i in range(depth):
        last = i == depth - 1
        sh1, sc1, g1, sh2, sc2, g2 = (mods[i, q] for q in range(6))
        pn = functools.partial(prenorm, tm=tm, tiles_per_batch=tpb, n_batch=n_batch)
        mm = functools.partial(matmul, tm=tm, tiles_per_batch=tpb, n_batch=n_batch)
        mix, j = i % N_MIXERS, i // N_MIXERS
        if mix == 0:
            (hb,) = pn(xs, norm_g[i, 0], 1.0 + sc1, sh1, out_dtypes=(BF16,))
            n_heads = na_rpb.shape[1]
            qscale = jnp.concatenate([jnp.full((d,), (d // n_heads) ** -0.5, F32), jnp.ones((2 * d,), F32)])
            qkv = mm(hb, na_w_qkv[j], out_dtype=BF16, colscale=qscale)
            o = na_attention(qkv, _na_bias_table(na_rpb[j], GRID_W), n_batch, seq, ctx_len, not last)
            if last:
                xs = xs[:n_lat]
            xs = mm(o, na_w_o[j], mode="resid", resid=xs, gate=g1)
        elif mix == 1:
            (hf,) = pn(xs, norm_g[i, 0], 1.0 + sc1, sh1, out_dtypes=(F32,))
            wts = _s5_weights(s5_lam_re[j], s5_lam_im[j], s5_log_dt[j], s5_b_re[j], s5_b_im[j],
                              s5_c_re[j], s5_c_im[j], S5_CHUNK)
            y = s5_core(hf, wts, s5_d[j], n_batch, seq, ctx_len)
            if last:
                xs, y = xs[:n_lat], y[:n_lat]
            xs = mm(y, s5_w_glu[j], mode="glu", resid=xs, gate=g1)
        else:
            (hf,) = pn(xs, norm_g[i, 0], 1.0 + sc1, sh1, out_dtypes=(F32,))
            y = rwkv_mixer_tokens(hf, rw_mu[j], rw_w_rkv[j], rw_w0[j], rw_w1[j], rw_w2[j], rw_a0[j], rw_a1[j],
                                  rw_a2[j], rw_k_k[j], rw_k_a[j], rw_r_k[j], rw_ln_g[j], rw_ln_b[j],
                                  rw_g1[j], rw_g2[j], tm, n_batch, seq, ctx_len)
            if last:
                xs, y = xs[:n_lat], y[:n_lat]
            xs = mm(y, rw_w_o[j], mode="resid", resid=xs, gate=g1)
        h2f, h2b = pn(xs, norm_g[i, 1], 1.0 + sc2, sh2, out_dtypes=(F32, BF16))
        f = hier_moe(h2f, h2b, moe_wg[i], moe_bg[i], moe_we[i], moe_be[i], moe_w_gu[i], moe_w_down[i], tm)
        seg_gate = jnp.concatenate(
            [jnp.repeat(g2[:n_batch, 0], seq, axis=0), jnp.repeat(g2[n_batch:, 0], n_ctx, axis=0)], axis=0)
        xs = xs + seg_gate[:xs.shape[0]] * f
    ones = jnp.ones((n_batch + 1, 1, d), F32)
    (out,) = prenorm(xs[:n_lat], final_g, ones, jnp.zeros_like(ones), tm, tpb, n_batch, out_dtypes=(x.dtype,))
    return out.reshape(n_batch, seq, d)
```

```python
import functools
import math

import jax
import jax.numpy as jnp
from jax import lax
from jax.experimental import pallas as pl
from jax.experimental.pallas import tpu as pltpu

F32 = jnp.float32
BF16 = jnp.bfloat16

GRID_W = 64
N_MIXERS = 3
NORM_EPS = 1e-6
RW_GN_EPS = 64e-5
MOE_TOP_K = 2
MOE_BLOCK = 128
S5_CHUNK = 16
NEG_BIG = -1e30
VMEM_LIMIT_BYTES = 48 * 1024 * 1024
VMEM_LIMIT_BIG_BYTES = 56 * 1024 * 1024


def _cparams(sem, limit=VMEM_LIMIT_BYTES):
    return pltpu.CompilerParams(dimension_semantics=sem, vmem_limit_bytes=limit)


def _row_tile(n_lat_per_batch, n_ctx_total, cap=512):
    t = cap
    while t > 8 and (n_lat_per_batch % t or (n_ctx_total and n_ctx_total % t)):
        t //= 2
    return t


def _seg_fn(tiles_per_batch, n_batch):
    def seg(i):
        return jnp.minimum(i // tiles_per_batch, n_batch)
    return seg


def _bdot(a, b):
    return jnp.dot(a.astype(BF16), b.astype(BF16), preferred_element_type=F32)


def _split3(x):
    hi = x.astype(BF16)
    r1 = x - hi.astype(F32)
    mid = r1.astype(BF16)
    lo = (r1 - mid.astype(F32)).astype(BF16)
    return hi, mid, lo


def _prenorm_body(x_ref, g_ref, sc_ref, sh_ref, *o_refs):
    x = x_ref[...]
    ms = jnp.mean(x * x, axis=-1, keepdims=True)
    h = (x * lax.rsqrt(ms + NORM_EPS)) * g_ref[...]
    h = h * sc_ref[0] + sh_ref[0]
    for o_ref in o_refs:
        o_ref[...] = h.astype(o_ref.dtype)


def prenorm(x, g, scale1p, shift, tm, tiles_per_batch, n_batch, out_dtypes):
    m, d = x.shape
    seg = _seg_fn(tiles_per_batch, n_batch)
    row = pl.BlockSpec((tm, d), lambda i: (i, 0))
    tab = pl.BlockSpec((1, 1, d), lambda i: (seg(i), 0, 0))
    outs = pl.pallas_call(
        _prenorm_body,
        grid=(m // tm,),
        in_specs=[row, pl.BlockSpec((1, d), lambda i: (0, 0)), tab, tab],
        out_specs=[row for _ in out_dtypes],
        out_shape=[jax.ShapeDtypeStruct((m, d), dt) for dt in out_dtypes],
        compiler_params=_cparams(("parallel",)),
    )(x, g.reshape(1, d), scale1p, shift)
    return outs


def _softplus(z):
    return jnp.maximum(z, 0.0) + jnp.log(1.0 + jnp.exp(-jnp.abs(z)))


def _apply_act(y, act):
    if act is None:
        return y
    if act == "tanh":
        return jnp.tanh(y)
    if act == "sigmoid":
        return jax.nn.sigmoid(y)
    if act == "decay":
        return jnp.exp(-jnp.exp(-_softplus(-y) - 0.5))
    raise ValueError(act)


def _mm_body(*refs, mode, act, has_bias, has_scale):
    it = iter(refs)
    x_ref = next(it)
    w_ref = next(it)
    w2_ref = next(it) if mode == "glu" else None
    b_ref = next(it) if has_bias else None
    s_ref = next(it) if has_scale else None
    r_ref = next(it) if mode in ("resid", "glu") else None
    g_ref = next(it) if mode in ("resid", "glu") else None
    o_ref = next(it)
    x = x_ref[...].astype(BF16)
    acc = jnp.dot(x, w_ref[...].astype(BF16), preferred_element_type=F32)
    if mode == "glu":
        acc2 = jnp.dot(x, w2_ref[...].astype(BF16), preferred_element_type=F32)
        acc = acc * jax.nn.sigmoid(acc2)
    if has_bias:
        acc = acc + b_ref[...]
    acc = _apply_act(acc, act)
    if has_scale:
        acc = acc * s_ref[...]
    if mode in ("resid", "glu"):
        acc = r_ref[...] + g_ref[0] * acc
    o_ref[...] = acc.astype(o_ref.dtype)


def matmul(x, w, *, tm, tn=512, out_dtype=F32, mode="plain", act=None, bias=None, colscale=None,
           resid=None, gate=None, tiles_per_batch=None, n_batch=None):
    m, k = x.shape
    n = w.shape[1] // 2 if mode == "glu" else w.shape[1]
    tn = min(tn, n)
    assert m % tm == 0 and n % tn == 0, (m, tm, n, tn)
    nj = n // tn
    in_specs = [pl.BlockSpec((tm, k), lambda i, j: (i, 0)), pl.BlockSpec((k, tn), lambda i, j: (0, j))]
    args = [x, w]
    if mode == "glu":
        in_specs.append(pl.BlockSpec((k, tn), lambda i, j: (0, j + nj)))
        args.append(w)
    col = pl.BlockSpec((1, tn), lambda i, j: (0, j))
    if bias is not None:
        in_specs.append(col)
        args.append(bias.reshape(1, n).astype(F32))
    if colscale is not None:
        in_specs.append(col)
        args.append(colscale.reshape(1, n).astype(F32))
    if mode in ("resid", "glu"):
        seg = _seg_fn(tiles_per_batch, n_batch)
        in_specs.append(pl.BlockSpec((tm, tn), lambda i, j: (i, j)))
        in_specs.append(pl.BlockSpec((1, 1, tn), lambda i, j: (seg(i), 0, j)))
        args += [resid, gate]
    return pl.pallas_call(
        functools.partial(_mm_body, mode=mode, act=act, has_bias=bias is not None,
                          has_scale=colscale is not None),
        grid=(m // tm, nj),
        in_specs=in_specs,
        out_specs=pl.BlockSpec((tm, tn), lambda i, j: (i, j)),
        out_shape=jax.ShapeDtypeStruct((m, n), out_dtype),
        compiler_params=_cparams(("parallel", "parallel")),
    )(*args)


def _router_body(x_ref, w_ref, b_ref, o_ref, *, n_grp, epg):
    xh, xm, xl = _split3(x_ref[...])
    wh, wm, wl = _split3(w_ref[...])
    dot = functools.partial(jnp.dot, preferred_element_type=F32)
    acc = dot(xh, wh) + (dot(xh, wm) + dot(xm, wh)) + (dot(xh, wl) + dot(xm, wm) + dot(xl, wh))
    acc = acc + b_ref[...]
    lane = lax.broadcasted_iota(jnp.int32, acc.shape, 1)
    lane_f = lane.astype(F32)
    far = 1e9

    def first_max(vals):
        m = jnp.max(vals, axis=-1, keepdims=True)
        return m, jnp.min(jnp.where(vals == m, lane_f, far), axis=-1, keepdims=True)

    gl = jnp.where(lane < n_grp, acc, NEG_BIG)
    gmax, grp = first_max(gl)
    p_grp = 1.0 / jnp.sum(jnp.exp(gl - gmax), axis=-1, keepdims=True)
    lo = n_grp + grp * epg
    el = jnp.where((lane_f >= lo) & (lane_f < lo + epg), acc, NEG_BIG)
    m1, i1 = first_max(el)
    m2, i2 = first_max(jnp.where(lane_f == i1, NEG_BIG, el))
    e21 = jnp.exp(m2 - m1)
    g1 = p_grp / (1.0 + e21)
    g2 = p_grp * e21 / (1.0 + e21)
    out = jnp.where(lane == 0, i1 - n_grp, jnp.where(lane == 1, i2 - n_grp,
                                                    jnp.where(lane == 2, g1, jnp.where(lane == 3, g2, 0.0))))
    o_ref[...] = out


def router(x, w, bias, tm, n_grp, epg):
    m, k = x.shape
    n = w.shape[1]
    return pl.pallas_call(
        functools.partial(_router_body, n_grp=n_grp, epg=epg),
        grid=(m // tm,),
        in_specs=[pl.BlockSpec((tm, k), lambda i: (i, 0)), pl.BlockSpec((k, n), lambda i: (0, 0)),
                  pl.BlockSpec((1, n), lambda i: (0, 0))],
        out_specs=pl.BlockSpec((tm, n), lambda i: (i, 0)),
        out_shape=jax.ShapeDtypeStruct((m, n), F32),
        compiler_params=_cparams(("parallel",)),
        name="moe_router",
    )(x, w, bias.reshape(1, n))


def _mod_body(s_ref, w_ref, b_ref, o_ref):
    s = s_ref[...]
    s = s * jax.nn.sigmoid(s)
    o_ref[0] = _bdot(s, w_ref[0]) + b_ref[0]


def modulation(cvecs, mod_w, mod_b, tn=768):
    depth, d, n = mod_w.shape
    rows = cvecs.shape[0]
    while n % tn:
        tn //= 2
    return pl.pallas_call(
        _mod_body,
        grid=(depth, n // tn),
        in_specs=[pl.BlockSpec((rows, d), lambda l, j: (0, 0)),
                  pl.BlockSpec((1, d, tn), lambda l, j: (l, 0, j)),
                  pl.BlockSpec((1, 1, tn), lambda l, j: (l, 0, j))],
        out_specs=pl.BlockSpec((1, rows, tn), lambda l, j: (l, 0, j)),
        out_shape=jax.ShapeDtypeStruct((depth, rows, n), F32),
        compiler_params=_cparams(("parallel", "parallel")),
    )(cvecs, mod_w, mod_b.reshape(depth, 1, n))


def _na_bias_table(rpb, grid_w):
    n_heads = rpb.shape[0]
    kr = (rpb.shape[1] + 1) // 2
    kcw = (rpb.shape[2] + 1) // 2
    q = jnp.arange(grid_w)[:, None]
    kc = jnp.arange(grid_w)[None, :]
    win = jnp.clip(q - kcw // 2, 0, grid_w - kcw)
    valid = (kc >= win) & (kc < win + kcw)
    dc = jnp.clip(kc - q, 1 - kcw, kcw - 1) + kcw - 1
    d = jnp.arange(kr)[:, None]
    j = jnp.arange(kr)[None, :]
    dr = j - d + kr - 1
    tab = rpb.astype(F32)[:, dr][:, :, :, dc]
    tab = jnp.where(valid[None, None, None], tab, NEG_BIG)
    return tab.transpose(0, 1, 3, 2, 4).reshape(n_heads, kr, grid_w, kr * grid_w)


def _na_body(q_ref, k_ref, v_ref, kc_ref, vc_ref, bias_ref, o_ref, *, hpb, dh, grid_w, kr, rows):
    r = pl.program_id(2)
    rs = jnp.clip(r - kr // 2, 0, rows - kr)
    start = pl.multiple_of(rs * grid_w, grid_w)
    nt = (((1,), (1,)), ((), ()))
    for h in range(hpb):
        sl = slice(h * dh, (h + 1) * dh)
        q = q_ref[:, sl]
        k = k_ref[pl.ds(start, kr * grid_w), sl]
        v = v_ref[pl.ds(start, kr * grid_w), sl]
        s = lax.dot_general(q, k, nt, preferred_element_type=F32) + bias_ref[h, 0]
        sc = lax.dot_general(q, kc_ref[:, sl], nt, preferred_element_type=F32)
        m = jnp.maximum(jnp.max(s, axis=-1, keepdims=True), jnp.max(sc, axis=-1, keepdims=True))
        p = jnp.exp(s - m)
        pc = jnp.exp(sc - m)
        den = jnp.sum(p, axis=-1, keepdims=True) + jnp.sum(pc, axis=-1, keepdims=True)
        o = (jnp.dot(p.astype(BF16), v, preferred_element_type=F32)
             + jnp.dot(pc.astype(BF16), vc_ref[:, sl], preferred_element_type=F32))
        o_ref[:, sl] = (o / den).astype(o_ref.dtype)


def _ctx_attn_body(q_ref, k_ref, v_ref, o_ref, *, hpb, dh):
    nt = (((1,), (1,)), ((), ()))
    for h in range(hpb):
        sl = slice(h * dh, (h + 1) * dh)
        s = lax.dot_general(q_ref[:, sl], k_ref[:, sl], nt, preferred_element_type=F32)
        m = jnp.max(s, axis=-1, keepdims=True)
        p = jnp.exp(s - m)
        den = jnp.sum(p, axis=-1, keepdims=True)
        o = jnp.dot(p.astype(BF16), v_ref[:, sl], preferred_element_type=F32)
        o_ref[:, sl] = (o / den).astype(o_ref.dtype)


def na_attention(qkv, bias_tab, n_batch, seq, ctx_len, need_ctx):
    d = qkv.shape[1] // 3
    n_heads, kr = bias_tab.shape[0], bias_tab.shape[1]
    dh = d // n_heads
    hpb = max(1, min(n_heads, 256 // dh))
    bw = hpb * dh
    n_d = d // bw
    rows = seq // GRID_W
    n_lat = n_batch * seq
    ctx_blk0 = n_lat // ctx_len

    def d_of(r):
        return r - jnp.clip(r - kr // 2, 0, rows - kr)

    o_lat = pl.pallas_call(
        functools.partial(_na_body, hpb=hpb, dh=dh, grid_w=GRID_W, kr=kr, rows=rows),
        grid=(n_batch, n_d, rows),
        in_specs=[
            pl.BlockSpec((GRID_W, bw), lambda b, g, r: (b * rows + r, g)),
            pl.BlockSpec((seq, bw), lambda b, g, r: (b, n_d + g)),
            pl.BlockSpec((seq, bw), lambda b, g, r: (b, 2 * n_d + g)),
            pl.BlockSpec((ctx_len, bw), lambda b, g, r: (ctx_blk0 + b, n_d + g)),
            pl.BlockSpec((ctx_len, bw), lambda b, g, r: (ctx_blk0 + b, 2 * n_d + g)),
            pl.BlockSpec((hpb, 1, GRID_W, kr * GRID_W), lambda b, g, r: (g, d_of(r), 0, 0)),
        ],
        out_specs=pl.BlockSpec((GRID_W, bw), lambda b, g, r: (b * rows + r, g)),
        out_shape=jax.ShapeDtypeStruct((n_lat, d), BF16),
        compiler_params=_cparams(("parallel", "parallel", "arbitrary")),
    )(qkv, qkv, qkv, qkv, qkv, bias_tab)
    if not need_ctx:
        return o_lat
    o_ctx = pl.pallas_call(
        functools.partial(_ctx_attn_body, hpb=hpb, dh=dh),
        grid=(n_batch, n_d),
        in_specs=[
            pl.BlockSpec((ctx_len, bw), lambda b, g: (ctx_blk0 + b, g)),
            pl.BlockSpec((ctx_len, bw), lambda b, g: (ctx_blk0 + b, n_d + g)),
            pl.BlockSpec((ctx_len, bw), lambda b, g: (ctx_blk0 + b, 2 * n_d + g)),
        ],
        out_specs=pl.BlockSpec((ctx_len, bw), lambda b, g: (b, g)),
        out_shape=jax.ShapeDtypeStruct((n_batch * ctx_len, d), BF16),
        compiler_params=_cparams(("parallel", "parallel")),
    )(qkv, qkv, qkv)
    return jnp.concatenate([o_lat, o_ctx], axis=0)


def _expert_body(be_ref, idx_ref, x_hbm, wgu_ref, wd_ref, o_ref, xbuf, sem, wgu_s, wd_s, *, hidden):
    j = pl.program_id(0)

    def row_copy(r):
        return pltpu.make_async_copy(x_hbm.at[pl.ds(idx_ref[0, 0, r], 1)], xbuf.at[pl.ds(r, 1)], sem.at[0])

    for r in range(MOE_BLOCK):
        row_copy(r).start()

    prev = be_ref[jnp.maximum(j - 1, 0)]
    changed = jnp.logical_or(j == 0, be_ref[j] != prev)

    @pl.when(changed)
    def _():
        wgu_s[...] = wgu_ref[0].astype(BF16)
        wd_s[...] = wd_ref[0].astype(BF16)

    for r in range(MOE_BLOCK):
        row_copy(r).wait()

    gu = jnp.dot(xbuf[...].astype(BF16), wgu_s[...], preferred_element_type=F32)
    g = gu[:, :hidden]
    u = gu[:, hidden:]
    a = (g * jax.nn.sigmoid(g) * u).astype(BF16)
    o_ref[...] = jnp.dot(a, wd_s[...], preferred_element_type=F32)


def expert_blocks(x, slot_tok, blk_e, w_gu, w_down):
    d = x.shape[1]
    hidden = w_down.shape[1]
    n_blocks = blk_e.shape[0]
    return pl.pallas_call(
        functools.partial(_expert_body, hidden=hidden),
        grid_spec=pltpu.PrefetchScalarGridSpec(
            num_scalar_prefetch=1,
            grid=(n_blocks,),
            in_specs=[pl.BlockSpec((1, 1, MOE_BLOCK), lambda j, be: (j, 0, 0), memory_space=pltpu.SMEM),
                      pl.BlockSpec(memory_space=pl.ANY),
                      pl.BlockSpec((1, d, 2 * hidden), lambda j, be: (be[j], 0, 0)),
                      pl.BlockSpec((1, hidden, d), lambda j, be: (be[j], 0, 0))],
            out_specs=pl.BlockSpec((MOE_BLOCK, d), lambda j, be: (j, 0)),
            scratch_shapes=[pltpu.VMEM((MOE_BLOCK, d), F32), pltpu.SemaphoreType.DMA((1,)),
                            pltpu.VMEM((d, 2 * hidden), BF16), pltpu.VMEM((hidden, d), BF16)]),
        out_shape=jax.ShapeDtypeStruct((n_blocks * MOE_BLOCK, d), F32),
        compiler_params=_cparams(("arbitrary",), VMEM_LIMIT_BIG_BYTES),
        name="moe_experts",
    )(blk_e, slot_tok.reshape(n_blocks, 1, MOE_BLOCK), x, w_gu, w_down)


def _combine_body(idx_ref, route_ref, x_ref, g_ref, yb_hbm, o_ref, ybuf, sem, *, tmc):
    def row_copy(k, r):
        return pltpu.make_async_copy(yb_hbm.at[pl.ds(idx_ref[0, 0, k * tmc + r], 1)],
                                     ybuf.at[k, pl.ds(r, 1)], sem.at[0])

    for k in range(MOE_TOP_K):
        for r in range(tmc):
            row_copy(k, r).start()
    for k in range(MOE_TOP_K):
        for r in range(tmc):
            row_copy(k, r).wait()
    route = route_ref[...]
    f = ybuf[0] * route[:, MOE_TOP_K:MOE_TOP_K + 1]
    for k in range(1, MOE_TOP_K):
        f = f + ybuf[k] * route[:, MOE_TOP_K + k:MOE_TOP_K + k + 1]
    o_ref[...] = x_ref[...] + g_ref[0] * f


def moe_combine(xs, yb, slot_of_asg, route, gate_tab, n_batch, seq):
    n_tok, d = xs.shape
    tmc = MOE_BLOCK
    n_tiles = n_tok // tmc
    idx = slot_of_asg.reshape(n_tiles, tmc, MOE_TOP_K).transpose(0, 2, 1).reshape(n_tiles, 1, MOE_TOP_K * tmc)
    seg = _seg_fn(seq // tmc, n_batch)
    return pl.pallas_call(
        functools.partial(_combine_body, tmc=tmc),
        grid=(n_tiles,),
        in_specs=[pl.BlockSpec((1, 1, MOE_TOP_K * tmc), lambda i: (i, 0, 0), memory_space=pltpu.SMEM),
                  pl.BlockSpec((tmc, route.shape[1]), lambda i: (i, 0)),
                  pl.BlockSpec((tmc, d), lambda i: (i, 0)),
                  pl.BlockSpec((1, 1, d), lambda i: (seg(i), 0, 0)),
                  pl.BlockSpec(memory_space=pl.ANY)],
        out_specs=pl.BlockSpec((tmc, d), lambda i: (i, 0)),
        out_shape=jax.ShapeDtypeStruct((n_tok, d), F32),
        scratch_shapes=[pltpu.VMEM((MOE_TOP_K, tmc, d), F32), pltpu.SemaphoreType.DMA((1,))],
        compiler_params=_cparams(("arbitrary",)),
        name="moe_combine",
    )(idx, route, xs, gate_tab, yb)


def hier_moe(xs, h, wg, bg, we, be, w_gu, w_down, gate_tab, tm, n_batch, seq):
    n_tok, d = h.shape
    n_grp = wg.shape[1]
    n_exp = we.shape[1]
    epg = n_exp // n_grp
    n_logit = n_grp + n_exp
    n_pad = -(-n_logit // 128) * 128
    wcat = jnp.pad(jnp.concatenate([wg, we], axis=1), ((0, 0), (0, n_pad - n_logit)))
    bcat = jnp.pad(jnp.concatenate([bg, be], axis=0), (0, n_pad - n_logit))
    route = router(h, wcat, bcat, tm, n_grp, epg)

    flat_e = route[:, :MOE_TOP_K].astype(jnp.int32).reshape(-1)
    n_asg = flat_e.shape[0]
    onehot = (flat_e[:, None] == jnp.arange(n_exp, dtype=jnp.int32)[None, :]).astype(F32)
    ck = 128
    oh3 = onehot.reshape(n_asg // ck, ck, n_exp)
    tri = jnp.tril(jnp.ones((ck, ck), F32))
    within = jnp.einsum("ij,cjk->cik", tri, oh3, precision=lax.Precision.HIGHEST)
    tot = within[:, -1, :]
    before = jnp.cumsum(tot, axis=0) - tot
    rank = jnp.sum(oh3 * (within + before[:, None, :]), axis=-1).reshape(n_asg).astype(jnp.int32) - 1
    counts = jnp.sum(tot, axis=0).astype(jnp.int32)
    padded = (counts + MOE_BLOCK - 1) // MOE_BLOCK * MOE_BLOCK
    pad_end = jnp.cumsum(padded)
    pad_start = pad_end - padded
    slot = jnp.sum(onehot * pad_start.astype(F32)[None, :], axis=-1).astype(jnp.int32) + rank
    n_blocks = -(-n_asg // MOE_BLOCK) + n_exp
    slot_tok = jnp.zeros((n_blocks * MOE_BLOCK,), jnp.int32).at[slot].set(
        jnp.arange(n_asg, dtype=jnp.int32) // MOE_TOP_K)
    blk_e = jnp.minimum(jnp.searchsorted(pad_end, jnp.arange(n_blocks, dtype=jnp.int32) * MOE_BLOCK,
                                         side="right"), n_exp - 1).astype(jnp.int32)
    yb = expert_blocks(h, slot_tok, blk_e, w_gu, w_down)
    return moe_combine(xs, yb, slot.reshape(n_tok, MOE_TOP_K), route, gate_tab, n_batch, seq)


def _s5_weights(lam_re, lam_im, log_dt, b_re, b_im, c_re, c_im, t_chunk):
    n_grp, n_state = lam_re.shape[1], lam_re.shape[2]
    cg = b_re.shape[-1]
    lam = lax.complex(lam_re.astype(F32), lam_im.astype(F32))
    dt = jnp.exp(log_dt.astype(F32))[..., None]
    lam_bar = jnp.exp(lam * dt)
    b_bar = ((lam_bar - 1.0) / lam)[..., None] * lax.complex(b_re.astype(F32), b_im.astype(F32))
    c_mat = lax.complex(c_re.astype(F32), c_im.astype(F32))
    ks = jnp.arange(t_chunk + 1, dtype=F32)
    pw = jnp.exp((lam * dt)[..., None] * ks)
    hi = lax.Precision.HIGHEST
    kern = jnp.einsum("dgcp,dgpk,dgpi->dgkci", c_mat, pw[..., :t_chunk], b_bar, precision=hi).real
    t_idx = jnp.arange(t_chunk)
    lag = t_idx[None, :] - t_idx[:, None]
    k_f = kern[0][:, jnp.clip(lag, 0, t_chunk - 1)]
    k_r = kern[1][:, jnp.clip(-lag, 0, t_chunk - 1)]
    toep = (jnp.where((lag >= 0)[None, :, :, None, None], k_f, 0.0)
            + jnp.where((lag <= 0)[None, :, :, None, None], k_r, 0.0))
    toep = toep.transpose(0, 1, 4, 2, 3).reshape(n_grp, t_chunk * cg, t_chunk * cg)
    pf = pw[0][..., :t_chunk][..., ::-1]
    pr = pw[1][..., :t_chunk]
    bend_f = jnp.einsum("gps,gpc->gscp", pf, b_bar[0]).reshape(n_grp, t_chunk * cg, n_state)
    bend_r = jnp.einsum("gps,gpc->gscp", pr, b_bar[1]).reshape(n_grp, t_chunk * cg, n_state)
    bend = jnp.stack([bend_f.real, bend_f.imag, bend_r.real, bend_r.imag], axis=1)
    of = jnp.einsum("gcp,gpt->gptc", c_mat[0], pw[0][..., 1:]).reshape(n_grp, n_state, t_chunk * cg)
    orv = jnp.einsum("gcp,gpt->gptc", c_mat[1], pw[1][..., 1:][..., ::-1]).reshape(n_grp, n_state, t_chunk * cg)
    cout = jnp.stack([of.real, -of.imag, orv.real, -orv.imag], axis=1)

    def pair_diag(a, axis_r, axis_c):
        g2 = a.reshape(n_grp // 2, 2, *a.shape[1:])
        z = jnp.zeros_like(g2[:, 0])
        top = jnp.concatenate([g2[:, 0], z], axis=axis_c)
        bot = jnp.concatenate([z, g2[:, 1]], axis=axis_c)
        return jnp.concatenate([top, bot], axis=axis_r)

    bend2 = pair_diag(bend, 2, 3)
    cout2 = pair_diag(cout, 2, 3)
    lam_t = pw[..., t_chunk]
    lam_t4 = jnp.stack([lam_t[0].real, lam_t[0].imag, lam_t[1].real, lam_t[1].imag]).reshape(4, n_grp * n_state)
    return toep, bend2, cout2, lam_t4


def _s5_a_body(u_ref, toep_ref, bend_ref, y_ref, e0, e1, e2, e3, *, w1):
    u = u_ref[...].astype(BF16)
    for g in range(2):
        y_ref[:, g * w1:(g + 1) * w1] = jnp.dot(u[:, g * w1:(g + 1) * w1], toep_ref[g].astype(BF16),
                                                 preferred_element_type=F32)
    for c, e_ref in enumerate((e0, e1, e2, e3)):
        e_ref[...] = jnp.dot(u, bend_ref[0, c].astype(BF16), preferred_element_type=F32)


def _s5_b_body(e0, e1, e2, e3, lam_ref, xf_re, xf_im, xr_re, xr_im, *, n_batch, n_chunk, n_ctx_chunk):
    lam = lam_ref[...]
    lfr, lfi, lrr, lri = lam[0:1], lam[1:2], lam[2:3], lam[3:4]
    width = e0.shape[1]
    for b in range(n_batch):
        base = b * n_chunk

        def fwd(n, carry):
            sr, si = carry
            row = base + n
            xf_re[pl.ds(row, 1), :] = sr
            xf_im[pl.ds(row, 1), :] = si
            er = e0[pl.ds(row, 1), :]
            ei = e1[pl.ds(row, 1), :]
            return lfr * sr - lfi * si + er, lfr * si + lfi * sr + ei

        def rev(n, carry):
            sr, si = carry
            row = base + jnp.where(n < n_ctx_chunk, n_ctx_chunk - 1 - n, n_chunk + n_ctx_chunk - 1 - n)
            xr_re[pl.ds(row, 1), :] = sr
            xr_im[pl.ds(row, 1), :] = si
            er = e2[pl.ds(row, 1), :]
            ei = e3[pl.ds(row, 1), :]
            return lrr * sr - lri * si + er, lrr * si + lri * sr + ei

        z = jnp.zeros((1, width), F32)
        lax.fori_loop(0, n_chunk, fwd, (z, z))
        lax.fori_loop(0, n_chunk, rev, (z, z))


def _gelu(y):
    return 0.5 * y * (1.0 + jnp.tanh(0.7978845608028654 * (y + 0.044715 * (y * y * y))))


def _s5_c_body(y_ref, u_ref, x0, x1, x2, x3, cout_ref, dsk_ref, o_ref):
    acc = y_ref[...] + u_ref[...] * dsk_ref[...]
    for c, x_ref in enumerate((x0, x1, x2, x3)):
        acc = acc + jnp.dot(x_ref[...].astype(BF16), cout_ref[0, c].astype(BF16), preferred_element_type=F32)
    o_ref[...] = _gelu(acc).astype(o_ref.dtype)


def s5_core(h_f32, weights, d_skip, n_batch, seq, ctx_len):
    toep, bend2, cout2, lam_t4 = weights
    t = S5_CHUNK
    m, d = h_f32.shape
    n_grp = toep.shape[0]
    cg = d // n_grp
    w1 = t * cg
    n_state2 = bend2.shape[-1]
    n_lat = n_batch * seq
    ncc, ncl = ctx_len // t, seq // t
    n_chunk = ncc + ncl
    rows = n_batch * n_chunk
    hb = h_f32
    lat = hb[:n_lat].reshape(n_batch, ncl, t, n_grp, cg)
    ctx = hb[n_lat:].reshape(n_batch, ncc, t, n_grp, cg)
    u2 = jnp.concatenate([ctx, lat], axis=1).transpose(0, 1, 3, 2, 4).reshape(rows, n_grp * w1)
    n_pair = n_grp // 2
    ublk = pl.BlockSpec((rows, 2 * w1), lambda p: (0, p))
    eblk = pl.BlockSpec((rows, n_state2), lambda p: (0, p))
    e_shape = jax.ShapeDtypeStruct((rows, n_pair * n_state2), F32)
    y_intra, e0, e1, e2, e3 = pl.pallas_call(
        functools.partial(_s5_a_body, w1=w1),
        grid=(n_pair,),
        in_specs=[ublk,
                  pl.BlockSpec((2, w1, w1), lambda p: (p, 0, 0)),
                  pl.BlockSpec((1, 4, 2 * w1, n_state2), lambda p: (p, 0, 0, 0))],
        out_specs=[ublk, eblk, eblk, eblk, eblk],
        out_shape=[jax.ShapeDtypeStruct((rows, n_grp * w1), F32), e_shape, e_shape, e_shape, e_shape],
        compiler_params=_cparams(("parallel",)),
    )(u2, toep, bend2)

    lanes = n_pair * n_state2
    lb = min(lanes, 512)
    sblk = pl.BlockSpec((rows, lb), lambda p: (0, p))
    xs = pl.pallas_call(
        functools.partial(_s5_b_body, n_batch=n_batch, n_chunk=n_chunk, n_ctx_chunk=ncc),
        grid=(lanes // lb,),
        in_specs=[sblk, sblk, sblk, sblk, pl.BlockSpec((4, lb), lambda p: (0, p))],
        out_specs=[sblk, sblk, sblk, sblk],
        out_shape=[e_shape, e_shape, e_shape, e_shape],
        compiler_params=_cparams(("parallel",)),
    )(e0, e1, e2, e3, lam_t4)

    dsk = jnp.tile(d_skip.astype(F32).reshape(n_grp, 1, cg), (1, t, 1)).reshape(1, n_grp * w1)
    y2 = pl.pallas_call(
        _s5_c_body,
        grid=(n_pair,),
        in_specs=[ublk, ublk, eblk, eblk, eblk, eblk,
                  pl.BlockSpec((1, 4, n_state2, 2 * w1), lambda p: (p, 0, 0, 0)),
                  pl.BlockSpec((1, 2 * w1), lambda p: (0, p))],
        out_specs=ublk,
        out_shape=jax.ShapeDtypeStruct((rows, n_grp * w1), F32),
        compiler_params=_cparams(("parallel",)),
    )(y_intra, u2, *xs, cout2, dsk)
    y5 = y2.reshape(n_batch, n_chunk, n_grp, t, cg).transpose(0, 1, 3, 2, 4)
    y_ctx = y5[:, :ncc].reshape(n_batch * ctx_len, d)
    y_lat = y5[:, ncc:].reshape(n_lat, d)
    return jnp.concatenate([y_lat, y_ctx], axis=0)


def _shift_body(h_ref, hp_ref, hn_ref, mu_ref, *o_refs, tm, tiles_lat, tiles_ctx, n_lat_tiles):
    i = pl.program_id(0)
    h = h_ref[...]
    first = jnp.where(i < n_lat_tiles, i % tiles_lat == 0, (i - n_lat_tiles) % tiles_ctx == 0)
    last = jnp.where(i < n_lat_tiles, i % tiles_lat == tiles_lat - 1,
                     (i - n_lat_tiles) % tiles_ctx == tiles_ctx - 1)
    rid = lax.broadcasted_iota(jnp.int32, h.shape, 0)
    prev_edge = jnp.where(first, 0.0, hp_ref[7:8, :])
    next_edge = jnp.where(last, 0.0, hn_ref[0:1, :])
    prev = jnp.where(rid == 0, prev_edge, pltpu.roll(h, 1, 0))
    nxt = jnp.where(rid == tm - 1, next_edge, pltpu.roll(h, tm - 1, 0))
    xx = 0.5 * (prev + nxt) - h
    for m, o_ref in enumerate(o_refs):
        o_ref[...] = (h + xx * mu_ref[m:m + 1, :]).astype(o_ref.dtype)


def token_shift_mix(h, mu, tm, seq, ctx_len, n_batch):
    m, d = h.shape
    n_mix = mu.shape[0]
    n_lat_tiles = n_batch * seq // tm
    nb8 = m // 8
    per = tm // 8
    row = pl.BlockSpec((tm, d), lambda i: (i, 0))
    return pl.pallas_call(
        functools.partial(_shift_body, tm=tm, tiles_lat=seq // tm, tiles_ctx=max(ctx_len // tm, 1),
                          n_lat_tiles=n_lat_tiles),
        grid=(m // tm,),
        in_specs=[row,
                  pl.BlockSpec((8, d), lambda i: (jnp.maximum(i * per - 1, 0), 0)),
                  pl.BlockSpec((8, d), lambda i: (jnp.minimum((i + 1) * per, nb8 - 1), 0)),
                  pl.BlockSpec((n_mix, d), lambda i: (0, 0))],
        out_specs=[row] * n_mix,
        out_shape=[jax.ShapeDtypeStruct((m, d), BF16)] * n_mix,
        compiler_params=_cparams(("parallel",)),
    )(h, h, h, mu.astype(F32))


def _rwkv_scan_body(rf_ref, rr_ref, kf_ref, kr_ref, vf_ref, vr_ref, wf_ref, wr_ref, af_ref, ar_ref,
                    kk_ref, ka_ref, of_ref, or_ref, s_ref, *, tb, n, bh):
    @pl.when(pl.program_id(0) == 0)
    def _():
        s_ref[...] = jnp.zeros_like(s_ref)

    k_k = kk_ref[...]
    k_a = ka_ref[...]

    def step(t, carry):
        tr = tb - 1 - t
        r = jnp.concatenate([rf_ref[t], rr_ref[tr]], axis=-1)
        k = jnp.concatenate([kf_ref[t], kr_ref[tr]], axis=-1)
        v = jnp.concatenate([vf_ref[t], vr_ref[tr]], axis=-1)
        w = jnp.concatenate([wf_ref[t], wr_ref[tr]], axis=-1)
        a = jnp.concatenate([af_ref[t], ar_ref[tr]], axis=-1)
        kk = k * k_k
        nrm = jnp.sqrt(jnp.sum(kk * kk, axis=0, keepdims=True))
        kk = kk / jnp.maximum(nrm, 1e-12)
        nkk = -kk
        akk = kk * a
        kd = k * (1.0 + (a - 1.0) * k_a)
        wr = w * r
        sa = jnp.zeros_like(v)
        so = jnp.zeros_like(v)
        for j in range(n):
            s_j = s_ref[j]
            sa = sa + s_j * nkk[j:j + 1, :]
            so = so + s_j * wr[j:j + 1, :]
        c_a = jnp.sum(akk * r, axis=0, keepdims=True)
        c_k = jnp.sum(kd * r, axis=0, keepdims=True)
        o = so + sa * c_a + v * c_k
        of_ref[t] = o[:, :bh]
        or_ref[tr] = o[:, bh:]
        for j in range(n):
            s_ref[j] = s_ref[j] * w[j:j + 1, :] + sa * akk[j:j + 1, :] + v * kd[j:j + 1, :]
        return carry

    lax.fori_loop(0, tb, step, 0)


def rwkv_scan(r, k, v, w0, w1, a0, a1, k_k, k_a, ctx_len, tb=32):
    steps, n, bh = r.shape
    while steps % tb or ctx_len % tb:
        tb //= 2
    nblk, nblk_ctx = steps // tb, ctx_len // tb

    def rev_blk(i):
        return jnp.where(i < nblk_ctx, nblk_ctx - 1 - i, nblk + nblk_ctx - 1 - i)

    fwd = pl.BlockSpec((tb, n, bh), lambda i: (i, 0, 0))
    rev = pl.BlockSpec((tb, n, bh), lambda i: (rev_blk(i), 0, 0))
    tab = pl.BlockSpec((n, 2 * bh), lambda i: (0, 0))
    out = jax.ShapeDtypeStruct((steps, n, bh), F32)
    return pl.pallas_call(
        functools.partial(_rwkv_scan_body, tb=tb, n=n, bh=bh),
        grid=(nblk,),
        in_specs=[fwd, rev, fwd, rev, fwd, rev, fwd, rev, fwd, rev, tab, tab],
        out_specs=[fwd, rev],
        out_shape=[out, out],
        scratch_shapes=[pltpu.VMEM((n, n, 2 * bh), F32)],
        compiler_params=_cparams(("arbitrary",)),
        name="rwkv_scan",
    )(r, r, k, k, v, v, w0, w1, a0, a1, k_k, k_a)


def _rwkv_post_body(of_ref, or_ref, r_ref, k_ref, v_ref, a0_ref, a1_ref, g_ref, ka_ref, rk_ref, lg_ref, lb_ref,
                    y_ref):
    o = of_ref[...] + or_ref[...]
    mean = jnp.mean(o, axis=1, keepdims=True)
    var = jnp.mean(jnp.square(o - mean), axis=1, keepdims=True)
    on = (o - mean) * lax.rsqrt(var + RW_GN_EPS) * lg_ref[...] + lb_ref[...]
    r = r_ref[...]
    k = k_ref[...]
    k_a = ka_ref[...]
    rk = rk_ref[...]
    kd0 = k * (1.0 + (a0_ref[...] - 1.0) * k_a)
    kd1 = k * (1.0 + (a1_ref[...] - 1.0) * k_a)
    bonus = (jnp.sum(r * kd0 * rk, axis=1, keepdims=True)
             + jnp.sum(r * kd1 * rk, axis=1, keepdims=True)) * v_ref[...]
    y = (on + bonus)
    y_ref[...] = (y * g_ref[...]).astype(y_ref.dtype)


def rwkv_post(o_f, o_r, r, k, v, a0, a1, gate, k_a, r_k, ln_g, ln_b, tb=32):
    steps, n, ch = o_f.shape
    while steps % tb:
        tb //= 2
    blk = pl.BlockSpec((tb, n, ch), lambda i: (i, 0, 0))
    tab = pl.BlockSpec((1, n, ch), lambda i: (0, 0, 0))
    return pl.pallas_call(
        _rwkv_post_body,
        grid=(steps // tb,),
        in_specs=[blk] * 8 + [tab] * 4,
        out_specs=blk,
        out_shape=jax.ShapeDtypeStruct((steps, n, ch), BF16),
        compiler_params=_cparams(("parallel",)),
        name="rwkv_post",
    )(o_f, o_r, r, k, v, a0, a1, gate, k_a[None], r_k[None], ln_g[None], ln_b[None])


def _pad_cols(w, mult=128):
    n = w.shape[-1]
    p = -(-n // mult) * mult - n
    return jnp.pad(w, [(0, 0)] * (w.ndim - 1) + [(0, p)]) if p else w


def _pad_rows(w, mult=128):
    n = w.shape[-2]
    p = -(-n // mult) * mult - n
    return jnp.pad(w, [(0, 0)] * (w.ndim - 2) + [(0, p), (0, 0)]) if p else w


def rwkv_mixer_tokens(h, mu, w_rkv, w0, w1, w2, a0, a1, a2, k_k, k_a, r_k, ln_g, ln_b, g1, g2,
                      tm, n_batch, seq, ctx_len):
    m, d = h.shape
    n_heads, n = r_k.shape
    n_lat = n_batch * seq
    ts = _row_tile(seq, ctx_len, cap=256)
    x_r, x_k, x_v, x_w, x_a, x_g = token_shift_mix(h, mu, ts, seq, ctx_len, n_batch)
    mm = functools.partial(matmul, tm=tm)
    r = mm(x_r, w_rkv[0])
    k = mm(x_k, w_rkv[1])
    v = mm(x_v, w_rkv[2])
    gate = mm(mm(x_g, _pad_cols(g1), act="sigmoid", out_dtype=BF16), _pad_rows(g2))
    dec, aa = [], []
    for dn in range(2):
        lw = mm(x_w, _pad_cols(w1[dn]), act="tanh", out_dtype=BF16)
        dec.append(mm(lw, _pad_rows(w2[dn]), bias=w0[dn], act="decay"))
        la = mm(x_a, _pad_cols(a1[dn]), out_dtype=BF16)
        aa.append(mm(la, _pad_rows(a2[dn]), bias=a0[dn], act="sigmoid"))

    def to_seq(x):
        lat = x[:n_lat].reshape(n_batch, seq, n_heads, n)
        ctx = x[n_lat:].reshape(n_batch, ctx_len, n_heads, n)
        return jnp.concatenate([ctx, lat], axis=1).transpose(1, 3, 0, 2).reshape(ctx_len + seq, n, n_batch * n_heads)

    r_t, k_t, v_t, g_t = to_seq(r), to_seq(k), to_seq(v), to_seq(gate)
    d0_t, d1_t, a0_t, a1_t = to_seq(dec[0]), to_seq(dec[1]), to_seq(aa[0]), to_seq(aa[1])

    def table(vec):
        return jnp.tile(vec.astype(F32).reshape(n_heads, n).T[:, None, :], (1, n_batch, 1)).reshape(n, n_batch * n_heads)

    kk_tab, ka_tab = table(k_k), table(k_a)
    o_f, o_r = rwkv_scan(r_t, k_t, v_t, d0_t, d1_t, a0_t, a1_t, jnp.concatenate([kk_tab, kk_tab], axis=-1),
                         jnp.concatenate([ka_tab, ka_tab], axis=-1), ctx_len)
    y_t = rwkv_post(o_f, o_r, r_t, k_t, v_t, a0_t, a1_t, g_t, ka_tab, table(r_k.reshape(-1)),
                    table(ln_g), table(ln_b))
    y4 = y_t.reshape(ctx_len + seq, n, n_batch, n_heads).transpose(2, 0, 3, 1)
    y_ctx = y4[:, :ctx_len].reshape(n_batch * ctx_len, d)
    y_lat = y4[:, ctx_len:].reshape(n_lat, d)
    return jnp.concatenate([y_lat, y_ctx], axis=0)


def kernel(x, c, ctx, c_ctx, mod_w, mod_b, norm_g, final_g, na_w_qkv, na_w_o, na_rpb, s5_lam_re, s5_lam_im, s5_log_dt, s5_b_re, s5_b_im, s5_c_re, s5_c_im, s5_d, s5_w_glu, rw_mu, rw_w_rkv, rw_w0, rw_w1, rw_w2, rw_a0, rw_a1, rw_a2, rw_k_k, rw_k_a, rw_r_k, rw_ln_g, rw_ln_b, rw_g1, rw_g2, rw_w_o, moe_wg, moe_bg, moe_we, moe_be, moe_w_gu, moe_w_down):
    n_batch, seq, d = x.shape
    ctx_len = ctx.shape[1]
    depth = mod_w.shape[0]
    n_lat = n_batch * seq
    n_ctx = n_batch * ctx_len
    tm = _row_tile(seq, n_ctx)
    tpb = seq // tm
    xs = jnp.concatenate([x.reshape(n_lat, d), ctx.reshape(n_ctx, d)], axis=0).astype(F32)

    cvecs = jnp.concatenate([c.astype(F32), c_ctx.astype(F32)[None]], axis=0)
    rows_pad = -(-(n_batch + 1) // 8) * 8
    cvecs = jnp.pad(cvecs, ((0, rows_pad - n_batch - 1), (0, 0)))
    mods = modulation(cvecs, mod_w, mod_b)[:, :n_batch + 1].reshape(depth, n_batch + 1, 6, 1, d)
    mods = mods.transpose(0, 2, 1, 3, 4)

    for i in range(depth):
        last = i == depth - 1
        sh1, sc1, g1, sh2, sc2, g2 = (mods[i, q] for q in range(6))
        pn = functools.partial(prenorm, tm=tm, tiles_per_batch=tpb, n_batch=n_batch)
        mm = functools.partial(matmul, tm=tm, tiles_per_batch=tpb, n_batch=n_batch)
        mix, j = i % N_MIXERS, i // N_MIXERS
        if mix == 0:
            (hb,) = pn(xs, norm_g[i, 0], 1.0 + sc1, sh1, out_dtypes=(BF16,))
            n_heads = na_rpb.shape[1]
            qscale = jnp.concatenate([jnp.full((d,), (d // n_heads) ** -0.5, F32), jnp.ones((2 * d,), F32)])
            qkv = mm(hb, na_w_qkv[j], out_dtype=BF16, colscale=qscale)
            o = na_attention(qkv, _na_bias_table(na_rpb[j], GRID_W), n_batch, seq, ctx_len, not last)
            if last:
                xs = xs[:n_lat]
            xs = mm(o, na_w_o[j], mode="resid", resid=xs, gate=g1)
        elif mix == 1:
            (hf,) = pn(xs, norm_g[i, 0], 1.0 + sc1, sh1, out_dtypes=(F32,))
            wts = _s5_weights(s5_lam_re[j], s5_lam_im[j], s5_log_dt[j], s5_b_re[j], s5_b_im[j],
                              s5_c_re[j], s5_c_im[j], S5_CHUNK)
            y = s5_core(hf, wts, s5_d[j], n_batch, seq, ctx_len)
            if last:
                xs, y = xs[:n_lat], y[:n_lat]
            xs = mm(y, s5_w_glu[j], mode="glu", resid=xs, gate=g1)
        else:
            (hf,) = pn(xs, norm_g[i, 0], 1.0 + sc1, sh1, out_dtypes=(F32,))
            y = rwkv_mixer_tokens(hf, rw_mu[j], rw_w_rkv[j], rw_w0[j], rw_w1[j], rw_w2[j], rw_a0[j], rw_a1[j],
                                  rw_a2[j], rw_k_k[j], rw_k_a[j], rw_r_k[j], rw_ln_g[j], rw_ln_b[j],
                                  rw_g1[j], rw_g2[j], tm, n_batch, seq, ctx_len)
            if last:
                xs, y = xs[:n_lat], y[:n_lat]
            xs = mm(y, rw_w_o[j], mode="resid", resid=xs, gate=g1)
        (h2,) = pn(xs, norm_g[i, 1], 1.0 + sc2, sh2, out_dtypes=(F32,))
        xs = hier_moe(xs, h2, moe_wg[i], moe_bg[i], moe_we[i], moe_be[i], moe_w_gu[i], moe_w_down[i], g2,
                      tm, n_batch, seq)
    ones = jnp.ones((n_batch + 1, 1, d), F32)
    (out,) = prenorm(xs[:n_lat], final_g, ones, jnp.zeros_like(ones), tm, tpb, n_batch, out_dtypes=(x.dtype,))
    return out.reshape(n_batch, seq, d)
```

```python
import functools
import math

import jax
import jax.numpy as jnp
from jax import lax
from jax.experimental import pallas as pl
from jax.experimental.pallas import tpu as pltpu

F32 = jnp.float32
BF16 = jnp.bfloat16

GRID_W = 64
N_MIXERS = 3
NORM_EPS = 1e-6
RW_GN_EPS = 64e-5
MOE_TOP_K = 2
MOE_BLOCK = 128
S5_CHUNK = 16
SUBLANES = 8
NEG_BIG = -1e30
VMEM_LIMIT_BYTES = 48 * 1024 * 1024
VMEM_LIMIT_BIG_BYTES = 56 * 1024 * 1024


def _cparams(sem, limit=VMEM_LIMIT_BYTES):
    return pltpu.CompilerParams(dimension_semantics=sem, vmem_limit_bytes=limit)


def _row_tile(n_lat_per_batch, n_ctx_total, cap=512):
    t = cap
    while t > 8 and (n_lat_per_batch % t or (n_ctx_total and n_ctx_total % t)):
        t //= 2
    return t


def _seg_fn(tiles_per_batch, n_batch):
    def seg(i):
        return jnp.minimum(i // tiles_per_batch, n_batch)
    return seg


def _bdot(a, b):
    return jnp.dot(a.astype(BF16), b.astype(BF16), preferred_element_type=F32)


def _split3(x):
    hi = x.astype(BF16)
    r1 = x - hi.astype(F32)
    mid = r1.astype(BF16)
    lo = (r1 - mid.astype(F32)).astype(BF16)
    return hi, mid, lo


def _prenorm_body(x_ref, g_ref, sc_ref, sh_ref, *o_refs):
    x = x_ref[...]
    ms = jnp.mean(x * x, axis=-1, keepdims=True)
    h = (x * lax.rsqrt(ms + NORM_EPS)) * g_ref[...]
    h = h * sc_ref[0] + sh_ref[0]
    for o_ref in o_refs:
        o_ref[...] = h.astype(o_ref.dtype)


def prenorm(x, g, scale1p, shift, tm, tiles_per_batch, n_batch, out_dtypes):
    m, d = x.shape
    seg = _seg_fn(tiles_per_batch, n_batch)
    row = pl.BlockSpec((tm, d), lambda i: (i, 0))
    tab = pl.BlockSpec((1, 1, d), lambda i: (seg(i), 0, 0))
    outs = pl.pallas_call(
        _prenorm_body,
        grid=(m // tm,),
        in_specs=[row, pl.BlockSpec((1, d), lambda i: (0, 0)), tab, tab],
        out_specs=[row for _ in out_dtypes],
        out_shape=[jax.ShapeDtypeStruct((m, d), dt) for dt in out_dtypes],
        compiler_params=_cparams(("parallel",)),
    )(x, g.reshape(1, d), scale1p, shift)
    return outs


def _softplus(z):
    return jnp.maximum(z, 0.0) + jnp.log(1.0 + jnp.exp(-jnp.abs(z)))


def _apply_act(y, act):
    if act is None:
        return y
    if act == "tanh":
        return jnp.tanh(y)
    if act == "sigmoid":
        return jax.nn.sigmoid(y)
    if act == "decay":
        return jnp.exp(-jnp.exp(-_softplus(-y) - 0.5))
    raise ValueError(act)


def _mm_body(*refs, mode, act, has_bias, has_scale):
    it = iter(refs)
    x_ref = next(it)
    w_ref = next(it)
    w2_ref = next(it) if mode == "glu" else None
    b_ref = next(it) if has_bias else None
    s_ref = next(it) if has_scale else None
    r_ref = next(it) if mode in ("resid", "glu") else None
    g_ref = next(it) if mode in ("resid", "glu") else None
    o_ref = next(it)
    x = x_ref[...].astype(BF16)
    acc = jnp.dot(x, w_ref[...].astype(BF16), preferred_element_type=F32)
    if mode == "glu":
        acc2 = jnp.dot(x, w2_ref[...].astype(BF16), preferred_element_type=F32)
        acc = acc * jax.nn.sigmoid(acc2)
    if has_bias:
        acc = acc + b_ref[...]
    acc = _apply_act(acc, act)
    if has_scale:
        acc = acc * s_ref[...]
    if mode in ("resid", "glu"):
        acc = r_ref[...] + g_ref[0] * acc
    o_ref[...] = acc.astype(o_ref.dtype)


def matmul(x, w, *, tm, tn=512, out_dtype=F32, mode="plain", act=None, bias=None, colscale=None,
           resid=None, gate=None, tiles_per_batch=None, n_batch=None, w_index=0):
    m, k = x.shape
    if w.ndim == 2:
        w, w_index = w[None], 0
    n = w.shape[2] // 2 if mode == "glu" else w.shape[2]
    tn = min(tn, n)
    assert m % tm == 0 and n % tn == 0, (m, tm, n, tn)
    nj = n // tn
    in_specs = [pl.BlockSpec((tm, k), lambda i, j: (i, 0)),
                pl.BlockSpec((None, k, tn), lambda i, j: (w_index, 0, j))]
    args = [x, w]
    if mode == "glu":
        in_specs.append(pl.BlockSpec((None, k, tn), lambda i, j: (w_index, 0, j + nj)))
        args.append(w)
    col = pl.BlockSpec((1, tn), lambda i, j: (0, j))
    if bias is not None:
        in_specs.append(col)
        args.append(bias.reshape(1, n).astype(F32))
    if colscale is not None:
        in_specs.append(col)
        args.append(colscale.reshape(1, n).astype(F32))
    if mode in ("resid", "glu"):
        seg = _seg_fn(tiles_per_batch, n_batch)
        in_specs.append(pl.BlockSpec((tm, tn), lambda i, j: (i, j)))
        in_specs.append(pl.BlockSpec((1, 1, tn), lambda i, j: (seg(i), 0, j)))
        args += [resid, gate]
    return pl.pallas_call(
        functools.partial(_mm_body, mode=mode, act=act, has_bias=bias is not None,
                          has_scale=colscale is not None),
        grid=(m // tm, nj),
        in_specs=in_specs,
        out_specs=pl.BlockSpec((tm, tn), lambda i, j: (i, j)),
        out_shape=jax.ShapeDtypeStruct((m, n), out_dtype),
        compiler_params=_cparams(("parallel", "parallel")),
    )(*args)


def _router_body(x_ref, w_ref, b_ref, o_ref, *, n_grp, epg):
    xh, xm, xl = _split3(x_ref[...])
    wh, wm, wl = _split3(w_ref[...])
    dot = functools.partial(jnp.dot, preferred_element_type=F32)
    acc = dot(xh, wh) + (dot(xh, wm) + dot(xm, wh)) + (dot(xh, wl) + dot(xm, wm) + dot(xl, wh))
    acc = acc + b_ref[...]
    lane = lax.broadcasted_iota(jnp.int32, acc.shape, 1)
    lane_f = lane.astype(F32)
    far = 1e9

    def first_max(vals):
        m = jnp.max(vals, axis=-1, keepdims=True)
        return m, jnp.min(jnp.where(vals == m, lane_f, far), axis=-1, keepdims=True)

    gl = jnp.where(lane < n_grp, acc, NEG_BIG)
    gmax, grp = first_max(gl)
    p_grp = 1.0 / jnp.sum(jnp.exp(gl - gmax), axis=-1, keepdims=True)
    lo = n_grp + grp * epg
    el = jnp.where((lane_f >= lo) & (lane_f < lo + epg), acc, NEG_BIG)
    m1, i1 = first_max(el)
    m2, i2 = first_max(jnp.where(lane_f == i1, NEG_BIG, el))
    e21 = jnp.exp(m2 - m1)
    g1 = p_grp / (1.0 + e21)
    g2 = p_grp * e21 / (1.0 + e21)
    out = jnp.where(lane == 0, i1 - n_grp, jnp.where(lane == 1, i2 - n_grp,
                                                    jnp.where(lane == 2, g1, jnp.where(lane == 3, g2, 0.0))))
    o_ref[...] = out


def router(x, w, bias, tm, n_grp, epg):
    m, k = x.shape
    n = w.shape[1]
    return pl.pallas_call(
        functools.partial(_router_body, n_grp=n_grp, epg=epg),
        grid=(m // tm,),
        in_specs=[pl.BlockSpec((tm, k), lambda i: (i, 0)), pl.BlockSpec((k, n), lambda i: (0, 0)),
                  pl.BlockSpec((1, n), lambda i: (0, 0))],
        out_specs=pl.BlockSpec((tm, n), lambda i: (i, 0)),
        out_shape=jax.ShapeDtypeStruct((m, n), F32),
        compiler_params=_cparams(("parallel",)),
        name="moe_router",
    )(x, w, bias.reshape(1, n))


def _mod_body(s_ref, w_ref, b_ref, o_ref):
    s = s_ref[...]
    s = s * jax.nn.sigmoid(s)
    o_ref[0] = _bdot(s, w_ref[0]) + b_ref[0]


def modulation(cvecs, mod_w, mod_b, tn=768):
    depth, d, n = mod_w.shape
    rows = cvecs.shape[0]
    while n % tn:
        tn //= 2
    return pl.pallas_call(
        _mod_body,
        grid=(depth, n // tn),
        in_specs=[pl.BlockSpec((rows, d), lambda l, j: (0, 0)),
                  pl.BlockSpec((1, d, tn), lambda l, j: (l, 0, j)),
                  pl.BlockSpec((1, 1, tn), lambda l, j: (l, 0, j))],
        out_specs=pl.BlockSpec((1, rows, tn), lambda l, j: (l, 0, j)),
        out_shape=jax.ShapeDtypeStruct((depth, rows, n), F32),
        compiler_params=_cparams(("parallel", "parallel")),
    )(cvecs, mod_w, mod_b.reshape(depth, 1, n))


def _na_bias_table(rpb, grid_w):
    n_heads = rpb.shape[0]
    kr = (rpb.shape[1] + 1) // 2
    kcw = (rpb.shape[2] + 1) // 2
    q = jnp.arange(grid_w)[:, None]
    kc = jnp.arange(grid_w)[None, :]
    win = jnp.clip(q - kcw // 2, 0, grid_w - kcw)
    valid = (kc >= win) & (kc < win + kcw)
    dc = jnp.clip(kc - q, 1 - kcw, kcw - 1) + kcw - 1
    d = jnp.arange(kr)[:, None]
    j = jnp.arange(kr)[None, :]
    dr = j - d + kr - 1
    tab = rpb.astype(F32)[:, dr][:, :, :, dc]
    tab = jnp.where(valid[None, None, None], tab, NEG_BIG)
    return tab.transpose(0, 1, 3, 2, 4).reshape(n_heads, kr, grid_w, kr * grid_w)


def _na_body(q_ref, k_ref, v_ref, kc_ref, vc_ref, bias_ref, o_ref, *, hpb, dh, grid_w, kr, rows):
    r = pl.program_id(2)
    rs = jnp.clip(r - kr // 2, 0, rows - kr)
    start = pl.multiple_of(rs * grid_w, grid_w)
    nt = (((1,), (1,)), ((), ()))
    for h in range(hpb):
        sl = slice(h * dh, (h + 1) * dh)
        q = q_ref[:, sl]
        k = k_ref[pl.ds(start, kr * grid_w), sl]
        v = v_ref[pl.ds(start, kr * grid_w), sl]
        s = lax.dot_general(q, k, nt, preferred_element_type=F32) + bias_ref[h, 0]
        sc = lax.dot_general(q, kc_ref[:, sl], nt, preferred_element_type=F32)
        m = jnp.maximum(jnp.max(s, axis=-1, keepdims=True), jnp.max(sc, axis=-1, keepdims=True))
        p = jnp.exp(s - m)
        pc = jnp.exp(sc - m)
        den = jnp.sum(p, axis=-1, keepdims=True) + jnp.sum(pc, axis=-1, keepdims=True)
        o = (jnp.dot(p.astype(BF16), v, preferred_element_type=F32)
             + jnp.dot(pc.astype(BF16), vc_ref[:, sl], preferred_element_type=F32))
        o_ref[:, sl] = (o / den).astype(o_ref.dtype)


def _ctx_attn_body(q_ref, k_ref, v_ref, o_ref, *, hpb, dh):
    nt = (((1,), (1,)), ((), ()))
    for h in range(hpb):
        sl = slice(h * dh, (h + 1) * dh)
        s = lax.dot_general(q_ref[:, sl], k_ref[:, sl], nt, preferred_element_type=F32)
        m = jnp.max(s, axis=-1, keepdims=True)
        p = jnp.exp(s - m)
        den = jnp.sum(p, axis=-1, keepdims=True)
        o = jnp.dot(p.astype(BF16), v_ref[:, sl], preferred_element_type=F32)
        o_ref[:, sl] = (o / den).astype(o_ref.dtype)


def na_attention(qkv, bias_tab, n_batch, seq, ctx_len, need_ctx):
    d = qkv.shape[1] // 3
    n_heads, kr = bias_tab.shape[0], bias_tab.shape[1]
    dh = d // n_heads
    hpb = max(1, min(n_heads, 256 // dh))
    bw = hpb * dh
    n_d = d // bw
    rows = seq // GRID_W
    n_lat = n_batch * seq
    ctx_blk0 = n_lat // ctx_len

    def d_of(r):
        return r - jnp.clip(r - kr // 2, 0, rows - kr)

    o_lat = pl.pallas_call(
        functools.partial(_na_body, hpb=hpb, dh=dh, grid_w=GRID_W, kr=kr, rows=rows),
        grid=(n_batch, n_d, rows),
        in_specs=[
            pl.BlockSpec((GRID_W, bw), lambda b, g, r: (b * rows + r, g)),
            pl.BlockSpec((seq, bw), lambda b, g, r: (b, n_d + g)),
            pl.BlockSpec((seq, bw), lambda b, g, r: (b, 2 * n_d + g)),
            pl.BlockSpec((ctx_len, bw), lambda b, g, r: (ctx_blk0 + b, n_d + g)),
            pl.BlockSpec((ctx_len, bw), lambda b, g, r: (ctx_blk0 + b, 2 * n_d + g)),
            pl.BlockSpec((hpb, 1, GRID_W, kr * GRID_W), lambda b, g, r: (g, d_of(r), 0, 0)),
        ],
        out_specs=pl.BlockSpec((GRID_W, bw), lambda b, g, r: (b * rows + r, g)),
        out_shape=jax.ShapeDtypeStruct((n_lat, d), BF16),
        compiler_params=_cparams(("parallel", "parallel", "arbitrary")),
    )(qkv, qkv, qkv, qkv, qkv, bias_tab)
    if not need_ctx:
        return o_lat
    o_ctx = pl.pallas_call(
        functools.partial(_ctx_attn_body, hpb=hpb, dh=dh),
        grid=(n_batch, n_d),
        in_specs=[
            pl.BlockSpec((ctx_len, bw), lambda b, g: (ctx_blk0 + b, g)),
            pl.BlockSpec((ctx_len, bw), lambda b, g: (ctx_blk0 + b, n_d + g)),
            pl.BlockSpec((ctx_len, bw), lambda b, g: (ctx_blk0 + b, 2 * n_d + g)),
        ],
        out_specs=pl.BlockSpec((ctx_len, bw), lambda b, g: (b, g)),
        out_shape=jax.ShapeDtypeStruct((n_batch * ctx_len, d), BF16),
        compiler_params=_cparams(("parallel", "parallel")),
    )(qkv, qkv, qkv)
    return jnp.concatenate([o_lat, o_ctx], axis=0)


def _expert_body(be_ref, idx_ref, idx_next_ref, x_hbm, wgu_ref, wd_ref, o_ref, xbuf, sem, wgu_s, wd_s, *,
                 hidden, n_blocks):
    j = pl.program_id(0)
    cur = j % 2

    def row_copy(idx, buf, r):
        return pltpu.make_async_copy(x_hbm.at[pl.ds(idx[0, 0, r], 1)], xbuf.at[buf, pl.ds(r, 1)], sem.at[buf])

    @pl.when(j == 0)
    def _():
        for r in range(MOE_BLOCK):
            row_copy(idx_ref, 0, r).start()

    @pl.when(j + 1 < n_blocks)
    def _():
        for r in range(MOE_BLOCK):
            row_copy(idx_next_ref, 1 - cur, r).start()

    prev = be_ref[jnp.maximum(j - 1, 0)]
    changed = jnp.logical_or(j == 0, be_ref[j] != prev)

    @pl.when(changed)
    def _():
        wgu_s[...] = wgu_ref[...].astype(BF16)
        wd_s[...] = wd_ref[...].astype(BF16)

    for r in range(MOE_BLOCK):
        row_copy(idx_ref, cur, r).wait()

    gu = jnp.dot(xbuf[cur].astype(BF16), wgu_s[...], preferred_element_type=F32)
    g = gu[:, :hidden]
    u = gu[:, hidden:]
    a = (g * jax.nn.sigmoid(g) * u).astype(BF16)
    o_ref[...] = jnp.dot(a, wd_s[...], preferred_element_type=F32)


def expert_blocks(x, slot_tok, blk_e, w_gu, w_down, layer):
    d = x.shape[1]
    hidden = w_down.shape[2]
    n_blocks = blk_e.shape[0]
    idx = slot_tok.reshape(n_blocks, 1, MOE_BLOCK)
    return pl.pallas_call(
        functools.partial(_expert_body, hidden=hidden, n_blocks=n_blocks),
        grid_spec=pltpu.PrefetchScalarGridSpec(
            num_scalar_prefetch=1,
            grid=(n_blocks,),
            in_specs=[pl.BlockSpec((1, 1, MOE_BLOCK), lambda j, be: (j, 0, 0), memory_space=pltpu.SMEM),
                      pl.BlockSpec((1, 1, MOE_BLOCK), lambda j, be: (jnp.minimum(j + 1, n_blocks - 1), 0, 0),
                                   memory_space=pltpu.SMEM),
                      pl.BlockSpec(memory_space=pl.ANY),
                      pl.BlockSpec((None, None, d, 2 * hidden), lambda j, be: (layer, be[j], 0, 0)),
                      pl.BlockSpec((None, None, hidden, d), lambda j, be: (layer, be[j], 0, 0))],
            out_specs=pl.BlockSpec((MOE_BLOCK, d), lambda j, be: (j, 0)),
            scratch_shapes=[pltpu.VMEM((2, MOE_BLOCK, d), F32), pltpu.SemaphoreType.DMA((2,)),
                            pltpu.VMEM((d, 2 * hidden), BF16), pltpu.VMEM((hidden, d), BF16)]),
        out_shape=jax.ShapeDtypeStruct((n_blocks * MOE_BLOCK, d), F32),
        compiler_params=_cparams(("arbitrary",), VMEM_LIMIT_BIG_BYTES),
        name="moe_experts",
    )(blk_e, idx, idx, x, w_gu, w_down)


def _combine_body(idx_ref, route_ref, x_ref, g_ref, yb_hbm, o_ref, ybuf, sem, *, tmc):
    def row_copy(k, r):
        return pltpu.make_async_copy(yb_hbm.at[pl.ds(idx_ref[0, 0, k * tmc + r], 1)],
                                     ybuf.at[k, pl.ds(r, 1)], sem.at[0])

    for k in range(MOE_TOP_K):
        for r in range(tmc):
            row_copy(k, r).start()
    for k in range(MOE_TOP_K):
        for r in range(tmc):
            row_copy(k, r).wait()
    route = route_ref[...]
    f = ybuf[0] * route[:, MOE_TOP_K:MOE_TOP_K + 1]
    for k in range(1, MOE_TOP_K):
        f = f + ybuf[k] * route[:, MOE_TOP_K + k:MOE_TOP_K + k + 1]
    o_ref[...] = x_ref[...] + g_ref[0] * f


def moe_combine(xs, yb, slot_of_asg, route, gate_tab, n_batch, seq):
    n_tok, d = xs.shape
    tmc = MOE_BLOCK
    n_tiles = n_tok // tmc
    idx = slot_of_asg.reshape(n_tiles, tmc, MOE_TOP_K).transpose(0, 2, 1).reshape(n_tiles, 1, MOE_TOP_K * tmc)
    seg = _seg_fn(seq // tmc, n_batch)
    return pl.pallas_call(
        functools.partial(_combine_body, tmc=tmc),
        grid=(n_tiles,),
        in_specs=[pl.BlockSpec((1, 1, MOE_TOP_K * tmc), lambda i: (i, 0, 0), memory_space=pltpu.SMEM),
                  pl.BlockSpec((tmc, route.shape[1]), lambda i: (i, 0)),
                  pl.BlockSpec((tmc, d), lambda i: (i, 0)),
                  pl.BlockSpec((1, 1, d), lambda i: (seg(i), 0, 0)),
                  pl.BlockSpec(memory_space=pl.ANY)],
        out_specs=pl.BlockSpec((tmc, d), lambda i: (i, 0)),
        out_shape=jax.ShapeDtypeStruct((n_tok, d), F32),
        scratch_shapes=[pltpu.VMEM((MOE_TOP_K, tmc, d), F32), pltpu.SemaphoreType.DMA((1,))],
        compiler_params=_cparams(("arbitrary",)),
        name="moe_combine",
    )(idx, route, xs, gate_tab, yb)


def hier_moe(xs, h, wg, bg, we, be, w_gu, w_down, layer, gate_tab, tm, n_batch, seq):
    n_tok, d = h.shape
    n_grp = wg.shape[1]
    n_exp = we.shape[1]
    epg = n_exp // n_grp
    n_logit = n_grp + n_exp
    n_pad = -(-n_logit // 128) * 128
    wcat = jnp.pad(jnp.concatenate([wg, we], axis=1), ((0, 0), (0, n_pad - n_logit)))
    bcat = jnp.pad(jnp.concatenate([bg, be], axis=0), (0, n_pad - n_logit))
    route = router(h, wcat, bcat, tm, n_grp, epg)

    flat_e = route[:, :MOE_TOP_K].astype(jnp.int32).reshape(-1)
    n_asg = flat_e.shape[0]
    onehot = (flat_e[:, None] == jnp.arange(n_exp, dtype=jnp.int32)[None, :]).astype(F32)
    ck = 128
    oh3 = onehot.reshape(n_asg // ck, ck, n_exp)
    tri = jnp.tril(jnp.ones((ck, ck), F32))
    within = jnp.einsum("ij,cjk->cik", tri, oh3, precision=lax.Precision.HIGHEST)
    tot = within[:, -1, :]
    before = jnp.cumsum(tot, axis=0) - tot
    rank = jnp.sum(oh3 * (within + before[:, None, :]), axis=-1).reshape(n_asg).astype(jnp.int32) - 1
    counts = jnp.sum(tot, axis=0).astype(jnp.int32)
    padded = (counts + MOE_BLOCK - 1) // MOE_BLOCK * MOE_BLOCK
    pad_end = jnp.cumsum(padded)
    pad_start = pad_end - padded
    slot = jnp.sum(onehot * pad_start.astype(F32)[None, :], axis=-1).astype(jnp.int32) + rank
    n_blocks = -(-n_asg // MOE_BLOCK) + n_exp
    slot_tok = jnp.zeros((n_blocks * MOE_BLOCK,), jnp.int32).at[slot].set(
        jnp.arange(n_asg, dtype=jnp.int32) // MOE_TOP_K)
    blk_start = jnp.arange(n_blocks, dtype=jnp.int32) * MOE_BLOCK
    blk_e = jnp.minimum(jnp.sum((pad_end[None, :] <= blk_start[:, None]).astype(jnp.int32), axis=1), n_exp - 1)
    yb = expert_blocks(h, slot_tok, blk_e, w_gu, w_down, layer)
    return moe_combine(xs, yb, slot.reshape(n_tok, MOE_TOP_K), route, gate_tab, n_batch, seq)


def _s5_weights(lam_re, lam_im, log_dt, b_re, b_im, c_re, c_im, t_chunk):
    n_grp, n_state = lam_re.shape[1], lam_re.shape[2]
    cg = b_re.shape[-1]
    lam = lax.complex(lam_re.astype(F32), lam_im.astype(F32))
    dt = jnp.exp(log_dt.astype(F32))[..., None]
    lam_bar = jnp.exp(lam * dt)
    b_bar = ((lam_bar - 1.0) / lam)[..., None] * lax.complex(b_re.astype(F32), b_im.astype(F32))
    c_mat = lax.complex(c_re.astype(F32), c_im.astype(F32))
    ks = jnp.arange(t_chunk + 1, dtype=F32)
    pw = jnp.exp((lam * dt)[..., None] * ks)
    hi = lax.Precision.HIGHEST
    kern = jnp.einsum("dgcp,dgpk,dgpi->dgkci", c_mat, pw[..., :t_chunk], b_bar, precision=hi).real
    t_idx = jnp.arange(t_chunk)
    lag = t_idx[None, :] - t_idx[:, None]
    k_f = kern[0][:, jnp.clip(lag, 0, t_chunk - 1)]
    k_r = kern[1][:, jnp.clip(-lag, 0, t_chunk - 1)]
    toep = (jnp.where((lag >= 0)[None, :, :, None, None], k_f, 0.0)
            + jnp.where((lag <= 0)[None, :, :, None, None], k_r, 0.0))
    toep = toep.transpose(0, 1, 4, 2, 3).reshape(n_grp, t_chunk * cg, t_chunk * cg)
    pf = pw[0][..., :t_chunk][..., ::-1]
    pr = pw[1][..., :t_chunk]
    bend_f = jnp.einsum("gps,gpc->gscp", pf, b_bar[0]).reshape(n_grp, t_chunk * cg, n_state)
    bend_r = jnp.einsum("gps,gpc->gscp", pr, b_bar[1]).reshape(n_grp, t_chunk * cg, n_state)
    bend = jnp.stack([bend_f.real, bend_f.imag, bend_r.real, bend_r.imag], axis=1)
    of = jnp.einsum("gcp,gpt->gptc", c_mat[0], pw[0][..., 1:]).reshape(n_grp, n_state, t_chunk * cg)
    orv = jnp.einsum("gcp,gpt->gptc", c_mat[1], pw[1][..., 1:][..., ::-1]).reshape(n_grp, n_state, t_chunk * cg)
    cout = jnp.stack([of.real, -of.imag, orv.real, -orv.imag], axis=1)

    def pair_diag(a, axis_r, axis_c):
        g2 = a.reshape(n_grp // 2, 2, *a.shape[1:])
        z = jnp.zeros_like(g2[:, 0])
        top = jnp.concatenate([g2[:, 0], z], axis=axis_c)
        bot = jnp.concatenate([z, g2[:, 1]], axis=axis_c)
        return jnp.concatenate([top, bot], axis=axis_r)

    bend2 = pair_diag(bend, 2, 3)
    cout2 = pair_diag(cout, 2, 3)
    lam_t = pw[..., t_chunk]
    lam_t4 = jnp.stack([lam_t[0].real, lam_t[0].imag, lam_t[1].real, lam_t[1].imag]).reshape(4, n_grp * n_state)
    return toep, bend2, cout2, lam_t4


def _s5_a_body(u_ref, toep_ref, bend_ref, y_ref, e0, e1, e2, e3, *, w1):
    u = u_ref[...].astype(BF16)
    for g in range(2):
        y_ref[:, g * w1:(g + 1) * w1] = jnp.dot(u[:, g * w1:(g + 1) * w1], toep_ref[g].astype(BF16),
                                                 preferred_element_type=F32)
    for c, e_ref in enumerate((e0, e1, e2, e3)):
        e_ref[...] = jnp.dot(u, bend_ref[0, c].astype(BF16), preferred_element_type=F32)


def _s5_b_body(e0, e1, e2, e3, lam_ref, xf_re, xf_im, xr_re, xr_im, *, n_batch, n_chunk, n_ctx_chunk):
    lam = lam_ref[...]
    lfr, lfi, lrr, lri = lam[0:1], lam[1:2], lam[2:3], lam[3:4]
    width = e0.shape[1]
    for b in range(n_batch):
        base = b * n_chunk

        def fwd(n, carry):
            sr, si = carry
            row = base + n
            xf_re[pl.ds(row, 1), :] = sr
            xf_im[pl.ds(row, 1), :] = si
            er = e0[pl.ds(row, 1), :]
            ei = e1[pl.ds(row, 1), :]
            return lfr * sr - lfi * si + er, lfr * si + lfi * sr + ei

        def rev(n, carry):
            sr, si = carry
            row = base + jnp.where(n < n_ctx_chunk, n_ctx_chunk - 1 - n, n_chunk + n_ctx_chunk - 1 - n)
            xr_re[pl.ds(row, 1), :] = sr
            xr_im[pl.ds(row, 1), :] = si
            er = e2[pl.ds(row, 1), :]
            ei = e3[pl.ds(row, 1), :]
            return lrr * sr - lri * si + er, lrr * si + lri * sr + ei

        z = jnp.zeros((1, width), F32)
        lax.fori_loop(0, n_chunk, fwd, (z, z))
        lax.fori_loop(0, n_chunk, rev, (z, z))


def _gelu(y):
    return 0.5 * y * (1.0 + jnp.tanh(0.7978845608028654 * (y + 0.044715 * (y * y * y))))


def _s5_c_body(y_ref, u_ref, x0, x1, x2, x3, cout_ref, dsk_ref, o_ref):
    acc = y_ref[...] + u_ref[...] * dsk_ref[...]
    for c, x_ref in enumerate((x0, x1, x2, x3)):
        acc = acc + jnp.dot(x_ref[...].astype(BF16), cout_ref[0, c].astype(BF16), preferred_element_type=F32)
    o_ref[...] = _gelu(acc).astype(o_ref.dtype)


def s5_core(h_f32, weights, d_skip, n_batch, seq, ctx_len):
    toep, bend2, cout2, lam_t4 = weights
    t = S5_CHUNK
    m, d = h_f32.shape
    n_grp = toep.shape[0]
    cg = d // n_grp
    w1 = t * cg
    n_state2 = bend2.shape[-1]
    n_lat = n_batch * seq
    ncc, ncl = ctx_len // t, seq // t
    n_chunk = ncc + ncl
    rows = n_batch * n_chunk
    hb = h_f32
    lat = hb[:n_lat].reshape(n_batch, ncl, t, n_grp, cg)
    ctx = hb[n_lat:].reshape(n_batch, ncc, t, n_grp, cg)
    u2 = jnp.concatenate([ctx, lat], axis=1).transpose(0, 1, 3, 2, 4).reshape(rows, n_grp * w1)
    n_pair = n_grp // 2
    ublk = pl.BlockSpec((rows, 2 * w1), lambda p: (0, p))
    eblk = pl.BlockSpec((rows, n_state2), lambda p: (0, p))
    e_shape = jax.ShapeDtypeStruct((rows, n_pair * n_state2), F32)
    y_intra, e0, e1, e2, e3 = pl.pallas_call(
        functools.partial(_s5_a_body, w1=w1),
        grid=(n_pair,),
        in_specs=[ublk,
                  pl.BlockSpec((2, w1, w1), lambda p: (p, 0, 0)),
                  pl.BlockSpec((1, 4, 2 * w1, n_state2), lambda p: (p, 0, 0, 0))],
        out_specs=[ublk, eblk, eblk, eblk, eblk],
        out_shape=[jax.ShapeDtypeStruct((rows, n_grp * w1), F32), e_shape, e_shape, e_shape, e_shape],
        compiler_params=_cparams(("parallel",)),
    )(u2, toep, bend2)

    lanes = n_pair * n_state2
    lb = min(lanes, 512)
    sblk = pl.BlockSpec((rows, lb), lambda p: (0, p))
    xs = pl.pallas_call(
        functools.partial(_s5_b_body, n_batch=n_batch, n_chunk=n_chunk, n_ctx_chunk=ncc),
        grid=(lanes // lb,),
        in_specs=[sblk, sblk, sblk, sblk, pl.BlockSpec((4, lb), lambda p: (0, p))],
        out_specs=[sblk, sblk, sblk, sblk],
        out_shape=[e_shape, e_shape, e_shape, e_shape],
        compiler_params=_cparams(("parallel",)),
    )(e0, e1, e2, e3, lam_t4)

    dsk = jnp.tile(d_skip.astype(F32).reshape(n_grp, 1, cg), (1, t, 1)).reshape(1, n_grp * w1)
    y2 = pl.pallas_call(
        _s5_c_body,
        grid=(n_pair,),
        in_specs=[ublk, ublk, eblk, eblk, eblk, eblk,
                  pl.BlockSpec((1, 4, n_state2, 2 * w1), lambda p: (p, 0, 0, 0)),
                  pl.BlockSpec((1, 2 * w1), lambda p: (0, p))],
        out_specs=ublk,
        out_shape=jax.ShapeDtypeStruct((rows, n_grp * w1), F32),
        compiler_params=_cparams(("parallel",)),
    )(y_intra, u2, *xs, cout2, dsk)
    y5 = y2.reshape(n_batch, n_chunk, n_grp, t, cg).transpose(0, 1, 3, 2, 4)
    y_ctx = y5[:, :ncc].reshape(n_batch * ctx_len, d)
    y_lat = y5[:, ncc:].reshape(n_lat, d)
    return jnp.concatenate([y_lat, y_ctx], axis=0)


def _shift_body(h_ref, hp_ref, hn_ref, mu_ref, *o_refs, tm, tiles_lat, tiles_ctx, n_lat_tiles):
    i = pl.program_id(0)
    h = h_ref[...]
    first = jnp.where(i < n_lat_tiles, i % tiles_lat == 0, (i - n_lat_tiles) % tiles_ctx == 0)
    last = jnp.where(i < n_lat_tiles, i % tiles_lat == tiles_lat - 1,
                     (i - n_lat_tiles) % tiles_ctx == tiles_ctx - 1)
    rid = lax.broadcasted_iota(jnp.int32, h.shape, 0)
    prev_edge = jnp.where(first, 0.0, hp_ref[7:8, :])
    next_edge = jnp.where(last, 0.0, hn_ref[0:1, :])
    prev = jnp.where(rid == 0, prev_edge, pltpu.roll(h, 1, 0))
    nxt = jnp.where(rid == tm - 1, next_edge, pltpu.roll(h, tm - 1, 0))
    xx = 0.5 * (prev + nxt) - h
    for m, o_ref in enumerate(o_refs):
        o_ref[...] = (h + xx * mu_ref[m:m + 1, :]).astype(o_ref.dtype)


def token_shift_mix(h, mu, tm, seq, ctx_len, n_batch):
    m, d = h.shape
    n_mix = mu.shape[0]
    n_lat_tiles = n_batch * seq // tm
    nb8 = m // 8
    per = tm // 8
    row = pl.BlockSpec((tm, d), lambda i: (i, 0))
    return pl.pallas_call(
        functools.partial(_shift_body, tm=tm, tiles_lat=seq // tm, tiles_ctx=max(ctx_len // tm, 1),
                          n_lat_tiles=n_lat_tiles),
        grid=(m // tm,),
        in_specs=[row,
                  pl.BlockSpec((8, d), lambda i: (jnp.maximum(i * per - 1, 0), 0)),
                  pl.BlockSpec((8, d), lambda i: (jnp.minimum((i + 1) * per, nb8 - 1), 0)),
                  pl.BlockSpec((n_mix, d), lambda i: (0, 0))],
        out_specs=[row] * n_mix,
        out_shape=[jax.ShapeDtypeStruct((m, d), BF16)] * n_mix,
        compiler_params=_cparams(("parallel",)),
    )(h, h, h, mu.astype(F32))


def _rwkv_scan_body(rf_ref, rr_ref, kf_ref, kr_ref, vf_ref, vr_ref, wf_ref, wr_ref, af_ref, ar_ref,
                    kk_ref, ka_ref, of_ref, or_ref, s_ref, op_ref, *, tb, n, bh):
    @pl.when(pl.program_id(0) == 0)
    def _():
        s_ref[...] = jnp.zeros_like(s_ref)

    k_k = kk_ref[...]
    k_a = ka_ref[...]

    def step(t, carry):
        tr = tb - 1 - t
        r = jnp.concatenate([rf_ref[t], rr_ref[tr]], axis=-1)
        k = jnp.concatenate([kf_ref[t], kr_ref[tr]], axis=-1)
        v = jnp.concatenate([vf_ref[t], vr_ref[tr]], axis=-1)
        w = jnp.concatenate([wf_ref[t], wr_ref[tr]], axis=-1)
        a = jnp.concatenate([af_ref[t], ar_ref[tr]], axis=-1)
        kk = k * k_k
        nrm = jnp.sqrt(jnp.sum(kk * kk, axis=0, keepdims=True))
        kk = kk / jnp.maximum(nrm, 1e-12)
        nkk = -kk
        akk = kk * a
        kd = k * (1.0 + (a - 1.0) * k_a)
        op_ref[0] = nkk
        op_ref[1] = w * r
        op_ref[2] = w
        op_ref[3] = akk
        op_ref[4] = kd
        c_a = jnp.sum(akk * r, axis=0, keepdims=True)
        c_k = jnp.sum(kd * r, axis=0, keepdims=True)
        sa = jnp.zeros_like(v)
        so = jnp.zeros_like(v)
        for j in range(n):
            s_j = s_ref[j]
            sa = sa + s_j * op_ref[0, j:j + 1, :]
            so = so + s_j * op_ref[1, j:j + 1, :]
        o = so + sa * c_a + v * c_k
        of_ref[t] = o[:, :bh]
        or_ref[tr] = o[:, bh:]
        for j in range(n):
            s_ref[j] = s_ref[j] * op_ref[2, j:j + 1, :] + sa * op_ref[3, j:j + 1, :] + v * op_ref[4, j:j + 1, :]
        return carry

    lax.fori_loop(0, tb, step, 0)


def rwkv_scan(r, k, v, w0, w1, a0, a1, k_k, k_a, ctx_len, tb=32):
    steps, n, bh = r.shape
    while steps % tb or ctx_len % tb:
        tb //= 2
    nblk, nblk_ctx = steps // tb, ctx_len // tb

    def rev_blk(i):
        return jnp.where(i < nblk_ctx, nblk_ctx - 1 - i, nblk + nblk_ctx - 1 - i)

    fwd = pl.BlockSpec((tb, n, bh), lambda i: (i, 0, 0))
    rev = pl.BlockSpec((tb, n, bh), lambda i: (rev_blk(i), 0, 0))
    tab = pl.BlockSpec((n, 2 * bh), lambda i: (0, 0))
    out = jax.ShapeDtypeStruct((steps, n, bh), F32)
    return pl.pallas_call(
        functools.partial(_rwkv_scan_body, tb=tb, n=n, bh=bh),
        grid=(nblk,),
        in_specs=[fwd, rev, fwd, rev, fwd, rev, fwd, rev, fwd, rev, tab, tab],
        out_specs=[fwd, rev],
        out_shape=[out, out],
        scratch_shapes=[pltpu.VMEM((n, n, 2 * bh), F32), pltpu.VMEM((5, n, 2 * bh), F32)],
        compiler_params=_cparams(("arbitrary",)),
        name="rwkv_scan",
    )(r, r, k, k, v, v, w0, w1, a0, a1, k_k, k_a)


def _rwkv_post_body(of_ref, or_ref, r_ref, k_ref, v_ref, a0_ref, a1_ref, g_ref, ka_ref, rk_ref, lg_ref, lb_ref,
                    y_ref):
    o = of_ref[...] + or_ref[...]
    mean = jnp.mean(o, axis=1, keepdims=True)
    var = jnp.mean(jnp.square(o - mean), axis=1, keepdims=True)
    on = (o - mean) * lax.rsqrt(var + RW_GN_EPS) * lg_ref[...] + lb_ref[...]
    r = r_ref[...]
    k = k_ref[...]
    k_a = ka_ref[...]
    rk = rk_ref[...]
    kd0 = k * (1.0 + (a0_ref[...] - 1.0) * k_a)
    kd1 = k * (1.0 + (a1_ref[...] - 1.0) * k_a)
    bonus = (jnp.sum(r * kd0 * rk, axis=1, keepdims=True)
             + jnp.sum(r * kd1 * rk, axis=1, keepdims=True)) * v_ref[...]
    y = (on + bonus)
    y_ref[...] = (y * g_ref[...]).astype(y_ref.dtype)


def rwkv_post(o_f, o_r, r, k, v, a0, a1, gate, k_a, r_k, ln_g, ln_b, tb=32):
    steps, n, ch = o_f.shape
    while steps % tb:
        tb //= 2
    blk = pl.BlockSpec((tb, n, ch), lambda i: (i, 0, 0))
    tab = pl.BlockSpec((1, n, ch), lambda i: (0, 0, 0))
    return pl.pallas_call(
        _rwkv_post_body,
        grid=(steps // tb,),
        in_specs=[blk] * 8 + [tab] * 4,
        out_specs=blk,
        out_shape=jax.ShapeDtypeStruct((steps, n, ch), BF16),
        compiler_params=_cparams(("parallel",)),
        name="rwkv_post",
    )(o_f, o_r, r, k, v, a0, a1, gate, k_a[None], r_k[None], ln_g[None], ln_b[None])


def _pad_cols(w, mult=128):
    n = w.shape[-1]
    p = -(-n // mult) * mult - n
    return jnp.pad(w, [(0, 0)] * (w.ndim - 1) + [(0, p)]) if p else w


def _pad_rows(w, mult=128):
    n = w.shape[-2]
    p = -(-n // mult) * mult - n
    return jnp.pad(w, [(0, 0)] * (w.ndim - 2) + [(0, p), (0, 0)]) if p else w


def rwkv_mixer_tokens(h, mu, w_rkv, w0, w1, w2, a0, a1, a2, k_k, k_a, r_k, ln_g, ln_b, g1, g2,
                      tm, n_batch, seq, ctx_len):
    m, d = h.shape
    n_heads, n = r_k.shape
    n_lat = n_batch * seq
    ts = _row_tile(seq, ctx_len, cap=256)
    x_r, x_k, x_v, x_w, x_a, x_g = token_shift_mix(h, mu, ts, seq, ctx_len, n_batch)
    mm = functools.partial(matmul, tm=tm)
    r = mm(x_r, w_rkv, w_index=0)
    k = mm(x_k, w_rkv, w_index=1)
    v = mm(x_v, w_rkv, w_index=2)
    gate = mm(mm(x_g, _pad_cols(g1), act="sigmoid", out_dtype=BF16), _pad_rows(g2))
    dec, aa = [], []
    for dn in range(2):
        lw = mm(x_w, _pad_cols(w1[dn]), act="tanh", out_dtype=BF16)
        dec.append(mm(lw, _pad_rows(w2[dn]), bias=w0[dn], act="decay"))
        la = mm(x_a, _pad_cols(a1[dn]), out_dtype=BF16)
        aa.append(mm(la, _pad_rows(a2[dn]), bias=a0[dn], act="sigmoid"))

    def to_seq(x):
        lat = x[:n_lat].reshape(n_batch, seq, n_heads, n)
        ctx = x[n_lat:].reshape(n_batch, ctx_len, n_heads, n)
        return jnp.concatenate([ctx, lat], axis=1).transpose(1, 3, 0, 2).reshape(ctx_len + seq, n, n_batch * n_heads)

    r_t, k_t, v_t, g_t = to_seq(r), to_seq(k), to_seq(v), to_seq(gate)
    d0_t, d1_t, a0_t, a1_t = to_seq(dec[0]), to_seq(dec[1]), to_seq(aa[0]), to_seq(aa[1])

    def table(vec):
        return jnp.tile(vec.astype(F32).reshape(n_heads, n).T[:, None, :], (1, n_batch, 1)).reshape(n, n_batch * n_heads)

    kk_tab, ka_tab = table(k_k), table(k_a)
    o_f, o_r = rwkv_scan(r_t, k_t, v_t, d0_t, d1_t, a0_t, a1_t, jnp.concatenate([kk_tab, kk_tab], axis=-1),
                         jnp.concatenate([ka_tab, ka_tab], axis=-1), ctx_len)
    y_t = rwkv_post(o_f, o_r, r_t, k_t, v_t, a0_t, a1_t, g_t, ka_tab, table(r_k.reshape(-1)),
                    table(ln_g), table(ln_b))
    y4 = y_t.reshape(ctx_len + seq, n, n_batch, n_heads).transpose(2, 0, 3, 1)
    y_ctx = y4[:, :ctx_len].reshape(n_batch * ctx_len, d)
    y_lat = y4[:, ctx_len:].reshape(n_lat, d)
    return jnp.concatenate([y_lat, y_ctx], axis=0)


def kernel(x, c, ctx, c_ctx, mod_w, mod_b, norm_g, final_g, na_w_qkv, na_w_o, na_rpb, s5_lam_re, s5_lam_im, s5_log_dt, s5_b_re, s5_b_im, s5_c_re, s5_c_im, s5_d, s5_w_glu, rw_mu, rw_w_rkv, rw_w0, rw_w1, rw_w2, rw_a0, rw_a1, rw_a2, rw_k_k, rw_k_a, rw_r_k, rw_ln_g, rw_ln_b, rw_g1, rw_g2, rw_w_o, moe_wg, moe_bg, moe_we, moe_be, moe_w_gu, moe_w_down):
    n_batch, seq, d = x.shape
    ctx_len = ctx.shape[1]
    depth = mod_w.shape[0]
    n_lat = n_batch * seq
    n_ctx = n_batch * ctx_len
    tm = _row_tile(seq, n_ctx)
    tpb = seq // tm
    xs = jnp.concatenate([x.reshape(n_lat, d), ctx.reshape(n_ctx, d)], axis=0).astype(F32)

    cvecs = jnp.concatenate([c.astype(F32), c_ctx.astype(F32)[None]], axis=0)
    rows_pad = -(-(n_batch + 1) // 8) * 8
    cvecs = jnp.pad(cvecs, ((0, rows_pad - n_batch - 1), (0, 0)))
    mods = modulation(cvecs, mod_w, mod_b)[:, :n_batch + 1].reshape(depth, n_batch + 1, 6, 1, d)
    mods = mods.transpose(0, 2, 1, 3, 4)

    for i in range(depth):
        last = i == depth - 1
        sh1, sc1, g1, sh2, sc2, g2 = (mods[i, q] for q in range(6))
        pn = functools.partial(prenorm, tm=tm, tiles_per_batch=tpb, n_batch=n_batch)
        mm = functools.partial(matmul, tm=tm, tiles_per_batch=tpb, n_batch=n_batch)
        mix, j = i % N_MIXERS, i // N_MIXERS
        if mix == 0:
            (hb,) = pn(xs, norm_g[i, 0], 1.0 + sc1, sh1, out_dtypes=(BF16,))
            n_heads = na_rpb.shape[1]
            qscale = jnp.concatenate([jnp.full((d,), (d // n_heads) ** -0.5, F32), jnp.ones((2 * d,), F32)])
            qkv = mm(hb, na_w_qkv, w_index=j, out_dtype=BF16, colscale=qscale)
            o = na_attention(qkv, _na_bias_table(na_rpb[j], GRID_W), n_batch, seq, ctx_len, not last)
            if last:
                xs = xs[:n_lat]
            xs = mm(o, na_w_o, w_index=j, mode="resid", resid=xs, gate=g1)
        elif mix == 1:
            (hf,) = pn(xs, norm_g[i, 0], 1.0 + sc1, sh1, out_dtypes=(F32,))
            wts = _s5_weights(s5_lam_re[j], s5_lam_im[j], s5_log_dt[j], s5_b_re[j], s5_b_im[j],
                              s5_c_re[j], s5_c_im[j], S5_CHUNK)
            y = s5_core(hf, wts, s5_d[j], n_batch, seq, ctx_len)
            if last:
                xs, y = xs[:n_lat], y[:n_lat]
            xs = mm(y, s5_w_glu, w_index=j, mode="glu", resid=xs, gate=g1)
        else:
            (hf,) = pn(xs, norm_g[i, 0], 1.0 + sc1, sh1, out_dtypes=(F32,))
            y = rwkv_mixer_tokens(hf, rw_mu[j], rw_w_rkv[j], rw_w0[j], rw_w1[j], rw_w2[j], rw_a0[j], rw_a1[j],
                                  rw_a2[j], rw_k_k[j], rw_k_a[j], rw_r_k[j], rw_ln_g[j], rw_ln_b[j],
                                  rw_g1[j], rw_g2[j], tm, n_batch, seq, ctx_len)
            if last:
                xs, y = xs[:n_lat], y[:n_lat]
            xs = mm(y, rw_w_o, w_index=j, mode="resid", resid=xs, gate=g1)
        (h2,) = pn(xs, norm_g[i, 1], 1.0 + sc2, sh2, out_dtypes=(F32,))
        xs = hier_moe(xs, h2, moe_wg[i], moe_bg[i], moe_we[i], moe_be[i], moe_w_gu, moe_w_down, i, g2,
                      tm, n_batch, seq)
    ones = jnp.ones((n_batch + 1, 1, d), F32)
    (out,) = prenorm(xs[:n_lat], final_g, ones, jnp.zeros_like(ones), tm, tpb, n_batch, out_dtypes=(x.dtype,))
    return out.reshape(n_batch, seq, d)
```

```python
import functools
import math

import jax
import jax.numpy as jnp
from jax import lax
from jax.experimental import pallas as pl
from jax.experimental.pallas import tpu as pltpu

F32 = jnp.float32
BF16 = jnp.bfloat16

GRID_W = 64
N_MIXERS = 3
NORM_EPS = 1e-6
RW_GN_EPS = 64e-5
MOE_TOP_K = 2
MOE_BLOCK = 128
S5_CHUNK = 16
SCAN_KEY_BLOCK = 16
NA_Q_ROWS = 4
NA_KEY_ROWS = 12
NEG_BIG = -1e30
VMEM_LIMIT_BYTES = 48 * 1024 * 1024
VMEM_LIMIT_BIG_BYTES = 56 * 1024 * 1024


def _cparams(sem, limit=VMEM_LIMIT_BYTES):
    return pltpu.CompilerParams(dimension_semantics=sem, vmem_limit_bytes=limit)


def _row_tile(n_lat_per_batch, n_ctx_total, cap=512):
    t = cap
    while t > 8 and (n_lat_per_batch % t or (n_ctx_total and n_ctx_total % t)):
        t //= 2
    return t


def _seg_fn(tiles_per_batch, n_batch):
    def seg(i):
        return jnp.minimum(i // tiles_per_batch, n_batch)
    return seg


def _bdot(a, b):
    return jnp.dot(a.astype(BF16), b.astype(BF16), preferred_element_type=F32)


def _split3(x):
    hi = x.astype(BF16)
    r1 = x - hi.astype(F32)
    mid = r1.astype(BF16)
    lo = (r1 - mid.astype(F32)).astype(BF16)
    return hi, mid, lo


def _prenorm_body(x_ref, g_ref, sc_ref, sh_ref, *o_refs):
    x = x_ref[...]
    ms = jnp.mean(x * x, axis=-1, keepdims=True)
    h = (x * lax.rsqrt(ms + NORM_EPS)) * g_ref[...]
    h = h * sc_ref[0] + sh_ref[0]
    for o_ref in o_refs:
        o_ref[...] = h.astype(o_ref.dtype)


def prenorm(x, g, scale1p, shift, tm, tiles_per_batch, n_batch, out_dtypes):
    m, d = x.shape
    seg = _seg_fn(tiles_per_batch, n_batch)
    row = pl.BlockSpec((tm, d), lambda i: (i, 0))
    tab = pl.BlockSpec((1, 1, d), lambda i: (seg(i), 0, 0))
    outs = pl.pallas_call(
        _prenorm_body,
        grid=(m // tm,),
        in_specs=[row, pl.BlockSpec((1, d), lambda i: (0, 0)), tab, tab],
        out_specs=[row for _ in out_dtypes],
        out_shape=[jax.ShapeDtypeStruct((m, d), dt) for dt in out_dtypes],
        compiler_params=_cparams(("parallel",)),
    )(x, g.reshape(1, d), scale1p, shift)
    return outs


def _softplus(z):
    return jnp.maximum(z, 0.0) + jnp.log(1.0 + jnp.exp(-jnp.abs(z)))


def _apply_act(y, act):
    if act is None:
        return y
    if act == "tanh":
        return jnp.tanh(y)
    if act == "sigmoid":
        return jax.nn.sigmoid(y)
    if act == "decay":
        return jnp.exp(-jnp.exp(-_softplus(-y) - 0.5))
    raise ValueError(act)


def _mm_body(*refs, mode, act, has_bias, has_scale):
    it = iter(refs)
    x_ref = next(it)
    w_ref = next(it)
    w2_ref = next(it) if mode == "glu" else None
    b_ref = next(it) if has_bias else None
    s_ref = next(it) if has_scale else None
    r_ref = next(it) if mode in ("resid", "glu") else None
    g_ref = next(it) if mode in ("resid", "glu") else None
    o_ref = next(it)
    x = x_ref[...].astype(BF16)
    acc = jnp.dot(x, w_ref[...].astype(BF16), preferred_element_type=F32)
    if mode == "glu":
        acc2 = jnp.dot(x, w2_ref[...].astype(BF16), preferred_element_type=F32)
        acc = acc * jax.nn.sigmoid(acc2)
    if has_bias:
        acc = acc + b_ref[...]
    acc = _apply_act(acc, act)
    if has_scale:
        acc = acc * s_ref[...]
    if mode in ("resid", "glu"):
        acc = r_ref[...] + g_ref[0] * acc
    o_ref[...] = acc.astype(o_ref.dtype)


def matmul(x, w, *, tm, tn=512, out_dtype=F32, mode="plain", act=None, bias=None, colscale=None,
           resid=None, gate=None, tiles_per_batch=None, n_batch=None, w_index=0):
    m, k = x.shape
    if w.ndim == 2:
        w, w_index = w[None], 0
    n = w.shape[2] // 2 if mode == "glu" else w.shape[2]
    tn = min(tn, n)
    assert m % tm == 0 and n % tn == 0, (m, tm, n, tn)
    nj = n // tn
    in_specs = [pl.BlockSpec((tm, k), lambda i, j: (i, 0)),
                pl.BlockSpec((None, k, tn), lambda i, j: (w_index, 0, j))]
    args = [x, w]
    if mode == "glu":
        in_specs.append(pl.BlockSpec((None, k, tn), lambda i, j: (w_index, 0, j + nj)))
        args.append(w)
    col = pl.BlockSpec((1, tn), lambda i, j: (0, j))
    if bias is not None:
        in_specs.append(col)
        args.append(bias.reshape(1, n).astype(F32))
    if colscale is not None:
        in_specs.append(col)
        args.append(colscale.reshape(1, n).astype(F32))
    if mode in ("resid", "glu"):
        seg = _seg_fn(tiles_per_batch, n_batch)
        in_specs.append(pl.BlockSpec((tm, tn), lambda i, j: (i, j)))
        in_specs.append(pl.BlockSpec((1, 1, tn), lambda i, j: (seg(i), 0, j)))
        args += [resid, gate]
    return pl.pallas_call(
        functools.partial(_mm_body, mode=mode, act=act, has_bias=bias is not None,
                          has_scale=colscale is not None),
        grid=(m // tm, nj),
        in_specs=in_specs,
        out_specs=pl.BlockSpec((tm, tn), lambda i, j: (i, j)),
        out_shape=jax.ShapeDtypeStruct((m, n), out_dtype),
        compiler_params=_cparams(("parallel", "parallel")),
    )(*args)


def _router_body(x_ref, w_ref, b_ref, o_ref, *, n_grp, epg):
    xh, xm, xl = _split3(x_ref[...])
    wh, wm, wl = _split3(w_ref[...])
    dot = functools.partial(jnp.dot, preferred_element_type=F32)
    acc = dot(xh, wh) + (dot(xh, wm) + dot(xm, wh)) + (dot(xh, wl) + dot(xm, wm) + dot(xl, wh))
    acc = acc + b_ref[...]
    lane = lax.broadcasted_iota(jnp.int32, acc.shape, 1)
    lane_f = lane.astype(F32)
    far = 1e9

    def first_max(vals):
        m = jnp.max(vals, axis=-1, keepdims=True)
        return m, jnp.min(jnp.where(vals == m, lane_f, far), axis=-1, keepdims=True)

    gl = jnp.where(lane < n_grp, acc, NEG_BIG)
    gmax, grp = first_max(gl)
    p_grp = 1.0 / jnp.sum(jnp.exp(gl - gmax), axis=-1, keepdims=True)
    lo = n_grp + grp * epg
    el = jnp.where((lane_f >= lo) & (lane_f < lo + epg), acc, NEG_BIG)
    m1, i1 = first_max(el)
    m2, i2 = first_max(jnp.where(lane_f == i1, NEG_BIG, el))
    e21 = jnp.exp(m2 - m1)
    g1 = p_grp / (1.0 + e21)
    g2 = p_grp * e21 / (1.0 + e21)
    out = jnp.where(lane == 0, i1 - n_grp, jnp.where(lane == 1, i2 - n_grp,
                                                    jnp.where(lane == 2, g1, jnp.where(lane == 3, g2, 0.0))))
    o_ref[...] = out


def router(x, w, bias, tm, n_grp, epg):
    m, k = x.shape
    n = w.shape[1]
    return pl.pallas_call(
        functools.partial(_router_body, n_grp=n_grp, epg=epg),
        grid=(m // tm,),
        in_specs=[pl.BlockSpec((tm, k), lambda i: (i, 0)), pl.BlockSpec((k, n), lambda i: (0, 0)),
                  pl.BlockSpec((1, n), lambda i: (0, 0))],
        out_specs=pl.BlockSpec((tm, n), lambda i: (i, 0)),
        out_shape=jax.ShapeDtypeStruct((m, n), F32),
        compiler_params=_cparams(("parallel",)),
        name="moe_router",
    )(x, w, bias.reshape(1, n))


def _mod_body(s_ref, w_ref, b_ref, o_ref):
    s = s_ref[...]
    s = s * jax.nn.sigmoid(s)
    o_ref[0] = _bdot(s, w_ref[0]) + b_ref[0]


def modulation(cvecs, mod_w, mod_b, tn=768):
    depth, d, n = mod_w.shape
    rows = cvecs.shape[0]
    while n % tn:
        tn //= 2
    return pl.pallas_call(
        _mod_body,
        grid=(depth, n // tn),
        in_specs=[pl.BlockSpec((rows, d), lambda l, j: (0, 0)),
                  pl.BlockSpec((1, d, tn), lambda l, j: (l, 0, j)),
                  pl.BlockSpec((1, 1, tn), lambda l, j: (l, 0, j))],
        out_specs=pl.BlockSpec((1, rows, tn), lambda l, j: (l, 0, j)),
        out_shape=jax.ShapeDtypeStruct((depth, rows, n), F32),
        compiler_params=_cparams(("parallel", "parallel")),
    )(cvecs, mod_w, mod_b.reshape(depth, 1, n))


def _na_key_start(i, rows, kr):
    return jnp.clip(i * NA_Q_ROWS - kr // 2, 0, rows - NA_KEY_ROWS)


def _na_bias_table(rpb, grid_w, rows):
    n_heads = rpb.shape[0]
    kr = (rpb.shape[1] + 1) // 2
    kcw = (rpb.shape[2] + 1) // 2
    n_blk = rows // NA_Q_ROWS
    q = jnp.arange(grid_w)[:, None]
    kc = jnp.arange(grid_w)[None, :]
    win = jnp.clip(q - kcw // 2, 0, grid_w - kcw)
    col_ok = (kc >= win) & (kc < win + kcw)
    dc = jnp.clip(kc - q, 1 - kcw, kcw - 1) + kcw - 1
    blk = jnp.array([0, min(1, n_blk - 1), n_blk - 1])[:, None, None]
    r = blk * NA_Q_ROWS + jnp.arange(NA_Q_ROWS)[None, :, None]
    krow = _na_key_start(blk, rows, kr) + jnp.arange(NA_KEY_ROWS)[None, None, :]
    rs = jnp.clip(r - kr // 2, 0, rows - kr)
    row_ok = (krow >= rs) & (krow < rs + kr)
    dr = jnp.clip(krow - r + kr - 1, 0, 2 * kr - 2)
    tab = rpb.astype(F32)[:, dr][..., dc]
    ok = row_ok[None, :, :, :, None, None] & col_ok[None, None, None, None]
    tab = jnp.where(ok, tab, NEG_BIG)
    return tab.transpose(0, 1, 2, 4, 3, 5).reshape(n_heads, 3, NA_Q_ROWS * grid_w, NA_KEY_ROWS * grid_w)


def _na_body(q_ref, k_ref, v_ref, kc_ref, vc_ref, bias_ref, o_ref, *, hpb, dh, grid_w, kr, rows):
    ks = _na_key_start(pl.program_id(2), rows, kr)
    start = pl.multiple_of(ks * grid_w, grid_w)
    nt = (((1,), (1,)), ((), ()))
    for h in range(hpb):
        sl = slice(h * dh, (h + 1) * dh)
        q = q_ref[:, sl]
        k = k_ref[pl.ds(start, NA_KEY_ROWS * grid_w), sl]
        v = v_ref[pl.ds(start, NA_KEY_ROWS * grid_w), sl]
        s = lax.dot_general(q, k, nt, preferred_element_type=F32) + bias_ref[h, 0]
        sc = lax.dot_general(q, kc_ref[:, sl], nt, preferred_element_type=F32)
        m = jnp.maximum(jnp.max(s, axis=-1, keepdims=True), jnp.max(sc, axis=-1, keepdims=True))
        p = jnp.exp(s - m)
        pc = jnp.exp(sc - m)
        den = jnp.sum(p, axis=-1, keepdims=True) + jnp.sum(pc, axis=-1, keepdims=True)
        o = (jnp.dot(p.astype(BF16), v, preferred_element_type=F32)
             + jnp.dot(pc.astype(BF16), vc_ref[:, sl], preferred_element_type=F32))
        o_ref[:, sl] = (o / den).astype(o_ref.dtype)


def _ctx_attn_body(q_ref, k_ref, v_ref, o_ref, *, hpb, dh):
    nt = (((1,), (1,)), ((), ()))
    for h in range(hpb):
        sl = slice(h * dh, (h + 1) * dh)
        s = lax.dot_general(q_ref[:, sl], k_ref[:, sl], nt, preferred_element_type=F32)
        m = jnp.max(s, axis=-1, keepdims=True)
        p = jnp.exp(s - m)
        den = jnp.sum(p, axis=-1, keepdims=True)
        o = jnp.dot(p.astype(BF16), v_ref[:, sl], preferred_element_type=F32)
        o_ref[:, sl] = (o / den).astype(o_ref.dtype)


def na_attention(qkv, bias_tab, kr, n_batch, seq, ctx_len, need_ctx):
    d = qkv.shape[1] // 3
    n_heads = bias_tab.shape[0]
    dh = d // n_heads
    hpb = max(1, min(n_heads, 256 // dh))
    bw = hpb * dh
    n_d = d // bw
    rows = seq // GRID_W
    n_blk = rows // NA_Q_ROWS
    qb = NA_Q_ROWS * GRID_W
    n_lat = n_batch * seq
    ctx_blk0 = n_lat // ctx_len

    def cls_of(i):
        return jnp.where(i == 0, 0, jnp.where(i == n_blk - 1, 2, 1))

    o_lat = pl.pallas_call(
        functools.partial(_na_body, hpb=hpb, dh=dh, grid_w=GRID_W, kr=kr, rows=rows),
        grid=(n_batch, n_d, n_blk),
        in_specs=[
            pl.BlockSpec((qb, bw), lambda b, g, i: (b * n_blk + i, g)),
            pl.BlockSpec((seq, bw), lambda b, g, i: (b, n_d + g)),
            pl.BlockSpec((seq, bw), lambda b, g, i: (b, 2 * n_d + g)),
            pl.BlockSpec((ctx_len, bw), lambda b, g, i: (ctx_blk0 + b, n_d + g)),
            pl.BlockSpec((ctx_len, bw), lambda b, g, i: (ctx_blk0 + b, 2 * n_d + g)),
            pl.BlockSpec((hpb, 1, qb, NA_KEY_ROWS * GRID_W), lambda b, g, i: (g, cls_of(i), 0, 0)),
        ],
        out_specs=pl.BlockSpec((qb, bw), lambda b, g, i: (b * n_blk + i, g)),
        out_shape=jax.ShapeDtypeStruct((n_lat, d), BF16),
        compiler_params=_cparams(("parallel", "parallel", "arbitrary")),
        name="na_attention",
    )(qkv, qkv, qkv, qkv, qkv, bias_tab)
    if not need_ctx:
        return o_lat
    o_ctx = pl.pallas_call(
        functools.partial(_ctx_attn_body, hpb=hpb, dh=dh),
        grid=(n_batch, n_d),
        in_specs=[
            pl.BlockSpec((ctx_len, bw), lambda b, g: (ctx_blk0 + b, g)),
            pl.BlockSpec((ctx_len, bw), lambda b, g: (ctx_blk0 + b, n_d + g)),
            pl.BlockSpec((ctx_len, bw), lambda b, g: (ctx_blk0 + b, 2 * n_d + g)),
        ],
        out_specs=pl.BlockSpec((ctx_len, bw), lambda b, g: (b, g)),
        out_shape=jax.ShapeDtypeStruct((n_batch * ctx_len, d), BF16),
        compiler_params=_cparams(("parallel", "parallel")),
    )(qkv, qkv, qkv)
    return jnp.concatenate([o_lat, o_ctx], axis=0)


def _expert_body(be_ref, idx_ref, idx_next_ref, x_hbm, wgu_ref, wd_ref, o_ref, xbuf, sem, wgu_s, wd_s, *,
                 hidden, n_blocks):
    j = pl.program_id(0)
    cur = j % 2

    def row_copy(idx, buf, r):
        return pltpu.make_async_copy(x_hbm.at[pl.ds(idx[0, 0, r], 1)], xbuf.at[buf, pl.ds(r, 1)], sem.at[buf])

    @pl.when(j == 0)
    def _():
        for r in range(MOE_BLOCK):
            row_copy(idx_ref, 0, r).start()

    @pl.when(j + 1 < n_blocks)
    def _():
        for r in range(MOE_BLOCK):
            row_copy(idx_next_ref, 1 - cur, r).start()

    prev = be_ref[jnp.maximum(j - 1, 0)]
    changed = jnp.logical_and(jnp.logical_or(j == 0, be_ref[j] != prev), j < be_ref[n_blocks])

    @pl.when(changed)
    def _():
        wgu_s[...] = wgu_ref[...].astype(BF16)
        wd_s[...] = wd_ref[...].astype(BF16)

    for r in range(MOE_BLOCK):
        row_copy(idx_ref, cur, r).wait()

    used = j < be_ref[n_blocks]

    @pl.when(used)
    def _():
        gu = jnp.dot(xbuf[cur].astype(BF16), wgu_s[...], preferred_element_type=F32)
        g = gu[:, :hidden]
        u = gu[:, hidden:]
        a = (g * jax.nn.sigmoid(g) * u).astype(BF16)
        o_ref[...] = jnp.dot(a, wd_s[...], preferred_element_type=F32)

    @pl.when(jnp.logical_not(used))
    def _():
        o_ref[...] = jnp.zeros_like(o_ref)


def expert_blocks(x, slot_tok, blk_e, n_used, w_gu, w_down, layer):
    d = x.shape[1]
    hidden = w_down.shape[2]
    n_blocks = blk_e.shape[0]
    idx = slot_tok.reshape(n_blocks, 1, MOE_BLOCK)
    return pl.pallas_call(
        functools.partial(_expert_body, hidden=hidden, n_blocks=n_blocks),
        grid_spec=pltpu.PrefetchScalarGridSpec(
            num_scalar_prefetch=1,
            grid=(n_blocks,),
            in_specs=[pl.BlockSpec((1, 1, MOE_BLOCK), lambda j, be: (j, 0, 0), memory_space=pltpu.SMEM),
                      pl.BlockSpec((1, 1, MOE_BLOCK), lambda j, be: (jnp.minimum(j + 1, n_blocks - 1), 0, 0),
                                   memory_space=pltpu.SMEM),
                      pl.BlockSpec(memory_space=pl.ANY),
                      pl.BlockSpec((None, None, d, 2 * hidden), lambda j, be: (layer, be[j], 0, 0)),
                      pl.BlockSpec((None, None, hidden, d), lambda j, be: (layer, be[j], 0, 0))],
            out_specs=pl.BlockSpec((MOE_BLOCK, d), lambda j, be: (j, 0)),
            scratch_shapes=[pltpu.VMEM((2, MOE_BLOCK, d), F32), pltpu.SemaphoreType.DMA((2,)),
                            pltpu.VMEM((d, 2 * hidden), BF16), pltpu.VMEM((hidden, d), BF16)]),
        out_shape=jax.ShapeDtypeStruct((n_blocks * MOE_BLOCK, d), F32),
        compiler_params=_cparams(("arbitrary",), VMEM_LIMIT_BIG_BYTES),
        name="moe_experts",
    )(jnp.concatenate([blk_e, n_used.reshape(1).astype(jnp.int32)]), idx, idx, x, w_gu, w_down)


def _combine_body(idx_ref, idx_next_ref, route_ref, x_ref, g_ref, yb_hbm, o_ref, ybuf, sem, *, tmc, n_tiles):
    i = pl.program_id(0)
    cur = i % 2

    def row_copy(idx, buf, k, r):
        return pltpu.make_async_copy(yb_hbm.at[pl.ds(idx[0, 0, k * tmc + r], 1)],
                                     ybuf.at[buf, k, pl.ds(r, 1)], sem.at[buf])

    def start_all(idx, buf):
        for k in range(MOE_TOP_K):
            for r in range(tmc):
                row_copy(idx, buf, k, r).start()

    @pl.when(i == 0)
    def _():
        start_all(idx_ref, 0)

    @pl.when(i + 1 < n_tiles)
    def _():
        start_all(idx_next_ref, 1 - cur)

    for k in range(MOE_TOP_K):
        for r in range(tmc):
            row_copy(idx_ref, cur, k, r).wait()
    route = route_ref[...]
    f = ybuf[cur, 0] * route[:, MOE_TOP_K:MOE_TOP_K + 1]
    for k in range(1, MOE_TOP_K):
        f = f + ybuf[cur, k] * route[:, MOE_TOP_K + k:MOE_TOP_K + k + 1]
    o_ref[...] = x_ref[...] + g_ref[0] * f


def moe_combine(xs, yb, slot_of_asg, route, gate_tab, n_batch, seq):
    n_tok, d = xs.shape
    tmc = MOE_BLOCK
    n_tiles = n_tok // tmc
    idx = slot_of_asg.reshape(n_tiles, tmc, MOE_TOP_K).transpose(0, 2, 1).reshape(n_tiles, 1, MOE_TOP_K * tmc)
    seg = _seg_fn(seq // tmc, n_batch)
    return pl.pallas_call(
        functools.partial(_combine_body, tmc=tmc, n_tiles=n_tiles),
        grid=(n_tiles,),
        in_specs=[pl.BlockSpec((1, 1, MOE_TOP_K * tmc), lambda i: (i, 0, 0), memory_space=pltpu.SMEM),
                  pl.BlockSpec((1, 1, MOE_TOP_K * tmc), lambda i: (jnp.minimum(i + 1, n_tiles - 1), 0, 0),
                               memory_space=pltpu.SMEM),
                  pl.BlockSpec((tmc, route.shape[1]), lambda i: (i, 0)),
                  pl.BlockSpec((tmc, d), lambda i: (i, 0)),
                  pl.BlockSpec((1, 1, d), lambda i: (seg(i), 0, 0)),
                  pl.BlockSpec(memory_space=pl.ANY)],
        out_specs=pl.BlockSpec((tmc, d), lambda i: (i, 0)),
        out_shape=jax.ShapeDtypeStruct((n_tok, d), F32),
        scratch_shapes=[pltpu.VMEM((2, MOE_TOP_K, tmc, d), F32), pltpu.SemaphoreType.DMA((2,))],
        compiler_params=_cparams(("arbitrary",)),
        name="moe_combine",
    )(idx, idx, route, xs, gate_tab, yb)


def hier_moe(xs, h, wg, bg, we, be, w_gu, w_down, layer, gate_tab, tm, n_batch, seq):
    n_tok, d = h.shape
    n_grp = wg.shape[1]
    n_exp = we.shape[1]
    epg = n_exp // n_grp
    n_logit = n_grp + n_exp
    n_pad = -(-n_logit // 128) * 128
    wcat = jnp.pad(jnp.concatenate([wg, we], axis=1), ((0, 0), (0, n_pad - n_logit)))
    bcat = jnp.pad(jnp.concatenate([bg, be], axis=0), (0, n_pad - n_logit))
    route = router(h, wcat, bcat, tm, n_grp, epg)

    flat_e = route[:, :MOE_TOP_K].astype(jnp.int32).reshape(-1)
    n_asg = flat_e.shape[0]
    onehot = (flat_e[:, None] == jnp.arange(n_exp, dtype=jnp.int32)[None, :]).astype(F32)
    ck = 128
    oh3 = onehot.reshape(n_asg // ck, ck, n_exp)
    tri = jnp.tril(jnp.ones((ck, ck), F32))
    within = jnp.einsum("ij,cjk->cik", tri, oh3, precision=lax.Precision.HIGHEST)
    tot = within[:, -1, :]
    before = jnp.cumsum(tot, axis=0) - tot
    rank = jnp.sum(oh3 * (within + before[:, None, :]), axis=-1).reshape(n_asg).astype(jnp.int32) - 1
    counts = jnp.sum(tot, axis=0).astype(jnp.int32)
    padded = (counts + MOE_BLOCK - 1) // MOE_BLOCK * MOE_BLOCK
    pad_end = jnp.cumsum(padded)
    pad_start = pad_end - padded
    slot = jnp.sum(onehot * pad_start.astype(F32)[None, :], axis=-1).astype(jnp.int32) + rank
    n_blocks = -(-n_asg // MOE_BLOCK) + n_exp
    slot_tok = jnp.zeros((n_blocks * MOE_BLOCK,), jnp.int32).at[slot].set(
        jnp.arange(n_asg, dtype=jnp.int32) // MOE_TOP_K)
    blk_start = jnp.arange(n_blocks, dtype=jnp.int32) * MOE_BLOCK
    blk_e = jnp.minimum(jnp.sum((pad_end[None, :] <= blk_start[:, None]).astype(jnp.int32), axis=1), n_exp - 1)
    yb = expert_blocks(h, slot_tok, blk_e, pad_end[-1] // MOE_BLOCK, w_gu, w_down, layer)
    return moe_combine(xs, yb, slot.reshape(n_tok, MOE_TOP_K), route, gate_tab, n_batch, seq)


def _s5_weights(lam_re, lam_im, log_dt, b_re, b_im, c_re, c_im, t_chunk):
    n_grp, n_state = lam_re.shape[1], lam_re.shape[2]
    cg = b_re.shape[-1]
    lam = lax.complex(lam_re.astype(F32), lam_im.astype(F32))
    dt = jnp.exp(log_dt.astype(F32))[..., None]
    lam_bar = jnp.exp(lam * dt)
    b_bar = ((lam_bar - 1.0) / lam)[..., None] * lax.complex(b_re.astype(F32), b_im.astype(F32))
    c_mat = lax.complex(c_re.astype(F32), c_im.astype(F32))
    ks = jnp.arange(t_chunk + 1, dtype=F32)
    pw = jnp.exp((lam * dt)[..., None] * ks)
    hi = lax.Precision.HIGHEST
    kern = jnp.einsum("dgcp,dgpk,dgpi->dgkci", c_mat, pw[..., :t_chunk], b_bar, precision=hi).real
    t_idx = jnp.arange(t_chunk)
    lag = t_idx[None, :] - t_idx[:, None]
    k_f = kern[0][:, jnp.clip(lag, 0, t_chunk - 1)]
    k_r = kern[1][:, jnp.clip(-lag, 0, t_chunk - 1)]
    toep = (jnp.where((lag >= 0)[None, :, :, None, None], k_f, 0.0)
            + jnp.where((lag <= 0)[None, :, :, None, None], k_r, 0.0))
    toep = toep.transpose(0, 1, 4, 2, 3).reshape(n_grp, t_chunk * cg, t_chunk * cg)
    pf = pw[0][..., :t_chunk][..., ::-1]
    pr = pw[1][..., :t_chunk]
    bend_f = jnp.einsum("gps,gpc->gscp", pf, b_bar[0]).reshape(n_grp, t_chunk * cg, n_state)
    bend_r = jnp.einsum("gps,gpc->gscp", pr, b_bar[1]).reshape(n_grp, t_chunk * cg, n_state)
    bend = jnp.stack([bend_f.real, bend_f.imag, bend_r.real, bend_r.imag], axis=1)
    of = jnp.einsum("gcp,gpt->gptc", c_mat[0], pw[0][..., 1:]).reshape(n_grp, n_state, t_chunk * cg)
    orv = jnp.einsum("gcp,gpt->gptc", c_mat[1], pw[1][..., 1:][..., ::-1]).reshape(n_grp, n_state, t_chunk * cg)
    cout = jnp.stack([of.real, -of.imag, orv.real, -orv.imag], axis=1)

    def pair_diag(a, axis_r, axis_c):
        g2 = a.reshape(n_grp // 2, 2, *a.shape[1:])
        z = jnp.zeros_like(g2[:, 0])
        top = jnp.concatenate([g2[:, 0], z], axis=axis_c)
        bot = jnp.concatenate([z, g2[:, 1]], axis=axis_c)
        return jnp.concatenate([top, bot], axis=axis_r)

    bend2 = pair_diag(bend, 2, 3)
    cout2 = pair_diag(cout, 2, 3)
    lam_t = pw[..., t_chunk]
    lam_t4 = jnp.stack([lam_t[0].real, lam_t[0].imag, lam_t[1].real, lam_t[1].imag]).reshape(4, n_grp * n_state)
    return toep, bend2, cout2, lam_t4


def _s5_a_body(u_ref, toep_ref, bend_ref, y_ref, e0, e1, e2, e3, *, w1):
    u = u_ref[...].astype(BF16)
    for g in range(2):
        y_ref[:, g * w1:(g + 1) * w1] = jnp.dot(u[:, g * w1:(g + 1) * w1], toep_ref[g].astype(BF16),
                                                 preferred_element_type=F32)
    for c, e_ref in enumerate((e0, e1, e2, e3)):
        e_ref[...] = jnp.dot(u, bend_ref[0, c].astype(BF16), preferred_element_type=F32)


def _s5_b_body(e0, e1, e2, e3, lam_ref, xf_re, xf_im, xr_re, xr_im, *, n_batch, n_chunk, n_ctx_chunk):
    lam = lam_ref[...]
    lfr, lfi, lrr, lri = lam[0:1], lam[1:2], lam[2:3], lam[3:4]
    width = e0.shape[1]
    for b in range(n_batch):
        base = b * n_chunk

        def fwd(n, carry):
            sr, si = carry
            row = base + n
            xf_re[pl.ds(row, 1), :] = sr
            xf_im[pl.ds(row, 1), :] = si
            er = e0[pl.ds(row, 1), :]
            ei = e1[pl.ds(row, 1), :]
            return lfr * sr - lfi * si + er, lfr * si + lfi * sr + ei

        def rev(n, carry):
            sr, si = carry
            row = base + jnp.where(n < n_ctx_chunk, n_ctx_chunk - 1 - n, n_chunk + n_ctx_chunk - 1 - n)
            xr_re[pl.ds(row, 1), :] = sr
            xr_im[pl.ds(row, 1), :] = si
            er = e2[pl.ds(row, 1), :]
            ei = e3[pl.ds(row, 1), :]
            return lrr * sr - lri * si + er, lrr * si + lri * sr + ei

        z = jnp.zeros((1, width), F32)
        lax.fori_loop(0, n_chunk, fwd, (z, z))
        lax.fori_loop(0, n_chunk, rev, (z, z))


def _gelu(y):
    return 0.5 * y * (1.0 + jnp.tanh(0.7978845608028654 * (y + 0.044715 * (y * y * y))))


def _s5_c_body(y_ref, u_ref, x0, x1, x2, x3, cout_ref, dsk_ref, o_ref):
    acc = y_ref[...] + u_ref[...] * dsk_ref[...]
    for c, x_ref in enumerate((x0, x1, x2, x3)):
        acc = acc + jnp.dot(x_ref[...].astype(BF16), cout_ref[0, c].astype(BF16), preferred_element_type=F32)
    o_ref[...] = _gelu(acc).astype(o_ref.dtype)


def s5_core(h_f32, weights, d_skip, n_batch, seq, ctx_len):
    toep, bend2, cout2, lam_t4 = weights
    t = S5_CHUNK
    m, d = h_f32.shape
    n_grp = toep.shape[0]
    cg = d // n_grp
    w1 = t * cg
    n_state2 = bend2.shape[-1]
    n_lat = n_batch * seq
    ncc, ncl = ctx_len // t, seq // t
    n_chunk = ncc + ncl
    rows = n_batch * n_chunk
    hb = h_f32
    lat = hb[:n_lat].reshape(n_batch, ncl, t, n_grp, cg)
    ctx = hb[n_lat:].reshape(n_batch, ncc, t, n_grp, cg)
    u2 = jnp.concatenate([ctx, lat], axis=1).transpose(0, 1, 3, 2, 4).reshape(rows, n_grp * w1)
    n_pair = n_grp // 2
    ublk = pl.BlockSpec((rows, 2 * w1), lambda p: (0, p))
    eblk = pl.BlockSpec((rows, n_state2), lambda p: (0, p))
    e_shape = jax.ShapeDtypeStruct((rows, n_pair * n_state2), F32)
    y_intra, e0, e1, e2, e3 = pl.pallas_call(
        functools.partial(_s5_a_body, w1=w1),
        grid=(n_pair,),
        in_specs=[ublk,
                  pl.BlockSpec((2, w1, w1), lambda p: (p, 0, 0)),
                  pl.BlockSpec((1, 4, 2 * w1, n_state2), lambda p: (p, 0, 0, 0))],
        out_specs=[ublk, eblk, eblk, eblk, eblk],
        out_shape=[jax.ShapeDtypeStruct((rows, n_grp * w1), F32), e_shape, e_shape, e_shape, e_shape],
        compiler_params=_cparams(("parallel",)),
    )(u2, toep, bend2)

    lanes = n_pair * n_state2
    lb = min(lanes, 512)
    sblk = pl.BlockSpec((rows, lb), lambda p: (0, p))
    xs = pl.pallas_call(
        functools.partial(_s5_b_body, n_batch=n_batch, n_chunk=n_chunk, n_ctx_chunk=ncc),
        grid=(lanes // lb,),
        in_specs=[sblk, sblk, sblk, sblk, pl.BlockSpec((4, lb), lambda p: (0, p))],
        out_specs=[sblk, sblk, sblk, sblk],
        out_shape=[e_shape, e_shape, e_shape, e_shape],
        compiler_params=_cparams(("parallel",)),
    )(e0, e1, e2, e3, lam_t4)

    dsk = jnp.tile(d_skip.astype(F32).reshape(n_grp, 1, cg), (1, t, 1)).reshape(1, n_grp * w1)
    y2 = pl.pallas_call(
        _s5_c_body,
        grid=(n_pair,),
        in_specs=[ublk, ublk, eblk, eblk, eblk, eblk,
                  pl.BlockSpec((1, 4, n_state2, 2 * w1), lambda p: (p, 0, 0, 0)),
                  pl.BlockSpec((1, 2 * w1), lambda p: (0, p))],
        out_specs=ublk,
        out_shape=jax.ShapeDtypeStruct((rows, n_grp * w1), F32),
        compiler_params=_cparams(("parallel",)),
    )(y_intra, u2, *xs, cout2, dsk)
    y5 = y2.reshape(n_batch, n_chunk, n_grp, t, cg).transpose(0, 1, 3, 2, 4)
    y_ctx = y5[:, :ncc].reshape(n_batch * ctx_len, d)
    y_lat = y5[:, ncc:].reshape(n_lat, d)
    return jnp.concatenate([y_lat, y_ctx], axis=0)


def _shift_body(h_ref, hp_ref, hn_ref, mu_ref, *o_refs, tm, tiles_lat, tiles_ctx, n_lat_tiles):
    i = pl.program_id(0)
    h = h_ref[...]
    first = jnp.where(i < n_lat_tiles, i % tiles_lat == 0, (i - n_lat_tiles) % tiles_ctx == 0)
    last = jnp.where(i < n_lat_tiles, i % tiles_lat == tiles_lat - 1,
                     (i - n_lat_tiles) % tiles_ctx == tiles_ctx - 1)
    rid = lax.broadcasted_iota(jnp.int32, h.shape, 0)
    prev_edge = jnp.where(first, 0.0, hp_ref[7:8, :])
    next_edge = jnp.where(last, 0.0, hn_ref[0:1, :])
    prev = jnp.where(rid == 0, prev_edge, pltpu.roll(h, 1, 0))
    nxt = jnp.where(rid == tm - 1, next_edge, pltpu.roll(h, tm - 1, 0))
    xx = 0.5 * (prev + nxt) - h
    for m, o_ref in enumerate(o_refs):
        o_ref[...] = (h + xx * mu_ref[m:m + 1, :]).astype(o_ref.dtype)


def token_shift_mix(h, mu, tm, seq, ctx_len, n_batch):
    m, d = h.shape
    n_mix = mu.shape[0]
    n_lat_tiles = n_batch * seq // tm
    nb8 = m // 8
    per = tm // 8
    row = pl.BlockSpec((tm, d), lambda i: (i, 0))
    return pl.pallas_call(
        functools.partial(_shift_body, tm=tm, tiles_lat=seq // tm, tiles_ctx=max(ctx_len // tm, 1),
                          n_lat_tiles=n_lat_tiles),
        grid=(m // tm,),
        in_specs=[row,
                  pl.BlockSpec((8, d), lambda i: (jnp.maximum(i * per - 1, 0), 0)),
                  pl.BlockSpec((8, d), lambda i: (jnp.minimum((i + 1) * per, nb8 - 1), 0)),
                  pl.BlockSpec((n_mix, d), lambda i: (0, 0))],
        out_specs=[row] * n_mix,
        out_shape=[jax.ShapeDtypeStruct((m, d), BF16)] * n_mix,
        compiler_params=_cparams(("parallel",)),
    )(h, h, h, mu.astype(F32))


def _rwkv_scan_body(rf_ref, rr_ref, kf_ref, kr_ref, vf_ref, vr_ref, wf_ref, wr_ref, af_ref, ar_ref,
                    kk_ref, ka_ref, of_ref, or_ref, s_ref, op_ref, *, tb, n, bh):
    @pl.when(pl.program_id(0) == 0)
    def _():
        s_ref[...] = jnp.zeros_like(s_ref)

    k_k = kk_ref[...]
    k_a = ka_ref[...]

    def step(t, carry):
        tr = tb - 1 - t
        r = jnp.concatenate([rf_ref[t], rr_ref[tr]], axis=-1)
        k = jnp.concatenate([kf_ref[t], kr_ref[tr]], axis=-1)
        v = jnp.concatenate([vf_ref[t], vr_ref[tr]], axis=-1)
        w = jnp.concatenate([wf_ref[t], wr_ref[tr]], axis=-1)
        a = jnp.concatenate([af_ref[t], ar_ref[tr]], axis=-1)
        kk = k * k_k
        nrm = jnp.sqrt(jnp.sum(kk * kk, axis=0, keepdims=True))
        kk = kk / jnp.maximum(nrm, 1e-12)
        nkk = -kk
        akk = kk * a
        kd = k * (1.0 + (a - 1.0) * k_a)
        op_ref[0] = nkk
        op_ref[1] = w * r
        op_ref[2] = w
        op_ref[3] = akk
        op_ref[4] = kd
        c_a = jnp.sum(akk * r, axis=0, keepdims=True)
        c_k = jnp.sum(kd * r, axis=0, keepdims=True)
        def reduce_keys(jb, acc):
            sa, so = acc
            for jj in range(SCAN_KEY_BLOCK):
                j = jb * SCAN_KEY_BLOCK + jj
                s_j = s_ref[j]
                sa = sa + s_j * op_ref[0, pl.ds(j, 1), :]
                so = so + s_j * op_ref[1, pl.ds(j, 1), :]
            return sa, so

        zero = jnp.zeros_like(v)
        sa, so = lax.fori_loop(0, n // SCAN_KEY_BLOCK, reduce_keys, (zero, zero))
        o = so + sa * c_a + v * c_k
        of_ref[t] = o[:, :bh]
        or_ref[tr] = o[:, bh:]

        def update_keys(jb, c):
            for jj in range(SCAN_KEY_BLOCK):
                j = jb * SCAN_KEY_BLOCK + jj
                s_ref[j] = (s_ref[j] * op_ref[2, pl.ds(j, 1), :] + sa * op_ref[3, pl.ds(j, 1), :]
                            + v * op_ref[4, pl.ds(j, 1), :])
            return c

        lax.fori_loop(0, n // SCAN_KEY_BLOCK, update_keys, 0)
        return carry

    lax.fori_loop(0, tb, step, 0)


def rwkv_scan(r, k, v, w0, w1, a0, a1, k_k, k_a, ctx_len, tb=32):
    steps, n, bh = r.shape
    while steps % tb or ctx_len % tb:
        tb //= 2
    nblk, nblk_ctx = steps // tb, ctx_len // tb

    def rev_blk(i):
        return jnp.where(i < nblk_ctx, nblk_ctx - 1 - i, nblk + nblk_ctx - 1 - i)

    fwd = pl.BlockSpec((tb, n, bh), lambda i: (i, 0, 0))
    rev = pl.BlockSpec((tb, n, bh), lambda i: (rev_blk(i), 0, 0))
    tab = pl.BlockSpec((n, 2 * bh), lambda i: (0, 0))
    out = jax.ShapeDtypeStruct((steps, n, bh), F32)
    return pl.pallas_call(
        functools.partial(_rwkv_scan_body, tb=tb, n=n, bh=bh),
        grid=(nblk,),
        in_specs=[fwd, rev, fwd, rev, fwd, rev, fwd, rev, fwd, rev, tab, tab],
        out_specs=[fwd, rev],
        out_shape=[out, out],
        scratch_shapes=[pltpu.VMEM((n, n, 2 * bh), F32), pltpu.VMEM((5, n, 2 * bh), F32)],
        compiler_params=_cparams(("arbitrary",)),
        name="rwkv_scan",
    )(r, r, k, k, v, v, w0, w1, a0, a1, k_k, k_a)


def _rwkv_post_body(of_ref, or_ref, r_ref, k_ref, v_ref, a0_ref, a1_ref, g_ref, ka_ref, rk_ref, lg_ref, lb_ref,
                    y_ref):
    o = of_ref[...] + or_ref[...]
    mean = jnp.mean(o, axis=1, keepdims=True)
    var = jnp.mean(jnp.square(o - mean), axis=1, keepdims=True)
    on = (o - mean) * lax.rsqrt(var + RW_GN_EPS) * lg_ref[...] + lb_ref[...]
    r = r_ref[...]
    k = k_ref[...]
    k_a = ka_ref[...]
    rk = rk_ref[...]
    kd0 = k * (1.0 + (a0_ref[...] - 1.0) * k_a)
    kd1 = k * (1.0 + (a1_ref[...] - 1.0) * k_a)
    bonus = (jnp.sum(r * kd0 * rk, axis=1, keepdims=True)
             + jnp.sum(r * kd1 * rk, axis=1, keepdims=True)) * v_ref[...]
    y = (on + bonus)
    y_ref[...] = (y * g_ref[...]).astype(y_ref.dtype)


def rwkv_post(o_f, o_r, r, k, v, a0, a1, gate, k_a, r_k, ln_g, ln_b, tb=32):
    steps, n, ch = o_f.shape
    while steps % tb:
        tb //= 2
    blk = pl.BlockSpec((tb, n, ch), lambda i: (i, 0, 0))
    tab = pl.BlockSpec((1, n, ch), lambda i: (0, 0, 0))
    return pl.pallas_call(
        _rwkv_post_body,
        grid=(steps // tb,),
        in_specs=[blk] * 8 + [tab] * 4,
        out_specs=blk,
        out_shape=jax.ShapeDtypeStruct((steps, n, ch), BF16),
        compiler_params=_cparams(("parallel",)),
        name="rwkv_post",
    )(o_f, o_r, r, k, v, a0, a1, gate, k_a[None], r_k[None], ln_g[None], ln_b[None])


def _pad_cols(w, mult=128):
    n = w.shape[-1]
    p = -(-n // mult) * mult - n
    return jnp.pad(w, [(0, 0)] * (w.ndim - 1) + [(0, p)]) if p else w


def _pad_rows(w, mult=128):
    n = w.shape[-2]
    p = -(-n // mult) * mult - n
    return jnp.pad(w, [(0, 0)] * (w.ndim - 2) + [(0, p), (0, 0)]) if p else w


def rwkv_mixer_tokens(h, mu, w_rkv, w0, w1, w2, a0, a1, a2, k_k, k_a, r_k, ln_g, ln_b, g1, g2,
                      tm, n_batch, seq, ctx_len):
    m, d = h.shape
    n_heads, n = r_k.shape
    n_lat = n_batch * seq
    ts = _row_tile(seq, ctx_len, cap=256)
    x_r, x_k, x_v, x_w, x_a, x_g = token_shift_mix(h, mu, ts, seq, ctx_len, n_batch)
    mm = functools.partial(matmul, tm=tm)
    r = mm(x_r, w_rkv, w_index=0)
    k = mm(x_k, w_rkv, w_index=1)
    v = mm(x_v, w_rkv, w_index=2)
    gate = mm(mm(x_g, _pad_cols(g1), act="sigmoid", out_dtype=BF16), _pad_rows(g2))
    dec, aa = [], []
    for dn in range(2):
        lw = mm(x_w, _pad_cols(w1[dn]), act="tanh", out_dtype=BF16)
        dec.append(mm(lw, _pad_rows(w2[dn]), bias=w0[dn], act="decay"))
        la = mm(x_a, _pad_cols(a1[dn]), out_dtype=BF16)
        aa.append(mm(la, _pad_rows(a2[dn]), bias=a0[dn], act="sigmoid"))

    def to_seq(x):
        lat = x[:n_lat].reshape(n_batch, seq, n_heads, n)
        ctx = x[n_lat:].reshape(n_batch, ctx_len, n_heads, n)
        return jnp.concatenate([ctx, lat], axis=1).transpose(1, 3, 0, 2).reshape(ctx_len + seq, n, n_batch * n_heads)

    r_t, k_t, v_t, g_t = to_seq(r), to_seq(k), to_seq(v), to_seq(gate)
    d0_t, d1_t, a0_t, a1_t = to_seq(dec[0]), to_seq(dec[1]), to_seq(aa[0]), to_seq(aa[1])

    def table(vec):
        return jnp.tile(vec.astype(F32).reshape(n_heads, n).T[:, None, :], (1, n_batch, 1)).reshape(n, n_batch * n_heads)

    kk_tab, ka_tab = table(k_k), table(k_a)
    o_f, o_r = rwkv_scan(r_t, k_t, v_t, d0_t, d1_t, a0_t, a1_t, jnp.concatenate([kk_tab, kk_tab], axis=-1),
                         jnp.concatenate([ka_tab, ka_tab], axis=-1), ctx_len)
    y_t = rwkv_post(o_f, o_r, r_t, k_t, v_t, a0_t, a1_t, g_t, ka_tab, table(r_k.reshape(-1)),
                    table(ln_g), table(ln_b))
    y4 = y_t.reshape(ctx_len + seq, n, n_batch, n_heads).transpose(2, 0, 3, 1)
    y_ctx = y4[:, :ctx_len].reshape(n_batch * ctx_len, d)
    y_lat = y4[:, ctx_len:].reshape(n_lat, d)
    return jnp.concatenate([y_lat, y_ctx], axis=0)


def kernel(x, c, ctx, c_ctx, mod_w, mod_b, norm_g, final_g, na_w_qkv, na_w_o, na_rpb, s5_lam_re, s5_lam_im, s5_log_dt, s5_b_re, s5_b_im, s5_c_re, s5_c_im, s5_d, s5_w_glu, rw_mu, rw_w_rkv, rw_w0, rw_w1, rw_w2, rw_a0, rw_a1, rw_a2, rw_k_k, rw_k_a, rw_r_k, rw_ln_g, rw_ln_b, rw_g1, rw_g2, rw_w_o, moe_wg, moe_bg, moe_we, moe_be, moe_w_gu, moe_w_down):
    n_batch, seq, d = x.shape
    ctx_len = ctx.shape[1]
    depth = mod_w.shape[0]
    n_lat = n_batch * seq
    n_ctx = n_batch * ctx_len
    tm = _row_tile(seq, n_ctx)
    tpb = seq // tm
    xs = jnp.concatenate([x.reshape(n_lat, d), ctx.reshape(n_ctx, d)], axis=0).astype(F32)

    cvecs = jnp.concatenate([c.astype(F32), c_ctx.astype(F32)[None]], axis=0)
    rows_pad = -(-(n_batch + 1) // 8) * 8
    cvecs = jnp.pad(cvecs, ((0, rows_pad - n_batch - 1), (0, 0)))
    mods = modulation(cvecs, mod_w, mod_b)[:, :n_batch + 1].reshape(depth, n_batch + 1, 6, 1, d)
    mods = mods.transpose(0, 2, 1, 3, 4)

    for i in range(depth):
        last = i == depth - 1
        sh1, sc1, g1, sh2, sc2, g2 = (mods[i, q] for q in range(6))
        pn = functools.partial(prenorm, tm=tm, tiles_per_batch=tpb, n_batch=n_batch)
        mm = functools.partial(matmul, tm=tm, tiles_per_batch=tpb, n_batch=n_batch)
        mix, j = i % N_MIXERS, i // N_MIXERS
        if mix == 0:
            (hb,) = pn(xs, norm_g[i, 0], 1.0 + sc1, sh1, out_dtypes=(BF16,))
            n_heads = na_rpb.shape[1]
            qscale = jnp.concatenate([jnp.full((d,), (d // n_heads) ** -0.5, F32), jnp.ones((2 * d,), F32)])
            qkv = mm(hb, na_w_qkv, w_index=j, out_dtype=BF16, colscale=qscale)
            kr = (na_rpb.shape[2] + 1) // 2
            o = na_attention(qkv, _na_bias_table(na_rpb[j], GRID_W, seq // GRID_W), kr, n_batch, seq, ctx_len,
                             not last)
            if last:
                xs = xs[:n_lat]
            xs = mm(o, na_w_o, w_index=j, mode="resid", resid=xs, gate=g1)
        elif mix == 1:
            (hf,) = pn(xs, norm_g[i, 0], 1.0 + sc1, sh1, out_dtypes=(F32,))
            wts = _s5_weights(s5_lam_re[j], s5_lam_im[j], s5_log_dt[j], s5_b_re[j], s5_b_im[j],
                              s5_c_re[j], s5_c_im[j], S5_CHUNK)
            y = s5_core(hf, wts, s5_d[j], n_batch, seq, ctx_len)
            if last:
                xs, y = xs[:n_lat], y[:n_lat]
            xs = mm(y, s5_w_glu, w_index=j, mode="glu", resid=xs, gate=g1)
        else:
            (hf,) = pn(xs, norm_g[i, 0], 1.0 + sc1, sh1, out_dtypes=(F32,))
            y = rwkv_mixer_tokens(hf, rw_mu[j], rw_w_rkv[j], rw_w0[j], rw_w1[j], rw_w2[j], rw_a0[j], rw_a1[j],
                                  rw_a2[j], rw_k_k[j], rw_k_a[j], rw_r_k[j], rw_ln_g[j], rw_ln_b[j],
                                  rw_g1[j], rw_g2[j], tm, n_batch, seq, ctx_len)
            if last:
                xs, y = xs[:n_lat], y[:n_lat]
            xs = mm(y, rw_w_o, w_index=j, mode="resid", resid=xs, gate=g1)
        (h2,) = pn(xs, norm_g[i, 1], 1.0 + sc2, sh2, out_dtypes=(F32,))
        xs = hier_moe(xs, h2, moe_wg[i], moe_bg[i], moe_we[i], moe_be[i], moe_w_gu, moe_w_down, i, g2,
                      tm, n_batch, seq)
    ones = jnp.ones((n_batch + 1, 1, d), F32)
    (out,) = prenorm(xs[:n_lat], final_g, ones, jnp.zeros_like(ones), tm, tpb, n_batch, out_dtypes=(x.dtype,))
    return out.reshape(n_batch, seq, d)
```

```python
import functools
import math

import jax
import jax.numpy as jnp
from jax import lax
from jax.experimental import pallas as pl
from jax.experimental.pallas import tpu as pltpu

F32 = jnp.float32
BF16 = jnp.bfloat16

GRID_W = 64
N_MIXERS = 3
NORM_EPS = 1e-6
RW_GN_EPS = 64e-5
MOE_TOP_K = 2
MOE_BLOCK = 128
S5_CHUNK = 16
SCAN_KEY_BLOCK = 16
RELAYOUT_ROWS = 128
NA_Q_ROWS = 4
NA_KEY_ROWS = 12
NEG_BIG = -1e30
VMEM_LIMIT_BYTES = 48 * 1024 * 1024
VMEM_LIMIT_BIG_BYTES = 56 * 1024 * 1024


def _cparams(sem, limit=VMEM_LIMIT_BYTES):
    return pltpu.CompilerParams(dimension_semantics=sem, vmem_limit_bytes=limit)


def _row_tile(n_lat_per_batch, n_ctx_total, cap=512):
    t = cap
    while t > 8 and (n_lat_per_batch % t or (n_ctx_total and n_ctx_total % t)):
        t //= 2
    return t


def _seg_fn(tiles_per_batch, n_batch):
    def seg(i):
        return jnp.minimum(i // tiles_per_batch, n_batch)
    return seg


def _bdot(a, b):
    return jnp.dot(a.astype(BF16), b.astype(BF16), preferred_element_type=F32)


def _split3(x):
    hi = x.astype(BF16)
    r1 = x - hi.astype(F32)
    mid = r1.astype(BF16)
    lo = (r1 - mid.astype(F32)).astype(BF16)
    return hi, mid, lo


def _prenorm_body(x_ref, g_ref, sc_ref, sh_ref, *o_refs):
    x = x_ref[...]
    ms = jnp.mean(x * x, axis=-1, keepdims=True)
    h = (x * lax.rsqrt(ms + NORM_EPS)) * g_ref[...]
    h = h * sc_ref[0] + sh_ref[0]
    for o_ref in o_refs:
        o_ref[...] = h.astype(o_ref.dtype)


def prenorm(x, g, scale1p, shift, tm, tiles_per_batch, n_batch, out_dtypes):
    m, d = x.shape
    seg = _seg_fn(tiles_per_batch, n_batch)
    row = pl.BlockSpec((tm, d), lambda i: (i, 0))
    tab = pl.BlockSpec((1, 1, d), lambda i: (seg(i), 0, 0))
    outs = pl.pallas_call(
        _prenorm_body,
        grid=(m // tm,),
        in_specs=[row, pl.BlockSpec((1, d), lambda i: (0, 0)), tab, tab],
        out_specs=[row for _ in out_dtypes],
        out_shape=[jax.ShapeDtypeStruct((m, d), dt) for dt in out_dtypes],
        compiler_params=_cparams(("parallel",)),
    )(x, g.reshape(1, d), scale1p, shift)
    return outs


def _softplus(z):
    return jnp.maximum(z, 0.0) + jnp.log(1.0 + jnp.exp(-jnp.abs(z)))


def _apply_act(y, act):
    if act is None:
        return y
    if act == "tanh":
        return jnp.tanh(y)
    if act == "sigmoid":
        return jax.nn.sigmoid(y)
    if act == "decay":
        return jnp.exp(-jnp.exp(-_softplus(-y) - 0.5))
    raise ValueError(act)


def _mm_body(*refs, mode, act, has_bias, has_scale):
    it = iter(refs)
    x_ref = next(it)
    w_ref = next(it)
    w2_ref = next(it) if mode == "glu" else None
    b_ref = next(it) if has_bias else None
    s_ref = next(it) if has_scale else None
    r_ref = next(it) if mode in ("resid", "glu") else None
    g_ref = next(it) if mode in ("resid", "glu") else None
    o_ref = next(it)
    wb_ref = next(it)
    wb2_ref = next(it) if mode == "glu" else None

    @pl.when(pl.program_id(1) == 0)
    def _():
        wb_ref[...] = w_ref[...].astype(BF16)
        if mode == "glu":
            wb2_ref[...] = w2_ref[...].astype(BF16)

    x = x_ref[...].astype(BF16)
    acc = jnp.dot(x, wb_ref[...], preferred_element_type=F32)
    if mode == "glu":
        acc2 = jnp.dot(x, wb2_ref[...], preferred_element_type=F32)
        acc = acc * jax.nn.sigmoid(acc2)
    if has_bias:
        acc = acc + b_ref[...]
    acc = _apply_act(acc, act)
    if has_scale:
        acc = acc * s_ref[...]
    if mode in ("resid", "glu"):
        acc = r_ref[...] + g_ref[0] * acc
    o_ref[...] = acc.astype(o_ref.dtype)


def matmul(x, w, *, tm, tn=512, out_dtype=F32, mode="plain", act=None, bias=None, colscale=None,
           resid=None, gate=None, tiles_per_batch=None, n_batch=None, w_index=0):
    m, k = x.shape
    if w.ndim == 2:
        w, w_index = w[None], 0
    n = w.shape[2] // 2 if mode == "glu" else w.shape[2]
    tn = min(tn, n)
    assert m % tm == 0 and n % tn == 0, (m, tm, n, tn)
    nj = n // tn
    in_specs = [pl.BlockSpec((tm, k), lambda j, i: (i, 0)),
                pl.BlockSpec((None, k, tn), lambda j, i: (w_index, 0, j))]
    args = [x, w]
    if mode == "glu":
        in_specs.append(pl.BlockSpec((None, k, tn), lambda j, i: (w_index, 0, j + nj)))
        args.append(w)
    col = pl.BlockSpec((1, tn), lambda j, i: (0, j))
    if bias is not None:
        in_specs.append(col)
        args.append(bias.reshape(1, n).astype(F32))
    if colscale is not None:
        in_specs.append(col)
        args.append(colscale.reshape(1, n).astype(F32))
    if mode in ("resid", "glu"):
        seg = _seg_fn(tiles_per_batch, n_batch)
        in_specs.append(pl.BlockSpec((tm, tn), lambda j, i: (i, j)))
        in_specs.append(pl.BlockSpec((1, 1, tn), lambda j, i: (seg(i), 0, j)))
        args += [resid, gate]
    return pl.pallas_call(
        functools.partial(_mm_body, mode=mode, act=act, has_bias=bias is not None,
                          has_scale=colscale is not None),
        grid=(nj, m // tm),
        in_specs=in_specs,
        out_specs=pl.BlockSpec((tm, tn), lambda j, i: (i, j)),
        out_shape=jax.ShapeDtypeStruct((m, n), out_dtype),
        scratch_shapes=[pltpu.VMEM((k, tn), BF16)] * (2 if mode == "glu" else 1),
        compiler_params=_cparams(("parallel", "arbitrary")),
    )(*args)


def _router_body(x_ref, w_ref, b_ref, o_ref, *, n_grp, epg):
    xh, xm, xl = _split3(x_ref[...])
    wh, wm, wl = _split3(w_ref[...])
    dot = functools.partial(jnp.dot, preferred_element_type=F32)
    acc = dot(xh, wh) + (dot(xh, wm) + dot(xm, wh)) + (dot(xh, wl) + dot(xm, wm) + dot(xl, wh))
    acc = acc + b_ref[...]
    lane = lax.broadcasted_iota(jnp.int32, acc.shape, 1)
    lane_f = lane.astype(F32)
    far = 1e9

    def first_max(vals):
        m = jnp.max(vals, axis=-1, keepdims=True)
        return m, jnp.min(jnp.where(vals == m, lane_f, far), axis=-1, keepdims=True)

    gl = jnp.where(lane < n_grp, acc, NEG_BIG)
    gmax, grp = first_max(gl)
    p_grp = 1.0 / jnp.sum(jnp.exp(gl - gmax), axis=-1, keepdims=True)
    lo = n_grp + grp * epg
    el = jnp.where((lane_f >= lo) & (lane_f < lo + epg), acc, NEG_BIG)
    m1, i1 = first_max(el)
    m2, i2 = first_max(jnp.where(lane_f == i1, NEG_BIG, el))
    e21 = jnp.exp(m2 - m1)
    g1 = p_grp / (1.0 + e21)
    g2 = p_grp * e21 / (1.0 + e21)
    out = jnp.where(lane == 0, i1 - n_grp, jnp.where(lane == 1, i2 - n_grp,
                                                    jnp.where(lane == 2, g1, jnp.where(lane == 3, g2, 0.0))))
    o_ref[...] = out


def router(x, w, bias, tm, n_grp, epg):
    m, k = x.shape
    n = w.shape[1]
    return pl.pallas_call(
        functools.partial(_router_body, n_grp=n_grp, epg=epg),
        grid=(m // tm,),
        in_specs=[pl.BlockSpec((tm, k), lambda i: (i, 0)), pl.BlockSpec((k, n), lambda i: (0, 0)),
                  pl.BlockSpec((1, n), lambda i: (0, 0))],
        out_specs=pl.BlockSpec((tm, n), lambda i: (i, 0)),
        out_shape=jax.ShapeDtypeStruct((m, n), F32),
        compiler_params=_cparams(("parallel",)),
        name="moe_router",
    )(x, w, bias.reshape(1, n))


def _mod_body(s_ref, w_ref, b_ref, o_ref):
    s = s_ref[...]
    s = s * jax.nn.sigmoid(s)
    o_ref[0] = _bdot(s, w_ref[0]) + b_ref[0]


def modulation(cvecs, mod_w, mod_b, tn=768):
    depth, d, n = mod_w.shape
    rows = cvecs.shape[0]
    while n % tn:
        tn //= 2
    return pl.pallas_call(
        _mod_body,
        grid=(depth, n // tn),
        in_specs=[pl.BlockSpec((rows, d), lambda l, j: (0, 0)),
                  pl.BlockSpec((1, d, tn), lambda l, j: (l, 0, j)),
                  pl.BlockSpec((1, 1, tn), lambda l, j: (l, 0, j))],
        out_specs=pl.BlockSpec((1, rows, tn), lambda l, j: (l, 0, j)),
        out_shape=jax.ShapeDtypeStruct((depth, rows, n), F32),
        compiler_params=_cparams(("parallel", "parallel")),
    )(cvecs, mod_w, mod_b.reshape(depth, 1, n))


def _na_key_start(i, rows, kr):
    return jnp.clip(i * NA_Q_ROWS - kr // 2, 0, rows - NA_KEY_ROWS)


def _na_bias_table(rpb, grid_w, rows):
    n_heads = rpb.shape[0]
    kr = (rpb.shape[1] + 1) // 2
    kcw = (rpb.shape[2] + 1) // 2
    n_blk = rows // NA_Q_ROWS
    q = jnp.arange(grid_w)[:, None]
    kc = jnp.arange(grid_w)[None, :]
    win = jnp.clip(q - kcw // 2, 0, grid_w - kcw)
    col_ok = (kc >= win) & (kc < win + kcw)
    dc = jnp.clip(kc - q, 1 - kcw, kcw - 1) + kcw - 1
    blk = jnp.array([0, min(1, n_blk - 1), n_blk - 1])[:, None, None]
    r = blk * NA_Q_ROWS + jnp.arange(NA_Q_ROWS)[None, :, None]
    krow = _na_key_start(blk, rows, kr) + jnp.arange(NA_KEY_ROWS)[None, None, :]
    rs = jnp.clip(r - kr // 2, 0, rows - kr)
    row_ok = (krow >= rs) & (krow < rs + kr)
    dr = jnp.clip(krow - r + kr - 1, 0, 2 * kr - 2)
    hi = lax.Precision.HIGHEST
    sel_r = jax.nn.one_hot(dr.reshape(-1), rpb.shape[1], dtype=F32)
    sel_c = jax.nn.one_hot(dc.reshape(-1), rpb.shape[2], dtype=F32)
    tab = jnp.einsum("ra,has->hrs", sel_r, jnp.einsum("hab,sb->has", rpb.astype(F32), sel_c, precision=hi),
                     precision=hi)
    tab = tab.reshape(n_heads, 3, NA_Q_ROWS, NA_KEY_ROWS, grid_w, grid_w)
    ok = row_ok[None, :, :, :, None, None] & col_ok[None, None, None, None]
    tab = jnp.where(ok, tab, NEG_BIG)
    return tab.transpose(0, 1, 2, 4, 3, 5).reshape(n_heads, 3, NA_Q_ROWS * grid_w, NA_KEY_ROWS * grid_w)


def _na_body(q_ref, k_ref, v_ref, kc_ref, vc_ref, bias_ref, o_ref, *, hpb, dh, grid_w, kr, rows):
    ks = _na_key_start(pl.program_id(2), rows, kr)
    start = pl.multiple_of(ks * grid_w, grid_w)
    nt = (((1,), (1,)), ((), ()))
    for h in range(hpb):
        sl = slice(h * dh, (h + 1) * dh)
        q = q_ref[:, sl]
        k = k_ref[pl.ds(start, NA_KEY_ROWS * grid_w), sl]
        v = v_ref[pl.ds(start, NA_KEY_ROWS * grid_w), sl]
        s = lax.dot_general(q, k, nt, preferred_element_type=F32) + bias_ref[h, 0]
        sc = lax.dot_general(q, kc_ref[:, sl], nt, preferred_element_type=F32)
        m = jnp.maximum(jnp.max(s, axis=-1, keepdims=True), jnp.max(sc, axis=-1, keepdims=True))
        p = jnp.exp(s - m)
        pc = jnp.exp(sc - m)
        den = jnp.sum(p, axis=-1, keepdims=True) + jnp.sum(pc, axis=-1, keepdims=True)
        o = (jnp.dot(p.astype(BF16), v, preferred_element_type=F32)
             + jnp.dot(pc.astype(BF16), vc_ref[:, sl], preferred_element_type=F32))
        o_ref[:, sl] = (o / den).astype(o_ref.dtype)


def _ctx_attn_body(q_ref, k_ref, v_ref, o_ref, *, hpb, dh):
    nt = (((1,), (1,)), ((), ()))
    for h in range(hpb):
        sl = slice(h * dh, (h + 1) * dh)
        s = lax.dot_general(q_ref[:, sl], k_ref[:, sl], nt, preferred_element_type=F32)
        m = jnp.max(s, axis=-1, keepdims=True)
        p = jnp.exp(s - m)
        den = jnp.sum(p, axis=-1, keepdims=True)
        o = jnp.dot(p.astype(BF16), v_ref[:, sl], preferred_element_type=F32)
        o_ref[:, sl] = (o / den).astype(o_ref.dtype)


def na_attention(qkv, bias_tab, kr, n_batch, seq, ctx_len, need_ctx):
    d = qkv.shape[1] // 3
    n_heads = bias_tab.shape[0]
    dh = d // n_heads
    hpb = max(1, min(n_heads, 256 // dh))
    bw = hpb * dh
    n_d = d // bw
    rows = seq // GRID_W
    n_blk = rows // NA_Q_ROWS
    qb = NA_Q_ROWS * GRID_W
    n_lat = n_batch * seq
    ctx_blk0 = n_lat // ctx_len

    def cls_of(i):
        return jnp.where(i == 0, 0, jnp.where(i == n_blk - 1, 2, 1))

    o_lat = pl.pallas_call(
        functools.partial(_na_body, hpb=hpb, dh=dh, grid_w=GRID_W, kr=kr, rows=rows),
        grid=(n_batch, n_d, n_blk),
        in_specs=[
            pl.BlockSpec((qb, bw), lambda b, g, i: (b * n_blk + i, g)),
            pl.BlockSpec((seq, bw), lambda b, g, i: (b, n_d + g)),
            pl.BlockSpec((seq, bw), lambda b, g, i: (b, 2 * n_d + g)),
            pl.BlockSpec((ctx_len, bw), lambda b, g, i: (ctx_blk0 + b, n_d + g)),
            pl.BlockSpec((ctx_len, bw), lambda b, g, i: (ctx_blk0 + b, 2 * n_d + g)),
            pl.BlockSpec((hpb, 1, qb, NA_KEY_ROWS * GRID_W), lambda b, g, i: (g, cls_of(i), 0, 0)),
        ],
        out_specs=pl.BlockSpec((qb, bw), lambda b, g, i: (b * n_blk + i, g)),
        out_shape=jax.ShapeDtypeStruct((n_lat, d), BF16),
        compiler_params=_cparams(("parallel", "parallel", "arbitrary")),
        name="na_attention",
    )(qkv, qkv, qkv, qkv, qkv, bias_tab)
    if not need_ctx:
        return o_lat
    o_ctx = pl.pallas_call(
        functools.partial(_ctx_attn_body, hpb=hpb, dh=dh),
        grid=(n_batch, n_d),
        in_specs=[
            pl.BlockSpec((ctx_len, bw), lambda b, g: (ctx_blk0 + b, g)),
            pl.BlockSpec((ctx_len, bw), lambda b, g: (ctx_blk0 + b, n_d + g)),
            pl.BlockSpec((ctx_len, bw), lambda b, g: (ctx_blk0 + b, 2 * n_d + g)),
        ],
        out_specs=pl.BlockSpec((ctx_len, bw), lambda b, g: (b, g)),
        out_shape=jax.ShapeDtypeStruct((n_batch * ctx_len, d), BF16),
        compiler_params=_cparams(("parallel", "parallel")),
    )(qkv, qkv, qkv)
    return jnp.concatenate([o_lat, o_ctx], axis=0)


def _expert_body(be_ref, idx_ref, idx_next_ref, x_hbm, wgu_ref, wd_ref, o_ref, xbuf, sem, wgu_s, wd_s, *,
                 hidden, n_blocks):
    j = pl.program_id(0)
    cur = j % 2

    def row_copy(idx, buf, r):
        return pltpu.make_async_copy(x_hbm.at[pl.ds(idx[0, 0, r], 1)], xbuf.at[buf, pl.ds(r, 1)], sem.at[buf])

    @pl.when(j == 0)
    def _():
        for r in range(MOE_BLOCK):
            row_copy(idx_ref, 0, r).start()

    prev = be_ref[jnp.maximum(j - 1, 0)]
    changed = jnp.logical_or(j == 0, be_ref[j] != prev)

    @pl.when(changed)
    def _():
        wgu_s[...] = wgu_ref[...].astype(BF16)
        wd_s[...] = wd_ref[...].astype(BF16)

    for r in range(MOE_BLOCK):
        row_copy(idx_ref, cur, r).wait()

    for r in range(MOE_BLOCK):
        row_copy(idx_next_ref, 1 - cur, r).start()
    gu = jnp.dot(xbuf[cur].astype(BF16), wgu_s[...], preferred_element_type=F32)
    g = gu[:, :hidden]
    u = gu[:, hidden:]
    a = (g * jax.nn.sigmoid(g) * u).astype(BF16)
    o_ref[...] = jnp.dot(a, wd_s[...], preferred_element_type=F32)

    @pl.when(j == n_blocks - 1)
    def _():
        for r in range(MOE_BLOCK):
            row_copy(idx_next_ref, 1 - cur, r).wait()


def expert_blocks(x, slot_tok, blk_e, w_gu, w_down, layer):
    d = x.shape[1]
    hidden = w_down.shape[2]
    n_blocks = blk_e.shape[0]
    idx = slot_tok.reshape(n_blocks, 1, MOE_BLOCK)
    return pl.pallas_call(
        functools.partial(_expert_body, hidden=hidden, n_blocks=n_blocks),
        grid_spec=pltpu.PrefetchScalarGridSpec(
            num_scalar_prefetch=1,
            grid=(n_blocks,),
            in_specs=[pl.BlockSpec((1, 1, MOE_BLOCK), lambda j, be: (j, 0, 0), memory_space=pltpu.SMEM),
                      pl.BlockSpec((1, 1, MOE_BLOCK), lambda j, be: (jnp.minimum(j + 1, n_blocks - 1), 0, 0),
                                   memory_space=pltpu.SMEM),
                      pl.BlockSpec(memory_space=pl.ANY),
                      pl.BlockSpec((None, None, d, 2 * hidden), lambda j, be: (layer, be[j], 0, 0)),
                      pl.BlockSpec((None, None, hidden, d), lambda j, be: (layer, be[j], 0, 0))],
            out_specs=pl.BlockSpec((MOE_BLOCK, d), lambda j, be: (j, 0)),
            scratch_shapes=[pltpu.VMEM((2, MOE_BLOCK, d), F32), pltpu.SemaphoreType.DMA((2,)),
                            pltpu.VMEM((d, 2 * hidden), BF16), pltpu.VMEM((hidden, d), BF16)]),
        out_shape=jax.ShapeDtypeStruct((n_blocks * MOE_BLOCK, d), F32),
        compiler_params=_cparams(("arbitrary",), VMEM_LIMIT_BIG_BYTES),
        name="moe_experts",
    )(blk_e, idx, idx, x, w_gu, w_down)


def _combine_body(idx_ref, idx_next_ref, route_ref, x_ref, g_ref, yb_hbm, o_ref, ybuf, sem, *, tmc, n_tiles):
    i = pl.program_id(0)
    cur = i % 2

    def row_copy(idx, buf, k, r):
        return pltpu.make_async_copy(yb_hbm.at[pl.ds(idx[0, 0, k * tmc + r], 1)],
                                     ybuf.at[buf, k, pl.ds(r, 1)], sem.at[buf])

    def start_all(idx, buf):
        for k in range(MOE_TOP_K):
            for r in range(tmc):
                row_copy(idx, buf, k, r).start()

    @pl.when(i == 0)
    def _():
        start_all(idx_ref, 0)

    @pl.when(i + 1 < n_tiles)
    def _():
        start_all(idx_next_ref, 1 - cur)

    for k in range(MOE_TOP_K):
        for r in range(tmc):
            row_copy(idx_ref, cur, k, r).wait()
    route = route_ref[...]
    f = ybuf[cur, 0] * route[:, MOE_TOP_K:MOE_TOP_K + 1]
    for k in range(1, MOE_TOP_K):
        f = f + ybuf[cur, k] * route[:, MOE_TOP_K + k:MOE_TOP_K + k + 1]
    o_ref[...] = x_ref[...] + g_ref[0] * f


def moe_combine(xs, yb, slot_of_asg, route, gate_tab, n_batch, seq):
    n_tok, d = xs.shape
    tmc = MOE_BLOCK
    n_tiles = n_tok // tmc
    idx = slot_of_asg.reshape(n_tiles, tmc, MOE_TOP_K).transpose(0, 2, 1).reshape(n_tiles, 1, MOE_TOP_K * tmc)
    seg = _seg_fn(seq // tmc, n_batch)
    return pl.pallas_call(
        functools.partial(_combine_body, tmc=tmc, n_tiles=n_tiles),
        grid=(n_tiles,),
        in_specs=[pl.BlockSpec((1, 1, MOE_TOP_K * tmc), lambda i: (i, 0, 0), memory_space=pltpu.SMEM),
                  pl.BlockSpec((1, 1, MOE_TOP_K * tmc), lambda i: (jnp.minimum(i + 1, n_tiles - 1), 0, 0),
                               memory_space=pltpu.SMEM),
                  pl.BlockSpec((tmc, route.shape[1]), lambda i: (i, 0)),
                  pl.BlockSpec((tmc, d), lambda i: (i, 0)),
                  pl.BlockSpec((1, 1, d), lambda i: (seg(i), 0, 0)),
                  pl.BlockSpec(memory_space=pl.ANY)],
        out_specs=pl.BlockSpec((tmc, d), lambda i: (i, 0)),
        out_shape=jax.ShapeDtypeStruct((n_tok, d), F32),
        scratch_shapes=[pltpu.VMEM((2, MOE_TOP_K, tmc, d), F32), pltpu.SemaphoreType.DMA((2,))],
        compiler_params=_cparams(("arbitrary",)),
        name="moe_combine",
    )(idx, idx, route, xs, gate_tab, yb)


def hier_moe(xs, h, wg, bg, we, be, w_gu, w_down, layer, gate_tab, tm, n_batch, seq):
    n_tok, d = h.shape
    n_grp = wg.shape[1]
    n_exp = we.shape[1]
    epg = n_exp // n_grp
    n_logit = n_grp + n_exp
    n_pad = -(-n_logit // 128) * 128
    wcat = jnp.pad(jnp.concatenate([wg, we], axis=1), ((0, 0), (0, n_pad - n_logit)))
    bcat = jnp.pad(jnp.concatenate([bg, be], axis=0), (0, n_pad - n_logit))
    route = router(h, wcat, bcat, tm, n_grp, epg)

    flat_e = route[:, :MOE_TOP_K].astype(jnp.int32).reshape(-1)
    n_asg = flat_e.shape[0]
    onehot = (flat_e[:, None] == jnp.arange(n_exp, dtype=jnp.int32)[None, :]).astype(F32)
    ck = 128
    oh3 = onehot.reshape(n_asg // ck, ck, n_exp)
    tri = jnp.tril(jnp.ones((ck, ck), F32))
    within = jnp.einsum("ij,cjk->cik", tri, oh3, precision=lax.Precision.HIGHEST)
    tot = within[:, -1, :]
    before = jnp.cumsum(tot, axis=0) - tot
    rank = jnp.sum(oh3 * (within + before[:, None, :]), axis=-1).reshape(n_asg).astype(jnp.int32) - 1
    counts = jnp.sum(tot, axis=0).astype(jnp.int32)
    padded = (counts + MOE_BLOCK - 1) // MOE_BLOCK * MOE_BLOCK
    pad_end = jnp.cumsum(padded)
    pad_start = pad_end - padded
    slot = jnp.sum(onehot * pad_start.astype(F32)[None, :], axis=-1).astype(jnp.int32) + rank
    n_blocks = -(-n_asg // MOE_BLOCK) + n_exp
    slot_tok = jnp.zeros((n_blocks * MOE_BLOCK,), jnp.int32).at[slot].set(
        jnp.arange(n_asg, dtype=jnp.int32) // MOE_TOP_K)
    blk_start = jnp.arange(n_blocks, dtype=jnp.int32) * MOE_BLOCK
    blk_e = jnp.minimum(jnp.sum((pad_end[None, :] <= blk_start[:, None]).astype(jnp.int32), axis=1), n_exp - 1)
    yb = expert_blocks(h, slot_tok, blk_e, w_gu, w_down, layer)
    return moe_combine(xs, yb, slot.reshape(n_tok, MOE_TOP_K), route, gate_tab, n_batch, seq)


def _s5_weights(lam_re, lam_im, log_dt, b_re, b_im, c_re, c_im, t_chunk):
    n_grp, n_state = lam_re.shape[1], lam_re.shape[2]
    cg = b_re.shape[-1]
    lam = lax.complex(lam_re.astype(F32), lam_im.astype(F32))
    dt = jnp.exp(log_dt.astype(F32))[..., None]
    lam_bar = jnp.exp(lam * dt)
    b_bar = ((lam_bar - 1.0) / lam)[..., None] * lax.complex(b_re.astype(F32), b_im.astype(F32))
    c_mat = lax.complex(c_re.astype(F32), c_im.astype(F32))
    ks = jnp.arange(t_chunk + 1, dtype=F32)
    pw = jnp.exp((lam * dt)[..., None] * ks)
    hi = lax.Precision.HIGHEST
    kern = jnp.einsum("dgcp,dgpk,dgpi->dgkci", c_mat, pw[..., :t_chunk], b_bar, precision=hi).real
    t_idx = jnp.arange(t_chunk)
    lag = t_idx[None, :] - t_idx[:, None]
    k_f = kern[0][:, jnp.clip(lag, 0, t_chunk - 1)]
    k_r = kern[1][:, jnp.clip(-lag, 0, t_chunk - 1)]
    toep = (jnp.where((lag >= 0)[None, :, :, None, None], k_f, 0.0)
            + jnp.where((lag <= 0)[None, :, :, None, None], k_r, 0.0))
    toep = toep.transpose(0, 1, 4, 2, 3).reshape(n_grp, t_chunk * cg, t_chunk * cg)
    pf = pw[0][..., :t_chunk][..., ::-1]
    pr = pw[1][..., :t_chunk]
    bend_f = jnp.einsum("gps,gpc->gscp", pf, b_bar[0]).reshape(n_grp, t_chunk * cg, n_state)
    bend_r = jnp.einsum("gps,gpc->gscp", pr, b_bar[1]).reshape(n_grp, t_chunk * cg, n_state)
    bend = jnp.stack([bend_f.real, bend_f.imag, bend_r.real, bend_r.imag], axis=1)
    of = jnp.einsum("gcp,gpt->gptc", c_mat[0], pw[0][..., 1:]).reshape(n_grp, n_state, t_chunk * cg)
    orv = jnp.einsum("gcp,gpt->gptc", c_mat[1], pw[1][..., 1:][..., ::-1]).reshape(n_grp, n_state, t_chunk * cg)
    cout = jnp.stack([of.real, -of.imag, orv.real, -orv.imag], axis=1)

    def pair_diag(a, axis_r, axis_c):
        g2 = a.reshape(n_grp // 2, 2, *a.shape[1:])
        z = jnp.zeros_like(g2[:, 0])
        top = jnp.concatenate([g2[:, 0], z], axis=axis_c)
        bot = jnp.concatenate([z, g2[:, 1]], axis=axis_c)
        return jnp.concatenate([top, bot], axis=axis_r)

    bend2 = pair_diag(bend, 2, 3)
    cout2 = pair_diag(cout, 2, 3)
    lam_t = pw[..., t_chunk]
    lam_t4 = jnp.stack([lam_t[0].real, lam_t[0].imag, lam_t[1].real, lam_t[1].imag]).reshape(4, n_grp * n_state)
    return toep, bend2, cout2, lam_t4


def _s5_a_body(u_ref, toep_ref, bend_ref, y_ref, e0, e1, e2, e3, *, w1):
    u = u_ref[...].astype(BF16)
    for g in range(2):
        y_ref[:, g * w1:(g + 1) * w1] = jnp.dot(u[:, g * w1:(g + 1) * w1], toep_ref[g].astype(BF16),
                                                 preferred_element_type=F32)
    for c, e_ref in enumerate((e0, e1, e2, e3)):
        e_ref[...] = jnp.dot(u, bend_ref[0, c].astype(BF16), preferred_element_type=F32)


def _s5_b_body(e0, e1, e2, e3, lam_ref, xf_re, xf_im, xr_re, xr_im, *, n_batch, n_chunk, n_ctx_chunk):
    lam = lam_ref[...]
    lfr, lfi, lrr, lri = lam[0:1], lam[1:2], lam[2:3], lam[3:4]
    width = e0.shape[1]
    for b in range(n_batch):
        base = b * n_chunk

        def fwd(n, carry):
            sr, si = carry
            row = base + n
            xf_re[pl.ds(row, 1), :] = sr
            xf_im[pl.ds(row, 1), :] = si
            er = e0[pl.ds(row, 1), :]
            ei = e1[pl.ds(row, 1), :]
            return lfr * sr - lfi * si + er, lfr * si + lfi * sr + ei

        def rev(n, carry):
            sr, si = carry
            row = base + jnp.where(n < n_ctx_chunk, n_ctx_chunk - 1 - n, n_chunk + n_ctx_chunk - 1 - n)
            xr_re[pl.ds(row, 1), :] = sr
            xr_im[pl.ds(row, 1), :] = si
            er = e2[pl.ds(row, 1), :]
            ei = e3[pl.ds(row, 1), :]
            return lrr * sr - lri * si + er, lrr * si + lri * sr + ei

        z = jnp.zeros((1, width), F32)
        lax.fori_loop(0, n_chunk, fwd, (z, z))
        lax.fori_loop(0, n_chunk, rev, (z, z))


def _gelu(y):
    return 0.5 * y * (1.0 + jnp.tanh(0.7978845608028654 * (y + 0.044715 * (y * y * y))))


def _s5_c_body(y_ref, u_ref, x0, x1, x2, x3, cout_ref, dsk_ref, o_ref):
    acc = y_ref[...] + u_ref[...] * dsk_ref[...]
    for c, x_ref in enumerate((x0, x1, x2, x3)):
        acc = acc + jnp.dot(x_ref[...].astype(BF16), cout_ref[0, c].astype(BF16), preferred_element_type=F32)
    o_ref[...] = _gelu(acc).astype(o_ref.dtype)


def s5_core(h_f32, weights, d_skip, n_batch, seq, ctx_len):
    toep, bend2, cout2, lam_t4 = weights
    t = S5_CHUNK
    m, d = h_f32.shape
    n_grp = toep.shape[0]
    cg = d // n_grp
    w1 = t * cg
    n_state2 = bend2.shape[-1]
    n_lat = n_batch * seq
    ncc, ncl = ctx_len // t, seq // t
    n_chunk = ncc + ncl
    rows = n_batch * n_chunk
    hb = h_f32
    lat = hb[:n_lat].reshape(n_batch, ncl, t, n_grp, cg)
    ctx = hb[n_lat:].reshape(n_batch, ncc, t, n_grp, cg)
    u2 = jnp.concatenate([ctx, lat], axis=1).transpose(0, 1, 3, 2, 4).reshape(rows, n_grp * w1)
    n_pair = n_grp // 2
    ublk = pl.BlockSpec((rows, 2 * w1), lambda p: (0, p))
    eblk = pl.BlockSpec((rows, n_state2), lambda p: (0, p))
    e_shape = jax.ShapeDtypeStruct((rows, n_pair * n_state2), F32)
    y_intra, e0, e1, e2, e3 = pl.pallas_call(
        functools.partial(_s5_a_body, w1=w1),
        grid=(n_pair,),
        in_specs=[ublk,
                  pl.BlockSpec((2, w1, w1), lambda p: (p, 0, 0)),
                  pl.BlockSpec((1, 4, 2 * w1, n_state2), lambda p: (p, 0, 0, 0))],
        out_specs=[ublk, eblk, eblk, eblk, eblk],
        out_shape=[jax.ShapeDtypeStruct((rows, n_grp * w1), F32), e_shape, e_shape, e_shape, e_shape],
        compiler_params=_cparams(("parallel",)),
    )(u2, toep, bend2)

    lanes = n_pair * n_state2
    lb = min(lanes, 512)
    sblk = pl.BlockSpec((rows, lb), lambda p: (0, p))
    xs = pl.pallas_call(
        functools.partial(_s5_b_body, n_batch=n_batch, n_chunk=n_chunk, n_ctx_chunk=ncc),
        grid=(lanes // lb,),
        in_specs=[sblk, sblk, sblk, sblk, pl.BlockSpec((4, lb), lambda p: (0, p))],
        out_specs=[sblk, sblk, sblk, sblk],
        out_shape=[e_shape, e_shape, e_shape, e_shape],
        compiler_params=_cparams(("parallel",)),
    )(e0, e1, e2, e3, lam_t4)

    dsk = jnp.tile(d_skip.astype(F32).reshape(n_grp, 1, cg), (1, t, 1)).reshape(1, n_grp * w1)
    y2 = pl.pallas_call(
        _s5_c_body,
        grid=(n_pair,),
        in_specs=[ublk, ublk, eblk, eblk, eblk, eblk,
                  pl.BlockSpec((1, 4, n_state2, 2 * w1), lambda p: (p, 0, 0, 0)),
                  pl.BlockSpec((1, 2 * w1), lambda p: (0, p))],
        out_specs=ublk,
        out_shape=jax.ShapeDtypeStruct((rows, n_grp * w1), F32),
        compiler_params=_cparams(("parallel",)),
    )(y_intra, u2, *xs, cout2, dsk)
    y5 = y2.reshape(n_batch, n_chunk, n_grp, t, cg).transpose(0, 1, 3, 2, 4)
    y_ctx = y5[:, :ncc].reshape(n_batch * ctx_len, d)
    y_lat = y5[:, ncc:].reshape(n_lat, d)
    return jnp.concatenate([y_lat, y_ctx], axis=0)


def _shift_body(h_ref, hp_ref, hn_ref, mu_ref, *o_refs, tm, tiles_lat, tiles_ctx, n_lat_tiles):
    i = pl.program_id(0)
    h = h_ref[...]
    first = jnp.where(i < n_lat_tiles, i % tiles_lat == 0, (i - n_lat_tiles) % tiles_ctx == 0)
    last = jnp.where(i < n_lat_tiles, i % tiles_lat == tiles_lat - 1,
                     (i - n_lat_tiles) % tiles_ctx == tiles_ctx - 1)
    rid = lax.broadcasted_iota(jnp.int32, h.shape, 0)
    prev_edge = jnp.where(first, 0.0, hp_ref[7:8, :])
    next_edge = jnp.where(last, 0.0, hn_ref[0:1, :])
    prev = jnp.where(rid == 0, prev_edge, pltpu.roll(h, 1, 0))
    nxt = jnp.where(rid == tm - 1, next_edge, pltpu.roll(h, tm - 1, 0))
    xx = 0.5 * (prev + nxt) - h
    for m, o_ref in enumerate(o_refs):
        o_ref[...] = (h + xx * mu_ref[m:m + 1, :]).astype(o_ref.dtype)


def token_shift_mix(h, mu, tm, seq, ctx_len, n_batch):
    m, d = h.shape
    n_mix = mu.shape[0]
    n_lat_tiles = n_batch * seq // tm
    nb8 = m // 8
    per = tm // 8
    row = pl.BlockSpec((tm, d), lambda i: (i, 0))
    return pl.pallas_call(
        functools.partial(_shift_body, tm=tm, tiles_lat=seq // tm, tiles_ctx=max(ctx_len // tm, 1),
                          n_lat_tiles=n_lat_tiles),
        grid=(m // tm,),
        in_specs=[row,
                  pl.BlockSpec((8, d), lambda i: (jnp.maximum(i * per - 1, 0), 0)),
                  pl.BlockSpec((8, d), lambda i: (jnp.minimum((i + 1) * per, nb8 - 1), 0)),
                  pl.BlockSpec((n_mix, d), lambda i: (0, 0))],
        out_specs=[row] * n_mix,
        out_shape=[jax.ShapeDtypeStruct((m, d), BF16)] * n_mix,
        compiler_params=_cparams(("parallel",)),
    )(h, h, h, mu.astype(F32))


def _rwkv_scan_body(rf_ref, rr_ref, kf_ref, kr_ref, vf_ref, vr_ref, wf_ref, wr_ref, af_ref, ar_ref,
                    kk_ref, ka_ref, of_ref, or_ref, s_ref, op_ref, *, tb, n, bh):
    @pl.when(pl.program_id(0) == 0)
    def _():
        s_ref[...] = jnp.zeros_like(s_ref)

    k_k = kk_ref[...]
    k_a = ka_ref[...]

    def step(t, carry):
        tr = tb - 1 - t
        r = jnp.concatenate([rf_ref[t], rr_ref[tr]], axis=-1)
        k = jnp.concatenate([kf_ref[t], kr_ref[tr]], axis=-1)
        v = jnp.concatenate([vf_ref[t], vr_ref[tr]], axis=-1)
        w = jnp.concatenate([wf_ref[t], wr_ref[tr]], axis=-1)
        a = jnp.concatenate([af_ref[t], ar_ref[tr]], axis=-1)
        kk = k * k_k
        nrm = jnp.sqrt(jnp.sum(kk * kk, axis=0, keepdims=True))
        kk = kk / jnp.maximum(nrm, 1e-12)
        nkk = -kk
        akk = kk * a
        kd = k * (1.0 + (a - 1.0) * k_a)
        op_ref[0] = nkk
        op_ref[1] = w * r
        op_ref[2] = w
        op_ref[3] = akk
        op_ref[4] = kd
        c_a = jnp.sum(akk * r, axis=0, keepdims=True)
        c_k = jnp.sum(kd * r, axis=0, keepdims=True)
        def reduce_keys(jb, acc):
            sa, so = acc
            for jj in range(SCAN_KEY_BLOCK):
                j = jb * SCAN_KEY_BLOCK + jj
                s_j = s_ref[j]
                sa = sa + s_j * op_ref[0, pl.ds(j, 1), :]
                so = so + s_j * op_ref[1, pl.ds(j, 1), :]
            return sa, so

        zero = jnp.zeros_like(v)
        sa, so = lax.fori_loop(0, n // SCAN_KEY_BLOCK, reduce_keys, (zero, zero))
        o = so + sa * c_a + v * c_k
        of_ref[t] = o[:, :bh]
        or_ref[tr] = o[:, bh:]

        def update_keys(jb, c):
            for jj in range(SCAN_KEY_BLOCK):
                j = jb * SCAN_KEY_BLOCK + jj
                s_ref[j] = (s_ref[j] * op_ref[2, pl.ds(j, 1), :] + sa * op_ref[3, pl.ds(j, 1), :]
                            + v * op_ref[4, pl.ds(j, 1), :])
            return c

        lax.fori_loop(0, n // SCAN_KEY_BLOCK, update_keys, 0)
        return carry

    lax.fori_loop(0, tb, step, 0)


def rwkv_scan(r, k, v, w0, w1, a0, a1, k_k, k_a, ctx_len, tb=32):
    steps, n, bh = r.shape
    while steps % tb or ctx_len % tb:
        tb //= 2
    nblk, nblk_ctx = steps // tb, ctx_len // tb

    def rev_blk(i):
        return jnp.where(i < nblk_ctx, nblk_ctx - 1 - i, nblk + nblk_ctx - 1 - i)

    fwd = pl.BlockSpec((tb, n, bh), lambda i: (i, 0, 0))
    rev = pl.BlockSpec((tb, n, bh), lambda i: (rev_blk(i), 0, 0))
    tab = pl.BlockSpec((n, 2 * bh), lambda i: (0, 0))
    out = jax.ShapeDtypeStruct((steps, n, bh), F32)
    return pl.pallas_call(
        functools.partial(_rwkv_scan_body, tb=tb, n=n, bh=bh),
        grid=(nblk,),
        in_specs=[fwd, rev, fwd, rev, fwd, rev, fwd, rev, fwd, rev, tab, tab],
        out_specs=[fwd, rev],
        out_shape=[out, out],
        scratch_shapes=[pltpu.VMEM((n, n, 2 * bh), F32), pltpu.VMEM((5, n, 2 * bh), F32)],
        compiler_params=_cparams(("arbitrary",)),
        name="rwkv_scan",
    )(r, r, k, k, v, v, w0, w1, a0, a1, k_k, k_a)


def _rwkv_post_body(of_ref, or_ref, r_ref, k_ref, v_ref, a0_ref, a1_ref, g_ref, ka_ref, rk_ref, lg_ref, lb_ref,
                    y_ref):
    o = of_ref[...] + or_ref[...]
    mean = jnp.mean(o, axis=1, keepdims=True)
    var = jnp.mean(jnp.square(o - mean), axis=1, keepdims=True)
    on = (o - mean) * lax.rsqrt(var + RW_GN_EPS) * lg_ref[...] + lb_ref[...]
    r = r_ref[...]
    k = k_ref[...]
    k_a = ka_ref[...]
    rk = rk_ref[...]
    kd0 = k * (1.0 + (a0_ref[...] - 1.0) * k_a)
    kd1 = k * (1.0 + (a1_ref[...] - 1.0) * k_a)
    bonus = (jnp.sum(r * kd0 * rk, axis=1, keepdims=True)
             + jnp.sum(r * kd1 * rk, axis=1, keepdims=True)) * v_ref[...]
    y = (on + bonus)
    y_ref[...] = (y * g_ref[...]).astype(y_ref.dtype)


def rwkv_post(o_f, o_r, r, k, v, a0, a1, gate, k_a, r_k, ln_g, ln_b, tb=32):
    steps, n, ch = o_f.shape
    while steps % tb:
        tb //= 2
    blk = pl.BlockSpec((tb, n, ch), lambda i: (i, 0, 0))
    tab = pl.BlockSpec((1, n, ch), lambda i: (0, 0, 0))
    return pl.pallas_call(
        _rwkv_post_body,
        grid=(steps // tb,),
        in_specs=[blk] * 8 + [tab] * 4,
        out_specs=blk,
        out_shape=jax.ShapeDtypeStruct((steps, n, ch), F32),
        compiler_params=_cparams(("parallel",)),
        name="rwkv_post",
    )(o_f, o_r, r, k, v, a0, a1, gate, k_a[None], r_k[None], ln_g[None], ln_b[None])


def _step_block_row(s, b, n_ctx_blk, n_lat_blk, n_batch):
    return jnp.where(s < n_ctx_blk, n_batch * n_lat_blk + b * n_ctx_blk + s, b * n_lat_blk + s - n_ctx_blk)


def _to_steps_body(*refs, n_heads):
    x_refs, o_ref = refs[:-1], refs[-1]
    parts = []
    for x_ref in x_refs:
        x = x_ref[...]
        t = x.shape[0]
        parts.append(jnp.swapaxes(x.reshape(t, n_heads, x.shape[1] // n_heads), 1, 2))
    o_ref[...] = jnp.concatenate(parts, axis=-1)


def tokens_to_steps(x, n_batch, seq, ctx_len, n_heads):
    m, d = x.shape
    n = d // n_heads
    t = RELAYOUT_ROWS
    n_ctx_blk, n_lat_blk = ctx_len // t, seq // t

    def spec(b):
        return pl.BlockSpec((t, d), lambda s: (_step_block_row(s, b, n_ctx_blk, n_lat_blk, n_batch), 0))

    return pl.pallas_call(
        functools.partial(_to_steps_body, n_heads=n_heads),
        grid=(n_ctx_blk + n_lat_blk,),
        in_specs=[spec(b) for b in range(n_batch)],
        out_specs=pl.BlockSpec((t, n, n_batch * n_heads), lambda s: (s, 0, 0)),
        out_shape=jax.ShapeDtypeStruct((ctx_len + seq, n, n_batch * n_heads), x.dtype),
        compiler_params=_cparams(("parallel",)),
        name="tokens_to_steps",
    )(*([x] * n_batch))


def _to_tokens_body(y_ref, o_ref, *, n_heads, n_batch):
    y = y_ref[...]
    t, n = y.shape[0], y.shape[1]
    for b in range(n_batch):
        @pl.when(pl.program_id(1) == b)
        def _():
            yb = y[:, :, b * n_heads:(b + 1) * n_heads]
            o_ref[...] = jnp.swapaxes(yb, 1, 2).reshape(t, n_heads * n)


def steps_to_tokens(y, n_batch, seq, ctx_len, n_heads):
    steps, n, _ = y.shape
    d = n_heads * n
    t = RELAYOUT_ROWS
    n_ctx_blk, n_lat_blk = ctx_len // t, seq // t
    return pl.pallas_call(
        functools.partial(_to_tokens_body, n_heads=n_heads, n_batch=n_batch),
        grid=(n_ctx_blk + n_lat_blk, n_batch),
        in_specs=[pl.BlockSpec((t, n, n_batch * n_heads), lambda s, b: (s, 0, 0))],
        out_specs=pl.BlockSpec((t, d), lambda s, b: (_step_block_row(s, b, n_ctx_blk, n_lat_blk, n_batch), 0)),
        out_shape=jax.ShapeDtypeStruct((n_batch * (seq + ctx_len), d), y.dtype),
        compiler_params=_cparams(("parallel", "arbitrary")),
        name="steps_to_tokens",
    )(y)


def _pad_cols(w, mult=128):
    n = w.shape[-1]
    p = -(-n // mult) * mult - n
    return jnp.pad(w, [(0, 0)] * (w.ndim - 1) + [(0, p)]) if p else w


def _pad_rows(w, mult=128):
    n = w.shape[-2]
    p = -(-n // mult) * mult - n
    return jnp.pad(w, [(0, 0)] * (w.ndim - 2) + [(0, p), (0, 0)]) if p else w


def rwkv_mixer_tokens(h, mu, w_rkv, w0, w1, w2, a0, a1, a2, k_k, k_a, r_k, ln_g, ln_b, g1, g2,
                      tm, n_batch, seq, ctx_len):
    m, d = h.shape
    n_heads, n = r_k.shape
    n_lat = n_batch * seq
    ts = _row_tile(seq, ctx_len, cap=256)
    x_r, x_k, x_v, x_w, x_a, x_g = token_shift_mix(h, mu, ts, seq, ctx_len, n_batch)
    mm = functools.partial(matmul, tm=tm)
    r = mm(x_r, w_rkv, w_index=0)
    k = mm(x_k, w_rkv, w_index=1)
    v = mm(x_v, w_rkv, w_index=2)
    gate = mm(mm(x_g, _pad_cols(g1), act="sigmoid", out_dtype=BF16), _pad_rows(g2))
    dec, aa = [], []
    for dn in range(2):
        lw = mm(x_w, _pad_cols(w1[dn]), act="tanh", out_dtype=BF16)
        dec.append(mm(lw, _pad_rows(w2[dn]), bias=w0[dn], act="decay"))
        la = mm(x_a, _pad_cols(a1[dn]), out_dtype=BF16)
        aa.append(mm(la, _pad_rows(a2[dn]), bias=a0[dn], act="sigmoid"))

    to_seq = functools.partial(tokens_to_steps, n_batch=n_batch, seq=seq, ctx_len=ctx_len, n_heads=n_heads)
    r_t, k_t, v_t, g_t = to_seq(r), to_seq(k), to_seq(v), to_seq(gate)
    d0_t, d1_t, a0_t, a1_t = to_seq(dec[0]), to_seq(dec[1]), to_seq(aa[0]), to_seq(aa[1])

    def table(vec):
        return jnp.tile(vec.astype(F32).reshape(n_heads, n).T[:, None, :], (1, n_batch, 1)).reshape(n, n_batch * n_heads)

    kk_tab, ka_tab = table(k_k), table(k_a)
    o_f, o_r = rwkv_scan(r_t, k_t, v_t, d0_t, d1_t, a0_t, a1_t, jnp.concatenate([kk_tab, kk_tab], axis=-1),
                         jnp.concatenate([ka_tab, ka_tab], axis=-1), ctx_len)
    y_t = rwkv_post(o_f, o_r, r_t, k_t, v_t, a0_t, a1_t, g_t, ka_tab, table(r_k.reshape(-1)),
                    table(ln_g), table(ln_b))
    return steps_to_tokens(y_t, n_batch, seq, ctx_len, n_heads)


def kernel(x, c, ctx, c_ctx, mod_w, mod_b, norm_g, final_g, na_w_qkv, na_w_o, na_rpb, s5_lam_re, s5_lam_im, s5_log_dt, s5_b_re, s5_b_im, s5_c_re, s5_c_im, s5_d, s5_w_glu, rw_mu, rw_w_rkv, rw_w0, rw_w1, rw_w2, rw_a0, rw_a1, rw_a2, rw_k_k, rw_k_a, rw_r_k, rw_ln_g, rw_ln_b, rw_g1, rw_g2, rw_w_o, moe_wg, moe_bg, moe_we, moe_be, moe_w_gu, moe_w_down):
    n_batch, seq, d = x.shape
    ctx_len = ctx.shape[1]
    depth = mod_w.shape[0]
    n_lat = n_batch * seq
    n_ctx = n_batch * ctx_len
    tm = _row_tile(seq, n_ctx)
    tpb = seq // tm
    xs = jnp.concatenate([x.reshape(n_lat, d), ctx.reshape(n_ctx, d)], axis=0).astype(F32)

    cvecs = jnp.concatenate([c.astype(F32), c_ctx.astype(F32)[None]], axis=0)
    rows_pad = -(-(n_batch + 1) // 8) * 8
    cvecs = jnp.pad(cvecs, ((0, rows_pad - n_batch - 1), (0, 0)))
    mods = modulation(cvecs, mod_w, mod_b)[:, :n_batch + 1].reshape(depth, n_batch + 1, 6, 1, d)
    mods = mods.transpose(0, 2, 1, 3, 4)

    for i in range(depth):
        last = i == depth - 1
        sh1, sc1, g1, sh2, sc2, g2 = (mods[i, q] for q in range(6))
        pn = functools.partial(prenorm, tm=tm, tiles_per_batch=tpb, n_batch=n_batch)
        mm = functools.partial(matmul, tm=tm, tiles_per_batch=tpb, n_batch=n_batch)
        mix, j = i % N_MIXERS, i // N_MIXERS
        if mix == 0:
            (hb,) = pn(xs, norm_g[i, 0], 1.0 + sc1, sh1, out_dtypes=(BF16,))
            n_heads = na_rpb.shape[1]
            qscale = jnp.concatenate([jnp.full((d,), (d // n_heads) ** -0.5, F32), jnp.ones((2 * d,), F32)])
            qkv = mm(hb, na_w_qkv, w_index=j, out_dtype=BF16, colscale=qscale)
            kr = (na_rpb.shape[2] + 1) // 2
            o = na_attention(qkv, _na_bias_table(na_rpb[j], GRID_W, seq // GRID_W), kr, n_batch, seq, ctx_len,
                             not last)
            if last:
                xs = xs[:n_lat]
            xs = mm(o, na_w_o, w_index=j, mode="resid", resid=xs, gate=g1)
        elif mix == 1:
            (hf,) = pn(xs, norm_g[i, 0], 1.0 + sc1, sh1, out_dtypes=(F32,))
            wts = _s5_weights(s5_lam_re[j], s5_lam_im[j], s5_log_dt[j], s5_b_re[j], s5_b_im[j],
                              s5_c_re[j], s5_c_im[j], S5_CHUNK)
            y = s5_core(hf, wts, s5_d[j], n_batch, seq, ctx_len)
            if last:
                xs, y = xs[:n_lat], y[:n_lat]
            xs = mm(y, s5_w_glu, w_index=j, mode="glu", resid=xs, gate=g1)
        else:
            (hf,) = pn(xs, norm_g[i, 0], 1.0 + sc1, sh1, out_dtypes=(F32,))
            y = rwkv_mixer_tokens(hf, rw_mu[j], rw_w_rkv[j], rw_w0[j], rw_w1[j], rw_w2[j], rw_a0[j], rw_a1[j],
                                  rw_a2[j], rw_k_k[j], rw_k_a[j], rw_r_k[j], rw_ln_g[j], rw_ln_b[j],
                                  rw_g1[j], rw_g2[j], tm, n_batch, seq, ctx_len)
            if last:
                xs, y = xs[:n_lat], y[:n_lat]
            xs = mm(y, rw_w_o, w_index=j, mode="resid", resid=xs, gate=g1)
        (h2,) = pn(xs, norm_g[i, 1], 1.0 + sc2, sh2, out_dtypes=(F32,))
        xs = hier_moe(xs, h2, moe_wg[i], moe_bg[i], moe_we[i], moe_be[i], moe_w_gu, moe_w_down, i, g2,
                      tm, n_batch, seq)
    ones = jnp.ones((n_batch + 1, 1, d), F32)
    (out,) = prenorm(xs[:n_lat], final_g, ones, jnp.zeros_like(ones), tm, tpb, n_batch, out_dtypes=(x.dtype,))
    return out.reshape(n_batch, seq, d)
```

```python
import functools
import math

import jax
import jax.numpy as jnp
from jax import lax
from jax.experimental import pallas as pl
from jax.experimental.pallas import tpu as pltpu

F32 = jnp.float32
BF16 = jnp.bfloat16

GRID_W = 64
N_MIXERS = 3
NORM_EPS = 1e-6
RW_GN_EPS = 64e-5
MOE_TOP_K = 2
MOE_BLOCK = 256
MOE_COMBINE_ROWS = 128
S5_CHUNK = 16
SCAN_KEY_BLOCK = 16
LANES = 128
RELAYOUT_ROWS = 128
NA_Q_ROWS = 4
NA_KEY_ROWS = 12
NEG_BIG = -1e30
VMEM_LIMIT_BYTES = 48 * 1024 * 1024
VMEM_LIMIT_BIG_BYTES = 56 * 1024 * 1024


def _cparams(sem, limit=VMEM_LIMIT_BYTES):
    return pltpu.CompilerParams(dimension_semantics=sem, vmem_limit_bytes=limit)


def _row_tile(n_lat_per_batch, n_ctx_total, cap=512):
    t = cap
    while t > 8 and (n_lat_per_batch % t or (n_ctx_total and n_ctx_total % t)):
        t //= 2
    return t


def _seg_fn(tiles_per_batch, n_batch):
    def seg(i):
        return jnp.minimum(i // tiles_per_batch, n_batch)
    return seg


def _bdot(a, b):
    return jnp.dot(a.astype(BF16), b.astype(BF16), preferred_element_type=F32)


def _split3(x):
    hi = x.astype(BF16)
    r1 = x - hi.astype(F32)
    mid = r1.astype(BF16)
    lo = (r1 - mid.astype(F32)).astype(BF16)
    return hi, mid, lo


def _prenorm_body(x_ref, g_ref, sc_ref, sh_ref, *o_refs):
    x = x_ref[...]
    ms = jnp.mean(x * x, axis=-1, keepdims=True)
    h = (x * lax.rsqrt(ms + NORM_EPS)) * g_ref[...]
    h = h * sc_ref[0] + sh_ref[0]
    for o_ref in o_refs:
        o_ref[...] = h.astype(o_ref.dtype)


def prenorm(x, g, scale1p, shift, tm, tiles_per_batch, n_batch, out_dtypes):
    m, d = x.shape
    seg = _seg_fn(tiles_per_batch, n_batch)
    row = pl.BlockSpec((tm, d), lambda i: (i, 0))
    tab = pl.BlockSpec((1, 1, d), lambda i: (seg(i), 0, 0))
    outs = pl.pallas_call(
        _prenorm_body,
        grid=(m // tm,),
        in_specs=[row, pl.BlockSpec((1, d), lambda i: (0, 0)), tab, tab],
        out_specs=[row for _ in out_dtypes],
        out_shape=[jax.ShapeDtypeStruct((m, d), dt) for dt in out_dtypes],
        compiler_params=_cparams(("parallel",)),
    )(x, g.reshape(1, d), scale1p, shift)
    return outs


def _softplus(z):
    return jnp.maximum(z, 0.0) + jnp.log(1.0 + jnp.exp(-jnp.abs(z)))


def _apply_act(y, act):
    if act is None:
        return y
    if act == "tanh":
        return jnp.tanh(y)
    if act == "sigmoid":
        return jax.nn.sigmoid(y)
    if act == "decay":
        return jnp.exp(-jnp.exp(-_softplus(-y) - 0.5))
    raise ValueError(act)


def _mm_body(*refs, mode, act, has_bias, has_scale):
    it = iter(refs)
    x_ref = next(it)
    w_ref = next(it)
    w2_ref = next(it) if mode == "glu" else None
    b_ref = next(it) if has_bias else None
    s_ref = next(it) if has_scale else None
    r_ref = next(it) if mode in ("resid", "glu") else None
    g_ref = next(it) if mode in ("resid", "glu") else None
    o_ref = next(it)
    wb_ref = next(it)
    wb2_ref = next(it) if mode == "glu" else None

    @pl.when(pl.program_id(1) == 0)
    def _():
        wb_ref[...] = w_ref[...].astype(BF16)
        if mode == "glu":
            wb2_ref[...] = w2_ref[...].astype(BF16)

    x = x_ref[...].astype(BF16)
    acc = jnp.dot(x, wb_ref[...], preferred_element_type=F32)
    if mode == "glu":
        acc2 = jnp.dot(x, wb2_ref[...], preferred_element_type=F32)
        acc = acc * jax.nn.sigmoid(acc2)
    if has_bias:
        acc = acc + b_ref[...]
    acc = _apply_act(acc, act)
    if has_scale:
        acc = acc * s_ref[...]
    if mode in ("resid", "glu"):
        acc = r_ref[...] + g_ref[0] * acc
    o_ref[...] = acc.astype(o_ref.dtype)


def matmul(x, w, *, tm, tn=512, out_dtype=F32, mode="plain", act=None, bias=None, colscale=None,
           resid=None, gate=None, tiles_per_batch=None, n_batch=None, w_index=0):
    m, k = x.shape
    if w.ndim == 2:
        w, w_index = w[None], 0
    n = w.shape[2] // 2 if mode == "glu" else w.shape[2]
    tn = min(tn, n)
    assert m % tm == 0 and n % tn == 0, (m, tm, n, tn)
    nj = n // tn
    in_specs = [pl.BlockSpec((tm, k), lambda j, i: (i, 0)),
                pl.BlockSpec((None, k, tn), lambda j, i: (w_index, 0, j))]
    args = [x, w]
    if mode == "glu":
        in_specs.append(pl.BlockSpec((None, k, tn), lambda j, i: (w_index, 0, j + nj)))
        args.append(w)
    col = pl.BlockSpec((1, tn), lambda j, i: (0, j))
    if bias is not None:
        in_specs.append(col)
        args.append(bias.reshape(1, n).astype(F32))
    if colscale is not None:
        in_specs.append(col)
        args.append(colscale.reshape(1, n).astype(F32))
    if mode in ("resid", "glu"):
        seg = _seg_fn(tiles_per_batch, n_batch)
        in_specs.append(pl.BlockSpec((tm, tn), lambda j, i: (i, j)))
        in_specs.append(pl.BlockSpec((1, 1, tn), lambda j, i: (seg(i), 0, j)))
        args += [resid, gate]
    return pl.pallas_call(
        functools.partial(_mm_body, mode=mode, act=act, has_bias=bias is not None,
                          has_scale=colscale is not None),
        grid=(nj, m // tm),
        in_specs=in_specs,
        out_specs=pl.BlockSpec((tm, tn), lambda j, i: (i, j)),
        out_shape=jax.ShapeDtypeStruct((m, n), out_dtype),
        scratch_shapes=[pltpu.VMEM((k, tn), BF16)] * (2 if mode == "glu" else 1),
        compiler_params=_cparams(("parallel", "arbitrary")),
    )(*args)


def _router_body(x_ref, w_ref, b_ref, o_ref, *, n_grp, epg):
    xh, xm, xl = _split3(x_ref[...])
    wh, wm, wl = _split3(w_ref[...])
    dot = functools.partial(jnp.dot, preferred_element_type=F32)
    acc = dot(xh, wh) + (dot(xh, wm) + dot(xm, wh)) + (dot(xh, wl) + dot(xm, wm) + dot(xl, wh))
    acc = acc + b_ref[...]
    lane = lax.broadcasted_iota(jnp.int32, acc.shape, 1)
    lane_f = lane.astype(F32)
    far = 1e9

    def first_max(vals):
        m = jnp.max(vals, axis=-1, keepdims=True)
        return m, jnp.min(jnp.where(vals == m, lane_f, far), axis=-1, keepdims=True)

    gl = jnp.where(lane < n_grp, acc, NEG_BIG)
    gmax, grp = first_max(gl)
    p_grp = 1.0 / jnp.sum(jnp.exp(gl - gmax), axis=-1, keepdims=True)
    lo = n_grp + grp * epg
    el = jnp.where((lane_f >= lo) & (lane_f < lo + epg), acc, NEG_BIG)
    m1, i1 = first_max(el)
    m2, i2 = first_max(jnp.where(lane_f == i1, NEG_BIG, el))
    e21 = jnp.exp(m2 - m1)
    g1 = p_grp / (1.0 + e21)
    g2 = p_grp * e21 / (1.0 + e21)
    out = jnp.where(lane == 0, i1 - n_grp, jnp.where(lane == 1, i2 - n_grp,
                                                    jnp.where(lane == 2, g1, jnp.where(lane == 3, g2, 0.0))))
    o_ref[...] = out


def router(x, w, bias, tm, n_grp, epg):
    m, k = x.shape
    n = w.shape[1]
    return pl.pallas_call(
        functools.partial(_router_body, n_grp=n_grp, epg=epg),
        grid=(m // tm,),
        in_specs=[pl.BlockSpec((tm, k), lambda i: (i, 0)), pl.BlockSpec((k, n), lambda i: (0, 0)),
                  pl.BlockSpec((1, n), lambda i: (0, 0))],
        out_specs=pl.BlockSpec((tm, n), lambda i: (i, 0)),
        out_shape=jax.ShapeDtypeStruct((m, n), F32),
        compiler_params=_cparams(("parallel",)),
        name="moe_router",
    )(x, w, bias.reshape(1, n))


def _mod_body(s_ref, w_ref, b_ref, o_ref):
    s = s_ref[...]
    s = s * jax.nn.sigmoid(s)
    o_ref[0] = _bdot(s, w_ref[0]) + b_ref[0]


def modulation(cvecs, mod_w, mod_b, tn=768):
    depth, d, n = mod_w.shape
    rows = cvecs.shape[0]
    while n % tn:
        tn //= 2
    return pl.pallas_call(
        _mod_body,
        grid=(depth, n // tn),
        in_specs=[pl.BlockSpec((rows, d), lambda l, j: (0, 0)),
                  pl.BlockSpec((1, d, tn), lambda l, j: (l, 0, j)),
                  pl.BlockSpec((1, 1, tn), lambda l, j: (l, 0, j))],
        out_specs=pl.BlockSpec((1, rows, tn), lambda l, j: (l, 0, j)),
        out_shape=jax.ShapeDtypeStruct((depth, rows, n), F32),
        compiler_params=_cparams(("parallel", "parallel")),
    )(cvecs, mod_w, mod_b.reshape(depth, 1, n))


def _na_key_start(i, rows, kr):
    return jnp.clip(i * NA_Q_ROWS - kr // 2, 0, rows - NA_KEY_ROWS)


def _na_bias_table(rpb, grid_w, rows):
    n_heads = rpb.shape[0]
    kr = (rpb.shape[1] + 1) // 2
    kcw = (rpb.shape[2] + 1) // 2
    n_blk = rows // NA_Q_ROWS
    q = jnp.arange(grid_w)[:, None]
    kc = jnp.arange(grid_w)[None, :]
    win = jnp.clip(q - kcw // 2, 0, grid_w - kcw)
    col_ok = (kc >= win) & (kc < win + kcw)
    dc = jnp.clip(kc - q, 1 - kcw, kcw - 1) + kcw - 1
    blk = jnp.array([0, min(1, n_blk - 1), n_blk - 1])[:, None, None]
    r = blk * NA_Q_ROWS + jnp.arange(NA_Q_ROWS)[None, :, None]
    krow = _na_key_start(blk, rows, kr) + jnp.arange(NA_KEY_ROWS)[None, None, :]
    rs = jnp.clip(r - kr // 2, 0, rows - kr)
    row_ok = (krow >= rs) & (krow < rs + kr)
    dr = jnp.clip(krow - r + kr - 1, 0, 2 * kr - 2)
    hi = lax.Precision.HIGHEST
    sel_r = jax.nn.one_hot(dr.reshape(-1), rpb.shape[1], dtype=F32)
    sel_c = jax.nn.one_hot(dc.reshape(-1), rpb.shape[2], dtype=F32)
    tab = jnp.einsum("ra,has->hrs", sel_r, jnp.einsum("hab,sb->has", rpb.astype(F32), sel_c, precision=hi),
                     precision=hi)
    tab = tab.reshape(n_heads, 3, NA_Q_ROWS, NA_KEY_ROWS, grid_w, grid_w)
    ok = row_ok[None, :, :, :, None, None] & col_ok[None, None, None, None]
    tab = jnp.where(ok, tab, NEG_BIG)
    return tab.transpose(0, 1, 2, 4, 3, 5).reshape(n_heads, 3, NA_Q_ROWS * grid_w, NA_KEY_ROWS * grid_w)


def _na_body(q_ref, k_ref, v_ref, kc_ref, vc_ref, bias_ref, o_ref, *, hpb, dh, grid_w, kr, rows):
    ks = _na_key_start(pl.program_id(2), rows, kr)
    start = pl.multiple_of(ks * grid_w, grid_w)
    nt = (((1,), (1,)), ((), ()))
    for h in range(hpb):
        sl = slice(h * dh, (h + 1) * dh)
        q = q_ref[:, sl]
        k = k_ref[pl.ds(start, NA_KEY_ROWS * grid_w), sl]
        v = v_ref[pl.ds(start, NA_KEY_ROWS * grid_w), sl]
        s = lax.dot_general(q, k, nt, preferred_element_type=F32) + bias_ref[h, 0]
        sc = lax.dot_general(q, kc_ref[:, sl], nt, preferred_element_type=F32)
        m = jnp.maximum(jnp.max(s, axis=-1, keepdims=True), jnp.max(sc, axis=-1, keepdims=True))
        p = jnp.exp(s - m)
        pc = jnp.exp(sc - m)
        den = jnp.sum(p, axis=-1, keepdims=True) + jnp.sum(pc, axis=-1, keepdims=True)
        o = (jnp.dot(p.astype(BF16), v, preferred_element_type=F32)
             + jnp.dot(pc.astype(BF16), vc_ref[:, sl], preferred_element_type=F32))
        o_ref[:, sl] = (o / den).astype(o_ref.dtype)


def _ctx_attn_body(q_ref, k_ref, v_ref, o_ref, *, hpb, dh):
    nt = (((1,), (1,)), ((), ()))
    for h in range(hpb):
        sl = slice(h * dh, (h + 1) * dh)
        s = lax.dot_general(q_ref[:, sl], k_ref[:, sl], nt, preferred_element_type=F32)
        m = jnp.max(s, axis=-1, keepdims=True)
        p = jnp.exp(s - m)
        den = jnp.sum(p, axis=-1, keepdims=True)
        o = jnp.dot(p.astype(BF16), v_ref[:, sl], preferred_element_type=F32)
        o_ref[:, sl] = (o / den).astype(o_ref.dtype)


def na_attention(qkv, bias_tab, kr, n_batch, seq, ctx_len, need_ctx):
    d = qkv.shape[1] // 3
    n_heads = bias_tab.shape[0]
    dh = d // n_heads
    hpb = max(1, min(n_heads, 256 // dh))
    bw = hpb * dh
    n_d = d // bw
    rows = seq // GRID_W
    n_blk = rows // NA_Q_ROWS
    qb = NA_Q_ROWS * GRID_W
    n_lat = n_batch * seq
    ctx_blk0 = n_lat // ctx_len

    def cls_of(i):
        return jnp.where(i == 0, 0, jnp.where(i == n_blk - 1, 2, 1))

    o_lat = pl.pallas_call(
        functools.partial(_na_body, hpb=hpb, dh=dh, grid_w=GRID_W, kr=kr, rows=rows),
        grid=(n_batch, n_d, n_blk),
        in_specs=[
            pl.BlockSpec((qb, bw), lambda b, g, i: (b * n_blk + i, g)),
            pl.BlockSpec((seq, bw), lambda b, g, i: (b, n_d + g)),
            pl.BlockSpec((seq, bw), lambda b, g, i: (b, 2 * n_d + g)),
            pl.BlockSpec((ctx_len, bw), lambda b, g, i: (ctx_blk0 + b, n_d + g)),
            pl.BlockSpec((ctx_len, bw), lambda b, g, i: (ctx_blk0 + b, 2 * n_d + g)),
            pl.BlockSpec((hpb, 1, qb, NA_KEY_ROWS * GRID_W), lambda b, g, i: (g, cls_of(i), 0, 0)),
        ],
        out_specs=pl.BlockSpec((qb, bw), lambda b, g, i: (b * n_blk + i, g)),
        out_shape=jax.ShapeDtypeStruct((n_lat, d), BF16),
        compiler_params=_cparams(("parallel", "parallel", "arbitrary")),
        name="na_attention",
    )(qkv, qkv, qkv, qkv, qkv, bias_tab)
    if not need_ctx:
        return o_lat
    o_ctx = pl.pallas_call(
        functools.partial(_ctx_attn_body, hpb=hpb, dh=dh),
        grid=(n_batch, n_d),
        in_specs=[
            pl.BlockSpec((ctx_len, bw), lambda b, g: (ctx_blk0 + b, g)),
            pl.BlockSpec((ctx_len, bw), lambda b, g: (ctx_blk0 + b, n_d + g)),
            pl.BlockSpec((ctx_len, bw), lambda b, g: (ctx_blk0 + b, 2 * n_d + g)),
        ],
        out_specs=pl.BlockSpec((ctx_len, bw), lambda b, g: (b, g)),
        out_shape=jax.ShapeDtypeStruct((n_batch * ctx_len, d), BF16),
        compiler_params=_cparams(("parallel", "parallel")),
    )(qkv, qkv, qkv)
    return jnp.concatenate([o_lat, o_ctx], axis=0)


def _expert_body(be_ref, idx_ref, idx_next_ref, x_hbm, wgu_ref, wd_ref, o_ref, xbuf, sem, wgu_s, wd_s, *,
                 hidden, n_blocks):
    j = pl.program_id(0)
    cur = j % 2

    def row_copy(idx, buf, r):
        return pltpu.make_async_copy(x_hbm.at[pl.ds(idx[0, 0, r], 1)], xbuf.at[buf, pl.ds(r, 1)], sem.at[buf])

    @pl.when(j == 0)
    def _():
        for r in range(MOE_BLOCK):
            row_copy(idx_ref, 0, r).start()

    prev = be_ref[jnp.maximum(j - 1, 0)]
    changed = jnp.logical_or(j == 0, be_ref[j] != prev)

    @pl.when(changed)
    def _():
        wgu_s[...] = wgu_ref[...].astype(BF16)
        wd_s[...] = wd_ref[...].astype(BF16)

    for r in range(MOE_BLOCK):
        row_copy(idx_ref, cur, r).wait()

    for r in range(MOE_BLOCK):
        row_copy(idx_next_ref, 1 - cur, r).start()
    gu = jnp.dot(xbuf[cur].astype(BF16), wgu_s[...], preferred_element_type=F32)
    g = gu[:, :hidden]
    u = gu[:, hidden:]
    a = (g * jax.nn.sigmoid(g) * u).astype(BF16)
    o_ref[...] = jnp.dot(a, wd_s[...], preferred_element_type=F32)

    @pl.when(j == n_blocks - 1)
    def _():
        for r in range(MOE_BLOCK):
            row_copy(idx_next_ref, 1 - cur, r).wait()


def expert_blocks(x, slot_tok, blk_e, w_gu, w_down, layer):
    d = x.shape[1]
    hidden = w_down.shape[2]
    n_blocks = blk_e.shape[0]
    idx = slot_tok.reshape(n_blocks, 1, MOE_BLOCK)
    return pl.pallas_call(
        functools.partial(_expert_body, hidden=hidden, n_blocks=n_blocks),
        grid_spec=pltpu.PrefetchScalarGridSpec(
            num_scalar_prefetch=1,
            grid=(n_blocks,),
            in_specs=[pl.BlockSpec((1, 1, MOE_BLOCK), lambda j, be: (j, 0, 0), memory_space=pltpu.SMEM),
                      pl.BlockSpec((1, 1, MOE_BLOCK), lambda j, be: (jnp.minimum(j + 1, n_blocks - 1), 0, 0),
                                   memory_space=pltpu.SMEM),
                      pl.BlockSpec(memory_space=pl.ANY),
                      pl.BlockSpec((None, None, d, 2 * hidden), lambda j, be: (layer, be[j], 0, 0)),
                      pl.BlockSpec((None, None, hidden, d), lambda j, be: (layer, be[j], 0, 0))],
            out_specs=pl.BlockSpec((MOE_BLOCK, d), lambda j, be: (j, 0)),
            scratch_shapes=[pltpu.VMEM((2, MOE_BLOCK, d), F32), pltpu.SemaphoreType.DMA((2,)),
                            pltpu.VMEM((d, 2 * hidden), BF16), pltpu.VMEM((hidden, d), BF16)]),
        out_shape=jax.ShapeDtypeStruct((n_blocks * MOE_BLOCK, d), F32),
        compiler_params=_cparams(("arbitrary",), VMEM_LIMIT_BIG_BYTES),
        name="moe_experts",
    )(blk_e, idx, idx, x, w_gu, w_down)


def _combine_body(idx_ref, idx_next_ref, route_ref, x_ref, g_ref, yb_hbm, o_ref, ybuf, sem, *, tmc, n_tiles):
    i = pl.program_id(0)
    cur = i % 2

    def row_copy(idx, buf, k, r):
        return pltpu.make_async_copy(yb_hbm.at[pl.ds(idx[0, 0, k * tmc + r], 1)],
                                     ybuf.at[buf, k, pl.ds(r, 1)], sem.at[buf])

    def start_all(idx, buf):
        for k in range(MOE_TOP_K):
            for r in range(tmc):
                row_copy(idx, buf, k, r).start()

    @pl.when(i == 0)
    def _():
        start_all(idx_ref, 0)

    @pl.when(i + 1 < n_tiles)
    def _():
        start_all(idx_next_ref, 1 - cur)

    for k in range(MOE_TOP_K):
        for r in range(tmc):
            row_copy(idx_ref, cur, k, r).wait()
    route = route_ref[...]
    f = ybuf[cur, 0] * route[:, MOE_TOP_K:MOE_TOP_K + 1]
    for k in range(1, MOE_TOP_K):
        f = f + ybuf[cur, k] * route[:, MOE_TOP_K + k:MOE_TOP_K + k + 1]
    o_ref[...] = x_ref[...] + g_ref[0] * f


def moe_combine(xs, yb, slot_of_asg, route, gate_tab, n_batch, seq):
    n_tok, d = xs.shape
    tmc = MOE_COMBINE_ROWS
    n_tiles = n_tok // tmc
    idx = slot_of_asg.reshape(n_tiles, tmc, MOE_TOP_K).transpose(0, 2, 1).reshape(n_tiles, 1, MOE_TOP_K * tmc)
    seg = _seg_fn(seq // tmc, n_batch)
    return pl.pallas_call(
        functools.partial(_combine_body, tmc=tmc, n_tiles=n_tiles),
        grid=(n_tiles,),
        in_specs=[pl.BlockSpec((1, 1, MOE_TOP_K * tmc), lambda i: (i, 0, 0), memory_space=pltpu.SMEM),
                  pl.BlockSpec((1, 1, MOE_TOP_K * tmc), lambda i: (jnp.minimum(i + 1, n_tiles - 1), 0, 0),
                               memory_space=pltpu.SMEM),
                  pl.BlockSpec((tmc, route.shape[1]), lambda i: (i, 0)),
                  pl.BlockSpec((tmc, d), lambda i: (i, 0)),
                  pl.BlockSpec((1, 1, d), lambda i: (seg(i), 0, 0)),
                  pl.BlockSpec(memory_space=pl.ANY)],
        out_specs=pl.BlockSpec((tmc, d), lambda i: (i, 0)),
        out_shape=jax.ShapeDtypeStruct((n_tok, d), F32),
        scratch_shapes=[pltpu.VMEM((2, MOE_TOP_K, tmc, d), F32), pltpu.SemaphoreType.DMA((2,))],
        compiler_params=_cparams(("arbitrary",)),
        name="moe_combine",
    )(idx, idx, route, xs, gate_tab, yb)


def hier_moe(xs, h, wg, bg, we, be, w_gu, w_down, layer, gate_tab, tm, n_batch, seq):
    n_tok, d = h.shape
    n_grp = wg.shape[1]
    n_exp = we.shape[1]
    epg = n_exp // n_grp
    n_logit = n_grp + n_exp
    n_pad = -(-n_logit // 128) * 128
    wcat = jnp.pad(jnp.concatenate([wg, we], axis=1), ((0, 0), (0, n_pad - n_logit)))
    bcat = jnp.pad(jnp.concatenate([bg, be], axis=0), (0, n_pad - n_logit))
    route = router(h, wcat, bcat, tm, n_grp, epg)

    flat_e = route[:, :MOE_TOP_K].astype(jnp.int32).reshape(-1)
    n_asg = flat_e.shape[0]
    onehot = (flat_e[:, None] == jnp.arange(n_exp, dtype=jnp.int32)[None, :]).astype(F32)
    ck = 128
    oh3 = onehot.reshape(n_asg // ck, ck, n_exp)
    tri = jnp.tril(jnp.ones((ck, ck), F32))
    within = jnp.einsum("ij,cjk->cik", tri, oh3, precision=lax.Precision.HIGHEST)
    tot = within[:, -1, :]
    before = jnp.cumsum(tot, axis=0) - tot
    rank = jnp.sum(oh3 * (within + before[:, None, :]), axis=-1).reshape(n_asg).astype(jnp.int32) - 1
    counts = jnp.sum(tot, axis=0).astype(jnp.int32)
    padded = (counts + MOE_BLOCK - 1) // MOE_BLOCK * MOE_BLOCK
    pad_end = jnp.cumsum(padded)
    pad_start = pad_end - padded
    slot = jnp.sum(onehot * pad_start.astype(F32)[None, :], axis=-1).astype(jnp.int32) + rank
    n_blocks = -(-n_asg // MOE_BLOCK) + n_exp
    slot_tok = jnp.zeros((n_blocks * MOE_BLOCK,), jnp.int32).at[slot].set(
        jnp.arange(n_asg, dtype=jnp.int32) // MOE_TOP_K)
    blk_start = jnp.arange(n_blocks, dtype=jnp.int32) * MOE_BLOCK
    blk_e = jnp.minimum(jnp.sum((pad_end[None, :] <= blk_start[:, None]).astype(jnp.int32), axis=1), n_exp - 1)
    yb = expert_blocks(h, slot_tok, blk_e, w_gu, w_down, layer)
    return moe_combine(xs, yb, slot.reshape(n_tok, MOE_TOP_K), route, gate_tab, n_batch, seq)


def _s5_weights(lam_re, lam_im, log_dt, b_re, b_im, c_re, c_im, t_chunk):
    n_grp, n_state = lam_re.shape[1], lam_re.shape[2]
    cg = b_re.shape[-1]
    lam = lax.complex(lam_re.astype(F32), lam_im.astype(F32))
    dt = jnp.exp(log_dt.astype(F32))[..., None]
    lam_bar = jnp.exp(lam * dt)
    b_bar = ((lam_bar - 1.0) / lam)[..., None] * lax.complex(b_re.astype(F32), b_im.astype(F32))
    c_mat = lax.complex(c_re.astype(F32), c_im.astype(F32))
    ks = jnp.arange(t_chunk + 1, dtype=F32)
    pw = jnp.exp((lam * dt)[..., None] * ks)
    hi = lax.Precision.HIGHEST
    kern = jnp.einsum("dgcp,dgpk,dgpi->dgkci", c_mat, pw[..., :t_chunk], b_bar, precision=hi).real
    t_idx = jnp.arange(t_chunk)
    lag = t_idx[None, :] - t_idx[:, None]
    k_f = kern[0][:, jnp.clip(lag, 0, t_chunk - 1)]
    k_r = kern[1][:, jnp.clip(-lag, 0, t_chunk - 1)]
    toep = (jnp.where((lag >= 0)[None, :, :, None, None], k_f, 0.0)
            + jnp.where((lag <= 0)[None, :, :, None, None], k_r, 0.0))
    toep = toep.transpose(0, 1, 4, 2, 3).reshape(n_grp, t_chunk * cg, t_chunk * cg)
    pf = pw[0][..., :t_chunk][..., ::-1]
    pr = pw[1][..., :t_chunk]
    bend_f = jnp.einsum("gps,gpc->gscp", pf, b_bar[0]).reshape(n_grp, t_chunk * cg, n_state)
    bend_r = jnp.einsum("gps,gpc->gscp", pr, b_bar[1]).reshape(n_grp, t_chunk * cg, n_state)
    bend = jnp.stack([bend_f.real, bend_f.imag, bend_r.real, bend_r.imag], axis=1)
    of = jnp.einsum("gcp,gpt->gptc", c_mat[0], pw[0][..., 1:]).reshape(n_grp, n_state, t_chunk * cg)
    orv = jnp.einsum("gcp,gpt->gptc", c_mat[1], pw[1][..., 1:][..., ::-1]).reshape(n_grp, n_state, t_chunk * cg)
    cout = jnp.stack([of.real, -of.imag, orv.real, -orv.imag], axis=1)

    gsz = LANES // cg
    nb = n_grp // gsz
    eye = jnp.eye(gsz, dtype=F32)
    wt = jnp.einsum("bgsitc,gh->bsgithc", toep.reshape(nb, gsz, t_chunk, cg, t_chunk, cg), eye)
    wt = wt.reshape(nb, t_chunk, LANES, t_chunk * LANES).astype(BF16)
    wb = jnp.einsum("bgxsip,gh->bsgixhp", bend.reshape(nb, gsz, 4, t_chunk, cg, n_state), eye)
    wb = wb.reshape(nb, t_chunk, LANES, 4 * gsz * n_state).astype(BF16)
    wc = jnp.einsum("bgxptc,gh->bxgpthc", cout.reshape(nb, gsz, 4, n_state, t_chunk, cg), eye)
    wc = wc.reshape(nb, 4 * gsz * n_state, t_chunk * LANES).astype(BF16)
    lam_t = pw[..., t_chunk]
    lam_t4 = jnp.stack([lam_t[0].real, lam_t[0].imag, lam_t[1].real, lam_t[1].imag]).reshape(4, n_grp * n_state)
    return wt, wb, wc, lam_t4


def _s5_a_body(h_ref, wb_ref, e_ref, *, t_chunk):
    acc = jnp.dot(h_ref[:, 0, :].astype(BF16), wb_ref[0], preferred_element_type=F32)
    for s in range(1, t_chunk):
        acc = acc + jnp.dot(h_ref[:, s, :].astype(BF16), wb_ref[s], preferred_element_type=F32)
    e_ref[...] = acc


def _s5_b_body(e_ref, lam_ref, x_ref, *, n_batch, n_lat_chunk, n_ctx_chunk, width):
    ctx0 = n_batch * n_lat_chunk
    n_chunk = n_lat_chunk + n_ctx_chunk

    def comp(x):
        return slice(x * width, (x + 1) * width)

    lfr, lfi, lrr, lri = (lam_ref[:, comp(x)] for x in range(4))

    for b in range(n_batch):
        def fwd(n, carry):
            sr, si = carry
            row = jnp.where(n < n_ctx_chunk, ctx0 + b * n_ctx_chunk + n, b * n_lat_chunk + n - n_ctx_chunk)
            x_ref[pl.ds(row, 1), comp(0)] = sr
            x_ref[pl.ds(row, 1), comp(1)] = si
            er = e_ref[pl.ds(row, 1), comp(0)]
            ei = e_ref[pl.ds(row, 1), comp(1)]
            return lfr * sr - lfi * si + er, lfr * si + lfi * sr + ei

        def rev(n, carry):
            sr, si = carry
            row = jnp.where(n < n_ctx_chunk, ctx0 + b * n_ctx_chunk + n_ctx_chunk - 1 - n,
                            b * n_lat_chunk + n_chunk - 1 - n)
            x_ref[pl.ds(row, 1), comp(2)] = sr
            x_ref[pl.ds(row, 1), comp(3)] = si
            er = e_ref[pl.ds(row, 1), comp(2)]
            ei = e_ref[pl.ds(row, 1), comp(3)]
            return lrr * sr - lri * si + er, lrr * si + lri * sr + ei

        z = jnp.zeros((1, width), F32)
        lax.fori_loop(0, n_chunk, fwd, (z, z))
        lax.fori_loop(0, n_chunk, rev, (z, z))


def _gelu(y):
    return 0.5 * y * (1.0 + jnp.tanh(0.7978845608028654 * (y + 0.044715 * (y * y * y))))


def _s5_c_body(h_ref, x_ref, wt_ref, wc_ref, dsk_ref, o_ref, *, t_chunk):
    acc = jnp.dot(x_ref[...].astype(BF16), wc_ref[...], preferred_element_type=F32)
    for s in range(t_chunk):
        acc = acc + jnp.dot(h_ref[:, s, :].astype(BF16), wt_ref[s], preferred_element_type=F32)
    dsk = dsk_ref[...]
    for t in range(t_chunk):
        o_ref[:, t, :] = _gelu(acc[:, t * LANES:(t + 1) * LANES] + h_ref[:, t, :] * dsk)


def s5_core(h_f32, weights, d_skip, n_batch, seq, ctx_len):
    wt, wb, wc, lam_t4 = weights
    t = S5_CHUNK
    m, d = h_f32.shape
    nb = wt.shape[0]
    sw = wb.shape[-1]
    width = sw // 4
    rows = m // t
    rbs = rows // 2 if rows % 16 == 0 else rows
    h3 = h_f32.reshape(rows, t, d)
    hblk = pl.BlockSpec((rbs, t, LANES), lambda g, i: (i, 0, g))
    sblk = pl.BlockSpec((rbs, sw), lambda g, i: (i, g))
    e = pl.pallas_call(
        functools.partial(_s5_a_body, t_chunk=t),
        grid=(nb, rows // rbs),
        in_specs=[hblk, pl.BlockSpec((None, t, LANES, sw), lambda g, i: (g, 0, 0, 0))],
        out_specs=sblk,
        out_shape=jax.ShapeDtypeStruct((rows, nb * sw), F32),
        compiler_params=_cparams(("parallel", "arbitrary")),
        name="s5_chunk_states",
    )(h3, wb)

    lam_blk = lam_t4.reshape(4, nb, width).transpose(1, 0, 2).reshape(nb, 1, sw)
    x_in = pl.pallas_call(
        functools.partial(_s5_b_body, n_batch=n_batch, n_lat_chunk=seq // t, n_ctx_chunk=ctx_len // t,
                          width=width),
        grid=(nb,),
        in_specs=[pl.BlockSpec((rows, sw), lambda g: (0, g)), pl.BlockSpec((None, 1, sw), lambda g: (g, 0, 0))],
        out_specs=pl.BlockSpec((rows, sw), lambda g: (0, g)),
        out_shape=jax.ShapeDtypeStruct((rows, nb * sw), F32),
        compiler_params=_cparams(("parallel",)),
        name="s5_chunk_recurrence",
    )(e, lam_blk)

    y3 = pl.pallas_call(
        functools.partial(_s5_c_body, t_chunk=t),
        grid=(nb, rows // rbs),
        in_specs=[hblk, sblk,
                  pl.BlockSpec((None, t, LANES, t * LANES), lambda g, i: (g, 0, 0, 0)),
                  pl.BlockSpec((None, sw, t * LANES), lambda g, i: (g, 0, 0)),
                  pl.BlockSpec((1, LANES), lambda g, i: (0, g))],
        out_specs=hblk,
        out_shape=jax.ShapeDtypeStruct((rows, t, d), F32),
        compiler_params=_cparams(("parallel", "arbitrary"), VMEM_LIMIT_BIG_BYTES),
        name="s5_outputs",
    )(h3, x_in, wt, wc, d_skip.astype(F32).reshape(1, d))
    return y3.reshape(m, d)


def _shift_body(h_ref, hp_ref, hn_ref, mu_ref, *o_refs, tm, tiles_lat, tiles_ctx, n_lat_tiles):
    i = pl.program_id(0)
    h = h_ref[...]
    first = jnp.where(i < n_lat_tiles, i % tiles_lat == 0, (i - n_lat_tiles) % tiles_ctx == 0)
    last = jnp.where(i < n_lat_tiles, i % tiles_lat == tiles_lat - 1,
                     (i - n_lat_tiles) % tiles_ctx == tiles_ctx - 1)
    rid = lax.broadcasted_iota(jnp.int32, h.shape, 0)
    prev_edge = jnp.where(first, 0.0, hp_ref[7:8, :])
    next_edge = jnp.where(last, 0.0, hn_ref[0:1, :])
    prev = jnp.where(rid == 0, prev_edge, pltpu.roll(h, 1, 0))
    nxt = jnp.where(rid == tm - 1, next_edge, pltpu.roll(h, tm - 1, 0))
    xx = 0.5 * (prev + nxt) - h
    for m, o_ref in enumerate(o_refs):
        o_ref[...] = (h + xx * mu_ref[m:m + 1, :]).astype(o_ref.dtype)


def token_shift_mix(h, mu, tm, seq, ctx_len, n_batch):
    m, d = h.shape
    n_mix = mu.shape[0]
    n_lat_tiles = n_batch * seq // tm
    nb8 = m // 8
    per = tm // 8
    row = pl.BlockSpec((tm, d), lambda i: (i, 0))
    return pl.pallas_call(
        functools.partial(_shift_body, tm=tm, tiles_lat=seq // tm, tiles_ctx=max(ctx_len // tm, 1),
                          n_lat_tiles=n_lat_tiles),
        grid=(m // tm,),
        in_specs=[row,
                  pl.BlockSpec((8, d), lambda i: (jnp.maximum(i * per - 1, 0), 0)),
                  pl.BlockSpec((8, d), lambda i: (jnp.minimum((i + 1) * per, nb8 - 1), 0)),
                  pl.BlockSpec((n_mix, d), lambda i: (0, 0))],
        out_specs=[row] * n_mix,
        out_shape=[jax.ShapeDtypeStruct((m, d), BF16)] * n_mix,
        compiler_params=_cparams(("parallel",)),
    )(h, h, h, mu.astype(F32))


def _rwkv_scan_body(rf_ref, rr_ref, kf_ref, kr_ref, vf_ref, vr_ref, wf_ref, wr_ref, af_ref, ar_ref,
                    kk_ref, ka_ref, of_ref, or_ref, s_ref, op_ref, *, tb, n, bh):
    @pl.when(pl.program_id(0) == 0)
    def _():
        s_ref[...] = jnp.zeros_like(s_ref)

    k_k = kk_ref[...]
    k_a = ka_ref[...]

    def step(t, carry):
        tr = tb - 1 - t
        r = jnp.concatenate([rf_ref[t], rr_ref[tr]], axis=-1)
        k = jnp.concatenate([kf_ref[t], kr_ref[tr]], axis=-1)
        v = jnp.concatenate([vf_ref[t], vr_ref[tr]], axis=-1)
        w = jnp.concatenate([wf_ref[t], wr_ref[tr]], axis=-1)
        a = jnp.concatenate([af_ref[t], ar_ref[tr]], axis=-1)
        kk = k * k_k
        nrm = jnp.sqrt(jnp.sum(kk * kk, axis=0, keepdims=True))
        kk = kk / jnp.maximum(nrm, 1e-12)
        nkk = -kk
        akk = kk * a
        kd = k * (1.0 + (a - 1.0) * k_a)
        op_ref[0] = nkk
        op_ref[1] = w * r
        op_ref[2] = w
        op_ref[3] = akk
        op_ref[4] = kd
        c_a = jnp.sum(akk * r, axis=0, keepdims=True)
        c_k = jnp.sum(kd * r, axis=0, keepdims=True)
        def reduce_keys(jb, acc):
            sa, so = acc
            for jj in range(SCAN_KEY_BLOCK):
                j = jb * SCAN_KEY_BLOCK + jj
                s_j = s_ref[j]
                sa = sa + s_j * op_ref[0, pl.ds(j, 1), :]
                so = so + s_j * op_ref[1, pl.ds(j, 1), :]
            return sa, so

        zero = jnp.zeros_like(v)
        sa, so = lax.fori_loop(0, n // SCAN_KEY_BLOCK, reduce_keys, (zero, zero))
        o = so + sa * c_a + v * c_k
        of_ref[t] = o[:, :bh]
        or_ref[tr] = o[:, bh:]

        def update_keys(jb, c):
            for jj in range(SCAN_KEY_BLOCK):
                j = jb * SCAN_KEY_BLOCK + jj
                s_ref[j] = (s_ref[j] * op_ref[2, pl.ds(j, 1), :] + sa * op_ref[3, pl.ds(j, 1), :]
                            + v * op_ref[4, pl.ds(j, 1), :])
            return c

        lax.fori_loop(0, n // SCAN_KEY_BLOCK, update_keys, 0)
        return carry

    lax.fori_loop(0, tb, step, 0)


def rwkv_scan(r, k, v, w0, w1, a0, a1, k_k, k_a, ctx_len, tb=32):
    steps, n, bh = r.shape
    while steps % tb or ctx_len % tb:
        tb //= 2
    nblk, nblk_ctx = steps // tb, ctx_len // tb

    def rev_blk(i):
        return jnp.where(i < nblk_ctx, nblk_ctx - 1 - i, nblk + nblk_ctx - 1 - i)

    fwd = pl.BlockSpec((tb, n, bh), lambda i: (i, 0, 0))
    rev = pl.BlockSpec((tb, n, bh), lambda i: (rev_blk(i), 0, 0))
    tab = pl.BlockSpec((n, 2 * bh), lambda i: (0, 0))
    out = jax.ShapeDtypeStruct((steps, n, bh), F32)
    return pl.pallas_call(
        functools.partial(_rwkv_scan_body, tb=tb, n=n, bh=bh),
        grid=(nblk,),
        in_specs=[fwd, rev, fwd, rev, fwd, rev, fwd, rev, fwd, rev, tab, tab],
        out_specs=[fwd, rev],
        out_shape=[out, out],
        scratch_shapes=[pltpu.VMEM((n, n, 2 * bh), F32), pltpu.VMEM((5, n, 2 * bh), F32)],
        compiler_params=_cparams(("arbitrary",)),
        name="rwkv_scan",
    )(r, r, k, k, v, v, w0, w1, a0, a1, k_k, k_a)


def _rwkv_post_body(of_ref, or_ref, r_ref, k_ref, v_ref, a0_ref, a1_ref, g_ref, ka_ref, rk_ref, lg_ref, lb_ref,
                    y_ref):
    o = of_ref[...] + or_ref[...]
    mean = jnp.mean(o, axis=1, keepdims=True)
    var = jnp.mean(jnp.square(o - mean), axis=1, keepdims=True)
    on = (o - mean) * lax.rsqrt(var + RW_GN_EPS) * lg_ref[...] + lb_ref[...]
    r = r_ref[...]
    k = k_ref[...]
    k_a = ka_ref[...]
    rk = rk_ref[...]
    kd0 = k * (1.0 + (a0_ref[...] - 1.0) * k_a)
    kd1 = k * (1.0 + (a1_ref[...] - 1.0) * k_a)
    bonus = (jnp.sum(r * kd0 * rk, axis=1, keepdims=True)
             + jnp.sum(r * kd1 * rk, axis=1, keepdims=True)) * v_ref[...]
    y = (on + bonus)
    y_ref[...] = (y * g_ref[...]).astype(y_ref.dtype)


def rwkv_post(o_f, o_r, r, k, v, a0, a1, gate, k_a, r_k, ln_g, ln_b, tb=32):
    steps, n, ch = o_f.shape
    while steps % tb:
        tb //= 2
    blk = pl.BlockSpec((tb, n, ch), lambda i: (i, 0, 0))
    tab = pl.BlockSpec((1, n, ch), lambda i: (0, 0, 0))
    return pl.pallas_call(
        _rwkv_post_body,
        grid=(steps // tb,),
        in_specs=[blk] * 8 + [tab] * 4,
        out_specs=blk,
        out_shape=jax.ShapeDtypeStruct((steps, n, ch), F32),
        compiler_params=_cparams(("parallel",)),
        name="rwkv_post",
    )(o_f, o_r, r, k, v, a0, a1, gate, k_a[None], r_k[None], ln_g[None], ln_b[None])


def _step_block_row(s, b, n_ctx_blk, n_lat_blk, n_batch):
    return jnp.where(s < n_ctx_blk, n_batch * n_lat_blk + b * n_ctx_blk + s, b * n_lat_blk + s - n_ctx_blk)


def _to_steps_body(*refs, n_heads):
    x_refs, o_ref = refs[:-1], refs[-1]
    parts = []
    for x_ref in x_refs:
        x = x_ref[...]
        t = x.shape[0]
        parts.append(jnp.swapaxes(x.reshape(t, n_heads, x.shape[1] // n_heads), 1, 2))
    o_ref[...] = jnp.concatenate(parts, axis=-1)


def tokens_to_steps(x, n_batch, seq, ctx_len, n_heads):
    m, d = x.shape
    n = d // n_heads
    t = RELAYOUT_ROWS
    n_ctx_blk, n_lat_blk = ctx_len // t, seq // t

    def spec(b):
        return pl.BlockSpec((t, d), lambda s: (_step_block_row(s, b, n_ctx_blk, n_lat_blk, n_batch), 0))

    return pl.pallas_call(
        functools.partial(_to_steps_body, n_heads=n_heads),
        grid=(n_ctx_blk + n_lat_blk,),
        in_specs=[spec(b) for b in range(n_batch)],
        out_specs=pl.BlockSpec((t, n, n_batch * n_heads), lambda s: (s, 0, 0)),
        out_shape=jax.ShapeDtypeStruct((ctx_len + seq, n, n_batch * n_heads), x.dtype),
        compiler_params=_cparams(("parallel",)),
        name="tokens_to_steps",
    )(*([x] * n_batch))


def _to_tokens_body(y_ref, o_ref, *, n_heads, n_batch):
    y = y_ref[...]
    t, n = y.shape[0], y.shape[1]
    for b in range(n_batch):
        @pl.when(pl.program_id(1) == b)
        def _():
            yb = y[:, :, b * n_heads:(b + 1) * n_heads]
            o_ref[...] = jnp.swapaxes(yb, 1, 2).reshape(t, n_heads * n)


def steps_to_tokens(y, n_batch, seq, ctx_len, n_heads):
    steps, n, _ = y.shape
    d = n_heads * n
    t = RELAYOUT_ROWS
    n_ctx_blk, n_lat_blk = ctx_len // t, seq // t
    return pl.pallas_call(
        functools.partial(_to_tokens_body, n_heads=n_heads, n_batch=n_batch),
        grid=(n_ctx_blk + n_lat_blk, n_batch),
        in_specs=[pl.BlockSpec((t, n, n_batch * n_heads), lambda s, b: (s, 0, 0))],
        out_specs=pl.BlockSpec((t, d), lambda s, b: (_step_block_row(s, b, n_ctx_blk, n_lat_blk, n_batch), 0)),
        out_shape=jax.ShapeDtypeStruct((n_batch * (seq + ctx_len), d), y.dtype),
        compiler_params=_cparams(("parallel", "arbitrary")),
        name="steps_to_tokens",
    )(y)


def _pad_cols(w, mult=128):
    n = w.shape[-1]
    p = -(-n // mult) * mult - n
    return jnp.pad(w, [(0, 0)] * (w.ndim - 1) + [(0, p)]) if p else w


def _pad_rows(w, mult=128):
    n = w.shape[-2]
    p = -(-n // mult) * mult - n
    return jnp.pad(w, [(0, 0)] * (w.ndim - 2) + [(0, p), (0, 0)]) if p else w


def rwkv_mixer_tokens(h, mu, w_rkv, w0, w1, w2, a0, a1, a2, k_k, k_a, r_k, ln_g, ln_b, g1, g2,
                      tm, n_batch, seq, ctx_len):
    m, d = h.shape
    n_heads, n = r_k.shape
    n_lat = n_batch * seq
    ts = _row_tile(seq, ctx_len, cap=256)
    x_r, x_k, x_v, x_w, x_a, x_g = token_shift_mix(h, mu, ts, seq, ctx_len, n_batch)
    mm = functools.partial(matmul, tm=tm)
    r = mm(x_r, w_rkv, w_index=0)
    k = mm(x_k, w_rkv, w_index=1)
    v = mm(x_v, w_rkv, w_index=2)
    gate = mm(mm(x_g, _pad_cols(g1), act="sigmoid", out_dtype=BF16), _pad_rows(g2))
    dec, aa = [], []
    for dn in range(2):
        lw = mm(x_w, _pad_cols(w1[dn]), act="tanh", out_dtype=BF16)
        dec.append(mm(lw, _pad_rows(w2[dn]), bias=w0[dn], act="decay"))
        la = mm(x_a, _pad_cols(a1[dn]), out_dtype=BF16)
        aa.append(mm(la, _pad_rows(a2[dn]), bias=a0[dn], act="sigmoid"))

    to_seq = functools.partial(tokens_to_steps, n_batch=n_batch, seq=seq, ctx_len=ctx_len, n_heads=n_heads)
    r_t, k_t, v_t, g_t = to_seq(r), to_seq(k), to_seq(v), to_seq(gate)
    d0_t, d1_t, a0_t, a1_t = to_seq(dec[0]), to_seq(dec[1]), to_seq(aa[0]), to_seq(aa[1])

    def table(vec):
        return jnp.tile(vec.astype(F32).reshape(n_heads, n).T[:, None, :], (1, n_batch, 1)).reshape(n, n_batch * n_heads)

    kk_tab, ka_tab = table(k_k), table(k_a)
    o_f, o_r = rwkv_scan(r_t, k_t, v_t, d0_t, d1_t, a0_t, a1_t, jnp.concatenate([kk_tab, kk_tab], axis=-1),
                         jnp.concatenate([ka_tab, ka_tab], axis=-1), ctx_len)
    y_t = rwkv_post(o_f, o_r, r_t, k_t, v_t, a0_t, a1_t, g_t, ka_tab, table(r_k.reshape(-1)),
                    table(ln_g), table(ln_b))
    return steps_to_tokens(y_t, n_batch, seq, ctx_len, n_heads)


def kernel(x, c, ctx, c_ctx, mod_w, mod_b, norm_g, final_g, na_w_qkv, na_w_o, na_rpb, s5_lam_re, s5_lam_im, s5_log_dt, s5_b_re, s5_b_im, s5_c_re, s5_c_im, s5_d, s5_w_glu, rw_mu, rw_w_rkv, rw_w0, rw_w1, rw_w2, rw_a0, rw_a1, rw_a2, rw_k_k, rw_k_a, rw_r_k, rw_ln_g, rw_ln_b, rw_g1, rw_g2, rw_w_o, moe_wg, moe_bg, moe_we, moe_be, moe_w_gu, moe_w_down):
    n_batch, seq, d = x.shape
    ctx_len = ctx.shape[1]
    depth = mod_w.shape[0]
    n_lat = n_batch * seq
    n_ctx = n_batch * ctx_len
    tm = _row_tile(seq, n_ctx)
    tpb = seq // tm
    xs = jnp.concatenate([x.reshape(n_lat, d), ctx.reshape(n_ctx, d)], axis=0).astype(F32)

    cvecs = jnp.concatenate([c.astype(F32), c_ctx.astype(F32)[None]], axis=0)
    rows_pad = -(-(n_batch + 1) // 8) * 8
    cvecs = jnp.pad(cvecs, ((0, rows_pad - n_batch - 1), (0, 0)))
    mods = modulation(cvecs, mod_w, mod_b)[:, :n_batch + 1].reshape(depth, n_batch + 1, 6, 1, d)
    mods = mods.transpose(0, 2, 1, 3, 4)

    for i in range(depth):
        last = i == depth - 1
        sh1, sc1, g1, sh2, sc2, g2 = (mods[i, q] for q in range(6))
        pn = functools.partial(prenorm, tm=tm, tiles_per_batch=tpb, n_batch=n_batch)
        mm = functools.partial(matmul, tm=tm, tiles_per_batch=tpb, n_batch=n_batch)
        mix, j = i % N_MIXERS, i // N_MIXERS
        if mix == 0:
            (hb,) = pn(xs, norm_g[i, 0], 1.0 + sc1, sh1, out_dtypes=(BF16,))
            n_heads = na_rpb.shape[1]
            qscale = jnp.concatenate([jnp.full((d,), (d // n_heads) ** -0.5, F32), jnp.ones((2 * d,), F32)])
            qkv = mm(hb, na_w_qkv, w_index=j, out_dtype=BF16, colscale=qscale)
            kr = (na_rpb.shape[2] + 1) // 2
            o = na_attention(qkv, _na_bias_table(na_rpb[j], GRID_W, seq // GRID_W), kr, n_batch, seq, ctx_len,
                             not last)
            if last:
                xs = xs[:n_lat]
            xs = mm(o, na_w_o, w_index=j, mode="resid", resid=xs, gate=g1)
        elif mix == 1:
            (hf,) = pn(xs, norm_g[i, 0], 1.0 + sc1, sh1, out_dtypes=(F32,))
            wts = _s5_weights(s5_lam_re[j], s5_lam_im[j], s5_log_dt[j], s5_b_re[j], s5_b_im[j],
                              s5_c_re[j], s5_c_im[j], S5_CHUNK)
            y = s5_core(hf, wts, s5_d[j], n_batch, seq, ctx_len)
            if last:
                xs, y = xs[:n_lat], y[:n_lat]
            xs = mm(y, s5_w_glu, w_index=j, mode="glu", resid=xs, gate=g1)
        else:
            (hf,) = pn(xs, norm_g[i, 0], 1.0 + sc1, sh1, out_dtypes=(F32,))
            y = rwkv_mixer_tokens(hf, rw_mu[j], rw_w_rkv[j], rw_w0[j], rw_w1[j], rw_w2[j], rw_a0[j], rw_a1[j],
                                  rw_a2[j], rw_k_k[j], rw_k_a[j], rw_r_k[j], rw_ln_g[j], rw_ln_b[j],
                                  rw_g1[j], rw_g2[j], tm, n_batch, seq, ctx_len)
            if last:
                xs, y = xs[:n_lat], y[:n_lat]
            xs = mm(y, rw_w_o, w_index=j, mode="resid", resid=xs, gate=g1)
        (h2,) = pn(xs, norm_g[i, 1], 1.0 + sc2, sh2, out_dtypes=(F32,))
        xs = hier_moe(xs, h2, moe_wg[i], moe_bg[i], moe_we[i], moe_be[i], moe_w_gu, moe_w_down, i, g2,
                      tm, n_batch, seq)
    ones = jnp.ones((n_batch + 1, 1, d), F32)
    (out,) = prenorm(xs[:n_lat], final_g, ones, jnp.zeros_like(ones), tm, tpb, n_batch, out_dtypes=(x.dtype,))
    return out.reshape(n_batch, seq, d)
```

```python
import functools
import math

import jax
import jax.numpy as jnp
from jax import lax
from jax.experimental import pallas as pl
from jax.experimental.pallas import tpu as pltpu

F32 = jnp.float32
BF16 = jnp.bfloat16

GRID_W = 64
N_MIXERS = 3
NORM_EPS = 1e-6
RW_GN_EPS = 64e-5
MOE_TOP_K = 2
MOE_BLOCK = 128
MOE_COMBINE_ROWS = 128
S5_CHUNK = 16
SCAN_KEY_BLOCK = 16
LANES = 128
RELAYOUT_ROWS = 128
NA_Q_ROWS = 4
NA_KEY_ROWS = 12
NEG_BIG = -1e30
VMEM_LIMIT_BYTES = 48 * 1024 * 1024
VMEM_LIMIT_BIG_BYTES = 56 * 1024 * 1024


def _cparams(sem, limit=VMEM_LIMIT_BYTES):
    return pltpu.CompilerParams(dimension_semantics=sem, vmem_limit_bytes=limit)


def _row_tile(n_lat_per_batch, n_ctx_total, cap=512):
    t = cap
    while t > 8 and (n_lat_per_batch % t or (n_ctx_total and n_ctx_total % t)):
        t //= 2
    return t


def _seg_fn(tiles_per_batch, n_batch):
    def seg(i):
        return jnp.minimum(i // tiles_per_batch, n_batch)
    return seg


def _bdot(a, b):
    return jnp.dot(a.astype(BF16), b.astype(BF16), preferred_element_type=F32)


def _split3(x):
    hi = x.astype(BF16)
    r1 = x - hi.astype(F32)
    mid = r1.astype(BF16)
    lo = (r1 - mid.astype(F32)).astype(BF16)
    return hi, mid, lo


def _prenorm_body(x_ref, g_ref, sc_ref, sh_ref, *o_refs):
    x = x_ref[...]
    ms = jnp.mean(x * x, axis=-1, keepdims=True)
    h = (x * lax.rsqrt(ms + NORM_EPS)) * g_ref[...]
    h = h * sc_ref[0] + sh_ref[0]
    for o_ref in o_refs:
        o_ref[...] = h.astype(o_ref.dtype)


def prenorm(x, g, scale1p, shift, tm, tiles_per_batch, n_batch, out_dtypes):
    m, d = x.shape
    seg = _seg_fn(tiles_per_batch, n_batch)
    row = pl.BlockSpec((tm, d), lambda i: (i, 0))
    tab = pl.BlockSpec((1, 1, d), lambda i: (seg(i), 0, 0))
    outs = pl.pallas_call(
        _prenorm_body,
        grid=(m // tm,),
        in_specs=[row, pl.BlockSpec((1, d), lambda i: (0, 0)), tab, tab],
        out_specs=[row for _ in out_dtypes],
        out_shape=[jax.ShapeDtypeStruct((m, d), dt) for dt in out_dtypes],
        compiler_params=_cparams(("parallel",)),
    )(x, g.reshape(1, d), scale1p, shift)
    return outs


def _softplus(z):
    return jnp.maximum(z, 0.0) + jnp.log(1.0 + jnp.exp(-jnp.abs(z)))


def _apply_act(y, act):
    if act is None:
        return y
    if act == "tanh":
        return jnp.tanh(y)
    if act == "sigmoid":
        return jax.nn.sigmoid(y)
    if act == "decay":
        return jnp.exp(-jnp.exp(-_softplus(-y) - 0.5))
    raise ValueError(act)


def _mm_body(*refs, mode, act, has_bias, has_scale, has_mask):
    it = iter(refs)
    x_ref = next(it)
    w_ref = next(it)
    w2_ref = next(it) if mode == "glu" else None
    b_ref = next(it) if has_bias else None
    s_ref = next(it) if has_scale else None
    m_ref = next(it) if has_mask else None
    r_ref = next(it) if mode in ("resid", "glu") else None
    g_ref = next(it) if mode in ("resid", "glu") else None
    o_ref = next(it)
    wb_ref = next(it)
    wb2_ref = next(it) if mode == "glu" else None

    @pl.when(pl.program_id(1) == 0)
    def _():
        wb_ref[...] = w_ref[...].astype(BF16)
        if mode == "glu":
            wb2_ref[...] = w2_ref[...].astype(BF16)

    x = x_ref[...].astype(BF16)
    acc = jnp.dot(x, wb_ref[...], preferred_element_type=F32)
    if mode == "glu":
        acc2 = jnp.dot(x, wb2_ref[...], preferred_element_type=F32)
        acc = acc * jax.nn.sigmoid(acc2)
    if has_bias:
        acc = acc + b_ref[...]
    acc = _apply_act(acc, act)
    if has_scale:
        acc = acc * s_ref[...]
    if has_mask:
        acc = acc * m_ref[...]
    if mode in ("resid", "glu"):
        acc = r_ref[...] + g_ref[0] * acc
    o_ref[...] = acc.astype(o_ref.dtype)


def matmul(x, w, *, tm, tn=512, out_dtype=F32, mode="plain", act=None, bias=None, colscale=None,
           resid=None, gate=None, tiles_per_batch=None, n_batch=None, w_index=0, tilemask=None):
    m, k = x.shape
    if w.ndim == 2:
        w, w_index = w[None], 0
    n = w.shape[2] // 2 if mode == "glu" else w.shape[2]
    tn = min(tn, n)
    assert m % tm == 0 and n % tn == 0, (m, tm, n, tn)
    nj = n // tn
    in_specs = [pl.BlockSpec((tm, k), lambda j, i: (i, 0)),
                pl.BlockSpec((None, k, tn), lambda j, i: (w_index, 0, j))]
    args = [x, w]
    if mode == "glu":
        in_specs.append(pl.BlockSpec((None, k, tn), lambda j, i: (w_index, 0, j + nj)))
        args.append(w)
    col = pl.BlockSpec((1, tn), lambda j, i: (0, j))
    if bias is not None:
        in_specs.append(col)
        args.append(bias.reshape(1, n).astype(F32))
    if colscale is not None:
        in_specs.append(col)
        args.append(colscale.reshape(1, n).astype(F32))
    if tilemask is not None:
        assert tilemask.shape == (tm, tn), (tilemask.shape, tm, tn)
        in_specs.append(pl.BlockSpec((tm, tn), lambda j, i: (0, 0)))
        args.append(tilemask.astype(F32))
    if mode in ("resid", "glu"):
        seg = _seg_fn(tiles_per_batch, n_batch)
        in_specs.append(pl.BlockSpec((tm, tn), lambda j, i: (i, j)))
        in_specs.append(pl.BlockSpec((1, 1, tn), lambda j, i: (seg(i), 0, j)))
        args += [resid, gate]
    return pl.pallas_call(
        functools.partial(_mm_body, mode=mode, act=act, has_bias=bias is not None,
                          has_scale=colscale is not None, has_mask=tilemask is not None),
        grid=(nj, m // tm),
        in_specs=in_specs,
        out_specs=pl.BlockSpec((tm, tn), lambda j, i: (i, j)),
        out_shape=jax.ShapeDtypeStruct((m, n), out_dtype),
        scratch_shapes=[pltpu.VMEM((k, tn), BF16)] * (2 if mode == "glu" else 1),
        compiler_params=_cparams(("parallel", "arbitrary")),
    )(*args)


def _router_body(x_ref, w_ref, b_ref, o_ref, *, n_grp, epg):
    xh, xm, xl = _split3(x_ref[...])
    wh, wm, wl = _split3(w_ref[...])
    dot = functools.partial(jnp.dot, preferred_element_type=F32)
    acc = dot(xh, wh) + (dot(xh, wm) + dot(xm, wh)) + (dot(xh, wl) + dot(xm, wm) + dot(xl, wh))
    acc = acc + b_ref[...]
    lane = lax.broadcasted_iota(jnp.int32, acc.shape, 1)
    lane_f = lane.astype(F32)
    far = 1e9

    def first_max(vals):
        m = jnp.max(vals, axis=-1, keepdims=True)
        return m, jnp.min(jnp.where(vals == m, lane_f, far), axis=-1, keepdims=True)

    gl = jnp.where(lane < n_grp, acc, NEG_BIG)
    gmax, grp = first_max(gl)
    p_grp = 1.0 / jnp.sum(jnp.exp(gl - gmax), axis=-1, keepdims=True)
    lo = n_grp + grp * epg
    el = jnp.where((lane_f >= lo) & (lane_f < lo + epg), acc, NEG_BIG)
    m1, i1 = first_max(el)
    m2, i2 = first_max(jnp.where(lane_f == i1, NEG_BIG, el))
    e21 = jnp.exp(m2 - m1)
    g1 = p_grp / (1.0 + e21)
    g2 = p_grp * e21 / (1.0 + e21)
    out = jnp.where(lane == 0, i1 - n_grp, jnp.where(lane == 1, i2 - n_grp,
                                                    jnp.where(lane == 2, g1, jnp.where(lane == 3, g2, 0.0))))
    o_ref[...] = out


def router(x, w, bias, tm, n_grp, epg):
    m, k = x.shape
    n = w.shape[1]
    return pl.pallas_call(
        functools.partial(_router_body, n_grp=n_grp, epg=epg),
        grid=(m // tm,),
        in_specs=[pl.BlockSpec((tm, k), lambda i: (i, 0)), pl.BlockSpec((k, n), lambda i: (0, 0)),
                  pl.BlockSpec((1, n), lambda i: (0, 0))],
        out_specs=pl.BlockSpec((tm, n), lambda i: (i, 0)),
        out_shape=jax.ShapeDtypeStruct((m, n), F32),
        compiler_params=_cparams(("parallel",)),
        name="moe_router",
    )(x, w, bias.reshape(1, n))


def _mod_body(s_ref, w_ref, b_ref, o_ref):
    s = s_ref[...]
    s = s * jax.nn.sigmoid(s)
    o_ref[0] = _bdot(s, w_ref[0]) + b_ref[0]


def modulation(cvecs, mod_w, mod_b, tn=768):
    depth, d, n = mod_w.shape
    rows = cvecs.shape[0]
    while n % tn:
        tn //= 2
    return pl.pallas_call(
        _mod_body,
        grid=(depth, n // tn),
        in_specs=[pl.BlockSpec((rows, d), lambda l, j: (0, 0)),
                  pl.BlockSpec((1, d, tn), lambda l, j: (l, 0, j)),
                  pl.BlockSpec((1, 1, tn), lambda l, j: (l, 0, j))],
        out_specs=pl.BlockSpec((1, rows, tn), lambda l, j: (l, 0, j)),
        out_shape=jax.ShapeDtypeStruct((depth, rows, n), F32),
        compiler_params=_cparams(("parallel", "parallel")),
    )(cvecs, mod_w, mod_b.reshape(depth, 1, n))


def _na_key_start(i, rows, kr):
    return jnp.clip(i * NA_Q_ROWS - kr // 2, 0, rows - NA_KEY_ROWS)


def _na_bias_table(rpb, grid_w, rows):
    n_heads = rpb.shape[0]
    kr = (rpb.shape[1] + 1) // 2
    kcw = (rpb.shape[2] + 1) // 2
    n_blk = rows // NA_Q_ROWS
    q = jnp.arange(grid_w)[:, None]
    kc = jnp.arange(grid_w)[None, :]
    win = jnp.clip(q - kcw // 2, 0, grid_w - kcw)
    col_ok = (kc >= win) & (kc < win + kcw)
    dc = jnp.clip(kc - q, 1 - kcw, kcw - 1) + kcw - 1
    blk = jnp.array([0, min(1, n_blk - 1), n_blk - 1])[:, None, None]
    r = blk * NA_Q_ROWS + jnp.arange(NA_Q_ROWS)[None, :, None]
    krow = _na_key_start(blk, rows, kr) + jnp.arange(NA_KEY_ROWS)[None, None, :]
    rs = jnp.clip(r - kr // 2, 0, rows - kr)
    row_ok = (krow >= rs) & (krow < rs + kr)
    dr = jnp.clip(krow - r + kr - 1, 0, 2 * kr - 2)
    hi = lax.Precision.HIGHEST
    sel_r = jax.nn.one_hot(dr.reshape(-1), rpb.shape[1], dtype=F32)
    sel_c = jax.nn.one_hot(dc.reshape(-1), rpb.shape[2], dtype=F32)
    tab = jnp.einsum("ra,has->hrs", sel_r, jnp.einsum("hab,sb->has", rpb.astype(F32), sel_c, precision=hi),
                     precision=hi)
    tab = tab.reshape(n_heads, 3, NA_Q_ROWS, NA_KEY_ROWS, grid_w, grid_w)
    ok = row_ok[None, :, :, :, None, None] & col_ok[None, None, None, None]
    tab = jnp.where(ok, tab, NEG_BIG)
    return tab.transpose(0, 1, 2, 4, 3, 5).reshape(n_heads, 3, NA_Q_ROWS * grid_w, NA_KEY_ROWS * grid_w)


def _na_body(q_ref, k_ref, v_ref, kc_ref, vc_ref, bias_ref, o_ref, *, hpb, dh, grid_w, kr, rows):
    ks = _na_key_start(pl.program_id(2), rows, kr)
    start = pl.multiple_of(ks * grid_w, grid_w)
    nt = (((1,), (1,)), ((), ()))
    for h in range(hpb):
        sl = slice(h * dh, (h + 1) * dh)
        q = q_ref[:, sl]
        k = k_ref[pl.ds(start, NA_KEY_ROWS * grid_w), sl]
        v = v_ref[pl.ds(start, NA_KEY_ROWS * grid_w), sl]
        s = lax.dot_general(q, k, nt, preferred_element_type=F32) + bias_ref[h, 0]
        sc = lax.dot_general(q, kc_ref[:, sl], nt, preferred_element_type=F32)
        m = jnp.maximum(jnp.max(s, axis=-1, keepdims=True), jnp.max(sc, axis=-1, keepdims=True))
        p = jnp.exp(s - m)
        pc = jnp.exp(sc - m)
        den = jnp.sum(p, axis=-1, keepdims=True) + jnp.sum(pc, axis=-1, keepdims=True)
        o = (jnp.dot(p.astype(BF16), v, preferred_element_type=F32)
             + jnp.dot(pc.astype(BF16), vc_ref[:, sl], preferred_element_type=F32))
        o_ref[:, sl] = (o / den).astype(o_ref.dtype)


def _ctx_attn_body(q_ref, k_ref, v_ref, o_ref, *, hpb, dh):
    nt = (((1,), (1,)), ((), ()))
    for h in range(hpb):
        sl = slice(h * dh, (h + 1) * dh)
        s = lax.dot_general(q_ref[:, sl], k_ref[:, sl], nt, preferred_element_type=F32)
        m = jnp.max(s, axis=-1, keepdims=True)
        p = jnp.exp(s - m)
        den = jnp.sum(p, axis=-1, keepdims=True)
        o = jnp.dot(p.astype(BF16), v_ref[:, sl], preferred_element_type=F32)
        o_ref[:, sl] = (o / den).astype(o_ref.dtype)


def na_attention(qkv, bias_tab, kr, n_batch, seq, ctx_len, need_ctx):
    d = qkv.shape[1] // 3
    n_heads = bias_tab.shape[0]
    dh = d // n_heads
    hpb = max(1, min(n_heads, 256 // dh))
    bw = hpb * dh
    n_d = d // bw
    rows = seq // GRID_W
    n_blk = rows // NA_Q_ROWS
    qb = NA_Q_ROWS * GRID_W
    n_lat = n_batch * seq
    ctx_blk0 = n_lat // ctx_len

    def cls_of(i):
        return jnp.where(i == 0, 0, jnp.where(i == n_blk - 1, 2, 1))

    o_lat = pl.pallas_call(
        functools.partial(_na_body, hpb=hpb, dh=dh, grid_w=GRID_W, kr=kr, rows=rows),
        grid=(n_batch, n_d, n_blk),
        in_specs=[
            pl.BlockSpec((qb, bw), lambda b, g, i: (b * n_blk + i, g)),
            pl.BlockSpec((seq, bw), lambda b, g, i: (b, n_d + g)),
            pl.BlockSpec((seq, bw), lambda b, g, i: (b, 2 * n_d + g)),
            pl.BlockSpec((ctx_len, bw), lambda b, g, i: (ctx_blk0 + b, n_d + g)),
            pl.BlockSpec((ctx_len, bw), lambda b, g, i: (ctx_blk0 + b, 2 * n_d + g)),
            pl.BlockSpec((hpb, 1, qb, NA_KEY_ROWS * GRID_W), lambda b, g, i: (g, cls_of(i), 0, 0)),
        ],
        out_specs=pl.BlockSpec((qb, bw), lambda b, g, i: (b * n_blk + i, g)),
        out_shape=jax.ShapeDtypeStruct((n_lat, d), BF16),
        compiler_params=_cparams(("parallel", "parallel", "arbitrary")),
        name="na_attention",
    )(qkv, qkv, qkv, qkv, qkv, bias_tab)
    if not need_ctx:
        return o_lat
    o_ctx = pl.pallas_call(
        functools.partial(_ctx_attn_body, hpb=hpb, dh=dh),
        grid=(n_batch, n_d),
        in_specs=[
            pl.BlockSpec((ctx_len, bw), lambda b, g: (ctx_blk0 + b, g)),
            pl.BlockSpec((ctx_len, bw), lambda b, g: (ctx_blk0 + b, n_d + g)),
            pl.BlockSpec((ctx_len, bw), lambda b, g: (ctx_blk0 + b, 2 * n_d + g)),
        ],
        out_specs=pl.BlockSpec((ctx_len, bw), lambda b, g: (b, g)),
        out_shape=jax.ShapeDtypeStruct((n_batch * ctx_len, d), BF16),
        compiler_params=_cparams(("parallel", "parallel")),
    )(qkv, qkv, qkv)
    return jnp.concatenate([o_lat, o_ctx], axis=0)


def _expert_body(be_ref, idx_ref, idx_next_ref, x_hbm, wgu_ref, wd_ref, o_ref, xbuf, sem, wgu_s, wd_s, *,
                 hidden, n_blocks):
    j = pl.program_id(0)
    cur = j % 2

    def row_copy(idx, buf, r):
        return pltpu.make_async_copy(x_hbm.at[pl.ds(idx[0, 0, r], 1)], xbuf.at[buf, pl.ds(r, 1)], sem.at[buf])

    @pl.when(j == 0)
    def _():
        for r in range(MOE_BLOCK):
            row_copy(idx_ref, 0, r).start()

    prev = be_ref[jnp.maximum(j - 1, 0)]
    changed = jnp.logical_or(j == 0, be_ref[j] != prev)

    @pl.when(changed)
    def _():
        wgu_s[...] = wgu_ref[...].astype(BF16)
        wd_s[...] = wd_ref[...].astype(BF16)

    for r in range(MOE_BLOCK):
        row_copy(idx_ref, cur, r).wait()

    for r in range(MOE_BLOCK):
        row_copy(idx_next_ref, 1 - cur, r).start()
    gu = jnp.dot(xbuf[cur].astype(BF16), wgu_s[...], preferred_element_type=F32)
    g = gu[:, :hidden]
    u = gu[:, hidden:]
    a = (g * jax.nn.sigmoid(g) * u).astype(BF16)
    o_ref[...] = jnp.dot(a, wd_s[...], preferred_element_type=F32)

    @pl.when(j == n_blocks - 1)
    def _():
        for r in range(MOE_BLOCK):
            row_copy(idx_next_ref, 1 - cur, r).wait()


def expert_blocks(x, slot_tok, blk_e, w_gu, w_down, layer):
    d = x.shape[1]
    hidden = w_down.shape[2]
    n_blocks = blk_e.shape[0]
    idx = slot_tok.reshape(n_blocks, 1, MOE_BLOCK)
    return pl.pallas_call(
        functools.partial(_expert_body, hidden=hidden, n_blocks=n_blocks),
        grid_spec=pltpu.PrefetchScalarGridSpec(
            num_scalar_prefetch=1,
            grid=(n_blocks,),
            in_specs=[pl.BlockSpec((1, 1, MOE_BLOCK), lambda j, be: (j, 0, 0), memory_space=pltpu.SMEM),
                      pl.BlockSpec((1, 1, MOE_BLOCK), lambda j, be: (jnp.minimum(j + 1, n_blocks - 1), 0, 0),
                                   memory_space=pltpu.SMEM),
                      pl.BlockSpec(memory_space=pl.ANY),
                      pl.BlockSpec((None, None, d, 2 * hidden), lambda j, be: (layer, be[j], 0, 0)),
                      pl.BlockSpec((None, None, hidden, d), lambda j, be: (layer, be[j], 0, 0))],
            out_specs=pl.BlockSpec((MOE_BLOCK, d), lambda j, be: (j, 0)),
            scratch_shapes=[pltpu.VMEM((2, MOE_BLOCK, d), F32), pltpu.SemaphoreType.DMA((2,)),
                            pltpu.VMEM((d, 2 * hidden), BF16), pltpu.VMEM((hidden, d), BF16)]),
        out_shape=jax.ShapeDtypeStruct((n_blocks * MOE_BLOCK, d), F32),
        compiler_params=_cparams(("arbitrary",), VMEM_LIMIT_BIG_BYTES),
        name="moe_experts",
    )(blk_e, idx, idx, x, w_gu, w_down)


def _combine_body(idx_ref, idx_next_ref, route_ref, x_ref, g_ref, yb_hbm, o_ref, ybuf, sem, *, tmc, n_tiles):
    i = pl.program_id(0)
    cur = i % 2

    def row_copy(idx, buf, k, r):
        return pltpu.make_async_copy(yb_hbm.at[pl.ds(idx[0, 0, k * tmc + r], 1)],
                                     ybuf.at[buf, k, pl.ds(r, 1)], sem.at[buf])

    def start_all(idx, buf):
        for k in range(MOE_TOP_K):
            for r in range(tmc):
                row_copy(idx, buf, k, r).start()

    @pl.when(i == 0)
    def _():
        start_all(idx_ref, 0)

    @pl.when(i + 1 < n_tiles)
    def _():
        start_all(idx_next_ref, 1 - cur)

    for k in range(MOE_TOP_K):
        for r in range(tmc):
            row_copy(idx_ref, cur, k, r).wait()
    route = route_ref[...]
    f = ybuf[cur, 0] * route[:, MOE_TOP_K:MOE_TOP_K + 1]
    for k in range(1, MOE_TOP_K):
        f = f + ybuf[cur, k] * route[:, MOE_TOP_K + k:MOE_TOP_K + k + 1]
    o_ref[...] = x_ref[...] + g_ref[0] * f


def moe_combine(xs, yb, slot_of_asg, route, gate_tab, n_batch, seq):
    n_tok, d = xs.shape
    tmc = MOE_COMBINE_ROWS
    n_tiles = n_tok // tmc
    idx = slot_of_asg.reshape(n_tiles, tmc, MOE_TOP_K).transpose(0, 2, 1).reshape(n_tiles, 1, MOE_TOP_K * tmc)
    seg = _seg_fn(seq // tmc, n_batch)
    return pl.pallas_call(
        functools.partial(_combine_body, tmc=tmc, n_tiles=n_tiles),
        grid=(n_tiles,),
        in_specs=[pl.BlockSpec((1, 1, MOE_TOP_K * tmc), lambda i: (i, 0, 0), memory_space=pltpu.SMEM),
                  pl.BlockSpec((1, 1, MOE_TOP_K * tmc), lambda i: (jnp.minimum(i + 1, n_tiles - 1), 0, 0),
                               memory_space=pltpu.SMEM),
                  pl.BlockSpec((tmc, route.shape[1]), lambda i: (i, 0)),
                  pl.BlockSpec((tmc, d), lambda i: (i, 0)),
                  pl.BlockSpec((1, 1, d), lambda i: (seg(i), 0, 0)),
                  pl.BlockSpec(memory_space=pl.ANY)],
        out_specs=pl.BlockSpec((tmc, d), lambda i: (i, 0)),
        out_shape=jax.ShapeDtypeStruct((n_tok, d), F32),
        scratch_shapes=[pltpu.VMEM((2, MOE_TOP_K, tmc, d), F32), pltpu.SemaphoreType.DMA((2,))],
        compiler_params=_cparams(("arbitrary",)),
        name="moe_combine",
    )(idx, idx, route, xs, gate_tab, yb)


def hier_moe(xs, h, wg, bg, we, be, w_gu, w_down, layer, gate_tab, tm, n_batch, seq):
    n_tok, d = h.shape
    n_grp = wg.shape[1]
    n_exp = we.shape[1]
    epg = n_exp // n_grp
    n_logit = n_grp + n_exp
    n_pad = -(-n_logit // 128) * 128
    wcat = jnp.pad(jnp.concatenate([wg, we], axis=1), ((0, 0), (0, n_pad - n_logit)))
    bcat = jnp.pad(jnp.concatenate([bg, be], axis=0), (0, n_pad - n_logit))
    route = router(h, wcat, bcat, tm, n_grp, epg)

    flat_e = route[:, :MOE_TOP_K].astype(jnp.int32).reshape(-1)
    n_asg = flat_e.shape[0]
    onehot = (flat_e[:, None] == jnp.arange(n_exp, dtype=jnp.int32)[None, :]).astype(F32)
    ck = 128
    oh3 = onehot.reshape(n_asg // ck, ck, n_exp)
    tri = jnp.tril(jnp.ones((ck, ck), F32))
    within = jnp.einsum("ij,cjk->cik", tri, oh3, precision=lax.Precision.HIGHEST)
    tot = within[:, -1, :]
    before = jnp.cumsum(tot, axis=0) - tot
    rank = jnp.sum(oh3 * (within + before[:, None, :]), axis=-1).reshape(n_asg).astype(jnp.int32) - 1
    counts = jnp.sum(tot, axis=0).astype(jnp.int32)
    padded = (counts + MOE_BLOCK - 1) // MOE_BLOCK * MOE_BLOCK
    pad_end = jnp.cumsum(padded)
    pad_start = pad_end - padded
    slot = jnp.sum(onehot * pad_start.astype(F32)[None, :], axis=-1).astype(jnp.int32) + rank
    n_blocks = -(-n_asg // MOE_BLOCK) + n_exp
    slot_tok = jnp.zeros((n_blocks * MOE_BLOCK,), jnp.int32).at[slot].set(
        jnp.arange(n_asg, dtype=jnp.int32) // MOE_TOP_K)
    blk_start = jnp.arange(n_blocks, dtype=jnp.int32) * MOE_BLOCK
    blk_e = jnp.minimum(jnp.sum((pad_end[None, :] <= blk_start[:, None]).astype(jnp.int32), axis=1), n_exp - 1)
    yb = expert_blocks(h, slot_tok, blk_e, w_gu, w_down, layer)
    return moe_combine(xs, yb, slot.reshape(n_tok, MOE_TOP_K), route, gate_tab, n_batch, seq)


def _s5_weights(lam_re, lam_im, log_dt, b_re, b_im, c_re, c_im, t_chunk):
    n_grp, n_state = lam_re.shape[1], lam_re.shape[2]
    cg = b_re.shape[-1]
    lam = lax.complex(lam_re.astype(F32), lam_im.astype(F32))
    dt = jnp.exp(log_dt.astype(F32))[..., None]
    lam_bar = jnp.exp(lam * dt)
    b_bar = ((lam_bar - 1.0) / lam)[..., None] * lax.complex(b_re.astype(F32), b_im.astype(F32))
    c_mat = lax.complex(c_re.astype(F32), c_im.astype(F32))
    ks = jnp.arange(t_chunk + 1, dtype=F32)
    pw = jnp.exp((lam * dt)[..., None] * ks)
    hi = lax.Precision.HIGHEST
    kern = jnp.einsum("dgcp,dgpk,dgpi->dgkci", c_mat, pw[..., :t_chunk], b_bar, precision=hi).real
    t_idx = jnp.arange(t_chunk)
    lag = t_idx[None, :] - t_idx[:, None]
    k_f = kern[0][:, jnp.clip(lag, 0, t_chunk - 1)]
    k_r = kern[1][:, jnp.clip(-lag, 0, t_chunk - 1)]
    toep = (jnp.where((lag >= 0)[None, :, :, None, None], k_f, 0.0)
            + jnp.where((lag <= 0)[None, :, :, None, None], k_r, 0.0))
    toep = toep.transpose(0, 1, 4, 2, 3).reshape(n_grp, t_chunk * cg, t_chunk * cg)
    pf = pw[0][..., :t_chunk][..., ::-1]
    pr = pw[1][..., :t_chunk]
    bend_f = jnp.einsum("gps,gpc->gscp", pf, b_bar[0]).reshape(n_grp, t_chunk * cg, n_state)
    bend_r = jnp.einsum("gps,gpc->gscp", pr, b_bar[1]).reshape(n_grp, t_chunk * cg, n_state)
    bend = jnp.stack([bend_f.real, bend_f.imag, bend_r.real, bend_r.imag], axis=1)
    of = jnp.einsum("gcp,gpt->gptc", c_mat[0], pw[0][..., 1:]).reshape(n_grp, n_state, t_chunk * cg)
    orv = jnp.einsum("gcp,gpt->gptc", c_mat[1], pw[1][..., 1:][..., ::-1]).reshape(n_grp, n_state, t_chunk * cg)
    cout = jnp.stack([of.real, -of.imag, orv.real, -orv.imag], axis=1)

    gsz = LANES // cg
    nb = n_grp // gsz
    tile = 512

    def expand(n_outer, n_inner):
        dst = jnp.arange(n_outer * gsz * n_inner)
        src = (dst // (gsz * n_inner)) * n_inner + dst % n_inner
        return (jnp.arange(n_outer * n_inner)[:, None] == src[None, :]).astype(F32)

    def grp(period, width):
        return (jnp.arange(tile) % period) // width

    def own_group(row_grp, col_grp):
        return (row_grp[:, None] == col_grp[None, :]).astype(F32)

    e_tc, e_xp = expand(t_chunk, cg), expand(4, n_state)
    grp_ch, grp_st = grp(LANES, cg), grp(gsz * n_state, n_state)
    kin = t_chunk * cg
    xt = toep.reshape(nb, gsz, t_chunk, cg, kin).transpose(0, 2, 1, 3, 4).reshape(n_grp * kin, kin)
    wt = matmul(xt, e_tc, tm=tile, tn=tile, out_dtype=BF16, tilemask=own_group(grp_ch, grp_ch))
    wt = wt.reshape(nb, t_chunk, LANES, t_chunk * LANES)
    xb = bend.reshape(nb, gsz, 4, t_chunk, cg, n_state).transpose(0, 3, 1, 4, 2, 5).reshape(n_grp * kin, 4 * n_state)
    wb = matmul(xb, e_xp, tm=tile, tn=tile, out_dtype=BF16, tilemask=own_group(grp_ch, grp_st))
    wb = wb.reshape(nb, t_chunk, LANES, 4 * gsz * n_state)
    xc = cout.reshape(nb, gsz, 4, n_state, kin).transpose(0, 2, 1, 3, 4).reshape(4 * n_grp * n_state, kin)
    wc = matmul(xc, e_tc, tm=tile, tn=tile, out_dtype=BF16, tilemask=own_group(grp_st, grp_ch))
    wc = wc.reshape(nb, 4 * gsz * n_state, t_chunk * LANES)
    lam_t = pw[..., t_chunk]
    lam_t4 = jnp.stack([lam_t[0].real, lam_t[0].imag, lam_t[1].real, lam_t[1].imag]).reshape(4, n_grp * n_state)
    return wt, wb, wc, lam_t4


def _s5_a_body(h_ref, wb_ref, e_ref, *, t_chunk):
    acc = jnp.dot(h_ref[:, 0, :].astype(BF16), wb_ref[0], preferred_element_type=F32)
    for s in range(1, t_chunk):
        acc = acc + jnp.dot(h_ref[:, s, :].astype(BF16), wb_ref[s], preferred_element_type=F32)
    e_ref[...] = acc


def _s5_b_body(e_ref, lam_ref, x_ref, *, n_batch, n_lat_chunk, n_ctx_chunk, width):
    ctx0 = n_batch * n_lat_chunk
    n_chunk = n_lat_chunk + n_ctx_chunk

    def comp(x):
        return slice(x * width, (x + 1) * width)

    lfr, lfi, lrr, lri = (lam_ref[:, comp(x)] for x in range(4))

    for b in range(n_batch):
        def fwd(n, carry):
            sr, si = carry
            row = jnp.where(n < n_ctx_chunk, ctx0 + b * n_ctx_chunk + n, b * n_lat_chunk + n - n_ctx_chunk)
            x_ref[pl.ds(row, 1), comp(0)] = sr
            x_ref[pl.ds(row, 1), comp(1)] = si
            er = e_ref[pl.ds(row, 1), comp(0)]
            ei = e_ref[pl.ds(row, 1), comp(1)]
            return lfr * sr - lfi * si + er, lfr * si + lfi * sr + ei

        def rev(n, carry):
            sr, si = carry
            row = jnp.where(n < n_ctx_chunk, ctx0 + b * n_ctx_chunk + n_ctx_chunk - 1 - n,
                            b * n_lat_chunk + n_chunk - 1 - n)
            x_ref[pl.ds(row, 1), comp(2)] = sr
            x_ref[pl.ds(row, 1), comp(3)] = si
            er = e_ref[pl.ds(row, 1), comp(2)]
            ei = e_ref[pl.ds(row, 1), comp(3)]
            return lrr * sr - lri * si + er, lrr * si + lri * sr + ei

        z = jnp.zeros((1, width), F32)
        lax.fori_loop(0, n_chunk, fwd, (z, z))
        lax.fori_loop(0, n_chunk, rev, (z, z))


def _gelu(y):
    return 0.5 * y * (1.0 + jnp.tanh(0.7978845608028654 * (y + 0.044715 * (y * y * y))))


def _s5_c_body(h_ref, x_ref, wt_ref, wc_ref, dsk_ref, o_ref, *, t_chunk):
    acc = jnp.dot(x_ref[...].astype(BF16), wc_ref[...], preferred_element_type=F32)
    for s in range(t_chunk):
        acc = acc + jnp.dot(h_ref[:, s, :].astype(BF16), wt_ref[s], preferred_element_type=F32)
    dsk = dsk_ref[...]
    for t in range(t_chunk):
        o_ref[:, t, :] = _gelu(acc[:, t * LANES:(t + 1) * LANES] + h_ref[:, t, :] * dsk)


def s5_core(h_f32, weights, d_skip, n_batch, seq, ctx_len):
    wt, wb, wc, lam_t4 = weights
    t = S5_CHUNK
    m, d = h_f32.shape
    nb = wt.shape[0]
    sw = wb.shape[-1]
    width = sw // 4
    rows = m // t
    rbs = rows // 2 if rows % 16 == 0 else rows
    h3 = h_f32.reshape(rows, t, d)
    hblk = pl.BlockSpec((rbs, t, LANES), lambda g, i: (i, 0, g))
    sblk = pl.BlockSpec((rbs, sw), lambda g, i: (i, g))
    e = pl.pallas_call(
        functools.partial(_s5_a_body, t_chunk=t),
        grid=(nb, rows // rbs),
        in_specs=[hblk, pl.BlockSpec((None, t, LANES, sw), lambda g, i: (g, 0, 0, 0))],
        out_specs=sblk,
        out_shape=jax.ShapeDtypeStruct((rows, nb * sw), F32),
        compiler_params=_cparams(("parallel", "arbitrary")),
        name="s5_chunk_states",
    )(h3, wb)

    lam_blk = lam_t4.reshape(4, nb, width).transpose(1, 0, 2).reshape(nb, 1, sw)
    x_in = pl.pallas_call(
        functools.partial(_s5_b_body, n_batch=n_batch, n_lat_chunk=seq // t, n_ctx_chunk=ctx_len // t,
                          width=width),
        grid=(nb,),
        in_specs=[pl.BlockSpec((rows, sw), lambda g: (0, g)), pl.BlockSpec((None, 1, sw), lambda g: (g, 0, 0))],
        out_specs=pl.BlockSpec((rows, sw), lambda g: (0, g)),
        out_shape=jax.ShapeDtypeStruct((rows, nb * sw), F32),
        compiler_params=_cparams(("parallel",)),
        name="s5_chunk_recurrence",
    )(e, lam_blk)

    y3 = pl.pallas_call(
        functools.partial(_s5_c_body, t_chunk=t),
        grid=(nb, rows // rbs),
        in_specs=[hblk, sblk,
                  pl.BlockSpec((None, t, LANES, t * LANES), lambda g, i: (g, 0, 0, 0)),
                  pl.BlockSpec((None, sw, t * LANES), lambda g, i: (g, 0, 0)),
                  pl.BlockSpec((1, LANES), lambda g, i: (0, g))],
        out_specs=hblk,
        out_shape=jax.ShapeDtypeStruct((rows, t, d), F32),
        compiler_params=_cparams(("parallel", "arbitrary"), VMEM_LIMIT_BIG_BYTES),
        name="s5_outputs",
    )(h3, x_in, wt, wc, d_skip.astype(F32).reshape(1, d))
    return y3.reshape(m, d)


def _shift_body(h_ref, hp_ref, hn_ref, mu_ref, *o_refs, tm, tiles_lat, tiles_ctx, n_lat_tiles):
    i = pl.program_id(0)
    h = h_ref[...]
    first = jnp.where(i < n_lat_tiles, i % tiles_lat == 0, (i - n_lat_tiles) % tiles_ctx == 0)
    last = jnp.where(i < n_lat_tiles, i % tiles_lat == tiles_lat - 1,
                     (i - n_lat_tiles) % tiles_ctx == tiles_ctx - 1)
    rid = lax.broadcasted_iota(jnp.int32, h.shape, 0)
    prev_edge = jnp.where(first, 0.0, hp_ref[7:8, :])
    next_edge = jnp.where(last, 0.0, hn_ref[0:1, :])
    prev = jnp.where(rid == 0, prev_edge, pltpu.roll(h, 1, 0))
    nxt = jnp.where(rid == tm - 1, next_edge, pltpu.roll(h, tm - 1, 0))
    xx = 0.5 * (prev + nxt) - h
    for m, o_ref in enumerate(o_refs):
        o_ref[...] = (h + xx * mu_ref[m:m + 1, :]).astype(o_ref.dtype)


def token_shift_mix(h, mu, tm, seq, ctx_len, n_batch):
    m, d = h.shape
    n_mix = mu.shape[0]
    n_lat_tiles = n_batch * seq // tm
    nb8 = m // 8
    per = tm // 8
    row = pl.BlockSpec((tm, d), lambda i: (i, 0))
    return pl.pallas_call(
        functools.partial(_shift_body, tm=tm, tiles_lat=seq // tm, tiles_ctx=max(ctx_len // tm, 1),
                          n_lat_tiles=n_lat_tiles),
        grid=(m // tm,),
        in_specs=[row,
                  pl.BlockSpec((8, d), lambda i: (jnp.maximum(i * per - 1, 0), 0)),
                  pl.BlockSpec((8, d), lambda i: (jnp.minimum((i + 1) * per, nb8 - 1), 0)),
                  pl.BlockSpec((n_mix, d), lambda i: (0, 0))],
        out_specs=[row] * n_mix,
        out_shape=[jax.ShapeDtypeStruct((m, d), BF16)] * n_mix,
        compiler_params=_cparams(("parallel",)),
    )(h, h, h, mu.astype(F32))


def _rwkv_scan_body(rf_ref, rr_ref, kf_ref, kr_ref, vf_ref, vr_ref, wf_ref, wr_ref, af_ref, ar_ref,
                    kk_ref, ka_ref, of_ref, or_ref, s_ref, op_ref, *, tb, n, bh):
    @pl.when(pl.program_id(0) == 0)
    def _():
        s_ref[...] = jnp.zeros_like(s_ref)

    k_k = kk_ref[...]
    k_a = ka_ref[...]

    def step(t, carry):
        tr = tb - 1 - t
        r = jnp.concatenate([rf_ref[t], rr_ref[tr]], axis=-1)
        k = jnp.concatenate([kf_ref[t], kr_ref[tr]], axis=-1)
        v = jnp.concatenate([vf_ref[t], vr_ref[tr]], axis=-1)
        w = jnp.concatenate([wf_ref[t], wr_ref[tr]], axis=-1)
        a = jnp.concatenate([af_ref[t], ar_ref[tr]], axis=-1)
        kk = k * k_k
        nrm = jnp.sqrt(jnp.sum(kk * kk, axis=0, keepdims=True))
        kk = kk / jnp.maximum(nrm, 1e-12)
        nkk = -kk
        akk = kk * a
        kd = k * (1.0 + (a - 1.0) * k_a)
        op_ref[0] = nkk
        op_ref[1] = w * r
        op_ref[2] = w
        op_ref[3] = akk
        op_ref[4] = kd
        c_a = jnp.sum(akk * r, axis=0, keepdims=True)
        c_k = jnp.sum(kd * r, axis=0, keepdims=True)
        def reduce_keys(jb, acc):
            sa, so = acc
            for jj in range(SCAN_KEY_BLOCK):
                j = jb * SCAN_KEY_BLOCK + jj
                s_j = s_ref[j]
                sa = sa + s_j * op_ref[0, pl.ds(j, 1), :]
                so = so + s_j * op_ref[1, pl.ds(j, 1), :]
            return sa, so

        zero = jnp.zeros_like(v)
        sa, so = lax.fori_loop(0, n // SCAN_KEY_BLOCK, reduce_keys, (zero, zero))
        o = so + sa * c_a + v * c_k
        of_ref[t] = o[:, :bh]
        or_ref[tr] = o[:, bh:]

        def update_keys(jb, c):
            for jj in range(SCAN_KEY_BLOCK):
                j = jb * SCAN_KEY_BLOCK + jj
                s_ref[j] = (s_ref[j] * op_ref[2, pl.ds(j, 1), :] + sa * op_ref[3, pl.ds(j, 1), :]
                            + v * op_ref[4, pl.ds(j, 1), :])
            return c

        lax.fori_loop(0, n // SCAN_KEY_BLOCK, update_keys, 0)
        return carry

    lax.fori_loop(0, tb, step, 0)


def rwkv_scan(r, k, v, w0, w1, a0, a1, k_k, k_a, ctx_len, tb=32):
    steps, n, bh = r.shape
    while steps % tb or ctx_len % tb:
        tb //= 2
    nblk, nblk_ctx = steps // tb, ctx_len // tb

    def rev_blk(i):
        return jnp.where(i < nblk_ctx, nblk_ctx - 1 - i, nblk + nblk_ctx - 1 - i)

    fwd = pl.BlockSpec((tb, n, bh), lambda i: (i, 0, 0))
    rev = pl.BlockSpec((tb, n, bh), lambda i: (rev_blk(i), 0, 0))
    tab = pl.BlockSpec((n, 2 * bh), lambda i: (0, 0))
    out = jax.ShapeDtypeStruct((steps, n, bh), F32)
    return pl.pallas_call(
        functools.partial(_rwkv_scan_body, tb=tb, n=n, bh=bh),
        grid=(nblk,),
        in_specs=[fwd, rev, fwd, rev, fwd, rev, fwd, rev, fwd, rev, tab, tab],
        out_specs=[fwd, rev],
        out_shape=[out, out],
        scratch_shapes=[pltpu.VMEM((n, n, 2 * bh), F32), pltpu.VMEM((5, n, 2 * bh), F32)],
        compiler_params=_cparams(("arbitrary",)),
        name="rwkv_scan",
    )(r, r, k, k, v, v, w0, w1, a0, a1, k_k, k_a)


def _rwkv_post_body(of_ref, or_ref, r_ref, k_ref, v_ref, a0_ref, a1_ref, g_ref, ka_ref, rk_ref, lg_ref, lb_ref,
                    y_ref):
    o = of_ref[...] + or_ref[...]
    mean = jnp.mean(o, axis=1, keepdims=True)
    var = jnp.mean(jnp.square(o - mean), axis=1, keepdims=True)
    on = (o - mean) * lax.rsqrt(var + RW_GN_EPS) * lg_ref[...] + lb_ref[...]
    r = r_ref[...]
    k = k_ref[...]
    k_a = ka_ref[...]
    rk = rk_ref[...]
    kd0 = k * (1.0 + (a0_ref[...] - 1.0) * k_a)
    kd1 = k * (1.0 + (a1_ref[...] - 1.0) * k_a)
    bonus = (jnp.sum(r * kd0 * rk, axis=1, keepdims=True)
             + jnp.sum(r * kd1 * rk, axis=1, keepdims=True)) * v_ref[...]
    y = (on + bonus)
    y_ref[...] = (y * g_ref[...]).astype(y_ref.dtype)


def rwkv_post(o_f, o_r, r, k, v, a0, a1, gate, k_a, r_k, ln_g, ln_b, tb=32):
    steps, n, ch = o_f.shape
    while steps % tb:
        tb //= 2
    blk = pl.BlockSpec((tb, n, ch), lambda i: (i, 0, 0))
    tab = pl.BlockSpec((1, n, ch), lambda i: (0, 0, 0))
    return pl.pallas_call(
        _rwkv_post_body,
        grid=(steps // tb,),
        in_specs=[blk] * 8 + [tab] * 4,
        out_specs=blk,
        out_shape=jax.ShapeDtypeStruct((steps, n, ch), F32),
        compiler_params=_cparams(("parallel",)),
        name="rwkv_post",
    )(o_f, o_r, r, k, v, a0, a1, gate, k_a[None], r_k[None], ln_g[None], ln_b[None])


def _step_block_row(s, b, n_ctx_blk, n_lat_blk, n_batch):
    return jnp.where(s < n_ctx_blk, n_batch * n_lat_blk + b * n_ctx_blk + s, b * n_lat_blk + s - n_ctx_blk)


def _to_steps_body(*refs, n_heads):
    x_refs, o_ref = refs[:-1], refs[-1]
    parts = []
    for x_ref in x_refs:
        x = x_ref[...]
        t = x.shape[0]
        parts.append(jnp.swapaxes(x.reshape(t, n_heads, x.shape[1] // n_heads), 1, 2))
    o_ref[...] = jnp.concatenate(parts, axis=-1)


def tokens_to_steps(x, n_batch, seq, ctx_len, n_heads):
    m, d = x.shape
    n = d // n_heads
    t = RELAYOUT_ROWS
    n_ctx_blk, n_lat_blk = ctx_len // t, seq // t

    def spec(b):
        return pl.BlockSpec((t, d), lambda s: (_step_block_row(s, b, n_ctx_blk, n_lat_blk, n_batch), 0))

    return pl.pallas_call(
        functools.partial(_to_steps_body, n_heads=n_heads),
        grid=(n_ctx_blk + n_lat_blk,),
        in_specs=[spec(b) for b in range(n_batch)],
        out_specs=pl.BlockSpec((t, n, n_batch * n_heads), lambda s: (s, 0, 0)),
        out_shape=jax.ShapeDtypeStruct((ctx_len + seq, n, n_batch * n_heads), x.dtype),
        compiler_params=_cparams(("parallel",)),
        name="tokens_to_steps",
    )(*([x] * n_batch))


def _to_tokens_body(y_ref, o_ref, *, n_heads, n_batch):
    y = y_ref[...]
    t, n = y.shape[0], y.shape[1]
    for b in range(n_batch):
        @pl.when(pl.program_id(1) == b)
        def _():
            yb = y[:, :, b * n_heads:(b + 1) * n_heads]
            o_ref[...] = jnp.swapaxes(yb, 1, 2).reshape(t, n_heads * n)


def steps_to_tokens(y, n_batch, seq, ctx_len, n_heads):
    steps, n, _ = y.shape
    d = n_heads * n
    t = RELAYOUT_ROWS
    n_ctx_blk, n_lat_blk = ctx_len // t, seq // t
    return pl.pallas_call(
        functools.partial(_to_tokens_body, n_heads=n_heads, n_batch=n_batch),
        grid=(n_ctx_blk + n_lat_blk, n_batch),
        in_specs=[pl.BlockSpec((t, n, n_batch * n_heads), lambda s, b: (s, 0, 0))],
        out_specs=pl.BlockSpec((t, d), lambda s, b: (_step_block_row(s, b, n_ctx_blk, n_lat_blk, n_batch), 0)),
        out_shape=jax.ShapeDtypeStruct((n_batch * (seq + ctx_len), d), y.dtype),
        compiler_params=_cparams(("parallel", "arbitrary")),
        name="steps_to_tokens",
    )(y)


def _pad_cols(w, mult=128):
    n = w.shape[-1]
    p = -(-n // mult) * mult - n
    return jnp.pad(w, [(0, 0)] * (w.ndim - 1) + [(0, p)]) if p else w


def _pad_rows(w, mult=128):
    n = w.shape[-2]
    p = -(-n // mult) * mult - n
    return jnp.pad(w, [(0, 0)] * (w.ndim - 2) + [(0, p), (0, 0)]) if p else w


def rwkv_mixer_tokens(h, mu, w_rkv, w0, w1, w2, a0, a1, a2, k_k, k_a, r_k, ln_g, ln_b, g1, g2,
                      tm, n_batch, seq, ctx_len):
    m, d = h.shape
    n_heads, n = r_k.shape
    n_lat = n_batch * seq
    ts = _row_tile(seq, ctx_len, cap=256)
    x_r, x_k, x_v, x_w, x_a, x_g = token_shift_mix(h, mu, ts, seq, ctx_len, n_batch)
    mm = functools.partial(matmul, tm=tm)
    r = mm(x_r, w_rkv, w_index=0)
    k = mm(x_k, w_rkv, w_index=1)
    v = mm(x_v, w_rkv, w_index=2)
    gate = mm(mm(x_g, _pad_cols(g1), act="sigmoid", out_dtype=BF16), _pad_rows(g2))
    dec, aa = [], []
    for dn in range(2):
        lw = mm(x_w, _pad_cols(w1[dn]), act="tanh", out_dtype=BF16)
        dec.append(mm(lw, _pad_rows(w2[dn]), bias=w0[dn], act="decay"))
        la = mm(x_a, _pad_cols(a1[dn]), out_dtype=BF16)
        aa.append(mm(la, _pad_rows(a2[dn]), bias=a0[dn], act="sigmoid"))

    to_seq = functools.partial(tokens_to_steps, n_batch=n_batch, seq=seq, ctx_len=ctx_len, n_heads=n_heads)
    r_t, k_t, v_t, g_t = to_seq(r), to_seq(k), to_seq(v), to_seq(gate)
    d0_t, d1_t, a0_t, a1_t = to_seq(dec[0]), to_seq(dec[1]), to_seq(aa[0]), to_seq(aa[1])

    def table(vec):
        return jnp.tile(vec.astype(F32).reshape(n_heads, n).T[:, None, :], (1, n_batch, 1)).reshape(n, n_batch * n_heads)

    kk_tab, ka_tab = table(k_k), table(k_a)
    o_f, o_r = rwkv_scan(r_t, k_t, v_t, d0_t, d1_t, a0_t, a1_t, jnp.concatenate([kk_tab, kk_tab], axis=-1),
                         jnp.concatenate([ka_tab, ka_tab], axis=-1), ctx_len)
    y_t = rwkv_post(o_f, o_r, r_t, k_t, v_t, a0_t, a1_t, g_t, ka_tab, table(r_k.reshape(-1)),
                    table(ln_g), table(ln_b))
    return steps_to_tokens(y_t, n_batch, seq, ctx_len, n_heads)


def kernel(x, c, ctx, c_ctx, mod_w, mod_b, norm_g, final_g, na_w_qkv, na_w_o, na_rpb, s5_lam_re, s5_lam_im, s5_log_dt, s5_b_re, s5_b_im, s5_c_re, s5_c_im, s5_d, s5_w_glu, rw_mu, rw_w_rkv, rw_w0, rw_w1, rw_w2, rw_a0, rw_a1, rw_a2, rw_k_k, rw_k_a, rw_r_k, rw_ln_g, rw_ln_b, rw_g1, rw_g2, rw_w_o, moe_wg, moe_bg, moe_we, moe_be, moe_w_gu, moe_w_down):
    n_batch, seq, d = x.shape
    ctx_len = ctx.shape[1]
    depth = mod_w.shape[0]
    n_lat = n_batch * seq
    n_ctx = n_batch * ctx_len
    tm = _row_tile(seq, n_ctx)
    tpb = seq // tm
    xs = jnp.concatenate([x.reshape(n_lat, d), ctx.reshape(n_ctx, d)], axis=0).astype(F32)

    cvecs = jnp.concatenate([c.astype(F32), c_ctx.astype(F32)[None]], axis=0)
    rows_pad = -(-(n_batch + 1) // 8) * 8
    cvecs = jnp.pad(cvecs, ((0, rows_pad - n_batch - 1), (0, 0)))
    mods = modulation(cvecs, mod_w, mod_b)[:, :n_batch + 1].reshape(depth, n_batch + 1, 6, 1, d)
    mods = mods.transpose(0, 2, 1, 3, 4)

    for i in range(depth):
        last = i == depth - 1
        sh1, sc1, g1, sh2, sc2, g2 = (mods[i, q] for q in range(6))
        pn = functools.partial(prenorm, tm=tm, tiles_per_batch=tpb, n_batch=n_batch)
        mm = functools.partial(matmul, tm=tm, tiles_per_batch=tpb, n_batch=n_batch)
        mix, j = i % N_MIXERS, i // N_MIXERS
        if mix == 0:
            (hb,) = pn(xs, norm_g[i, 0], 1.0 + sc1, sh1, out_dtypes=(BF16,))
            n_heads = na_rpb.shape[1]
            qscale = jnp.concatenate([jnp.full((d,), (d // n_heads) ** -0.5, F32), jnp.ones((2 * d,), F32)])
            qkv = mm(hb, na_w_qkv, w_index=j, out_dtype=BF16, colscale=qscale)
            kr = (na_rpb.shape[2] + 1) // 2
            o = na_attention(qkv, _na_bias_table(na_rpb[j], GRID_W, seq // GRID_W), kr, n_batch, seq, ctx_len,
                             not last)
            if last:
                xs = xs[:n_lat]
            xs = mm(o, na_w_o, w_index=j, mode="resid", resid=xs, gate=g1)
        elif mix == 1:
            (hf,) = pn(xs, norm_g[i, 0], 1.0 + sc1, sh1, out_dtypes=(F32,))
            wts = _s5_weights(s5_lam_re[j], s5_lam_im[j], s5_log_dt[j], s5_b_re[j], s5_b_im[j],
                              s5_c_re[j], s5_c_im[j], S5_CHUNK)
            y = s5_core(hf, wts, s5_d[j], n_batch, seq, ctx_len)
            if last:
                xs, y = xs[:n_lat], y[:n_lat]
            xs = mm(y, s5_w_glu, w_index=j, mode="glu", resid=xs, gate=g1)
        else:
            (hf,) = pn(xs, norm_g[i, 0], 1.0 + sc1, sh1, out_dtypes=(F32,))
            y = rwkv_mixer_tokens(hf, rw_mu[j], rw_w_rkv[j], rw_w0[j], rw_w1[j], rw_w2[j], rw_a0[j], rw_a1[j],
                                  rw_a2[j], rw_k_k[j], rw_k_a[j], rw_r_k[j], rw_ln_g[j], rw_ln_b[j],
                                  rw_g1[j], rw_g2[j], tm, n_batch, seq, ctx_len)
            if last:
                xs, y = xs[:n_lat], y[:n_lat]
            xs = mm(y, rw_w_o, w_index=j, mode="resid", resid=xs, gate=g1)
        (h2,) = pn(xs, norm_g[i, 1], 1.0 + sc2, sh2, out_dtypes=(F32,))
        xs = hier_moe(xs, h2, moe_wg[i], moe_bg[i], moe_we[i], moe_be[i], moe_w_gu, moe_w_down, i, g2,
                      tm, n_batch, seq)
    ones = jnp.ones((n_batch + 1, 1, d), F32)
    (out,) = prenorm(xs[:n_lat], final_g, ones, jnp.zeros_like(ones), tm, tpb, n_batch, out_dtypes=(x.dtype,))
    return out.reshape(n_batch, seq, d)
```

```python
import functools
import math

import jax
import jax.numpy as jnp
from jax import lax
from jax.experimental import pallas as pl
from jax.experimental.pallas import tpu as pltpu

F32 = jnp.float32
BF16 = jnp.bfloat16

GRID_W = 64
N_MIXERS = 3
NORM_EPS = 1e-6
RW_GN_EPS = 64e-5
MOE_TOP_K = 2
MOE_BLOCK = 128
MOE_COMBINE_ROWS = 128
S5_CHUNK = 16
SCAN_KEY_BLOCK = 16
LANES = 128
RELAYOUT_ROWS = 128
NA_Q_ROWS = 4
NA_KEY_ROWS = 12
NEG_BIG = -1e30
VMEM_LIMIT_BYTES = 48 * 1024 * 1024
VMEM_LIMIT_BIG_BYTES = 56 * 1024 * 1024


def _cparams(sem, limit=VMEM_LIMIT_BYTES):
    return pltpu.CompilerParams(dimension_semantics=sem, vmem_limit_bytes=limit)


def _row_tile(n_lat_per_batch, n_ctx_total, cap=512):
    t = cap
    while t > 8 and (n_lat_per_batch % t or (n_ctx_total and n_ctx_total % t)):
        t //= 2
    return t


def _seg_fn(tiles_per_batch, n_batch):
    def seg(i):
        return jnp.minimum(i // tiles_per_batch, n_batch)
    return seg


def _bdot(a, b):
    return jnp.dot(a.astype(BF16), b.astype(BF16), preferred_element_type=F32)


def _split3(x):
    hi = x.astype(BF16)
    r1 = x - hi.astype(F32)
    mid = r1.astype(BF16)
    lo = (r1 - mid.astype(F32)).astype(BF16)
    return hi, mid, lo


def _prenorm_body(x_ref, g_ref, sc_ref, sh_ref, *o_refs):
    x = x_ref[...]
    ms = jnp.mean(x * x, axis=-1, keepdims=True)
    h = (x * lax.rsqrt(ms + NORM_EPS)) * g_ref[...]
    h = h * sc_ref[0] + sh_ref[0]
    for o_ref in o_refs:
        o_ref[...] = h.astype(o_ref.dtype).reshape(o_ref.shape)


def prenorm(x, g, scale1p, shift, tm, tiles_per_batch, n_batch, out_dtypes, lane_split=None):
    m, d = x.shape
    seg = _seg_fn(tiles_per_batch, n_batch)
    row = pl.BlockSpec((tm, d), lambda i: (i, 0))
    row3 = pl.BlockSpec((tm, d // LANES, LANES), lambda i: (i, 0, 0))
    tab = pl.BlockSpec((1, 1, d), lambda i: (seg(i), 0, 0))
    lane_split = lane_split or (False,) * len(out_dtypes)
    outs = pl.pallas_call(
        _prenorm_body,
        grid=(m // tm,),
        in_specs=[row, pl.BlockSpec((1, d), lambda i: (0, 0)), tab, tab],
        out_specs=[row3 if sp else row for sp in lane_split],
        out_shape=[jax.ShapeDtypeStruct((m, d // LANES, LANES) if sp else (m, d), dt)
                   for dt, sp in zip(out_dtypes, lane_split)],
        compiler_params=_cparams(("parallel",)),
    )(x, g.reshape(1, d), scale1p, shift)
    return outs


def _softplus(z):
    return jnp.maximum(z, 0.0) + jnp.log(1.0 + jnp.exp(-jnp.abs(z)))


def _apply_act(y, act):
    if act is None:
        return y
    if act == "tanh":
        return jnp.tanh(y)
    if act == "sigmoid":
        return jax.nn.sigmoid(y)
    if act == "decay":
        return jnp.exp(-jnp.exp(-_softplus(-y) - 0.5))
    raise ValueError(act)


def _mm_body(*refs, mode, act, has_bias, has_scale, has_mask):
    it = iter(refs)
    x_ref = next(it)
    w_ref = next(it)
    w2_ref = next(it) if mode == "glu" else None
    b_ref = next(it) if has_bias else None
    s_ref = next(it) if has_scale else None
    m_ref = next(it) if has_mask else None
    r_ref = next(it) if mode in ("resid", "glu") else None
    g_ref = next(it) if mode in ("resid", "glu") else None
    o_ref = next(it)
    wb_ref = next(it)
    wb2_ref = next(it) if mode == "glu" else None

    @pl.when(pl.program_id(1) == 0)
    def _():
        wb_ref[...] = w_ref[...].astype(BF16)
        if mode == "glu":
            wb2_ref[...] = w2_ref[...].astype(BF16)

    x = x_ref[...].astype(BF16)
    acc = jnp.dot(x, wb_ref[...], preferred_element_type=F32)
    if mode == "glu":
        acc2 = jnp.dot(x, wb2_ref[...], preferred_element_type=F32)
        acc = acc * jax.nn.sigmoid(acc2)
    if has_bias:
        acc = acc + b_ref[...]
    acc = _apply_act(acc, act)
    if has_scale:
        acc = acc * s_ref[...]
    if has_mask:
        acc = acc * m_ref[...]
    if mode in ("resid", "glu"):
        acc = r_ref[...] + g_ref[0] * acc
    o_ref[...] = acc.astype(o_ref.dtype)


def matmul(x, w, *, tm, tn=512, out_dtype=F32, mode="plain", act=None, bias=None, colscale=None,
           resid=None, gate=None, tiles_per_batch=None, n_batch=None, w_index=0, tilemask=None):
    m, k = x.shape
    if w.ndim == 2:
        w, w_index = w[None], 0
    n = w.shape[2] // 2 if mode == "glu" else w.shape[2]
    tn = min(tn, n)
    assert m % tm == 0 and n % tn == 0, (m, tm, n, tn)
    nj = n // tn
    in_specs = [pl.BlockSpec((tm, k), lambda j, i: (i, 0)),
                pl.BlockSpec((None, k, tn), lambda j, i: (w_index, 0, j))]
    args = [x, w]
    if mode == "glu":
        in_specs.append(pl.BlockSpec((None, k, tn), lambda j, i: (w_index, 0, j + nj)))
        args.append(w)
    col = pl.BlockSpec((1, tn), lambda j, i: (0, j))
    if bias is not None:
        in_specs.append(col)
        args.append(bias.reshape(1, n).astype(F32))
    if colscale is not None:
        in_specs.append(col)
        args.append(colscale.reshape(1, n).astype(F32))
    if tilemask is not None:
        assert tilemask.shape == (tm, tn), (tilemask.shape, tm, tn)
        in_specs.append(pl.BlockSpec((tm, tn), lambda j, i: (0, 0)))
        args.append(tilemask.astype(F32))
    if mode in ("resid", "glu"):
        seg = _seg_fn(tiles_per_batch, n_batch)
        in_specs.append(pl.BlockSpec((tm, tn), lambda j, i: (i, j)))
        in_specs.append(pl.BlockSpec((1, 1, tn), lambda j, i: (seg(i), 0, j)))
        args += [resid, gate]
    return pl.pallas_call(
        functools.partial(_mm_body, mode=mode, act=act, has_bias=bias is not None,
                          has_scale=colscale is not None, has_mask=tilemask is not None),
        grid=(nj, m // tm),
        in_specs=in_specs,
        out_specs=pl.BlockSpec((tm, tn), lambda j, i: (i, j)),
        out_shape=jax.ShapeDtypeStruct((m, n), out_dtype),
        scratch_shapes=[pltpu.VMEM((k, tn), BF16)] * (2 if mode == "glu" else 1),
        compiler_params=_cparams(("parallel", "arbitrary")),
    )(*args)


def _router_body(x_ref, w_ref, b_ref, o_ref, *, n_grp, epg):
    xh, xm, xl = _split3(x_ref[...])
    wh, wm, wl = _split3(w_ref[...])
    dot = functools.partial(jnp.dot, preferred_element_type=F32)
    acc = dot(xh, wh) + (dot(xh, wm) + dot(xm, wh)) + (dot(xh, wl) + dot(xm, wm) + dot(xl, wh))
    acc = acc + b_ref[...]
    lane = lax.broadcasted_iota(jnp.int32, acc.shape, 1)
    lane_f = lane.astype(F32)
    far = 1e9

    def first_max(vals):
        m = jnp.max(vals, axis=-1, keepdims=True)
        return m, jnp.min(jnp.where(vals == m, lane_f, far), axis=-1, keepdims=True)

    gl = jnp.where(lane < n_grp, acc, NEG_BIG)
    gmax, grp = first_max(gl)
    p_grp = 1.0 / jnp.sum(jnp.exp(gl - gmax), axis=-1, keepdims=True)
    lo = n_grp + grp * epg
    el = jnp.where((lane_f >= lo) & (lane_f < lo + epg), acc, NEG_BIG)
    m1, i1 = first_max(el)
    m2, i2 = first_max(jnp.where(lane_f == i1, NEG_BIG, el))
    e21 = jnp.exp(m2 - m1)
    g1 = p_grp / (1.0 + e21)
    g2 = p_grp * e21 / (1.0 + e21)
    out = jnp.where(lane == 0, i1 - n_grp, jnp.where(lane == 1, i2 - n_grp,
                                                    jnp.where(lane == 2, g1, jnp.where(lane == 3, g2, 0.0))))
    o_ref[...] = out


def router(x, w, bias, tm, n_grp, epg):
    m, k = x.shape
    n = w.shape[1]
    return pl.pallas_call(
        functools.partial(_router_body, n_grp=n_grp, epg=epg),
        grid=(m // tm,),
        in_specs=[pl.BlockSpec((tm, k), lambda i: (i, 0)), pl.BlockSpec((k, n), lambda i: (0, 0)),
                  pl.BlockSpec((1, n), lambda i: (0, 0))],
        out_specs=pl.BlockSpec((tm, n), lambda i: (i, 0)),
        out_shape=jax.ShapeDtypeStruct((m, n), F32),
        compiler_params=_cparams(("parallel",)),
        name="moe_router",
    )(x, w, bias.reshape(1, n))


def _mod_body(s_ref, w_ref, b_ref, o_ref):
    s = s_ref[...]
    s = s * jax.nn.sigmoid(s)
    o_ref[0] = _bdot(s, w_ref[0]) + b_ref[0]


def modulation(cvecs, mod_w, mod_b, tn=768):
    depth, d, n = mod_w.shape
    rows = cvecs.shape[0]
    while n % tn:
        tn //= 2
    return pl.pallas_call(
        _mod_body,
        grid=(depth, n // tn),
        in_specs=[pl.BlockSpec((rows, d), lambda l, j: (0, 0)),
                  pl.BlockSpec((1, d, tn), lambda l, j: (l, 0, j)),
                  pl.BlockSpec((1, 1, tn), lambda l, j: (l, 0, j))],
        out_specs=pl.BlockSpec((1, rows, tn), lambda l, j: (l, 0, j)),
        out_shape=jax.ShapeDtypeStruct((depth, rows, n), F32),
        compiler_params=_cparams(("parallel", "parallel")),
    )(cvecs, mod_w, mod_b.reshape(depth, 1, n))


def _na_key_start(i, rows, kr):
    return jnp.clip(i * NA_Q_ROWS - kr // 2, 0, rows - NA_KEY_ROWS)


def _na_bias_table(rpb, grid_w, rows):
    n_heads = rpb.shape[0]
    kr = (rpb.shape[1] + 1) // 2
    kcw = (rpb.shape[2] + 1) // 2
    n_blk = rows // NA_Q_ROWS
    q = jnp.arange(grid_w)[:, None]
    kc = jnp.arange(grid_w)[None, :]
    win = jnp.clip(q - kcw // 2, 0, grid_w - kcw)
    col_ok = (kc >= win) & (kc < win + kcw)
    dc = jnp.clip(kc - q, 1 - kcw, kcw - 1) + kcw - 1
    blk = jnp.array([0, min(1, n_blk - 1), n_blk - 1])[:, None, None]
    r = blk * NA_Q_ROWS + jnp.arange(NA_Q_ROWS)[None, :, None]
    krow = _na_key_start(blk, rows, kr) + jnp.arange(NA_KEY_ROWS)[None, None, :]
    rs = jnp.clip(r - kr // 2, 0, rows - kr)
    row_ok = (krow >= rs) & (krow < rs + kr)
    dr = jnp.clip(krow - r + kr - 1, 0, 2 * kr - 2)
    hi = lax.Precision.HIGHEST
    sel_r = jax.nn.one_hot(dr.reshape(-1), rpb.shape[1], dtype=F32)
    sel_c = jax.nn.one_hot(dc.reshape(-1), rpb.shape[2], dtype=F32)
    tab = jnp.einsum("ra,has->hrs", sel_r, jnp.einsum("hab,sb->has", rpb.astype(F32), sel_c, precision=hi),
                     precision=hi)
    tab = tab.reshape(n_heads, 3, NA_Q_ROWS, NA_KEY_ROWS, grid_w, grid_w)
    ok = row_ok[None, :, :, :, None, None] & col_ok[None, None, None, None]
    tab = jnp.where(ok, tab, NEG_BIG)
    return tab.transpose(0, 1, 2, 4, 3, 5).reshape(n_heads, 3, NA_Q_ROWS * grid_w, NA_KEY_ROWS * grid_w)


def _na_body(q_ref, k_ref, v_ref, kc_ref, vc_ref, bias_ref, o_ref, *, hpb, dh, grid_w, kr, rows):
    ks = _na_key_start(pl.program_id(2), rows, kr)
    start = pl.multiple_of(ks * grid_w, grid_w)
    nt = (((1,), (1,)), ((), ()))
    for h in range(hpb):
        sl = slice(h * dh, (h + 1) * dh)
        q = q_ref[:, sl]
        k = k_ref[pl.ds(start, NA_KEY_ROWS * grid_w), sl]
        v = v_ref[pl.ds(start, NA_KEY_ROWS * grid_w), sl]
        s = lax.dot_general(q, k, nt, preferred_element_type=F32) + bias_ref[h, 0]
        sc = lax.dot_general(q, kc_ref[:, sl], nt, preferred_element_type=F32)
        m = jnp.maximum(jnp.max(s, axis=-1, keepdims=True), jnp.max(sc, axis=-1, keepdims=True))
        p = jnp.exp(s - m)
        pc = jnp.exp(sc - m)
        den = jnp.sum(p, axis=-1, keepdims=True) + jnp.sum(pc, axis=-1, keepdims=True)
        o = (jnp.dot(p.astype(BF16), v, preferred_element_type=F32)
             + jnp.dot(pc.astype(BF16), vc_ref[:, sl], preferred_element_type=F32))
        o_ref[:, sl] = (o / den).astype(o_ref.dtype)


def _ctx_attn_body(q_ref, k_ref, v_ref, o_ref, *, hpb, dh):
    nt = (((1,), (1,)), ((), ()))
    for h in range(hpb):
        sl = slice(h * dh, (h + 1) * dh)
        s = lax.dot_general(q_ref[:, sl], k_ref[:, sl], nt, preferred_element_type=F32)
        m = jnp.max(s, axis=-1, keepdims=True)
        p = jnp.exp(s - m)
        den = jnp.sum(p, axis=-1, keepdims=True)
        o = jnp.dot(p.astype(BF16), v_ref[:, sl], preferred_element_type=F32)
        o_ref[:, sl] = (o / den).astype(o_ref.dtype)


def na_attention(qkv, bias_tab, kr, n_batch, seq, ctx_len, need_ctx):
    d = qkv.shape[1] // 3
    n_heads = bias_tab.shape[0]
    dh = d // n_heads
    hpb = max(1, min(n_heads, 256 // dh))
    bw = hpb * dh
    n_d = d // bw
    rows = seq // GRID_W
    n_blk = rows // NA_Q_ROWS
    qb = NA_Q_ROWS * GRID_W
    n_lat = n_batch * seq
    ctx_blk0 = n_lat // ctx_len

    def cls_of(i):
        return jnp.where(i == 0, 0, jnp.where(i == n_blk - 1, 2, 1))

    o_lat = pl.pallas_call(
        functools.partial(_na_body, hpb=hpb, dh=dh, grid_w=GRID_W, kr=kr, rows=rows),
        grid=(n_batch, n_d, n_blk),
        in_specs=[
            pl.BlockSpec((qb, bw), lambda b, g, i: (b * n_blk + i, g)),
            pl.BlockSpec((seq, bw), lambda b, g, i: (b, n_d + g)),
            pl.BlockSpec((seq, bw), lambda b, g, i: (b, 2 * n_d + g)),
            pl.BlockSpec((ctx_len, bw), lambda b, g, i: (ctx_blk0 + b, n_d + g)),
            pl.BlockSpec((ctx_len, bw), lambda b, g, i: (ctx_blk0 + b, 2 * n_d + g)),
            pl.BlockSpec((hpb, 1, qb, NA_KEY_ROWS * GRID_W), lambda b, g, i: (g, cls_of(i), 0, 0)),
        ],
        out_specs=pl.BlockSpec((qb, bw), lambda b, g, i: (b * n_blk + i, g)),
        out_shape=jax.ShapeDtypeStruct((n_lat, d), BF16),
        compiler_params=_cparams(("parallel", "parallel", "arbitrary")),
        name="na_attention",
    )(qkv, qkv, qkv, qkv, qkv, bias_tab)
    if not need_ctx:
        return o_lat
    o_ctx = pl.pallas_call(
        functools.partial(_ctx_attn_body, hpb=hpb, dh=dh),
        grid=(n_batch, n_d),
        in_specs=[
            pl.BlockSpec((ctx_len, bw), lambda b, g: (ctx_blk0 + b, g)),
            pl.BlockSpec((ctx_len, bw), lambda b, g: (ctx_blk0 + b, n_d + g)),
            pl.BlockSpec((ctx_len, bw), lambda b, g: (ctx_blk0 + b, 2 * n_d + g)),
        ],
        out_specs=pl.BlockSpec((ctx_len, bw), lambda b, g: (b, g)),
        out_shape=jax.ShapeDtypeStruct((n_batch * ctx_len, d), BF16),
        compiler_params=_cparams(("parallel", "parallel")),
    )(qkv, qkv, qkv)
    return jnp.concatenate([o_lat, o_ctx], axis=0)


def _expert_body(be_ref, idx_ref, idx_next_ref, x_hbm, wgu_ref, wd_ref, o_ref, xbuf, sem, wgu_s, wd_s, *,
                 hidden, n_blocks, n_slab):
    j = pl.program_id(0)
    cur = j % 2

    def row_copy(idx, buf, r):
        return pltpu.make_async_copy(x_hbm.at[pl.ds(idx[0, 0, r], 1)], xbuf.at[buf, pl.ds(r, 1)], sem.at[buf])

    @pl.when(j == 0)
    def _():
        for r in range(MOE_BLOCK):
            row_copy(idx_ref, 0, r).start()

    prev = be_ref[jnp.maximum(j - 1, 0)]
    changed = jnp.logical_or(j == 0, be_ref[j] != prev)

    @pl.when(changed)
    def _():
        wgu_s[...] = wgu_ref[...].astype(BF16)
        wd_s[...] = wd_ref[...].astype(BF16)

    for r in range(MOE_BLOCK):
        row_copy(idx_ref, cur, r).wait()

    for r in range(MOE_BLOCK):
        row_copy(idx_next_ref, 1 - cur, r).start()
    xb = xbuf.at[cur]
    gu = None
    for c in range(0, n_slab, 2):
        xa = jnp.concatenate([xb[:, c, :], xb[:, c + 1, :]], axis=-1).astype(BF16)
        part = jnp.dot(xa, wgu_s[c * LANES:(c + 2) * LANES, :], preferred_element_type=F32)
        gu = part if gu is None else gu + part
    g = gu[:, :hidden]
    u = gu[:, hidden:]
    a = (g * jax.nn.sigmoid(g) * u).astype(BF16)
    o_ref[...] = jnp.dot(a, wd_s[...], preferred_element_type=F32)

    @pl.when(j == n_blocks - 1)
    def _():
        for r in range(MOE_BLOCK):
            row_copy(idx_next_ref, 1 - cur, r).wait()


def expert_blocks(x, slot_tok, blk_e, w_gu, w_down, layer):
    n_slab = x.shape[1]
    d = n_slab * LANES
    hidden = w_down.shape[2]
    n_blocks = blk_e.shape[0]
    idx = slot_tok.reshape(n_blocks, 1, MOE_BLOCK)
    return pl.pallas_call(
        functools.partial(_expert_body, hidden=hidden, n_blocks=n_blocks, n_slab=n_slab),
        grid_spec=pltpu.PrefetchScalarGridSpec(
            num_scalar_prefetch=1,
            grid=(n_blocks,),
            in_specs=[pl.BlockSpec((1, 1, MOE_BLOCK), lambda j, be: (j, 0, 0), memory_space=pltpu.SMEM),
                      pl.BlockSpec((1, 1, MOE_BLOCK), lambda j, be: (jnp.minimum(j + 1, n_blocks - 1), 0, 0),
                                   memory_space=pltpu.SMEM),
                      pl.BlockSpec(memory_space=pl.ANY),
                      pl.BlockSpec((None, None, d, 2 * hidden), lambda j, be: (layer, be[j], 0, 0)),
                      pl.BlockSpec((None, None, hidden, d), lambda j, be: (layer, be[j], 0, 0))],
            out_specs=pl.BlockSpec((MOE_BLOCK, d), lambda j, be: (j, 0)),
            scratch_shapes=[pltpu.VMEM((2, MOE_BLOCK, n_slab, LANES), F32), pltpu.SemaphoreType.DMA((2,)),
                            pltpu.VMEM((d, 2 * hidden), BF16), pltpu.VMEM((hidden, d), BF16)]),
        out_shape=jax.ShapeDtypeStruct((n_blocks * MOE_BLOCK, d), F32),
        compiler_params=_cparams(("arbitrary",), VMEM_LIMIT_BIG_BYTES),
        name="moe_experts",
    )(blk_e, idx, idx, x, w_gu, w_down)


def _combine_body(idx_ref, idx_next_ref, route_ref, x_ref, g_ref, yb_hbm, o_ref, ybuf, sem, *, tmc, n_tiles):
    i = pl.program_id(0)
    cur = i % 2

    def row_copy(idx, buf, k, r):
        return pltpu.make_async_copy(yb_hbm.at[pl.ds(idx[0, 0, k * tmc + r], 1)],
                                     ybuf.at[buf, k, pl.ds(r, 1)], sem.at[buf])

    def start_all(idx, buf):
        for k in range(MOE_TOP_K):
            for r in range(tmc):
                row_copy(idx, buf, k, r).start()

    @pl.when(i == 0)
    def _():
        start_all(idx_ref, 0)

    @pl.when(i + 1 < n_tiles)
    def _():
        start_all(idx_next_ref, 1 - cur)

    for k in range(MOE_TOP_K):
        for r in range(tmc):
            row_copy(idx_ref, cur, k, r).wait()
    route = route_ref[...]
    f = ybuf[cur, 0] * route[:, MOE_TOP_K:MOE_TOP_K + 1]
    for k in range(1, MOE_TOP_K):
        f = f + ybuf[cur, k] * route[:, MOE_TOP_K + k:MOE_TOP_K + k + 1]
    o_ref[...] = x_ref[...] + g_ref[0] * f


def moe_combine(xs, yb, slot_of_asg, route, gate_tab, n_batch, seq):
    n_tok, d = xs.shape
    tmc = MOE_COMBINE_ROWS
    n_tiles = n_tok // tmc
    idx = slot_of_asg.reshape(n_tiles, tmc, MOE_TOP_K).transpose(0, 2, 1).reshape(n_tiles, 1, MOE_TOP_K * tmc)
    seg = _seg_fn(seq // tmc, n_batch)
    return pl.pallas_call(
        functools.partial(_combine_body, tmc=tmc, n_tiles=n_tiles),
        grid=(n_tiles,),
        in_specs=[pl.BlockSpec((1, 1, MOE_TOP_K * tmc), lambda i: (i, 0, 0), memory_space=pltpu.SMEM),
                  pl.BlockSpec((1, 1, MOE_TOP_K * tmc), lambda i: (jnp.minimum(i + 1, n_tiles - 1), 0, 0),
                               memory_space=pltpu.SMEM),
                  pl.BlockSpec((tmc, route.shape[1]), lambda i: (i, 0)),
                  pl.BlockSpec((tmc, d), lambda i: (i, 0)),
                  pl.BlockSpec((1, 1, d), lambda i: (seg(i), 0, 0)),
                  pl.BlockSpec(memory_space=pl.ANY)],
        out_specs=pl.BlockSpec((tmc, d), lambda i: (i, 0)),
        out_shape=jax.ShapeDtypeStruct((n_tok, d), F32),
        scratch_shapes=[pltpu.VMEM((2, MOE_TOP_K, tmc, d), F32), pltpu.SemaphoreType.DMA((2,))],
        compiler_params=_cparams(("arbitrary",)),
        name="moe_combine",
    )(idx, idx, route, xs, gate_tab, yb)


def hier_moe(xs, h, h_slabs, wg, bg, we, be, w_gu, w_down, layer, gate_tab, tm, n_batch, seq):
    n_tok, d = h.shape
    n_grp = wg.shape[1]
    n_exp = we.shape[1]
    epg = n_exp // n_grp
    n_logit = n_grp + n_exp
    n_pad = -(-n_logit // 128) * 128
    wcat = jnp.pad(jnp.concatenate([wg, we], axis=1), ((0, 0), (0, n_pad - n_logit)))
    bcat = jnp.pad(jnp.concatenate([bg, be], axis=0), (0, n_pad - n_logit))
    route = router(h, wcat, bcat, tm, n_grp, epg)

    flat_e = route[:, :MOE_TOP_K].astype(jnp.int32).reshape(-1)
    n_asg = flat_e.shape[0]
    onehot = (flat_e[:, None] == jnp.arange(n_exp, dtype=jnp.int32)[None, :]).astype(F32)
    ck = 128
    oh3 = onehot.reshape(n_asg // ck, ck, n_exp)
    tri = jnp.tril(jnp.ones((ck, ck), F32))
    within = jnp.einsum("ij,cjk->cik", tri, oh3, precision=lax.Precision.HIGHEST)
    tot = within[:, -1, :]
    before = jnp.cumsum(tot, axis=0) - tot
    rank = jnp.sum(oh3 * (within + before[:, None, :]), axis=-1).reshape(n_asg).astype(jnp.int32) - 1
    counts = jnp.sum(tot, axis=0).astype(jnp.int32)
    padded = (counts + MOE_BLOCK - 1) // MOE_BLOCK * MOE_BLOCK
    pad_end = jnp.cumsum(padded)
    pad_start = pad_end - padded
    slot = jnp.sum(onehot * pad_start.astype(F32)[None, :], axis=-1).astype(jnp.int32) + rank
    n_blocks = -(-n_asg // MOE_BLOCK) + n_exp
    slot_tok = jnp.zeros((n_blocks * MOE_BLOCK,), jnp.int32).at[slot].set(
        jnp.arange(n_asg, dtype=jnp.int32) // MOE_TOP_K)
    blk_start = jnp.arange(n_blocks, dtype=jnp.int32) * MOE_BLOCK
    blk_e = jnp.minimum(jnp.sum((pad_end[None, :] <= blk_start[:, None]).astype(jnp.int32), axis=1), n_exp - 1)
    yb = expert_blocks(h_slabs, slot_tok, blk_e, w_gu, w_down, layer)
    return moe_combine(xs, yb, slot.reshape(n_tok, MOE_TOP_K), route, gate_tab, n_batch, seq)


def _s5_weights(lam_re, lam_im, log_dt, b_re, b_im, c_re, c_im, t_chunk):
    n_grp, n_state = lam_re.shape[1], lam_re.shape[2]
    cg = b_re.shape[-1]
    lam = lax.complex(lam_re.astype(F32), lam_im.astype(F32))
    dt = jnp.exp(log_dt.astype(F32))[..., None]
    lam_bar = jnp.exp(lam * dt)
    b_bar = ((lam_bar - 1.0) / lam)[..., None] * lax.complex(b_re.astype(F32), b_im.astype(F32))
    c_mat = lax.complex(c_re.astype(F32), c_im.astype(F32))
    ks = jnp.arange(t_chunk + 1, dtype=F32)
    pw = jnp.exp((lam * dt)[..., None] * ks)
    hi = lax.Precision.HIGHEST
    kern = jnp.einsum("dgcp,dgpk,dgpi->dgkci", c_mat, pw[..., :t_chunk], b_bar, precision=hi).real
    t_idx = jnp.arange(t_chunk)
    lag = t_idx[None, :] - t_idx[:, None]
    k_f = kern[0][:, jnp.clip(lag, 0, t_chunk - 1)]
    k_r = kern[1][:, jnp.clip(-lag, 0, t_chunk - 1)]
    toep = (jnp.where((lag >= 0)[None, :, :, None, None], k_f, 0.0)
            + jnp.where((lag <= 0)[None, :, :, None, None], k_r, 0.0))
    toep = toep.transpose(0, 1, 4, 2, 3).reshape(n_grp, t_chunk * cg, t_chunk * cg)
    pf = pw[0][..., :t_chunk][..., ::-1]
    pr = pw[1][..., :t_chunk]
    bend_f = jnp.einsum("gps,gpc->gscp", pf, b_bar[0]).reshape(n_grp, t_chunk * cg, n_state)
    bend_r = jnp.einsum("gps,gpc->gscp", pr, b_bar[1]).reshape(n_grp, t_chunk * cg, n_state)
    bend = jnp.stack([bend_f.real, bend_f.imag, bend_r.real, bend_r.imag], axis=1)
    of = jnp.einsum("gcp,gpt->gptc", c_mat[0], pw[0][..., 1:]).reshape(n_grp, n_state, t_chunk * cg)
    orv = jnp.einsum("gcp,gpt->gptc", c_mat[1], pw[1][..., 1:][..., ::-1]).reshape(n_grp, n_state, t_chunk * cg)
    cout = jnp.stack([of.real, -of.imag, orv.real, -orv.imag], axis=1)

    gsz = LANES // cg
    nb = n_grp // gsz
    tile = 512

    def expand(n_outer, n_inner):
        dst = jnp.arange(n_outer * gsz * n_inner)
        src = (dst // (gsz * n_inner)) * n_inner + dst % n_inner
        return (jnp.arange(n_outer * n_inner)[:, None] == src[None, :]).astype(F32)

    def grp(period, width):
        return (jnp.arange(tile) % period) // width

    def own_group(row_grp, col_grp):
        return (row_grp[:, None] == col_grp[None, :]).astype(F32)

    e_tc, e_xp = expand(t_chunk, cg), expand(4, n_state)
    grp_ch, grp_st = grp(LANES, cg), grp(gsz * n_state, n_state)
    kin = t_chunk * cg
    xt = toep.reshape(nb, gsz, t_chunk, cg, kin).transpose(0, 2, 1, 3, 4).reshape(n_grp * kin, kin)
    wt = matmul(xt, e_tc, tm=tile, tn=tile, out_dtype=BF16, tilemask=own_group(grp_ch, grp_ch))
    wt = wt.reshape(nb, t_chunk, LANES, t_chunk * LANES)
    xb = bend.reshape(nb, gsz, 4, t_chunk, cg, n_state).transpose(0, 3, 1, 4, 2, 5).reshape(n_grp * kin, 4 * n_state)
    wb = matmul(xb, e_xp, tm=tile, tn=tile, out_dtype=BF16, tilemask=own_group(grp_ch, grp_st))
    wb = wb.reshape(nb, t_chunk, LANES, 4 * gsz * n_state)
    xc = cout.reshape(nb, gsz, 4, n_state, kin).transpose(0, 2, 1, 3, 4).reshape(4 * n_grp * n_state, kin)
    wc = matmul(xc, e_tc, tm=tile, tn=tile, out_dtype=BF16, tilemask=own_group(grp_st, grp_ch))
    wc = wc.reshape(nb, 4 * gsz * n_state, t_chunk * LANES)
    lam_t = pw[..., t_chunk]
    lam_t4 = jnp.stack([lam_t[0].real, lam_t[0].imag, lam_t[1].real, lam_t[1].imag]).reshape(4, n_grp * n_state)
    return wt, wb, wc, lam_t4


def _s5_a_body(h_ref, wb_ref, e_ref, *, t_chunk):
    acc = jnp.dot(h_ref[:, 0, :].astype(BF16), wb_ref[0], preferred_element_type=F32)
    for s in range(1, t_chunk):
        acc = acc + jnp.dot(h_ref[:, s, :].astype(BF16), wb_ref[s], preferred_element_type=F32)
    e_ref[...] = acc


def _s5_b_body(e_ref, lam_ref, x_ref, *, n_batch, n_lat_chunk, n_ctx_chunk, width):
    ctx0 = n_batch * n_lat_chunk
    n_chunk = n_lat_chunk + n_ctx_chunk

    def comp(x):
        return slice(x * width, (x + 1) * width)

    lfr, lfi, lrr, lri = (lam_ref[:, comp(x)] for x in range(4))

    for b in range(n_batch):
        def fwd(n, carry):
            sr, si = carry
            row = jnp.where(n < n_ctx_chunk, ctx0 + b * n_ctx_chunk + n, b * n_lat_chunk + n - n_ctx_chunk)
            x_ref[pl.ds(row, 1), comp(0)] = sr
            x_ref[pl.ds(row, 1), comp(1)] = si
            er = e_ref[pl.ds(row, 1), comp(0)]
            ei = e_ref[pl.ds(row, 1), comp(1)]
            return lfr * sr - lfi * si + er, lfr * si + lfi * sr + ei

        def rev(n, carry):
            sr, si = carry
            row = jnp.where(n < n_ctx_chunk, ctx0 + b * n_ctx_chunk + n_ctx_chunk - 1 - n,
                            b * n_lat_chunk + n_chunk - 1 - n)
            x_ref[pl.ds(row, 1), comp(2)] = sr
            x_ref[pl.ds(row, 1), comp(3)] = si
            er = e_ref[pl.ds(row, 1), comp(2)]
            ei = e_ref[pl.ds(row, 1), comp(3)]
            return lrr * sr - lri * si + er, lrr * si + lri * sr + ei

        z = jnp.zeros((1, width), F32)
        lax.fori_loop(0, n_chunk, fwd, (z, z))
        lax.fori_loop(0, n_chunk, rev, (z, z))


def _gelu(y):
    return 0.5 * y * (1.0 + jnp.tanh(0.7978845608028654 * (y + 0.044715 * (y * y * y))))


def _s5_c_body(h_ref, x_ref, wt_ref, wc_ref, dsk_ref, o_ref, *, t_chunk):
    acc = jnp.dot(x_ref[...].astype(BF16), wc_ref[...], preferred_element_type=F32)
    for s in range(t_chunk):
        acc = acc + jnp.dot(h_ref[:, s, :].astype(BF16), wt_ref[s], preferred_element_type=F32)
    dsk = dsk_ref[...]
    for t in range(t_chunk):
        o_ref[:, t, :] = _gelu(acc[:, t * LANES:(t + 1) * LANES] + h_ref[:, t, :] * dsk)


def s5_core(h_f32, weights, d_skip, n_batch, seq, ctx_len):
    wt, wb, wc, lam_t4 = weights
    t = S5_CHUNK
    m, d = h_f32.shape
    nb = wt.shape[0]
    sw = wb.shape[-1]
    width = sw // 4
    rows = m // t
    rbs = rows // 2 if rows % 16 == 0 else rows
    h3 = h_f32.reshape(rows, t, d)
    hblk = pl.BlockSpec((rbs, t, LANES), lambda g, i: (i, 0, g))
    sblk = pl.BlockSpec((rbs, sw), lambda g, i: (i, g))
    e = pl.pallas_call(
        functools.partial(_s5_a_body, t_chunk=t),
        grid=(nb, rows // rbs),
        in_specs=[hblk, pl.BlockSpec((None, t, LANES, sw), lambda g, i: (g, 0, 0, 0))],
        out_specs=sblk,
        out_shape=jax.ShapeDtypeStruct((rows, nb * sw), F32),
        compiler_params=_cparams(("parallel", "arbitrary")),
        name="s5_chunk_states",
    )(h3, wb)

    lam_blk = lam_t4.reshape(4, nb, width).transpose(1, 0, 2).reshape(nb, 1, sw)
    x_in = pl.pallas_call(
        functools.partial(_s5_b_body, n_batch=n_batch, n_lat_chunk=seq // t, n_ctx_chunk=ctx_len // t,
                          width=width),
        grid=(nb,),
        in_specs=[pl.BlockSpec((rows, sw), lambda g: (0, g)), pl.BlockSpec((None, 1, sw), lambda g: (g, 0, 0))],
        out_specs=pl.BlockSpec((rows, sw), lambda g: (0, g)),
        out_shape=jax.ShapeDtypeStruct((rows, nb * sw), F32),
        compiler_params=_cparams(("parallel",)),
        name="s5_chunk_recurrence",
    )(e, lam_blk)

    y3 = pl.pallas_call(
        functools.partial(_s5_c_body, t_chunk=t),
        grid=(nb, rows // rbs),
        in_specs=[hblk, sblk,
                  pl.BlockSpec((None, t, LANES, t * LANES), lambda g, i: (g, 0, 0, 0)),
                  pl.BlockSpec((None, sw, t * LANES), lambda g, i: (g, 0, 0)),
                  pl.BlockSpec((1, LANES), lambda g, i: (0, g))],
        out_specs=hblk,
        out_shape=jax.ShapeDtypeStruct((rows, t, d), F32),
        compiler_params=_cparams(("parallel", "arbitrary"), VMEM_LIMIT_BIG_BYTES),
        name="s5_outputs",
    )(h3, x_in, wt, wc, d_skip.astype(F32).reshape(1, d))
    return y3.reshape(m, d)


def _shift_body(h_ref, hp_ref, hn_ref, mu_ref, *o_refs, tm, tiles_lat, tiles_ctx, n_lat_tiles):
    i = pl.program_id(0)
    h = h_ref[...]
    first = jnp.where(i < n_lat_tiles, i % tiles_lat == 0, (i - n_lat_tiles) % tiles_ctx == 0)
    last = jnp.where(i < n_lat_tiles, i % tiles_lat == tiles_lat - 1,
                     (i - n_lat_tiles) % tiles_ctx == tiles_ctx - 1)
    rid = lax.broadcasted_iota(jnp.int32, h.shape, 0)
    prev_edge = jnp.where(first, 0.0, hp_ref[7:8, :])
    next_edge = jnp.where(last, 0.0, hn_ref[0:1, :])
    prev = jnp.where(rid == 0, prev_edge, pltpu.roll(h, 1, 0))
    nxt = jnp.where(rid == tm - 1, next_edge, pltpu.roll(h, tm - 1, 0))
    xx = 0.5 * (prev + nxt) - h
    for m, o_ref in enumerate(o_refs):
        o_ref[...] = (h + xx * mu_ref[m:m + 1, :]).astype(o_ref.dtype)


def token_shift_mix(h, mu, tm, seq, ctx_len, n_batch):
    m, d = h.shape
    n_mix = mu.shape[0]
    n_lat_tiles = n_batch * seq // tm
    nb8 = m // 8
    per = tm // 8
    row = pl.BlockSpec((tm, d), lambda i: (i, 0))
    return pl.pallas_call(
        functools.partial(_shift_body, tm=tm, tiles_lat=seq // tm, tiles_ctx=max(ctx_len // tm, 1),
                          n_lat_tiles=n_lat_tiles),
        grid=(m // tm,),
        in_specs=[row,
                  pl.BlockSpec((8, d), lambda i: (jnp.maximum(i * per - 1, 0), 0)),
                  pl.BlockSpec((8, d), lambda i: (jnp.minimum((i + 1) * per, nb8 - 1), 0)),
                  pl.BlockSpec((n_mix, d), lambda i: (0, 0))],
        out_specs=[row] * n_mix,
        out_shape=[jax.ShapeDtypeStruct((m, d), BF16)] * n_mix,
        compiler_params=_cparams(("parallel",)),
    )(h, h, h, mu.astype(F32))


def _rwkv_scan_body(rf_ref, rr_ref, kf_ref, kr_ref, vf_ref, vr_ref, wf_ref, wr_ref, af_ref, ar_ref,
                    kk_ref, ka_ref, of_ref, or_ref, s_ref, op_ref, *, tb, n, bh):
    @pl.when(pl.program_id(0) == 0)
    def _():
        s_ref[...] = jnp.zeros_like(s_ref)

    k_k = kk_ref[...]
    k_a = ka_ref[...]

    def step(t, carry):
        tr = tb - 1 - t
        r = jnp.concatenate([rf_ref[t], rr_ref[tr]], axis=-1)
        k = jnp.concatenate([kf_ref[t], kr_ref[tr]], axis=-1)
        v = jnp.concatenate([vf_ref[t], vr_ref[tr]], axis=-1)
        w = jnp.concatenate([wf_ref[t], wr_ref[tr]], axis=-1)
        a = jnp.concatenate([af_ref[t], ar_ref[tr]], axis=-1)
        kk = k * k_k
        nrm = jnp.sqrt(jnp.sum(kk * kk, axis=0, keepdims=True))
        kk = kk / jnp.maximum(nrm, 1e-12)
        nkk = -kk
        akk = kk * a
        kd = k * (1.0 + (a - 1.0) * k_a)
        op_ref[0] = nkk
        op_ref[1] = w * r
        op_ref[2] = w
        op_ref[3] = akk
        op_ref[4] = kd
        c_a = jnp.sum(akk * r, axis=0, keepdims=True)
        c_k = jnp.sum(kd * r, axis=0, keepdims=True)
        def reduce_keys(jb, acc):
            sa, so = acc
            for jj in range(SCAN_KEY_BLOCK):
                j = jb * SCAN_KEY_BLOCK + jj
                s_j = s_ref[j]
                sa = sa + s_j * op_ref[0, pl.ds(j, 1), :]
                so = so + s_j * op_ref[1, pl.ds(j, 1), :]
            return sa, so

        zero = jnp.zeros_like(v)
        sa, so = lax.fori_loop(0, n // SCAN_KEY_BLOCK, reduce_keys, (zero, zero))
        o = so + sa * c_a + v * c_k
        of_ref[t] = o[:, :bh]
        or_ref[tr] = o[:, bh:]

        def update_keys(jb, c):
            for jj in range(SCAN_KEY_BLOCK):
                j = jb * SCAN_KEY_BLOCK + jj
                s_ref[j] = (s_ref[j] * op_ref[2, pl.ds(j, 1), :] + sa * op_ref[3, pl.ds(j, 1), :]
                            + v * op_ref[4, pl.ds(j, 1), :])
            return c

        lax.fori_loop(0, n // SCAN_KEY_BLOCK, update_keys, 0)
        return carry

    lax.fori_loop(0, tb, step, 0)


def rwkv_scan(r, k, v, w0, w1, a0, a1, k_k, k_a, ctx_len, tb=32):
    steps, n, bh = r.shape
    while steps % tb or ctx_len % tb:
        tb //= 2
    nblk, nblk_ctx = steps // tb, ctx_len // tb

    def rev_blk(i):
        return jnp.where(i < nblk_ctx, nblk_ctx - 1 - i, nblk + nblk_ctx - 1 - i)

    fwd = pl.BlockSpec((tb, n, bh), lambda i: (i, 0, 0))
    rev = pl.BlockSpec((tb, n, bh), lambda i: (rev_blk(i), 0, 0))
    tab = pl.BlockSpec((n, 2 * bh), lambda i: (0, 0))
    out = jax.ShapeDtypeStruct((steps, n, bh), F32)
    return pl.pallas_call(
        functools.partial(_rwkv_scan_body, tb=tb, n=n, bh=bh),
        grid=(nblk,),
        in_specs=[fwd, rev, fwd, rev, fwd, rev, fwd, rev, fwd, rev, tab, tab],
        out_specs=[fwd, rev],
        out_shape=[out, out],
        scratch_shapes=[pltpu.VMEM((n, n, 2 * bh), F32), pltpu.VMEM((5, n, 2 * bh), F32)],
        compiler_params=_cparams(("arbitrary",)),
        name="rwkv_scan",
    )(r, r, k, k, v, v, w0, w1, a0, a1, k_k, k_a)


def _rwkv_post_body(of_ref, or_ref, r_ref, k_ref, v_ref, a0_ref, a1_ref, g_ref, ka_ref, rk_ref, lg_ref, lb_ref,
                    y_ref):
    o = of_ref[...] + or_ref[...]
    mean = jnp.mean(o, axis=1, keepdims=True)
    var = jnp.mean(jnp.square(o - mean), axis=1, keepdims=True)
    on = (o - mean) * lax.rsqrt(var + RW_GN_EPS) * lg_ref[...] + lb_ref[...]
    r = r_ref[...]
    k = k_ref[...]
    k_a = ka_ref[...]
    rk = rk_ref[...]
    kd0 = k * (1.0 + (a0_ref[...] - 1.0) * k_a)
    kd1 = k * (1.0 + (a1_ref[...] - 1.0) * k_a)
    bonus = (jnp.sum(r * kd0 * rk, axis=1, keepdims=True)
             + jnp.sum(r * kd1 * rk, axis=1, keepdims=True)) * v_ref[...]
    y = (on + bonus)
    y_ref[...] = (y * g_ref[...]).astype(y_ref.dtype)


def rwkv_post(o_f, o_r, r, k, v, a0, a1, gate, k_a, r_k, ln_g, ln_b, tb=32):
    steps, n, ch = o_f.shape
    while steps % tb:
        tb //= 2
    blk = pl.BlockSpec((tb, n, ch), lambda i: (i, 0, 0))
    tab = pl.BlockSpec((1, n, ch), lambda i: (0, 0, 0))
    return pl.pallas_call(
        _rwkv_post_body,
        grid=(steps // tb,),
        in_specs=[blk] * 8 + [tab] * 4,
        out_specs=blk,
        out_shape=jax.ShapeDtypeStruct((steps, n, ch), F32),
        compiler_params=_cparams(("parallel",)),
        name="rwkv_post",
    )(o_f, o_r, r, k, v, a0, a1, gate, k_a[None], r_k[None], ln_g[None], ln_b[None])


def _step_block_row(s, b, n_ctx_blk, n_lat_blk, n_batch):
    return jnp.where(s < n_ctx_blk, n_batch * n_lat_blk + b * n_ctx_blk + s, b * n_lat_blk + s - n_ctx_blk)


def _to_steps_body(*refs, n_heads):
    x_refs, o_ref = refs[:-1], refs[-1]
    parts = []
    for x_ref in x_refs:
        x = x_ref[...]
        t = x.shape[0]
        parts.append(jnp.swapaxes(x.reshape(t, n_heads, x.shape[1] // n_heads), 1, 2))
    o_ref[...] = jnp.concatenate(parts, axis=-1)


def tokens_to_steps(x, n_batch, seq, ctx_len, n_heads):
    m, d = x.shape
    n = d // n_heads
    t = RELAYOUT_ROWS
    n_ctx_blk, n_lat_blk = ctx_len // t, seq // t

    def spec(b):
        return pl.BlockSpec((t, d), lambda s: (_step_block_row(s, b, n_ctx_blk, n_lat_blk, n_batch), 0))

    return pl.pallas_call(
        functools.partial(_to_steps_body, n_heads=n_heads),
        grid=(n_ctx_blk + n_lat_blk,),
        in_specs=[spec(b) for b in range(n_batch)],
        out_specs=pl.BlockSpec((t, n, n_batch * n_heads), lambda s: (s, 0, 0)),
        out_shape=jax.ShapeDtypeStruct((ctx_len + seq, n, n_batch * n_heads), x.dtype),
        compiler_params=_cparams(("parallel",)),
        name="tokens_to_steps",
    )(*([x] * n_batch))


def _to_tokens_body(y_ref, o_ref, *, n_heads, n_batch):
    y = y_ref[...]
    t, n = y.shape[0], y.shape[1]
    for b in range(n_batch):
        @pl.when(pl.program_id(1) == b)
        def _():
            yb = y[:, :, b * n_heads:(b + 1) * n_heads]
            o_ref[...] = jnp.swapaxes(yb, 1, 2).reshape(t, n_heads * n)


def steps_to_tokens(y, n_batch, seq, ctx_len, n_heads):
    steps, n, _ = y.shape
    d = n_heads * n
    t = RELAYOUT_ROWS
    n_ctx_blk, n_lat_blk = ctx_len // t, seq // t
    return pl.pallas_call(
        functools.partial(_to_tokens_body, n_heads=n_heads, n_batch=n_batch),
        grid=(n_ctx_blk + n_lat_blk, n_batch),
        in_specs=[pl.BlockSpec((t, n, n_batch * n_heads), lambda s, b: (s, 0, 0))],
        out_specs=pl.BlockSpec((t, d), lambda s, b: (_step_block_row(s, b, n_ctx_blk, n_lat_blk, n_batch), 0)),
        out_shape=jax.ShapeDtypeStruct((n_batch * (seq + ctx_len), d), y.dtype),
        compiler_params=_cparams(("parallel", "arbitrary")),
        name="steps_to_tokens",
    )(y)


def _pad_cols(w, mult=128):
    n = w.shape[-1]
    p = -(-n // mult) * mult - n
    return jnp.pad(w, [(0, 0)] * (w.ndim - 1) + [(0, p)]) if p else w


def _pad_rows(w, mult=128):
    n = w.shape[-2]
    p = -(-n // mult) * mult - n
    return jnp.pad(w, [(0, 0)] * (w.ndim - 2) + [(0, p), (0, 0)]) if p else w


def rwkv_mixer_tokens(h, mu, w_rkv, w0, w1, w2, a0, a1, a2, k_k, k_a, r_k, ln_g, ln_b, g1, g2,
                      tm, n_batch, seq, ctx_len):
    m, d = h.shape
    n_heads, n = r_k.shape
    n_lat = n_batch * seq
    ts = _row_tile(seq, ctx_len, cap=256)
    x_r, x_k, x_v, x_w, x_a, x_g = token_shift_mix(h, mu, ts, seq, ctx_len, n_batch)
    mm = functools.partial(matmul, tm=tm)
    r = mm(x_r, w_rkv, w_index=0)
    k = mm(x_k, w_rkv, w_index=1)
    v = mm(x_v, w_rkv, w_index=2)
    gate = mm(mm(x_g, _pad_cols(g1), act="sigmoid", out_dtype=BF16), _pad_rows(g2))
    dec, aa = [], []
    for dn in range(2):
        lw = mm(x_w, _pad_cols(w1[dn]), act="tanh", out_dtype=BF16)
        dec.append(mm(lw, _pad_rows(w2[dn]), bias=w0[dn], act="decay"))
        la = mm(x_a, _pad_cols(a1[dn]), out_dtype=BF16)
        aa.append(mm(la, _pad_rows(a2[dn]), bias=a0[dn], act="sigmoid"))

    to_seq = functools.partial(tokens_to_steps, n_batch=n_batch, seq=seq, ctx_len=ctx_len, n_heads=n_heads)
    r_t, k_t, v_t, g_t = to_seq(r), to_seq(k), to_seq(v), to_seq(gate)
    d0_t, d1_t, a0_t, a1_t = to_seq(dec[0]), to_seq(dec[1]), to_seq(aa[0]), to_seq(aa[1])

    def table(vec):
        return jnp.tile(vec.astype(F32).reshape(n_heads, n).T[:, None, :], (1, n_batch, 1)).reshape(n, n_batch * n_heads)

    kk_tab, ka_tab = table(k_k), table(k_a)
    o_f, o_r = rwkv_scan(r_t, k_t, v_t, d0_t, d1_t, a0_t, a1_t, jnp.concatenate([kk_tab, kk_tab], axis=-1),
                         jnp.concatenate([ka_tab, ka_tab], axis=-1), ctx_len)
    y_t = rwkv_post(o_f, o_r, r_t, k_t, v_t, a0_t, a1_t, g_t, ka_tab, table(r_k.reshape(-1)),
                    table(ln_g), table(ln_b))
    return steps_to_tokens(y_t, n_batch, seq, ctx_len, n_heads)


def kernel(x, c, ctx, c_ctx, mod_w, mod_b, norm_g, final_g, na_w_qkv, na_w_o, na_rpb, s5_lam_re, s5_lam_im, s5_log_dt, s5_b_re, s5_b_im, s5_c_re, s5_c_im, s5_d, s5_w_glu, rw_mu, rw_w_rkv, rw_w0, rw_w1, rw_w2, rw_a0, rw_a1, rw_a2, rw_k_k, rw_k_a, rw_r_k, rw_ln_g, rw_ln_b, rw_g1, rw_g2, rw_w_o, moe_wg, moe_bg, moe_we, moe_be, moe_w_gu, moe_w_down):
    n_batch, seq, d = x.shape
    ctx_len = ctx.shape[1]
    depth = mod_w.shape[0]
    n_lat = n_batch * seq
    n_ctx = n_batch * ctx_len
    tm = _row_tile(seq, n_ctx)
    tpb = seq // tm
    xs = jnp.concatenate([x.reshape(n_lat, d), ctx.reshape(n_ctx, d)], axis=0).astype(F32)

    cvecs = jnp.concatenate([c.astype(F32), c_ctx.astype(F32)[None]], axis=0)
    rows_pad = -(-(n_batch + 1) // 8) * 8
    cvecs = jnp.pad(cvecs, ((0, rows_pad - n_batch - 1), (0, 0)))
    mods = modulation(cvecs, mod_w, mod_b)[:, :n_batch + 1].reshape(depth, n_batch + 1, 6, 1, d)
    mods = mods.transpose(0, 2, 1, 3, 4)

    for i in range(depth):
        last = i == depth - 1
        sh1, sc1, g1, sh2, sc2, g2 = (mods[i, q] for q in range(6))
        pn = functools.partial(prenorm, tm=tm, tiles_per_batch=tpb, n_batch=n_batch)
        mm = functools.partial(matmul, tm=tm, tiles_per_batch=tpb, n_batch=n_batch)
        mix, j = i % N_MIXERS, i // N_MIXERS
        if mix == 0:
            (hb,) = pn(xs, norm_g[i, 0], 1.0 + sc1, sh1, out_dtypes=(BF16,))
            n_heads = na_rpb.shape[1]
            qscale = jnp.concatenate([jnp.full((d,), (d // n_heads) ** -0.5, F32), jnp.ones((2 * d,), F32)])
            qkv = mm(hb, na_w_qkv, w_index=j, out_dtype=BF16, colscale=qscale)
            kr = (na_rpb.shape[2] + 1) // 2
            o = na_attention(qkv, _na_bias_table(na_rpb[j], GRID_W, seq // GRID_W), kr, n_batch, seq, ctx_len,
                             not last)
            if last:
                xs = xs[:n_lat]
            xs = mm(o, na_w_o, w_index=j, mode="resid", resid=xs, gate=g1)
        elif mix == 1:
            (hf,) = pn(xs, norm_g[i, 0], 1.0 + sc1, sh1, out_dtypes=(F32,))
            wts = _s5_weights(s5_lam_re[j], s5_lam_im[j], s5_log_dt[j], s5_b_re[j], s5_b_im[j],
                              s5_c_re[j], s5_c_im[j], S5_CHUNK)
            y = s5_core(hf, wts, s5_d[j], n_batch, seq, ctx_len)
            if last:
                xs, y = xs[:n_lat], y[:n_lat]
            xs = mm(y, s5_w_glu, w_index=j, mode="glu", resid=xs, gate=g1)
        else:
            (hf,) = pn(xs, norm_g[i, 0], 1.0 + sc1, sh1, out_dtypes=(F32,))
            y = rwkv_mixer_tokens(hf, rw_mu[j], rw_w_rkv[j], rw_w0[j], rw_w1[j], rw_w2[j], rw_a0[j], rw_a1[j],
                                  rw_a2[j], rw_k_k[j], rw_k_a[j], rw_r_k[j], rw_ln_g[j], rw_ln_b[j],
                                  rw_g1[j], rw_g2[j], tm, n_batch, seq, ctx_len)
            if last:
                xs, y = xs[:n_lat], y[:n_lat]
            xs = mm(y, rw_w_o, w_index=j, mode="resid", resid=xs, gate=g1)
        h2, h2_slabs = pn(xs, norm_g[i, 1], 1.0 + sc2, sh2, out_dtypes=(F32, F32), lane_split=(False, True))
        xs = hier_moe(xs, h2, h2_slabs, moe_wg[i], moe_bg[i], moe_we[i], moe_be[i], moe_w_gu, moe_w_down, i, g2,
                      tm, n_batch, seq)
    ones = jnp.ones((n_batch + 1, 1, d), F32)
    (out,) = prenorm(xs[:n_lat], final_g, ones, jnp.zeros_like(ones), tm, tpb, n_batch, out_dtypes=(x.dtype,))
    return out.reshape(n_batch, seq, d)
```

```python
import functools
import math

import jax
import jax.numpy as jnp
from jax import lax
from jax.experimental import pallas as pl
from jax.experimental.pallas import tpu as pltpu

F32 = jnp.float32
BF16 = jnp.bfloat16

GRID_W = 64
N_MIXERS = 3
NORM_EPS = 1e-6
RW_GN_EPS = 64e-5
MOE_TOP_K = 2
MOE_BLOCK = 128
MOE_COMBINE_ROWS = 128
S5_CHUNK = 16
SCAN_KEY_BLOCK = 16
LANES = 128
RELAYOUT_ROWS = 128
NA_Q_ROWS = 4
NA_KEY_ROWS = 12
NEG_BIG = -1e30
VMEM_LIMIT_BYTES = 48 * 1024 * 1024
VMEM_LIMIT_BIG_BYTES = 56 * 1024 * 1024


def _cparams(sem, limit=VMEM_LIMIT_BYTES):
    return pltpu.CompilerParams(dimension_semantics=sem, vmem_limit_bytes=limit)


def _row_tile(n_lat_per_batch, n_ctx_total, cap=512):
    t = cap
    while t > 8 and (n_lat_per_batch % t or (n_ctx_total and n_ctx_total % t)):
        t //= 2
    return t


def _seg_fn(tiles_per_batch, n_batch):
    def seg(i):
        return jnp.minimum(i // tiles_per_batch, n_batch)
    return seg


def _bdot(a, b):
    return jnp.dot(a.astype(BF16), b.astype(BF16), preferred_element_type=F32)


def _split3(x):
    hi = x.astype(BF16)
    r1 = x - hi.astype(F32)
    mid = r1.astype(BF16)
    lo = (r1 - mid.astype(F32)).astype(BF16)
    return hi, mid, lo


def _prenorm_body(x_ref, g_ref, sc_ref, sh_ref, *o_refs):
    x = x_ref[...]
    ms = jnp.mean(x * x, axis=-1, keepdims=True)
    h = (x * lax.rsqrt(ms + NORM_EPS)) * g_ref[...]
    h = h * sc_ref[0] + sh_ref[0]
    for o_ref in o_refs:
        o_ref[...] = h.astype(o_ref.dtype).reshape(o_ref.shape)


def prenorm(x, g, scale1p, shift, tm, tiles_per_batch, n_batch, out_dtypes, lane_split=None):
    m, d = x.shape
    seg = _seg_fn(tiles_per_batch, n_batch)
    row = pl.BlockSpec((tm, d), lambda i: (i, 0))
    row3 = pl.BlockSpec((tm, d // LANES, LANES), lambda i: (i, 0, 0))
    tab = pl.BlockSpec((1, 1, d), lambda i: (seg(i), 0, 0))
    lane_split = lane_split or (False,) * len(out_dtypes)
    outs = pl.pallas_call(
        _prenorm_body,
        grid=(m // tm,),
        in_specs=[row, pl.BlockSpec((1, d), lambda i: (0, 0)), tab, tab],
        out_specs=[row3 if sp else row for sp in lane_split],
        out_shape=[jax.ShapeDtypeStruct((m, d // LANES, LANES) if sp else (m, d), dt)
                   for dt, sp in zip(out_dtypes, lane_split)],
        compiler_params=_cparams(("parallel",)),
    )(x, g.reshape(1, d), scale1p, shift)
    return outs


def _softplus(z):
    return jnp.maximum(z, 0.0) + jnp.log(1.0 + jnp.exp(-jnp.abs(z)))


def _apply_act(y, act):
    if act is None:
        return y
    if act == "tanh":
        return jnp.tanh(y)
    if act == "sigmoid":
        return jax.nn.sigmoid(y)
    if act == "decay":
        return jnp.exp(-jnp.exp(-_softplus(-y) - 0.5))
    raise ValueError(act)


def _mm_body(*refs, mode, act, has_bias, has_scale, has_mask):
    it = iter(refs)
    x_ref = next(it)
    w_ref = next(it)
    w2_ref = next(it) if mode == "glu" else None
    b_ref = next(it) if has_bias else None
    s_ref = next(it) if has_scale else None
    m_ref = next(it) if has_mask else None
    r_ref = next(it) if mode in ("resid", "glu") else None
    g_ref = next(it) if mode in ("resid", "glu") else None
    o_ref = next(it)
    wb_ref = next(it)
    wb2_ref = next(it) if mode == "glu" else None

    @pl.when(pl.program_id(1) == 0)
    def _():
        wb_ref[...] = w_ref[...].astype(BF16)
        if mode == "glu":
            wb2_ref[...] = w2_ref[...].astype(BF16)

    x = x_ref[...].astype(BF16)
    acc = jnp.dot(x, wb_ref[...], preferred_element_type=F32)
    if mode == "glu":
        acc2 = jnp.dot(x, wb2_ref[...], preferred_element_type=F32)
        acc = acc * jax.nn.sigmoid(acc2)
    if has_bias:
        acc = acc + b_ref[...]
    acc = _apply_act(acc, act)
    if has_scale:
        acc = acc * s_ref[...]
    if has_mask:
        acc = acc * m_ref[...]
    if mode in ("resid", "glu"):
        acc = r_ref[...] + g_ref[0] * acc
    o_ref[...] = acc.astype(o_ref.dtype)


def matmul(x, w, *, tm, tn=512, out_dtype=F32, mode="plain", act=None, bias=None, colscale=None,
           resid=None, gate=None, tiles_per_batch=None, n_batch=None, w_index=0, tilemask=None):
    m, k = x.shape
    if w.ndim == 2:
        w, w_index = w[None], 0
    n = w.shape[2] // 2 if mode == "glu" else w.shape[2]
    tn = min(tn, n)
    assert m % tm == 0 and n % tn == 0, (m, tm, n, tn)
    nj = n // tn
    in_specs = [pl.BlockSpec((tm, k), lambda j, i: (i, 0)),
                pl.BlockSpec((None, k, tn), lambda j, i: (w_index, 0, j))]
    args = [x, w]
    if mode == "glu":
        in_specs.append(pl.BlockSpec((None, k, tn), lambda j, i: (w_index, 0, j + nj)))
        args.append(w)
    col = pl.BlockSpec((1, tn), lambda j, i: (0, j))
    if bias is not None:
        in_specs.append(col)
        args.append(bias.reshape(1, n).astype(F32))
    if colscale is not None:
        in_specs.append(col)
        args.append(colscale.reshape(1, n).astype(F32))
    if tilemask is not None:
        assert tilemask.shape == (tm, tn), (tilemask.shape, tm, tn)
        in_specs.append(pl.BlockSpec((tm, tn), lambda j, i: (0, 0)))
        args.append(tilemask.astype(F32))
    if mode in ("resid", "glu"):
        seg = _seg_fn(tiles_per_batch, n_batch)
        in_specs.append(pl.BlockSpec((tm, tn), lambda j, i: (i, j)))
        in_specs.append(pl.BlockSpec((1, 1, tn), lambda j, i: (seg(i), 0, j)))
        args += [resid, gate]
    return pl.pallas_call(
        functools.partial(_mm_body, mode=mode, act=act, has_bias=bias is not None,
                          has_scale=colscale is not None, has_mask=tilemask is not None),
        grid=(nj, m // tm),
        in_specs=in_specs,
        out_specs=pl.BlockSpec((tm, tn), lambda j, i: (i, j)),
        out_shape=jax.ShapeDtypeStruct((m, n), out_dtype),
        scratch_shapes=[pltpu.VMEM((k, tn), BF16)] * (2 if mode == "glu" else 1),
        compiler_params=_cparams(("parallel", "arbitrary")),
    )(*args)


def _router_body(x_ref, w_ref, b_ref, o_ref, *, n_grp, epg):
    xh, xm, xl = _split3(x_ref[...])
    wh, wm, wl = _split3(w_ref[...])
    dot = functools.partial(jnp.dot, preferred_element_type=F32)
    acc = dot(xh, wh) + (dot(xh, wm) + dot(xm, wh)) + (dot(xh, wl) + dot(xm, wm) + dot(xl, wh))
    acc = acc + b_ref[...]
    lane = lax.broadcasted_iota(jnp.int32, acc.shape, 1)
    lane_f = lane.astype(F32)
    far = 1e9

    def first_max(vals):
        m = jnp.max(vals, axis=-1, keepdims=True)
        return m, jnp.min(jnp.where(vals == m, lane_f, far), axis=-1, keepdims=True)

    gl = jnp.where(lane < n_grp, acc, NEG_BIG)
    gmax, grp = first_max(gl)
    p_grp = 1.0 / jnp.sum(jnp.exp(gl - gmax), axis=-1, keepdims=True)
    lo = n_grp + grp * epg
    el = jnp.where((lane_f >= lo) & (lane_f < lo + epg), acc, NEG_BIG)
    m1, i1 = first_max(el)
    m2, i2 = first_max(jnp.where(lane_f == i1, NEG_BIG, el))
    e21 = jnp.exp(m2 - m1)
    g1 = p_grp / (1.0 + e21)
    g2 = p_grp * e21 / (1.0 + e21)
    out = jnp.where(lane == 0, i1 - n_grp, jnp.where(lane == 1, i2 - n_grp,
                                                    jnp.where(lane == 2, g1, jnp.where(lane == 3, g2, 0.0))))
    o_ref[...] = out


def router(x, w, bias, tm, n_grp, epg):
    m, k = x.shape
    n = w.shape[1]
    return pl.pallas_call(
        functools.partial(_router_body, n_grp=n_grp, epg=epg),
        grid=(m // tm,),
        in_specs=[pl.BlockSpec((tm, k), lambda i: (i, 0)), pl.BlockSpec((k, n), lambda i: (0, 0)),
                  pl.BlockSpec((1, n), lambda i: (0, 0))],
        out_specs=pl.BlockSpec((tm, n), lambda i: (i, 0)),
        out_shape=jax.ShapeDtypeStruct((m, n), F32),
        compiler_params=_cparams(("parallel",)),
        name="moe_router",
    )(x, w, bias.reshape(1, n))


def _mod_body(s_ref, w_ref, b_ref, o_ref):
    s = s_ref[...]
    s = s * jax.nn.sigmoid(s)
    o_ref[0] = _bdot(s, w_ref[0]) + b_ref[0]


def modulation(cvecs, mod_w, mod_b, tn=768):
    depth, d, n = mod_w.shape
    rows = cvecs.shape[0]
    while n % tn:
        tn //= 2
    return pl.pallas_call(
        _mod_body,
        grid=(depth, n // tn),
        in_specs=[pl.BlockSpec((rows, d), lambda l, j: (0, 0)),
                  pl.BlockSpec((1, d, tn), lambda l, j: (l, 0, j)),
                  pl.BlockSpec((1, 1, tn), lambda l, j: (l, 0, j))],
        out_specs=pl.BlockSpec((1, rows, tn), lambda l, j: (l, 0, j)),
        out_shape=jax.ShapeDtypeStruct((depth, rows, n), F32),
        compiler_params=_cparams(("parallel", "parallel")),
    )(cvecs, mod_w, mod_b.reshape(depth, 1, n))


def _na_key_start(i, rows, kr):
    return jnp.clip(i * NA_Q_ROWS - kr // 2, 0, rows - NA_KEY_ROWS)


def _na_bias_table(rpb, grid_w, rows):
    n_heads = rpb.shape[0]
    kr = (rpb.shape[1] + 1) // 2
    kcw = (rpb.shape[2] + 1) // 2
    n_blk = rows // NA_Q_ROWS
    q = jnp.arange(grid_w)[:, None]
    kc = jnp.arange(grid_w)[None, :]
    win = jnp.clip(q - kcw // 2, 0, grid_w - kcw)
    col_ok = (kc >= win) & (kc < win + kcw)
    dc = jnp.clip(kc - q, 1 - kcw, kcw - 1) + kcw - 1
    blk = jnp.array([0, min(1, n_blk - 1), n_blk - 1])[:, None, None]
    r = blk * NA_Q_ROWS + jnp.arange(NA_Q_ROWS)[None, :, None]
    krow = _na_key_start(blk, rows, kr) + jnp.arange(NA_KEY_ROWS)[None, None, :]
    rs = jnp.clip(r - kr // 2, 0, rows - kr)
    row_ok = (krow >= rs) & (krow < rs + kr)
    dr = jnp.clip(krow - r + kr - 1, 0, 2 * kr - 2)
    hi = lax.Precision.HIGHEST
    sel_r = jax.nn.one_hot(dr.reshape(-1), rpb.shape[1], dtype=F32)
    sel_c = jax.nn.one_hot(dc.reshape(-1), rpb.shape[2], dtype=F32)
    tab = jnp.einsum("ra,has->hrs", sel_r, jnp.einsum("hab,sb->has", rpb.astype(F32), sel_c, precision=hi),
                     precision=hi)
    tab = tab.reshape(n_heads, 3, NA_Q_ROWS, NA_KEY_ROWS, grid_w, grid_w)
    ok = row_ok[None, :, :, :, None, None] & col_ok[None, None, None, None]
    tab = jnp.where(ok, tab, NEG_BIG)
    return tab.transpose(0, 1, 2, 4, 3, 5).reshape(n_heads, 3, NA_Q_ROWS * grid_w, NA_KEY_ROWS * grid_w)


def _na_body(q_ref, k_ref, v_ref, kc_ref, vc_ref, bias_ref, o_ref, *, hpb, dh, grid_w, kr, rows):
    ks = _na_key_start(pl.program_id(2), rows, kr)
    start = pl.multiple_of(ks * grid_w, grid_w)
    nt = (((1,), (1,)), ((), ()))
    for h in range(hpb):
        sl = slice(h * dh, (h + 1) * dh)
        q = q_ref[:, sl]
        k = k_ref[pl.ds(start, NA_KEY_ROWS * grid_w), sl]
        v = v_ref[pl.ds(start, NA_KEY_ROWS * grid_w), sl]
        s = lax.dot_general(q, k, nt, preferred_element_type=F32) + bias_ref[h, 0]
        sc = lax.dot_general(q, kc_ref[:, sl], nt, preferred_element_type=F32)
        m = jnp.maximum(jnp.max(s, axis=-1, keepdims=True), jnp.max(sc, axis=-1, keepdims=True))
        p = jnp.exp(s - m)
        pc = jnp.exp(sc - m)
        den = jnp.sum(p, axis=-1, keepdims=True) + jnp.sum(pc, axis=-1, keepdims=True)
        o = (jnp.dot(p.astype(BF16), v, preferred_element_type=F32)
             + jnp.dot(pc.astype(BF16), vc_ref[:, sl], preferred_element_type=F32))
        o_ref[:, sl] = (o / den).astype(o_ref.dtype)


def _ctx_attn_body(q_ref, k_ref, v_ref, o_ref, *, hpb, dh):
    nt = (((1,), (1,)), ((), ()))
    for h in range(hpb):
        sl = slice(h * dh, (h + 1) * dh)
        s = lax.dot_general(q_ref[:, sl], k_ref[:, sl], nt, preferred_element_type=F32)
        m = jnp.max(s, axis=-1, keepdims=True)
        p = jnp.exp(s - m)
        den = jnp.sum(p, axis=-1, keepdims=True)
        o = jnp.dot(p.astype(BF16), v_ref[:, sl], preferred_element_type=F32)
        o_ref[:, sl] = (o / den).astype(o_ref.dtype)


def na_attention(qkv, bias_tab, kr, n_batch, seq, ctx_len, need_ctx):
    d = qkv.shape[1] // 3
    n_heads = bias_tab.shape[0]
    dh = d // n_heads
    hpb = max(1, min(n_heads, 256 // dh))
    bw = hpb * dh
    n_d = d // bw
    rows = seq // GRID_W
    n_blk = rows // NA_Q_ROWS
    qb = NA_Q_ROWS * GRID_W
    n_lat = n_batch * seq
    ctx_blk0 = n_lat // ctx_len

    def cls_of(i):
        return jnp.where(i == 0, 0, jnp.where(i == n_blk - 1, 2, 1))

    o_lat = pl.pallas_call(
        functools.partial(_na_body, hpb=hpb, dh=dh, grid_w=GRID_W, kr=kr, rows=rows),
        grid=(n_batch, n_d, n_blk),
        in_specs=[
            pl.BlockSpec((qb, bw), lambda b, g, i: (b * n_blk + i, g)),
            pl.BlockSpec((seq, bw), lambda b, g, i: (b, n_d + g)),
            pl.BlockSpec((seq, bw), lambda b, g, i: (b, 2 * n_d + g)),
            pl.BlockSpec((ctx_len, bw), lambda b, g, i: (ctx_blk0 + b, n_d + g)),
            pl.BlockSpec((ctx_len, bw), lambda b, g, i: (ctx_blk0 + b, 2 * n_d + g)),
            pl.BlockSpec((hpb, 1, qb, NA_KEY_ROWS * GRID_W), lambda b, g, i: (g, cls_of(i), 0, 0)),
        ],
        out_specs=pl.BlockSpec((qb, bw), lambda b, g, i: (b * n_blk + i, g)),
        out_shape=jax.ShapeDtypeStruct((n_lat, d), BF16),
        compiler_params=_cparams(("parallel", "parallel", "arbitrary")),
        name="na_attention",
    )(qkv, qkv, qkv, qkv, qkv, bias_tab)
    if not need_ctx:
        return o_lat
    o_ctx = pl.pallas_call(
        functools.partial(_ctx_attn_body, hpb=hpb, dh=dh),
        grid=(n_batch, n_d),
        in_specs=[
            pl.BlockSpec((ctx_len, bw), lambda b, g: (ctx_blk0 + b, g)),
            pl.BlockSpec((ctx_len, bw), lambda b, g: (ctx_blk0 + b, n_d + g)),
            pl.BlockSpec((ctx_len, bw), lambda b, g: (ctx_blk0 + b, 2 * n_d + g)),
        ],
        out_specs=pl.BlockSpec((ctx_len, bw), lambda b, g: (b, g)),
        out_shape=jax.ShapeDtypeStruct((n_batch * ctx_len, d), BF16),
        compiler_params=_cparams(("parallel", "parallel")),
    )(qkv, qkv, qkv)
    return jnp.concatenate([o_lat, o_ctx], axis=0)


def _expert_body(be_ref, idx_ref, idx_next_ref, x_hbm, wgu_ref, wd_ref, o_ref, xbuf, sem, wgu_s, wd_s, *,
                 hidden, n_blocks, n_slab):
    j = pl.program_id(0)
    cur = j % 2

    def row_copy(idx, buf, r):
        return pltpu.make_async_copy(x_hbm.at[pl.ds(idx[0, 0, r], 1)], xbuf.at[buf, pl.ds(r, 1)], sem.at[buf])

    @pl.when(j == 0)
    def _():
        for r in range(MOE_BLOCK):
            row_copy(idx_ref, 0, r).start(priority=r % 2)

    prev = be_ref[jnp.maximum(j - 1, 0)]
    changed = jnp.logical_or(j == 0, be_ref[j] != prev)

    @pl.when(changed)
    def _():
        wgu_s[...] = wgu_ref[...].astype(BF16)
        wd_s[...] = wd_ref[...].astype(BF16)

    for r in range(MOE_BLOCK):
        row_copy(idx_ref, cur, r).wait()

    for r in range(MOE_BLOCK):
        row_copy(idx_next_ref, 1 - cur, r).start(priority=r % 2)
    xb = xbuf.at[cur]
    gu = None
    for c in range(0, n_slab, 2):
        xa = jnp.concatenate([xb[:, c, :], xb[:, c + 1, :]], axis=-1).astype(BF16)
        part = jnp.dot(xa, wgu_s[c * LANES:(c + 2) * LANES, :], preferred_element_type=F32)
        gu = part if gu is None else gu + part
    g = gu[:, :hidden]
    u = gu[:, hidden:]
    a = (g * jax.nn.sigmoid(g) * u).astype(BF16)
    o_ref[...] = jnp.dot(a, wd_s[...], preferred_element_type=F32)

    @pl.when(j == n_blocks - 1)
    def _():
        for r in range(MOE_BLOCK):
            row_copy(idx_next_ref, 1 - cur, r).wait()


def expert_blocks(x, slot_tok, blk_e, w_gu, w_down, layer):
    n_slab = x.shape[1]
    d = n_slab * LANES
    hidden = w_down.shape[2]
    n_blocks = blk_e.shape[0]
    idx = slot_tok.reshape(n_blocks, 1, MOE_BLOCK)
    return pl.pallas_call(
        functools.partial(_expert_body, hidden=hidden, n_blocks=n_blocks, n_slab=n_slab),
        grid_spec=pltpu.PrefetchScalarGridSpec(
            num_scalar_prefetch=1,
            grid=(n_blocks,),
            in_specs=[pl.BlockSpec((1, 1, MOE_BLOCK), lambda j, be: (j, 0, 0), memory_space=pltpu.SMEM),
                      pl.BlockSpec((1, 1, MOE_BLOCK), lambda j, be: (jnp.minimum(j + 1, n_blocks - 1), 0, 0),
                                   memory_space=pltpu.SMEM),
                      pl.BlockSpec(memory_space=pl.ANY),
                      pl.BlockSpec((None, None, d, 2 * hidden), lambda j, be: (layer, be[j], 0, 0)),
                      pl.BlockSpec((None, None, hidden, d), lambda j, be: (layer, be[j], 0, 0))],
            out_specs=pl.BlockSpec((MOE_BLOCK, d), lambda j, be: (j, 0)),
            scratch_shapes=[pltpu.VMEM((2, MOE_BLOCK, n_slab, LANES), F32), pltpu.SemaphoreType.DMA((2,)),
                            pltpu.VMEM((d, 2 * hidden), BF16), pltpu.VMEM((hidden, d), BF16)]),
        out_shape=jax.ShapeDtypeStruct((n_blocks * MOE_BLOCK, d), F32),
        compiler_params=_cparams(("arbitrary",), VMEM_LIMIT_BIG_BYTES),
        name="moe_experts",
    )(blk_e, idx, idx, x, w_gu, w_down)


def _combine_body(idx_ref, idx_next_ref, route_ref, x_ref, g_ref, yb_hbm, o_ref, ybuf, sem, *, tmc, n_tiles):
    i = pl.program_id(0)
    cur = i % 2

    def row_copy(idx, buf, k, r):
        return pltpu.make_async_copy(yb_hbm.at[pl.ds(idx[0, 0, k * tmc + r], 1)],
                                     ybuf.at[buf, k, pl.ds(r, 1)], sem.at[buf])

    def start_all(idx, buf):
        for k in range(MOE_TOP_K):
            for r in range(tmc):
                row_copy(idx, buf, k, r).start(priority=r % 2)

    @pl.when(i == 0)
    def _():
        start_all(idx_ref, 0)

    @pl.when(i + 1 < n_tiles)
    def _():
        start_all(idx_next_ref, 1 - cur)

    for k in range(MOE_TOP_K):
        for r in range(tmc):
            row_copy(idx_ref, cur, k, r).wait()
    route = route_ref[...]
    f = ybuf[cur, 0] * route[:, MOE_TOP_K:MOE_TOP_K + 1]
    for k in range(1, MOE_TOP_K):
        f = f + ybuf[cur, k] * route[:, MOE_TOP_K + k:MOE_TOP_K + k + 1]
    o_ref[...] = x_ref[...] + g_ref[0] * f


def moe_combine(xs, yb, slot_of_asg, route, gate_tab, n_batch, seq):
    n_tok, d = xs.shape
    tmc = MOE_COMBINE_ROWS
    n_tiles = n_tok // tmc
    idx = slot_of_asg.reshape(n_tiles, tmc, MOE_TOP_K).transpose(0, 2, 1).reshape(n_tiles, 1, MOE_TOP_K * tmc)
    seg = _seg_fn(seq // tmc, n_batch)
    return pl.pallas_call(
        functools.partial(_combine_body, tmc=tmc, n_tiles=n_tiles),
        grid=(n_tiles,),
        in_specs=[pl.BlockSpec((1, 1, MOE_TOP_K * tmc), lambda i: (i, 0, 0), memory_space=pltpu.SMEM),
                  pl.BlockSpec((1, 1, MOE_TOP_K * tmc), lambda i: (jnp.minimum(i + 1, n_tiles - 1), 0, 0),
                               memory_space=pltpu.SMEM),
                  pl.BlockSpec((tmc, route.shape[1]), lambda i: (i, 0)),
                  pl.BlockSpec((tmc, d), lambda i: (i, 0)),
                  pl.BlockSpec((1, 1, d), lambda i: (seg(i), 0, 0)),
                  pl.BlockSpec(memory_space=pl.ANY)],
        out_specs=pl.BlockSpec((tmc, d), lambda i: (i, 0)),
        out_shape=jax.ShapeDtypeStruct((n_tok, d), F32),
        scratch_shapes=[pltpu.VMEM((2, MOE_TOP_K, tmc, d), F32), pltpu.SemaphoreType.DMA((2,))],
        compiler_params=_cparams(("arbitrary",)),
        name="moe_combine",
    )(idx, idx, route, xs, gate_tab, yb)


def hier_moe(xs, h, h_slabs, wg, bg, we, be, w_gu, w_down, layer, gate_tab, tm, n_batch, seq):
    n_tok, d = h.shape
    n_grp = wg.shape[1]
    n_exp = we.shape[1]
    epg = n_exp // n_grp
    n_logit = n_grp + n_exp
    n_pad = -(-n_logit // 128) * 128
    wcat = jnp.pad(jnp.concatenate([wg, we], axis=1), ((0, 0), (0, n_pad - n_logit)))
    bcat = jnp.pad(jnp.concatenate([bg, be], axis=0), (0, n_pad - n_logit))
    route = router(h, wcat, bcat, tm, n_grp, epg)

    flat_e = route[:, :MOE_TOP_K].astype(jnp.int32).reshape(-1)
    n_asg = flat_e.shape[0]
    onehot = (flat_e[:, None] == jnp.arange(n_exp, dtype=jnp.int32)[None, :]).astype(F32)
    ck = 128
    oh3 = onehot.reshape(n_asg // ck, ck, n_exp)
    tri = jnp.tril(jnp.ones((ck, ck), F32))
    within = jnp.einsum("ij,cjk->cik", tri, oh3, precision=lax.Precision.HIGHEST)
    tot = within[:, -1, :]
    before = jnp.cumsum(tot, axis=0) - tot
    rank = jnp.sum(oh3 * (within + before[:, None, :]), axis=-1).reshape(n_asg).astype(jnp.int32) - 1
    counts = jnp.sum(tot, axis=0).astype(jnp.int32)
    padded = (counts + MOE_BLOCK - 1) // MOE_BLOCK * MOE_BLOCK
    pad_end = jnp.cumsum(padded)
    pad_start = pad_end - padded
    slot = jnp.sum(onehot * pad_start.astype(F32)[None, :], axis=-1).astype(jnp.int32) + rank
    n_blocks = -(-n_asg // MOE_BLOCK) + n_exp
    slot_tok = jnp.zeros((n_blocks * MOE_BLOCK,), jnp.int32).at[slot].set(
        jnp.arange(n_asg, dtype=jnp.int32) // MOE_TOP_K)
    blk_start = jnp.arange(n_blocks, dtype=jnp.int32) * MOE_BLOCK
    blk_e = jnp.minimum(jnp.sum((pad_end[None, :] <= blk_start[:, None]).astype(jnp.int32), axis=1), n_exp - 1)
    yb = expert_blocks(h_slabs, slot_tok, blk_e, w_gu, w_down, layer)
    return moe_combine(xs, yb, slot.reshape(n_tok, MOE_TOP_K), route, gate_tab, n_batch, seq)


def _s5_weights(lam_re, lam_im, log_dt, b_re, b_im, c_re, c_im, t_chunk):
    n_grp, n_state = lam_re.shape[1], lam_re.shape[2]
    cg = b_re.shape[-1]
    lam = lax.complex(lam_re.astype(F32), lam_im.astype(F32))
    dt = jnp.exp(log_dt.astype(F32))[..., None]
    lam_bar = jnp.exp(lam * dt)
    b_bar = ((lam_bar - 1.0) / lam)[..., None] * lax.complex(b_re.astype(F32), b_im.astype(F32))
    c_mat = lax.complex(c_re.astype(F32), c_im.astype(F32))
    ks = jnp.arange(t_chunk + 1, dtype=F32)
    pw = jnp.exp((lam * dt)[..., None] * ks)
    hi = lax.Precision.HIGHEST
    kern = jnp.einsum("dgcp,dgpk,dgpi->dgkci", c_mat, pw[..., :t_chunk], b_bar, precision=hi).real
    t_idx = jnp.arange(t_chunk)
    lag = t_idx[None, :] - t_idx[:, None]
    k_f = kern[0][:, jnp.clip(lag, 0, t_chunk - 1)]
    k_r = kern[1][:, jnp.clip(-lag, 0, t_chunk - 1)]
    toep = (jnp.where((lag >= 0)[None, :, :, None, None], k_f, 0.0)
            + jnp.where((lag <= 0)[None, :, :, None, None], k_r, 0.0))
    toep = toep.transpose(0, 1, 4, 2, 3).reshape(n_grp, t_chunk * cg, t_chunk * cg)
    pf = pw[0][..., :t_chunk][..., ::-1]
    pr = pw[1][..., :t_chunk]
    bend_f = jnp.einsum("gps,gpc->gscp", pf, b_bar[0]).reshape(n_grp, t_chunk * cg, n_state)
    bend_r = jnp.einsum("gps,gpc->gscp", pr, b_bar[1]).reshape(n_grp, t_chunk * cg, n_state)
    bend = jnp.stack([bend_f.real, bend_f.imag, bend_r.real, bend_r.imag], axis=1)
    of = jnp.einsum("gcp,gpt->gptc", c_mat[0], pw[0][..., 1:]).reshape(n_grp, n_state, t_chunk * cg)
    orv = jnp.einsum("gcp,gpt->gptc", c_mat[1], pw[1][..., 1:][..., ::-1]).reshape(n_grp, n_state, t_chunk * cg)
    cout = jnp.stack([of.real, -of.imag, orv.real, -orv.imag], axis=1)

    gsz = LANES // cg
    nb = n_grp // gsz
    tile = 512

    def expand(n_outer, n_inner):
        dst = jnp.arange(n_outer * gsz * n_inner)
        src = (dst // (gsz * n_inner)) * n_inner + dst % n_inner
        return (jnp.arange(n_outer * n_inner)[:, None] == src[None, :]).astype(F32)

    def grp(period, width):
        return (jnp.arange(tile) % period) // width

    def own_group(row_grp, col_grp):
        return (row_grp[:, None] == col_grp[None, :]).astype(F32)

    e_tc, e_xp = expand(t_chunk, cg), expand(4, n_state)
    grp_ch, grp_st = grp(LANES, cg), grp(gsz * n_state, n_state)
    kin = t_chunk * cg
    xt = toep.reshape(nb, gsz, t_chunk, cg, kin).transpose(0, 2, 1, 3, 4).reshape(n_grp * kin, kin)
    wt = matmul(xt, e_tc, tm=tile, tn=tile, out_dtype=BF16, tilemask=own_group(grp_ch, grp_ch))
    wt = wt.reshape(nb, t_chunk, LANES, t_chunk * LANES)
    xb = bend.reshape(nb, gsz, 4, t_chunk, cg, n_state).transpose(0, 3, 1, 4, 2, 5).reshape(n_grp * kin, 4 * n_state)
    wb = matmul(xb, e_xp, tm=tile, tn=tile, out_dtype=BF16, tilemask=own_group(grp_ch, grp_st))
    wb = wb.reshape(nb, t_chunk, LANES, 4 * gsz * n_state)
    xc = cout.reshape(nb, gsz, 4, n_state, kin).transpose(0, 2, 1, 3, 4).reshape(4 * n_grp * n_state, kin)
    wc = matmul(xc, e_tc, tm=tile, tn=tile, out_dtype=BF16, tilemask=own_group(grp_st, grp_ch))
    wc = wc.reshape(nb, 4 * gsz * n_state, t_chunk * LANES)
    lam_t = pw[..., t_chunk]
    lam_t4 = jnp.stack([lam_t[0].real, lam_t[0].imag, lam_t[1].real, lam_t[1].imag]).reshape(4, n_grp * n_state)
    return wt, wb, wc, lam_t4


def _s5_a_body(h_ref, wb_ref, e_ref, *, t_chunk):
    acc = jnp.dot(h_ref[:, 0, :].astype(BF16), wb_ref[0], preferred_element_type=F32)
    for s in range(1, t_chunk):
        acc = acc + jnp.dot(h_ref[:, s, :].astype(BF16), wb_ref[s], preferred_element_type=F32)
    e_ref[...] = acc


def _s5_b_body(e_ref, lam_ref, x_ref, *, n_batch, n_lat_chunk, n_ctx_chunk, width):
    ctx0 = n_batch * n_lat_chunk
    n_chunk = n_lat_chunk + n_ctx_chunk

    def comp(x):
        return slice(x * width, (x + 1) * width)

    lfr, lfi, lrr, lri = (lam_ref[:, comp(x)] for x in range(4))

    for b in range(n_batch):
        def fwd(n, carry):
            sr, si = carry
            row = jnp.where(n < n_ctx_chunk, ctx0 + b * n_ctx_chunk + n, b * n_lat_chunk + n - n_ctx_chunk)
            x_ref[pl.ds(row, 1), comp(0)] = sr
            x_ref[pl.ds(row, 1), comp(1)] = si
            er = e_ref[pl.ds(row, 1), comp(0)]
            ei = e_ref[pl.ds(row, 1), comp(1)]
            return lfr * sr - lfi * si + er, lfr * si + lfi * sr + ei

        def rev(n, carry):
            sr, si = carry
            row = jnp.where(n < n_ctx_chunk, ctx0 + b * n_ctx_chunk + n_ctx_chunk - 1 - n,
                            b * n_lat_chunk + n_chunk - 1 - n)
            x_ref[pl.ds(row, 1), comp(2)] = sr
            x_ref[pl.ds(row, 1), comp(3)] = si
            er = e_ref[pl.ds(row, 1), comp(2)]
            ei = e_ref[pl.ds(row, 1), comp(3)]
            return lrr * sr - lri * si + er, lrr * si + lri * sr + ei

        z = jnp.zeros((1, width), F32)
        lax.fori_loop(0, n_chunk, fwd, (z, z))
        lax.fori_loop(0, n_chunk, rev, (z, z))


def _gelu(y):
    return 0.5 * y * (1.0 + jnp.tanh(0.7978845608028654 * (y + 0.044715 * (y * y * y))))


def _s5_c_body(h_ref, x_ref, wt_ref, wc_ref, dsk_ref, o_ref, *, t_chunk):
    acc = jnp.dot(x_ref[...].astype(BF16), wc_ref[...], preferred_element_type=F32)
    for s in range(t_chunk):
        acc = acc + jnp.dot(h_ref[:, s, :].astype(BF16), wt_ref[s], preferred_element_type=F32)
    dsk = dsk_ref[...]
    for t in range(t_chunk):
        o_ref[:, t, :] = _gelu(acc[:, t * LANES:(t + 1) * LANES] + h_ref[:, t, :] * dsk)


def s5_core(h_f32, weights, d_skip, n_batch, seq, ctx_len):
    wt, wb, wc, lam_t4 = weights
    t = S5_CHUNK
    m, d = h_f32.shape
    nb = wt.shape[0]
    sw = wb.shape[-1]
    width = sw // 4
    rows = m // t
    rbs = rows // 2 if rows % 16 == 0 else rows
    h3 = h_f32.reshape(rows, t, d)
    hblk = pl.BlockSpec((rbs, t, LANES), lambda g, i: (i, 0, g))
    sblk = pl.BlockSpec((rbs, sw), lambda g, i: (i, g))
    e = pl.pallas_call(
        functools.partial(_s5_a_body, t_chunk=t),
        grid=(nb, rows // rbs),
        in_specs=[hblk, pl.BlockSpec((None, t, LANES, sw), lambda g, i: (g, 0, 0, 0))],
        out_specs=sblk,
        out_shape=jax.ShapeDtypeStruct((rows, nb * sw), F32),
        compiler_params=_cparams(("parallel", "arbitrary")),
        name="s5_chunk_states",
    )(h3, wb)

    lam_blk = lam_t4.reshape(4, nb, width).transpose(1, 0, 2).reshape(nb, 1, sw)
    x_in = pl.pallas_call(
        functools.partial(_s5_b_body, n_batch=n_batch, n_lat_chunk=seq // t, n_ctx_chunk=ctx_len // t,
                          width=width),
        grid=(nb,),
        in_specs=[pl.BlockSpec((rows, sw), lambda g: (0, g)), pl.BlockSpec((None, 1, sw), lambda g: (g, 0, 0))],
        out_specs=pl.BlockSpec((rows, sw), lambda g: (0, g)),
        out_shape=jax.ShapeDtypeStruct((rows, nb * sw), F32),
        compiler_params=_cparams(("parallel",)),
        name="s5_chunk_recurrence",
    )(e, lam_blk)

    y3 = pl.pallas_call(
        functools.partial(_s5_c_body, t_chunk=t),
        grid=(nb, rows // rbs),
        in_specs=[hblk, sblk,
                  pl.BlockSpec((None, t, LANES, t * LANES), lambda g, i: (g, 0, 0, 0)),
                  pl.BlockSpec((None, sw, t * LANES), lambda g, i: (g, 0, 0)),
                  pl.BlockSpec((1, LANES), lambda g, i: (0, g))],
        out_specs=hblk,
        out_shape=jax.ShapeDtypeStruct((rows, t, d), F32),
        compiler_params=_cparams(("parallel", "arbitrary"), VMEM_LIMIT_BIG_BYTES),
        name="s5_outputs",
    )(h3, x_in, wt, wc, d_skip.astype(F32).reshape(1, d))
    return y3.reshape(m, d)


def _shift_body(h_ref, hp_ref, hn_ref, mu_ref, *o_refs, tm, tiles_lat, tiles_ctx, n_lat_tiles):
    i = pl.program_id(0)
    h = h_ref[...]
    first = jnp.where(i < n_lat_tiles, i % tiles_lat == 0, (i - n_lat_tiles) % tiles_ctx == 0)
    last = jnp.where(i < n_lat_tiles, i % tiles_lat == tiles_lat - 1,
                     (i - n_lat_tiles) % tiles_ctx == tiles_ctx - 1)
    rid = lax.broadcasted_iota(jnp.int32, h.shape, 0)
    prev_edge = jnp.where(first, 0.0, hp_ref[7:8, :])
    next_edge = jnp.where(last, 0.0, hn_ref[0:1, :])
    prev = jnp.where(rid == 0, prev_edge, pltpu.roll(h, 1, 0))
    nxt = jnp.where(rid == tm - 1, next_edge, pltpu.roll(h, tm - 1, 0))
    xx = 0.5 * (prev + nxt) - h
    for m, o_ref in enumerate(o_refs):
        o_ref[...] = (h + xx * mu_ref[m:m + 1, :]).astype(o_ref.dtype)


def token_shift_mix(h, mu, tm, seq, ctx_len, n_batch):
    m, d = h.shape
    n_mix = mu.shape[0]
    n_lat_tiles = n_batch * seq // tm
    nb8 = m // 8
    per = tm // 8
    row = pl.BlockSpec((tm, d), lambda i: (i, 0))
    return pl.pallas_call(
        functools.partial(_shift_body, tm=tm, tiles_lat=seq // tm, tiles_ctx=max(ctx_len // tm, 1),
                          n_lat_tiles=n_lat_tiles),
        grid=(m // tm,),
        in_specs=[row,
                  pl.BlockSpec((8, d), lambda i: (jnp.maximum(i * per - 1, 0), 0)),
                  pl.BlockSpec((8, d), lambda i: (jnp.minimum((i + 1) * per, nb8 - 1), 0)),
                  pl.BlockSpec((n_mix, d), lambda i: (0, 0))],
        out_specs=[row] * n_mix,
        out_shape=[jax.ShapeDtypeStruct((m, d), BF16)] * n_mix,
        compiler_params=_cparams(("parallel",)),
    )(h, h, h, mu.astype(F32))


def _rwkv_scan_body(rf_ref, rr_ref, kf_ref, kr_ref, vf_ref, vr_ref, wf_ref, wr_ref, af_ref, ar_ref,
                    kk_ref, ka_ref, of_ref, or_ref, s_ref, op_ref, *, tb, n, bh):
    @pl.when(pl.program_id(0) == 0)
    def _():
        s_ref[...] = jnp.zeros_like(s_ref)

    k_k = kk_ref[...]
    k_a = ka_ref[...]

    def step(t, carry):
        tr = tb - 1 - t
        r = jnp.concatenate([rf_ref[t], rr_ref[tr]], axis=-1)
        k = jnp.concatenate([kf_ref[t], kr_ref[tr]], axis=-1)
        v = jnp.concatenate([vf_ref[t], vr_ref[tr]], axis=-1)
        w = jnp.concatenate([wf_ref[t], wr_ref[tr]], axis=-1)
        a = jnp.concatenate([af_ref[t], ar_ref[tr]], axis=-1)
        kk = k * k_k
        nrm = jnp.sqrt(jnp.sum(kk * kk, axis=0, keepdims=True))
        kk = kk / jnp.maximum(nrm, 1e-12)
        nkk = -kk
        akk = kk * a
        kd = k * (1.0 + (a - 1.0) * k_a)
        op_ref[0] = nkk
        op_ref[1] = w * r
        op_ref[2] = w
        op_ref[3] = akk
        op_ref[4] = kd
        c_a = jnp.sum(akk * r, axis=0, keepdims=True)
        c_k = jnp.sum(kd * r, axis=0, keepdims=True)
        def reduce_keys(jb, acc):
            sa, so = acc
            for jj in range(SCAN_KEY_BLOCK):
                j = jb * SCAN_KEY_BLOCK + jj
                s_j = s_ref[j]
                sa = sa + s_j * op_ref[0, pl.ds(j, 1), :]
                so = so + s_j * op_ref[1, pl.ds(j, 1), :]
            return sa, so

        zero = jnp.zeros_like(v)
        sa, so = lax.fori_loop(0, n // SCAN_KEY_BLOCK, reduce_keys, (zero, zero))
        o = so + sa * c_a + v * c_k
        of_ref[t] = o[:, :bh]
        or_ref[tr] = o[:, bh:]

        def update_keys(jb, c):
            for jj in range(SCAN_KEY_BLOCK):
                j = jb * SCAN_KEY_BLOCK + jj
                s_ref[j] = (s_ref[j] * op_ref[2, pl.ds(j, 1), :] + sa * op_ref[3, pl.ds(j, 1), :]
                            + v * op_ref[4, pl.ds(j, 1), :])
            return c

        lax.fori_loop(0, n // SCAN_KEY_BLOCK, update_keys, 0)
        return carry

    lax.fori_loop(0, tb, step, 0)


def rwkv_scan(r, k, v, w0, w1, a0, a1, k_k, k_a, ctx_len, tb=32):
    steps, n, bh = r.shape
    while steps % tb or ctx_len % tb:
        tb //= 2
    nblk, nblk_ctx = steps // tb, ctx_len // tb

    def rev_blk(i):
        return jnp.where(i < nblk_ctx, nblk_ctx - 1 - i, nblk + nblk_ctx - 1 - i)

    fwd = pl.BlockSpec((tb, n, bh), lambda i: (i, 0, 0))
    rev = pl.BlockSpec((tb, n, bh), lambda i: (rev_blk(i), 0, 0))
    tab = pl.BlockSpec((n, 2 * bh), lambda i: (0, 0))
    out = jax.ShapeDtypeStruct((steps, n, bh), F32)
    return pl.pallas_call(
        functools.partial(_rwkv_scan_body, tb=tb, n=n, bh=bh),
        grid=(nblk,),
        in_specs=[fwd, rev, fwd, rev, fwd, rev, fwd, rev, fwd, rev, tab, tab],
        out_specs=[fwd, rev],
        out_shape=[out, out],
        scratch_shapes=[pltpu.VMEM((n, n, 2 * bh), F32), pltpu.VMEM((5, n, 2 * bh), F32)],
        compiler_params=_cparams(("arbitrary",)),
        name="rwkv_scan",
    )(r, r, k, k, v, v, w0, w1, a0, a1, k_k, k_a)


def _rwkv_post_body(of_ref, or_ref, r_ref, k_ref, v_ref, a0_ref, a1_ref, g_ref, ka_ref, rk_ref, lg_ref, lb_ref,
                    y_ref):
    o = of_ref[...] + or_ref[...]
    mean = jnp.mean(o, axis=1, keepdims=True)
    var = jnp.mean(jnp.square(o - mean), axis=1, keepdims=True)
    on = (o - mean) * lax.rsqrt(var + RW_GN_EPS) * lg_ref[...] + lb_ref[...]
    r = r_ref[...]
    k = k_ref[...]
    k_a = ka_ref[...]
    rk = rk_ref[...]
    kd0 = k * (1.0 + (a0_ref[...] - 1.0) * k_a)
    kd1 = k * (1.0 + (a1_ref[...] - 1.0) * k_a)
    bonus = (jnp.sum(r * kd0 * rk, axis=1, keepdims=True)
             + jnp.sum(r * kd1 * rk, axis=1, keepdims=True)) * v_ref[...]
    y = (on + bonus)
    y_ref[...] = (y * g_ref[...]).astype(y_ref.dtype)


def rwkv_post(o_f, o_r, r, k, v, a0, a1, gate, k_a, r_k, ln_g, ln_b, tb=32):
    steps, n, ch = o_f.shape
    while steps % tb:
        tb //= 2
    blk = pl.BlockSpec((tb, n, ch), lambda i: (i, 0, 0))
    tab = pl.BlockSpec((1, n, ch), lambda i: (0, 0, 0))
    return pl.pallas_call(
        _rwkv_post_body,
        grid=(steps // tb,),
        in_specs=[blk] * 8 + [tab] * 4,
        out_specs=blk,
        out_shape=jax.ShapeDtypeStruct((steps, n, ch), F32),
        compiler_params=_cparams(("parallel",)),
        name="rwkv_post",
    )(o_f, o_r, r, k, v, a0, a1, gate, k_a[None], r_k[None], ln_g[None], ln_b[None])


def _step_block_row(s, b, n_ctx_blk, n_lat_blk, n_batch):
    return jnp.where(s < n_ctx_blk, n_batch * n_lat_blk + b * n_ctx_blk + s, b * n_lat_blk + s - n_ctx_blk)


def _to_steps_body(*refs, n_heads):
    x_refs, o_ref = refs[:-1], refs[-1]
    parts = []
    for x_ref in x_refs:
        x = x_ref[...]
        t = x.shape[0]
        parts.append(jnp.swapaxes(x.reshape(t, n_heads, x.shape[1] // n_heads), 1, 2))
    o_ref[...] = jnp.concatenate(parts, axis=-1)


def tokens_to_steps(x, n_batch, seq, ctx_len, n_heads):
    m, d = x.shape
    n = d // n_heads
    t = RELAYOUT_ROWS
    n_ctx_blk, n_lat_blk = ctx_len // t, seq // t

    def spec(b):
        return pl.BlockSpec((t, d), lambda s: (_step_block_row(s, b, n_ctx_blk, n_lat_blk, n_batch), 0))

    return pl.pallas_call(
        functools.partial(_to_steps_body, n_heads=n_heads),
        grid=(n_ctx_blk + n_lat_blk,),
        in_specs=[spec(b) for b in range(n_batch)],
        out_specs=pl.BlockSpec((t, n, n_batch * n_heads), lambda s: (s, 0, 0)),
        out_shape=jax.ShapeDtypeStruct((ctx_len + seq, n, n_batch * n_heads), x.dtype),
        compiler_params=_cparams(("parallel",)),
        name="tokens_to_steps",
    )(*([x] * n_batch))


def _to_tokens_body(y_ref, o_ref, *, n_heads, n_batch):
    y = y_ref[...]
    t, n = y.shape[0], y.shape[1]
    for b in range(n_batch):
        @pl.when(pl.program_id(1) == b)
        def _():
            yb = y[:, :, b * n_heads:(b + 1) * n_heads]
            o_ref[...] = jnp.swapaxes(yb, 1, 2).reshape(t, n_heads * n)


def steps_to_tokens(y, n_batch, seq, ctx_len, n_heads):
    steps, n, _ = y.shape
    d = n_heads * n
    t = RELAYOUT_ROWS
    n_ctx_blk, n_lat_blk = ctx_len // t, seq // t
    return pl.pallas_call(
        functools.partial(_to_tokens_body, n_heads=n_heads, n_batch=n_batch),
        grid=(n_ctx_blk + n_lat_blk, n_batch),
        in_specs=[pl.BlockSpec((t, n, n_batch * n_heads), lambda s, b: (s, 0, 0))],
        out_specs=pl.BlockSpec((t, d), lambda s, b: (_step_block_row(s, b, n_ctx_blk, n_lat_blk, n_batch), 0)),
        out_shape=jax.ShapeDtypeStruct((n_batch * (seq + ctx_len), d), y.dtype),
        compiler_params=_cparams(("parallel", "arbitrary")),
        name="steps_to_tokens",
    )(y)


def _pad_cols(w, mult=128):
    n = w.shape[-1]
    p = -(-n // mult) * mult - n
    return jnp.pad(w, [(0, 0)] * (w.ndim - 1) + [(0, p)]) if p else w


def _pad_rows(w, mult=128):
    n = w.shape[-2]
    p = -(-n // mult) * mult - n
    return jnp.pad(w, [(0, 0)] * (w.ndim - 2) + [(0, p), (0, 0)]) if p else w


def rwkv_mixer_tokens(h, mu, w_rkv, w0, w1, w2, a0, a1, a2, k_k, k_a, r_k, ln_g, ln_b, g1, g2,
                      tm, n_batch, seq, ctx_len):
    m, d = h.shape
    n_heads, n = r_k.shape
    n_lat = n_batch * seq
    ts = _row_tile(seq, ctx_len, cap=256)
    x_r, x_k, x_v, x_w, x_a, x_g = token_shift_mix(h, mu, ts, seq, ctx_len, n_batch)
    mm = functools.partial(matmul, tm=tm)
    r = mm(x_r, w_rkv, w_index=0, tn=1024)
    k = mm(x_k, w_rkv, w_index=1, tn=1024)
    v = mm(x_v, w_rkv, w_index=2, tn=1024)
    gate = mm(mm(x_g, _pad_cols(g1), act="sigmoid", out_dtype=BF16), _pad_rows(g2))
    dec, aa = [], []
    for dn in range(2):
        lw = mm(x_w, _pad_cols(w1[dn]), act="tanh", out_dtype=BF16)
        dec.append(mm(lw, _pad_rows(w2[dn]), bias=w0[dn], act="decay"))
        la = mm(x_a, _pad_cols(a1[dn]), out_dtype=BF16)
        aa.append(mm(la, _pad_rows(a2[dn]), bias=a0[dn], act="sigmoid"))

    to_seq = functools.partial(tokens_to_steps, n_batch=n_batch, seq=seq, ctx_len=ctx_len, n_heads=n_heads)
    r_t, k_t, v_t, g_t = to_seq(r), to_seq(k), to_seq(v), to_seq(gate)
    d0_t, d1_t, a0_t, a1_t = to_seq(dec[0]), to_seq(dec[1]), to_seq(aa[0]), to_seq(aa[1])

    def table(vec):
        return jnp.tile(vec.astype(F32).reshape(n_heads, n).T[:, None, :], (1, n_batch, 1)).reshape(n, n_batch * n_heads)

    kk_tab, ka_tab = table(k_k), table(k_a)
    o_f, o_r = rwkv_scan(r_t, k_t, v_t, d0_t, d1_t, a0_t, a1_t, jnp.concatenate([kk_tab, kk_tab], axis=-1),
                         jnp.concatenate([ka_tab, ka_tab], axis=-1), ctx_len)
    y_t = rwkv_post(o_f, o_r, r_t, k_t, v_t, a0_t, a1_t, g_t, ka_tab, table(r_k.reshape(-1)),
                    table(ln_g), table(ln_b))
    return steps_to_tokens(y_t, n_batch, seq, ctx_len, n_heads)


def kernel(x, c, ctx, c_ctx, mod_w, mod_b, norm_g, final_g, na_w_qkv, na_w_o, na_rpb, s5_lam_re, s5_lam_im, s5_log_dt, s5_b_re, s5_b_im, s5_c_re, s5_c_im, s5_d, s5_w_glu, rw_mu, rw_w_rkv, rw_w0, rw_w1, rw_w2, rw_a0, rw_a1, rw_a2, rw_k_k, rw_k_a, rw_r_k, rw_ln_g, rw_ln_b, rw_g1, rw_g2, rw_w_o, moe_wg, moe_bg, moe_we, moe_be, moe_w_gu, moe_w_down):
    n_batch, seq, d = x.shape
    ctx_len = ctx.shape[1]
    depth = mod_w.shape[0]
    n_lat = n_batch * seq
    n_ctx = n_batch * ctx_len
    tm = _row_tile(seq, n_ctx)
    tpb = seq // tm
    xs = jnp.concatenate([x.reshape(n_lat, d), ctx.reshape(n_ctx, d)], axis=0).astype(F32)

    cvecs = jnp.concatenate([c.astype(F32), c_ctx.astype(F32)[None]], axis=0)
    rows_pad = -(-(n_batch + 1) // 8) * 8
    cvecs = jnp.pad(cvecs, ((0, rows_pad - n_batch - 1), (0, 0)))
    mods = modulation(cvecs, mod_w, mod_b)[:, :n_batch + 1].reshape(depth, n_batch + 1, 6, 1, d)
    mods = mods.transpose(0, 2, 1, 3, 4)

    for i in range(depth):
        last = i == depth - 1
        sh1, sc1, g1, sh2, sc2, g2 = (mods[i, q] for q in range(6))
        pn = functools.partial(prenorm, tm=tm, tiles_per_batch=tpb, n_batch=n_batch)
        mm = functools.partial(matmul, tm=tm, tiles_per_batch=tpb, n_batch=n_batch)
        mix, j = i % N_MIXERS, i // N_MIXERS
        if mix == 0:
            (hb,) = pn(xs, norm_g[i, 0], 1.0 + sc1, sh1, out_dtypes=(BF16,))
            n_heads = na_rpb.shape[1]
            qscale = jnp.concatenate([jnp.full((d,), (d // n_heads) ** -0.5, F32), jnp.ones((2 * d,), F32)])
            qkv = mm(hb, na_w_qkv, w_index=j, out_dtype=BF16, colscale=qscale)
            kr = (na_rpb.shape[2] + 1) // 2
            o = na_attention(qkv, _na_bias_table(na_rpb[j], GRID_W, seq // GRID_W), kr, n_batch, seq, ctx_len,
                             not last)
            if last:
                xs = xs[:n_lat]
            xs = mm(o, na_w_o, w_index=j, mode="resid", resid=xs, gate=g1)
        elif mix == 1:
            (hf,) = pn(xs, norm_g[i, 0], 1.0 + sc1, sh1, out_dtypes=(F32,))
            wts = _s5_weights(s5_lam_re[j], s5_lam_im[j], s5_log_dt[j], s5_b_re[j], s5_b_im[j],
                              s5_c_re[j], s5_c_im[j], S5_CHUNK)
            y = s5_core(hf, wts, s5_d[j], n_batch, seq, ctx_len)
            if last:
                xs, y = xs[:n_lat], y[:n_lat]
            xs = mm(y, s5_w_glu, w_index=j, mode="glu", resid=xs, gate=g1)
        else:
            (hf,) = pn(xs, norm_g[i, 0], 1.0 + sc1, sh1, out_dtypes=(F32,))
            y = rwkv_mixer_tokens(hf, rw_mu[j], rw_w_rkv[j], rw_w0[j], rw_w1[j], rw_w2[j], rw_a0[j], rw_a1[j],
                                  rw_a2[j], rw_k_k[j], rw_k_a[j], rw_r_k[j], rw_ln_g[j], rw_ln_b[j],
                                  rw_g1[j], rw_g2[j], tm, n_batch, seq, ctx_len)
            if last:
                xs, y = xs[:n_lat], y[:n_lat]
            xs = mm(y, rw_w_o, w_index=j, mode="resid", resid=xs, gate=g1)
        h2, h2_slabs = pn(xs, norm_g[i, 1], 1.0 + sc2, sh2, out_dtypes=(F32, F32), lane_split=(False, True))
        xs = hier_moe(xs, h2, h2_slabs, moe_wg[i], moe_bg[i], moe_we[i], moe_be[i], moe_w_gu, moe_w_down, i, g2,
                      tm, n_batch, seq)
    ones = jnp.ones((n_batch + 1, 1, d), F32)
    (out,) = prenorm(xs[:n_lat], final_g, ones, jnp.zeros_like(ones), tm, tpb, n_batch, out_dtypes=(x.dtype,))
    return out.reshape(n_batch, seq, d)
```

```python
import functools
import math

import jax
import jax.numpy as jnp
from jax import lax
from jax.experimental import pallas as pl
from jax.experimental.pallas import tpu as pltpu

F32 = jnp.float32
BF16 = jnp.bfloat16

GRID_W = 64
N_MIXERS = 3
NORM_EPS = 1e-6
RW_GN_EPS = 64e-5
MOE_TOP_K = 2
MOE_BLOCK = 128
MOE_COMBINE_ROWS = 128
FULL_ROW_TILE = 256
S5_CHUNK = 16
SCAN_KEY_BLOCK = 16
LANES = 128
RELAYOUT_ROWS = 128
NA_Q_ROWS = 4
NA_KEY_ROWS = 12
NEG_BIG = -1e30
VMEM_LIMIT_BYTES = 48 * 1024 * 1024
VMEM_LIMIT_BIG_BYTES = 56 * 1024 * 1024


def _cparams(sem, limit=VMEM_LIMIT_BYTES):
    return pltpu.CompilerParams(dimension_semantics=sem, vmem_limit_bytes=limit)


def _row_tile(n_lat_per_batch, n_ctx_total, cap=512):
    t = cap
    while t > 8 and (n_lat_per_batch % t or (n_ctx_total and n_ctx_total % t)):
        t //= 2
    return t


def _seg_fn(tiles_per_batch, n_batch):
    def seg(i):
        return jnp.minimum(i // tiles_per_batch, n_batch)
    return seg


def _bdot(a, b):
    return jnp.dot(a.astype(BF16), b.astype(BF16), preferred_element_type=F32)


def _split3(x):
    hi = x.astype(BF16)
    r1 = x - hi.astype(F32)
    mid = r1.astype(BF16)
    lo = (r1 - mid.astype(F32)).astype(BF16)
    return hi, mid, lo


def _prenorm_body(x_ref, g_ref, sc_ref, sh_ref, *o_refs):
    x = x_ref[...]
    ms = jnp.mean(x * x, axis=-1, keepdims=True)
    h = (x * lax.rsqrt(ms + NORM_EPS)) * g_ref[...]
    h = h * sc_ref[0] + sh_ref[0]
    for o_ref in o_refs:
        o_ref[...] = h.astype(o_ref.dtype).reshape(o_ref.shape)


def prenorm(x, g, scale1p, shift, tm, tiles_per_batch, n_batch, out_dtypes, lane_split=None):
    m, d = x.shape
    seg = _seg_fn(tiles_per_batch, n_batch)
    row = pl.BlockSpec((tm, d), lambda i: (i, 0))
    row3 = pl.BlockSpec((tm, d // LANES, LANES), lambda i: (i, 0, 0))
    tab = pl.BlockSpec((1, 1, d), lambda i: (seg(i), 0, 0))
    lane_split = lane_split or (False,) * len(out_dtypes)
    outs = pl.pallas_call(
        _prenorm_body,
        grid=(m // tm,),
        in_specs=[row, pl.BlockSpec((1, d), lambda i: (0, 0)), tab, tab],
        out_specs=[row3 if sp else row for sp in lane_split],
        out_shape=[jax.ShapeDtypeStruct((m, d // LANES, LANES) if sp else (m, d), dt)
                   for dt, sp in zip(out_dtypes, lane_split)],
        compiler_params=_cparams(("parallel",)),
    )(x, g.reshape(1, d), scale1p, shift)
    return outs


def _softplus(z):
    return jnp.maximum(z, 0.0) + jnp.log(1.0 + jnp.exp(-jnp.abs(z)))


def _apply_act(y, act):
    if act is None:
        return y
    if act == "tanh":
        return jnp.tanh(y)
    if act == "sigmoid":
        return jax.nn.sigmoid(y)
    if act == "decay":
        return jnp.exp(-jnp.exp(-_softplus(-y) - 0.5))
    raise ValueError(act)


def _mm_body(*refs, mode, act, has_bias, has_scale, has_mask):
    it = iter(refs)
    x_ref = next(it)
    w_ref = next(it)
    w2_ref = next(it) if mode == "glu" else None
    b_ref = next(it) if has_bias else None
    s_ref = next(it) if has_scale else None
    m_ref = next(it) if has_mask else None
    r_ref = next(it) if mode in ("resid", "glu") else None
    g_ref = next(it) if mode in ("resid", "glu") else None
    o_ref = next(it)
    wb_ref = next(it)
    wb2_ref = next(it) if mode == "glu" else None

    @pl.when(pl.program_id(1) == 0)
    def _():
        wb_ref[...] = w_ref[...].astype(BF16)
        if mode == "glu":
            wb2_ref[...] = w2_ref[...].astype(BF16)

    x = x_ref[...].astype(BF16)
    acc = jnp.dot(x, wb_ref[...], preferred_element_type=F32)
    if mode == "glu":
        acc2 = jnp.dot(x, wb2_ref[...], preferred_element_type=F32)
        acc = acc * jax.nn.sigmoid(acc2)
    if has_bias:
        acc = acc + b_ref[...]
    acc = _apply_act(acc, act)
    if has_scale:
        acc = acc * s_ref[...]
    if has_mask:
        acc = acc * m_ref[...]
    if mode in ("resid", "glu"):
        acc = r_ref[...] + g_ref[0] * acc
    o_ref[...] = acc.astype(o_ref.dtype)


def matmul(x, w, *, tm, tn=512, out_dtype=F32, mode="plain", act=None, bias=None, colscale=None,
           resid=None, gate=None, tiles_per_batch=None, n_batch=None, w_index=0, tilemask=None,
           full_rows=False):
    m, k = x.shape
    if w.ndim == 2:
        w, w_index = w[None], 0
    n = w.shape[2] // 2 if mode == "glu" else w.shape[2]
    if full_rows:
        if tiles_per_batch is not None:
            tiles_per_batch = tiles_per_batch * (tm // FULL_ROW_TILE)
        tm, tn = FULL_ROW_TILE, n
    tn = min(tn, n)
    assert m % tm == 0 and n % tn == 0, (m, tm, n, tn)
    nj = n // tn
    in_specs = [pl.BlockSpec((tm, k), lambda j, i: (i, 0)),
                pl.BlockSpec((None, k, tn), lambda j, i: (w_index, 0, j))]
    args = [x, w]
    if mode == "glu":
        in_specs.append(pl.BlockSpec((None, k, tn), lambda j, i: (w_index, 0, j + nj)))
        args.append(w)
    col = pl.BlockSpec((1, tn), lambda j, i: (0, j))
    if bias is not None:
        in_specs.append(col)
        args.append(bias.reshape(1, n).astype(F32))
    if colscale is not None:
        in_specs.append(col)
        args.append(colscale.reshape(1, n).astype(F32))
    if tilemask is not None:
        assert tilemask.shape == (tm, tn), (tilemask.shape, tm, tn)
        in_specs.append(pl.BlockSpec((tm, tn), lambda j, i: (0, 0)))
        args.append(tilemask.astype(F32))
    if mode in ("resid", "glu"):
        seg = _seg_fn(tiles_per_batch, n_batch)
        in_specs.append(pl.BlockSpec((tm, tn), lambda j, i: (i, j)))
        in_specs.append(pl.BlockSpec((1, 1, tn), lambda j, i: (seg(i), 0, j)))
        args += [resid, gate]
    return pl.pallas_call(
        functools.partial(_mm_body, mode=mode, act=act, has_bias=bias is not None,
                          has_scale=colscale is not None, has_mask=tilemask is not None),
        grid=(nj, m // tm),
        in_specs=in_specs,
        out_specs=pl.BlockSpec((tm, tn), lambda j, i: (i, j)),
        out_shape=jax.ShapeDtypeStruct((m, n), out_dtype),
        scratch_shapes=[pltpu.VMEM((k, tn), BF16)] * (2 if mode == "glu" else 1),
        compiler_params=_cparams(("parallel", "arbitrary"),
                                 VMEM_LIMIT_BIG_BYTES if full_rows else VMEM_LIMIT_BYTES),
    )(*args)


def _router_body(x_ref, w_ref, b_ref, o_ref, *, n_grp, epg):
    xh, xm, xl = _split3(x_ref[...])
    wh, wm, wl = _split3(w_ref[...])
    dot = functools.partial(jnp.dot, preferred_element_type=F32)
    acc = dot(xh, wh) + (dot(xh, wm) + dot(xm, wh)) + (dot(xh, wl) + dot(xm, wm) + dot(xl, wh))
    acc = acc + b_ref[...]
    lane = lax.broadcasted_iota(jnp.int32, acc.shape, 1)
    lane_f = lane.astype(F32)
    far = 1e9

    def first_max(vals):
        m = jnp.max(vals, axis=-1, keepdims=True)
        return m, jnp.min(jnp.where(vals == m, lane_f, far), axis=-1, keepdims=True)

    gl = jnp.where(lane < n_grp, acc, NEG_BIG)
    gmax, grp = first_max(gl)
    p_grp = 1.0 / jnp.sum(jnp.exp(gl - gmax), axis=-1, keepdims=True)
    lo = n_grp + grp * epg
    el = jnp.where((lane_f >= lo) & (lane_f < lo + epg), acc, NEG_BIG)
    m1, i1 = first_max(el)
    m2, i2 = first_max(jnp.where(lane_f == i1, NEG_BIG, el))
    e21 = jnp.exp(m2 - m1)
    g1 = p_grp / (1.0 + e21)
    g2 = p_grp * e21 / (1.0 + e21)
    out = jnp.where(lane == 0, i1 - n_grp, jnp.where(lane == 1, i2 - n_grp,
                                                    jnp.where(lane == 2, g1, jnp.where(lane == 3, g2, 0.0))))
    o_ref[...] = out


def router(x, w, bias, tm, n_grp, epg):
    m, k = x.shape
    n = w.shape[1]
    return pl.pallas_call(
        functools.partial(_router_body, n_grp=n_grp, epg=epg),
        grid=(m // tm,),
        in_specs=[pl.BlockSpec((tm, k), lambda i: (i, 0)), pl.BlockSpec((k, n), lambda i: (0, 0)),
                  pl.BlockSpec((1, n), lambda i: (0, 0))],
        out_specs=pl.BlockSpec((tm, n), lambda i: (i, 0)),
        out_shape=jax.ShapeDtypeStruct((m, n), F32),
        compiler_params=_cparams(("parallel",)),
        name="moe_router",
    )(x, w, bias.reshape(1, n))


def _mod_body(s_ref, w_ref, b_ref, o_ref):
    s = s_ref[...]
    s = s * jax.nn.sigmoid(s)
    o_ref[0] = _bdot(s, w_ref[0]) + b_ref[0]


def modulation(cvecs, mod_w, mod_b, tn=768):
    depth, d, n = mod_w.shape
    rows = cvecs.shape[0]
    while n % tn:
        tn //= 2
    return pl.pallas_call(
        _mod_body,
        grid=(depth, n // tn),
        in_specs=[pl.BlockSpec((rows, d), lambda l, j: (0, 0)),
                  pl.BlockSpec((1, d, tn), lambda l, j: (l, 0, j)),
                  pl.BlockSpec((1, 1, tn), lambda l, j: (l, 0, j))],
        out_specs=pl.BlockSpec((1, rows, tn), lambda l, j: (l, 0, j)),
        out_shape=jax.ShapeDtypeStruct((depth, rows, n), F32),
        compiler_params=_cparams(("parallel", "parallel")),
    )(cvecs, mod_w, mod_b.reshape(depth, 1, n))


def _na_key_start(i, rows, kr):
    return jnp.clip(i * NA_Q_ROWS - kr // 2, 0, rows - NA_KEY_ROWS)


def _na_bias_table(rpb, grid_w, rows):
    n_heads = rpb.shape[0]
    kr = (rpb.shape[1] + 1) // 2
    kcw = (rpb.shape[2] + 1) // 2
    n_blk = rows // NA_Q_ROWS
    q = jnp.arange(grid_w)[:, None]
    kc = jnp.arange(grid_w)[None, :]
    win = jnp.clip(q - kcw // 2, 0, grid_w - kcw)
    col_ok = (kc >= win) & (kc < win + kcw)
    dc = jnp.clip(kc - q, 1 - kcw, kcw - 1) + kcw - 1
    blk = jnp.array([0, min(1, n_blk - 1), n_blk - 1])[:, None, None]
    r = blk * NA_Q_ROWS + jnp.arange(NA_Q_ROWS)[None, :, None]
    krow = _na_key_start(blk, rows, kr) + jnp.arange(NA_KEY_ROWS)[None, None, :]
    rs = jnp.clip(r - kr // 2, 0, rows - kr)
    row_ok = (krow >= rs) & (krow < rs + kr)
    dr = jnp.clip(krow - r + kr - 1, 0, 2 * kr - 2)
    hi = lax.Precision.HIGHEST
    sel_r = jax.nn.one_hot(dr.reshape(-1), rpb.shape[1], dtype=F32)
    sel_c = jax.nn.one_hot(dc.reshape(-1), rpb.shape[2], dtype=F32)
    tab = jnp.einsum("ra,has->hrs", sel_r, jnp.einsum("hab,sb->has", rpb.astype(F32), sel_c, precision=hi),
                     precision=hi)
    tab = tab.reshape(n_heads, 3, NA_Q_ROWS, NA_KEY_ROWS, grid_w, grid_w)
    ok = row_ok[None, :, :, :, None, None] & col_ok[None, None, None, None]
    tab = jnp.where(ok, tab, NEG_BIG)
    return tab.transpose(0, 1, 2, 4, 3, 5).reshape(n_heads, 3, NA_Q_ROWS * grid_w, NA_KEY_ROWS * grid_w)


def _na_body(q_ref, k_ref, v_ref, kc_ref, vc_ref, bias_ref, o_ref, *, hpb, dh, grid_w, kr, rows):
    ks = _na_key_start(pl.program_id(2), rows, kr)
    start = pl.multiple_of(ks * grid_w, grid_w)
    nt = (((1,), (1,)), ((), ()))
    for h in range(hpb):
        sl = slice(h * dh, (h + 1) * dh)
        q = q_ref[:, sl]
        k = k_ref[pl.ds(start, NA_KEY_ROWS * grid_w), sl]
        v = v_ref[pl.ds(start, NA_KEY_ROWS * grid_w), sl]
        s = lax.dot_general(q, k, nt, preferred_element_type=F32) + bias_ref[h, 0]
        sc = lax.dot_general(q, kc_ref[:, sl], nt, preferred_element_type=F32)
        m = jnp.maximum(jnp.max(s, axis=-1, keepdims=True), jnp.max(sc, axis=-1, keepdims=True))
        p = jnp.exp(s - m)
        pc = jnp.exp(sc - m)
        den = jnp.sum(p, axis=-1, keepdims=True) + jnp.sum(pc, axis=-1, keepdims=True)
        o = (jnp.dot(p.astype(BF16), v, preferred_element_type=F32)
             + jnp.dot(pc.astype(BF16), vc_ref[:, sl], preferred_element_type=F32))
        o_ref[:, sl] = (o / den).astype(o_ref.dtype)


def _ctx_attn_body(q_ref, k_ref, v_ref, o_ref, *, hpb, dh):
    nt = (((1,), (1,)), ((), ()))
    for h in range(hpb):
        sl = slice(h * dh, (h + 1) * dh)
        s = lax.dot_general(q_ref[:, sl], k_ref[:, sl], nt, preferred_element_type=F32)
        m = jnp.max(s, axis=-1, keepdims=True)
        p = jnp.exp(s - m)
        den = jnp.sum(p, axis=-1, keepdims=True)
        o = jnp.dot(p.astype(BF16), v_ref[:, sl], preferred_element_type=F32)
        o_ref[:, sl] = (o / den).astype(o_ref.dtype)


def na_attention(qkv, bias_tab, kr, n_batch, seq, ctx_len, need_ctx):
    d = qkv.shape[1] // 3
    n_heads = bias_tab.shape[0]
    dh = d // n_heads
    hpb = max(1, min(n_heads, 256 // dh))
    bw = hpb * dh
    n_d = d // bw
    rows = seq // GRID_W
    n_blk = rows // NA_Q_ROWS
    qb = NA_Q_ROWS * GRID_W
    n_lat = n_batch * seq
    ctx_blk0 = n_lat // ctx_len

    def cls_of(i):
        return jnp.where(i == 0, 0, jnp.where(i == n_blk - 1, 2, 1))

    o_lat = pl.pallas_call(
        functools.partial(_na_body, hpb=hpb, dh=dh, grid_w=GRID_W, kr=kr, rows=rows),
        grid=(n_batch, n_d, n_blk),
        in_specs=[
            pl.BlockSpec((qb, bw), lambda b, g, i: (b * n_blk + i, g)),
            pl.BlockSpec((seq, bw), lambda b, g, i: (b, n_d + g)),
            pl.BlockSpec((seq, bw), lambda b, g, i: (b, 2 * n_d + g)),
            pl.BlockSpec((ctx_len, bw), lambda b, g, i: (ctx_blk0 + b, n_d + g)),
            pl.BlockSpec((ctx_len, bw), lambda b, g, i: (ctx_blk0 + b, 2 * n_d + g)),
            pl.BlockSpec((hpb, 1, qb, NA_KEY_ROWS * GRID_W), lambda b, g, i: (g, cls_of(i), 0, 0)),
        ],
        out_specs=pl.BlockSpec((qb, bw), lambda b, g, i: (b * n_blk + i, g)),
        out_shape=jax.ShapeDtypeStruct((n_lat, d), BF16),
        compiler_params=_cparams(("parallel", "parallel", "arbitrary")),
        name="na_attention",
    )(qkv, qkv, qkv, qkv, qkv, bias_tab)
    if not need_ctx:
        return o_lat
    o_ctx = pl.pallas_call(
        functools.partial(_ctx_attn_body, hpb=hpb, dh=dh),
        grid=(n_batch, n_d),
        in_specs=[
            pl.BlockSpec((ctx_len, bw), lambda b, g: (ctx_blk0 + b, g)),
            pl.BlockSpec((ctx_len, bw), lambda b, g: (ctx_blk0 + b, n_d + g)),
            pl.BlockSpec((ctx_len, bw), lambda b, g: (ctx_blk0 + b, 2 * n_d + g)),
        ],
        out_specs=pl.BlockSpec((ctx_len, bw), lambda b, g: (b, g)),
        out_shape=jax.ShapeDtypeStruct((n_batch * ctx_len, d), BF16),
        compiler_params=_cparams(("parallel", "parallel")),
    )(qkv, qkv, qkv)
    return jnp.concatenate([o_lat, o_ctx], axis=0)


def _expert_body(be_ref, idx_ref, idx_next_ref, x_hbm, wgu_ref, wd_ref, o_ref, xbuf, sem, wgu_s, wd_s, *,
                 hidden, n_blocks, n_slab):
    j = pl.program_id(0)
    cur = j % 2

    def row_copy(idx, buf, r):
        return pltpu.make_async_copy(x_hbm.at[pl.ds(idx[0, 0, r], 1)], xbuf.at[buf, pl.ds(r, 1)], sem.at[buf])

    @pl.when(j == 0)
    def _():
        for r in range(MOE_BLOCK):
            row_copy(idx_ref, 0, r).start(priority=r % 2)

    prev = be_ref[jnp.maximum(j - 1, 0)]
    changed = jnp.logical_or(j == 0, be_ref[j] != prev)

    @pl.when(changed)
    def _():
        wgu_s[...] = wgu_ref[...].astype(BF16)
        wd_s[...] = wd_ref[...].astype(BF16)

    for r in range(MOE_BLOCK):
        row_copy(idx_ref, cur, r).wait()

    for r in range(MOE_BLOCK):
        row_copy(idx_next_ref, 1 - cur, r).start(priority=r % 2)
    xb = xbuf.at[cur]
    gu = None
    for c in range(0, n_slab, 2):
        xa = jnp.concatenate([xb[:, c, :], xb[:, c + 1, :]], axis=-1).astype(BF16)
        part = jnp.dot(xa, wgu_s[c * LANES:(c + 2) * LANES, :], preferred_element_type=F32)
        gu = part if gu is None else gu + part
    g = gu[:, :hidden]
    u = gu[:, hidden:]
    a = (g * jax.nn.sigmoid(g) * u).astype(BF16)
    o_ref[...] = jnp.dot(a, wd_s[...], preferred_element_type=F32)

    @pl.when(j == n_blocks - 1)
    def _():
        for r in range(MOE_BLOCK):
            row_copy(idx_next_ref, 1 - cur, r).wait()


def expert_blocks(x, slot_tok, blk_e, w_gu, w_down, layer):
    n_slab = x.shape[1]
    d = n_slab * LANES
    hidden = w_down.shape[2]
    n_blocks = blk_e.shape[0]
    idx = slot_tok.reshape(n_blocks, 1, MOE_BLOCK)
    return pl.pallas_call(
        functools.partial(_expert_body, hidden=hidden, n_blocks=n_blocks, n_slab=n_slab),
        grid_spec=pltpu.PrefetchScalarGridSpec(
            num_scalar_prefetch=1,
            grid=(n_blocks,),
            in_specs=[pl.BlockSpec((1, 1, MOE_BLOCK), lambda j, be: (j, 0, 0), memory_space=pltpu.SMEM),
                      pl.BlockSpec((1, 1, MOE_BLOCK), lambda j, be: (jnp.minimum(j + 1, n_blocks - 1), 0, 0),
                                   memory_space=pltpu.SMEM),
                      pl.BlockSpec(memory_space=pl.ANY),
                      pl.BlockSpec((None, None, d, 2 * hidden), lambda j, be: (layer, be[j], 0, 0)),
                      pl.BlockSpec((None, None, hidden, d), lambda j, be: (layer, be[j], 0, 0))],
            out_specs=pl.BlockSpec((MOE_BLOCK, d), lambda j, be: (j, 0)),
            scratch_shapes=[pltpu.VMEM((2, MOE_BLOCK, n_slab, LANES), F32), pltpu.SemaphoreType.DMA((2,)),
                            pltpu.VMEM((d, 2 * hidden), BF16), pltpu.VMEM((hidden, d), BF16)]),
        out_shape=jax.ShapeDtypeStruct((n_blocks * MOE_BLOCK, d), F32),
        compiler_params=_cparams(("arbitrary",), VMEM_LIMIT_BIG_BYTES),
        name="moe_experts",
    )(blk_e, idx, idx, x, w_gu, w_down)


def _combine_body(idx_ref, idx_next_ref, route_ref, x_ref, g_ref, yb_hbm, o_ref, ybuf, sem, *, tmc, n_tiles):
    i = pl.program_id(0)
    cur = i % 2

    def row_copy(idx, buf, k, r):
        return pltpu.make_async_copy(yb_hbm.at[pl.ds(idx[0, 0, k * tmc + r], 1)],
                                     ybuf.at[buf, k, pl.ds(r, 1)], sem.at[buf])

    def start_all(idx, buf):
        for k in range(MOE_TOP_K):
            for r in range(tmc):
                row_copy(idx, buf, k, r).start(priority=r % 2)

    @pl.when(i == 0)
    def _():
        start_all(idx_ref, 0)

    @pl.when(i + 1 < n_tiles)
    def _():
        start_all(idx_next_ref, 1 - cur)

    for k in range(MOE_TOP_K):
        for r in range(tmc):
            row_copy(idx_ref, cur, k, r).wait()
    route = route_ref[...]
    f = ybuf[cur, 0] * route[:, MOE_TOP_K:MOE_TOP_K + 1]
    for k in range(1, MOE_TOP_K):
        f = f + ybuf[cur, k] * route[:, MOE_TOP_K + k:MOE_TOP_K + k + 1]
    o_ref[...] = x_ref[...] + g_ref[0] * f


def moe_combine(xs, yb, slot_of_asg, route, gate_tab, n_batch, seq):
    n_tok, d = xs.shape
    tmc = MOE_COMBINE_ROWS
    n_tiles = n_tok // tmc
    idx = slot_of_asg.reshape(n_tiles, tmc, MOE_TOP_K).transpose(0, 2, 1).reshape(n_tiles, 1, MOE_TOP_K * tmc)
    seg = _seg_fn(seq // tmc, n_batch)
    return pl.pallas_call(
        functools.partial(_combine_body, tmc=tmc, n_tiles=n_tiles),
        grid=(n_tiles,),
        in_specs=[pl.BlockSpec((1, 1, MOE_TOP_K * tmc), lambda i: (i, 0, 0), memory_space=pltpu.SMEM),
                  pl.BlockSpec((1, 1, MOE_TOP_K * tmc), lambda i: (jnp.minimum(i + 1, n_tiles - 1), 0, 0),
                               memory_space=pltpu.SMEM),
                  pl.BlockSpec((tmc, route.shape[1]), lambda i: (i, 0)),
                  pl.BlockSpec((tmc, d), lambda i: (i, 0)),
                  pl.BlockSpec((1, 1, d), lambda i: (seg(i), 0, 0)),
                  pl.BlockSpec(memory_space=pl.ANY)],
        out_specs=pl.BlockSpec((tmc, d), lambda i: (i, 0)),
        out_shape=jax.ShapeDtypeStruct((n_tok, d), F32),
        scratch_shapes=[pltpu.VMEM((2, MOE_TOP_K, tmc, d), F32), pltpu.SemaphoreType.DMA((2,))],
        compiler_params=_cparams(("arbitrary",)),
        name="moe_combine",
    )(idx, idx, route, xs, gate_tab, yb)


def hier_moe(xs, h, h_slabs, wg, bg, we, be, w_gu, w_down, layer, gate_tab, tm, n_batch, seq):
    n_tok, d = h.shape
    n_grp = wg.shape[1]
    n_exp = we.shape[1]
    epg = n_exp // n_grp
    n_logit = n_grp + n_exp
    n_pad = -(-n_logit // 128) * 128
    wcat = jnp.pad(jnp.concatenate([wg, we], axis=1), ((0, 0), (0, n_pad - n_logit)))
    bcat = jnp.pad(jnp.concatenate([bg, be], axis=0), (0, n_pad - n_logit))
    route = router(h, wcat, bcat, tm, n_grp, epg)

    flat_e = route[:, :MOE_TOP_K].astype(jnp.int32).reshape(-1)
    n_asg = flat_e.shape[0]
    onehot = (flat_e[:, None] == jnp.arange(n_exp, dtype=jnp.int32)[None, :]).astype(F32)
    ck = 128
    oh3 = onehot.reshape(n_asg // ck, ck, n_exp)
    tri = jnp.tril(jnp.ones((ck, ck), F32))
    within = jnp.einsum("ij,cjk->cik", tri, oh3, precision=lax.Precision.HIGHEST)
    tot = within[:, -1, :]
    before = jnp.cumsum(tot, axis=0) - tot
    rank = jnp.sum(oh3 * (within + before[:, None, :]), axis=-1).reshape(n_asg).astype(jnp.int32) - 1
    counts = jnp.sum(tot, axis=0).astype(jnp.int32)
    padded = (counts + MOE_BLOCK - 1) // MOE_BLOCK * MOE_BLOCK
    pad_end = jnp.cumsum(padded)
    pad_start = pad_end - padded
    slot = jnp.sum(onehot * pad_start.astype(F32)[None, :], axis=-1).astype(jnp.int32) + rank
    n_blocks = -(-n_asg // MOE_BLOCK) + n_exp
    slot_tok = jnp.zeros((n_blocks * MOE_BLOCK,), jnp.int32).at[slot].set(
        jnp.arange(n_asg, dtype=jnp.int32) // MOE_TOP_K)
    blk_start = jnp.arange(n_blocks, dtype=jnp.int32) * MOE_BLOCK
    blk_e = jnp.minimum(jnp.sum((pad_end[None, :] <= blk_start[:, None]).astype(jnp.int32), axis=1), n_exp - 1)
    yb = expert_blocks(h_slabs, slot_tok, blk_e, w_gu, w_down, layer)
    return moe_combine(xs, yb, slot.reshape(n_tok, MOE_TOP_K), route, gate_tab, n_batch, seq)


def _s5_weights(lam_re, lam_im, log_dt, b_re, b_im, c_re, c_im, t_chunk):
    n_grp, n_state = lam_re.shape[1], lam_re.shape[2]
    cg = b_re.shape[-1]
    lam = lax.complex(lam_re.astype(F32), lam_im.astype(F32))
    dt = jnp.exp(log_dt.astype(F32))[..., None]
    lam_bar = jnp.exp(lam * dt)
    b_bar = ((lam_bar - 1.0) / lam)[..., None] * lax.complex(b_re.astype(F32), b_im.astype(F32))
    c_mat = lax.complex(c_re.astype(F32), c_im.astype(F32))
    ks = jnp.arange(t_chunk + 1, dtype=F32)
    pw = jnp.exp((lam * dt)[..., None] * ks)
    hi = lax.Precision.HIGHEST
    kern = jnp.einsum("dgcp,dgpk,dgpi->dgkci", c_mat, pw[..., :t_chunk], b_bar, precision=hi).real
    t_idx = jnp.arange(t_chunk)
    lag = t_idx[None, :] - t_idx[:, None]
    k_f = kern[0][:, jnp.clip(lag, 0, t_chunk - 1)]
    k_r = kern[1][:, jnp.clip(-lag, 0, t_chunk - 1)]
    toep = (jnp.where((lag >= 0)[None, :, :, None, None], k_f, 0.0)
            + jnp.where((lag <= 0)[None, :, :, None, None], k_r, 0.0))
    toep = toep.transpose(0, 1, 4, 2, 3).reshape(n_grp, t_chunk * cg, t_chunk * cg)
    pf = pw[0][..., :t_chunk][..., ::-1]
    pr = pw[1][..., :t_chunk]
    bend_f = jnp.einsum("gps,gpc->gscp", pf, b_bar[0]).reshape(n_grp, t_chunk * cg, n_state)
    bend_r = jnp.einsum("gps,gpc->gscp", pr, b_bar[1]).reshape(n_grp, t_chunk * cg, n_state)
    bend = jnp.stack([bend_f.real, bend_f.imag, bend_r.real, bend_r.imag], axis=1)
    of = jnp.einsum("gcp,gpt->gptc", c_mat[0], pw[0][..., 1:]).reshape(n_grp, n_state, t_chunk * cg)
    orv = jnp.einsum("gcp,gpt->gptc", c_mat[1], pw[1][..., 1:][..., ::-1]).reshape(n_grp, n_state, t_chunk * cg)
    cout = jnp.stack([of.real, -of.imag, orv.real, -orv.imag], axis=1)

    gsz = LANES // cg
    nb = n_grp // gsz
    tile = 512

    def expand(n_outer, n_inner):
        dst = jnp.arange(n_outer * gsz * n_inner)
        src = (dst // (gsz * n_inner)) * n_inner + dst % n_inner
        return (jnp.arange(n_outer * n_inner)[:, None] == src[None, :]).astype(F32)

    def grp(period, width):
        return (jnp.arange(tile) % period) // width

    def own_group(row_grp, col_grp):
        return (row_grp[:, None] == col_grp[None, :]).astype(F32)

    e_tc, e_xp = expand(t_chunk, cg), expand(4, n_state)
    grp_ch, grp_st = grp(LANES, cg), grp(gsz * n_state, n_state)
    kin = t_chunk * cg
    xt = toep.reshape(nb, gsz, t_chunk, cg, kin).transpose(0, 2, 1, 3, 4).reshape(n_grp * kin, kin)
    wt = matmul(xt, e_tc, tm=tile, tn=tile, out_dtype=BF16, tilemask=own_group(grp_ch, grp_ch))
    wt = wt.reshape(nb, t_chunk, LANES, t_chunk * LANES)
    xb = bend.reshape(nb, gsz, 4, t_chunk, cg, n_state).transpose(0, 3, 1, 4, 2, 5).reshape(n_grp * kin, 4 * n_state)
    wb = matmul(xb, e_xp, tm=tile, tn=tile, out_dtype=BF16, tilemask=own_group(grp_ch, grp_st))
    wb = wb.reshape(nb, t_chunk, LANES, 4 * gsz * n_state)
    xc = cout.reshape(nb, gsz, 4, n_state, kin).transpose(0, 2, 1, 3, 4).reshape(4 * n_grp * n_state, kin)
    wc = matmul(xc, e_tc, tm=tile, tn=tile, out_dtype=BF16, tilemask=own_group(grp_st, grp_ch))
    wc = wc.reshape(nb, 4 * gsz * n_state, t_chunk * LANES)
    lam_t = pw[..., t_chunk]
    lam_t4 = jnp.stack([lam_t[0].real, lam_t[0].imag, lam_t[1].real, lam_t[1].imag]).reshape(4, n_grp * n_state)
    return wt, wb, wc, lam_t4


def _s5_a_body(h_ref, wb_ref, e_ref, *, t_chunk):
    acc = jnp.dot(h_ref[:, 0, :].astype(BF16), wb_ref[0], preferred_element_type=F32)
    for s in range(1, t_chunk):
        acc = acc + jnp.dot(h_ref[:, s, :].astype(BF16), wb_ref[s], preferred_element_type=F32)
    e_ref[...] = acc


def _s5_b_body(e_ref, lam_ref, x_ref, *, n_batch, n_lat_chunk, n_ctx_chunk, width):
    ctx0 = n_batch * n_lat_chunk
    n_chunk = n_lat_chunk + n_ctx_chunk

    def comp(x):
        return slice(x * width, (x + 1) * width)

    lfr, lfi, lrr, lri = (lam_ref[:, comp(x)] for x in range(4))

    for b in range(n_batch):
        def fwd(n, carry):
            sr, si = carry
            row = jnp.where(n < n_ctx_chunk, ctx0 + b * n_ctx_chunk + n, b * n_lat_chunk + n - n_ctx_chunk)
            x_ref[pl.ds(row, 1), comp(0)] = sr
            x_ref[pl.ds(row, 1), comp(1)] = si
            er = e_ref[pl.ds(row, 1), comp(0)]
            ei = e_ref[pl.ds(row, 1), comp(1)]
            return lfr * sr - lfi * si + er, lfr * si + lfi * sr + ei

        def rev(n, carry):
            sr, si = carry
            row = jnp.where(n < n_ctx_chunk, ctx0 + b * n_ctx_chunk + n_ctx_chunk - 1 - n,
                            b * n_lat_chunk + n_chunk - 1 - n)
            x_ref[pl.ds(row, 1), comp(2)] = sr
            x_ref[pl.ds(row, 1), comp(3)] = si
            er = e_ref[pl.ds(row, 1), comp(2)]
            ei = e_ref[pl.ds(row, 1), comp(3)]
            return lrr * sr - lri * si + er, lrr * si + lri * sr + ei

        z = jnp.zeros((1, width), F32)
        lax.fori_loop(0, n_chunk, fwd, (z, z))
        lax.fori_loop(0, n_chunk, rev, (z, z))


def _gelu(y):
    return 0.5 * y * (1.0 + jnp.tanh(0.7978845608028654 * (y + 0.044715 * (y * y * y))))


def _s5_c_body(h_ref, x_ref, wt_ref, wc_ref, dsk_ref, o_ref, *, t_chunk):
    acc = jnp.dot(x_ref[...].astype(BF16), wc_ref[...], preferred_element_type=F32)
    for s in range(t_chunk):
        acc = acc + jnp.dot(h_ref[:, s, :].astype(BF16), wt_ref[s], preferred_element_type=F32)
    dsk = dsk_ref[...]
    for t in range(t_chunk):
        o_ref[:, t, :] = _gelu(acc[:, t * LANES:(t + 1) * LANES] + h_ref[:, t, :] * dsk)


def s5_core(h_f32, weights, d_skip, n_batch, seq, ctx_len):
    wt, wb, wc, lam_t4 = weights
    t = S5_CHUNK
    m, d = h_f32.shape
    nb = wt.shape[0]
    sw = wb.shape[-1]
    width = sw // 4
    rows = m // t
    rbs = rows // 2 if rows % 16 == 0 else rows
    h3 = h_f32.reshape(rows, t, d)
    hblk = pl.BlockSpec((rbs, t, LANES), lambda g, i: (i, 0, g))
    sblk = pl.BlockSpec((rbs, sw), lambda g, i: (i, g))
    e = pl.pallas_call(
        functools.partial(_s5_a_body, t_chunk=t),
        grid=(nb, rows // rbs),
        in_specs=[hblk, pl.BlockSpec((None, t, LANES, sw), lambda g, i: (g, 0, 0, 0))],
        out_specs=sblk,
        out_shape=jax.ShapeDtypeStruct((rows, nb * sw), F32),
        compiler_params=_cparams(("parallel", "arbitrary")),
        name="s5_chunk_states",
    )(h3, wb)

    lam_blk = lam_t4.reshape(4, nb, width).transpose(1, 0, 2).reshape(nb, 1, sw)
    x_in = pl.pallas_call(
        functools.partial(_s5_b_body, n_batch=n_batch, n_lat_chunk=seq // t, n_ctx_chunk=ctx_len // t,
                          width=width),
        grid=(nb,),
        in_specs=[pl.BlockSpec((rows, sw), lambda g: (0, g)), pl.BlockSpec((None, 1, sw), lambda g: (g, 0, 0))],
        out_specs=pl.BlockSpec((rows, sw), lambda g: (0, g)),
        out_shape=jax.ShapeDtypeStruct((rows, nb * sw), F32),
        compiler_params=_cparams(("parallel",)),
        name="s5_chunk_recurrence",
    )(e, lam_blk)

    y3 = pl.pallas_call(
        functools.partial(_s5_c_body, t_chunk=t),
        grid=(nb, rows // rbs),
        in_specs=[hblk, sblk,
                  pl.BlockSpec((None, t, LANES, t * LANES), lambda g, i: (g, 0, 0, 0)),
                  pl.BlockSpec((None, sw, t * LANES), lambda g, i: (g, 0, 0)),
                  pl.BlockSpec((1, LANES), lambda g, i: (0, g))],
        out_specs=hblk,
        out_shape=jax.ShapeDtypeStruct((rows, t, d), F32),
        compiler_params=_cparams(("parallel", "arbitrary"), VMEM_LIMIT_BIG_BYTES),
        name="s5_outputs",
    )(h3, x_in, wt, wc, d_skip.astype(F32).reshape(1, d))
    return y3.reshape(m, d)


def _shift_body(h_ref, hp_ref, hn_ref, mu_ref, *o_refs, tm, tiles_lat, tiles_ctx, n_lat_tiles):
    i = pl.program_id(0)
    h = h_ref[...]
    first = jnp.where(i < n_lat_tiles, i % tiles_lat == 0, (i - n_lat_tiles) % tiles_ctx == 0)
    last = jnp.where(i < n_lat_tiles, i % tiles_lat == tiles_lat - 1,
                     (i - n_lat_tiles) % tiles_ctx == tiles_ctx - 1)
    rid = lax.broadcasted_iota(jnp.int32, h.shape, 0)
    prev_edge = jnp.where(first, 0.0, hp_ref[7:8, :])
    next_edge = jnp.where(last, 0.0, hn_ref[0:1, :])
    prev = jnp.where(rid == 0, prev_edge, pltpu.roll(h, 1, 0))
    nxt = jnp.where(rid == tm - 1, next_edge, pltpu.roll(h, tm - 1, 0))
    xx = 0.5 * (prev + nxt) - h
    for m, o_ref in enumerate(o_refs):
        o_ref[...] = (h + xx * mu_ref[m:m + 1, :]).astype(o_ref.dtype)


def token_shift_mix(h, mu, tm, seq, ctx_len, n_batch):
    m, d = h.shape
    n_mix = mu.shape[0]
    n_lat_tiles = n_batch * seq // tm
    nb8 = m // 8
    per = tm // 8
    row = pl.BlockSpec((tm, d), lambda i: (i, 0))
    return pl.pallas_call(
        functools.partial(_shift_body, tm=tm, tiles_lat=seq // tm, tiles_ctx=max(ctx_len // tm, 1),
                          n_lat_tiles=n_lat_tiles),
        grid=(m // tm,),
        in_specs=[row,
                  pl.BlockSpec((8, d), lambda i: (jnp.maximum(i * per - 1, 0), 0)),
                  pl.BlockSpec((8, d), lambda i: (jnp.minimum((i + 1) * per, nb8 - 1), 0)),
                  pl.BlockSpec((n_mix, d), lambda i: (0, 0))],
        out_specs=[row] * n_mix,
        out_shape=[jax.ShapeDtypeStruct((m, d), BF16)] * n_mix,
        compiler_params=_cparams(("parallel",)),
    )(h, h, h, mu.astype(F32))


def _rwkv_scan_body(rf_ref, rr_ref, kf_ref, kr_ref, vf_ref, vr_ref, wf_ref, wr_ref, af_ref, ar_ref,
                    kk_ref, ka_ref, of_ref, or_ref, s_ref, op_ref, *, tb, n, bh):
    @pl.when(pl.program_id(0) == 0)
    def _():
        s_ref[...] = jnp.zeros_like(s_ref)

    k_k = kk_ref[...]
    k_a = ka_ref[...]

    def step(t, carry):
        tr = tb - 1 - t
        r = jnp.concatenate([rf_ref[t], rr_ref[tr]], axis=-1)
        k = jnp.concatenate([kf_ref[t], kr_ref[tr]], axis=-1)
        v = jnp.concatenate([vf_ref[t], vr_ref[tr]], axis=-1)
        w = jnp.concatenate([wf_ref[t], wr_ref[tr]], axis=-1)
        a = jnp.concatenate([af_ref[t], ar_ref[tr]], axis=-1)
        kk = k * k_k
        nrm = jnp.sqrt(jnp.sum(kk * kk, axis=0, keepdims=True))
        kk = kk / jnp.maximum(nrm, 1e-12)
        nkk = -kk
        akk = kk * a
        kd = k * (1.0 + (a - 1.0) * k_a)
        op_ref[0] = nkk
        op_ref[1] = w * r
        op_ref[2] = w
        op_ref[3] = akk
        op_ref[4] = kd
        c_a = jnp.sum(akk * r, axis=0, keepdims=True)
        c_k = jnp.sum(kd * r, axis=0, keepdims=True)
        def reduce_keys(jb, acc):
            sa, so = acc
            for jj in range(SCAN_KEY_BLOCK):
                j = jb * SCAN_KEY_BLOCK + jj
                s_j = s_ref[j]
                sa = sa + s_j * op_ref[0, pl.ds(j, 1), :]
                so = so + s_j * op_ref[1, pl.ds(j, 1), :]
            return sa, so

        zero = jnp.zeros_like(v)
        sa, so = lax.fori_loop(0, n // SCAN_KEY_BLOCK, reduce_keys, (zero, zero))
        o = so + sa * c_a + v * c_k
        of_ref[t] = o[:, :bh]
        or_ref[tr] = o[:, bh:]

        def update_keys(jb, c):
            for jj in range(SCAN_KEY_BLOCK):
                j = jb * SCAN_KEY_BLOCK + jj
                s_ref[j] = (s_ref[j] * op_ref[2, pl.ds(j, 1), :] + sa * op_ref[3, pl.ds(j, 1), :]
                            + v * op_ref[4, pl.ds(j, 1), :])
            return c

        lax.fori_loop(0, n // SCAN_KEY_BLOCK, update_keys, 0)
        return carry

    lax.fori_loop(0, tb, step, 0)


def rwkv_scan(r, k, v, w0, w1, a0, a1, k_k, k_a, ctx_len, tb=32):
    steps, n, bh = r.shape
    while steps % tb or ctx_len % tb:
        tb //= 2
    nblk, nblk_ctx = steps // tb, ctx_len // tb

    def rev_blk(i):
        return jnp.where(i < nblk_ctx, nblk_ctx - 1 - i, nblk + nblk_ctx - 1 - i)

    fwd = pl.BlockSpec((tb, n, bh), lambda i: (i, 0, 0))
    rev = pl.BlockSpec((tb, n, bh), lambda i: (rev_blk(i), 0, 0))
    tab = pl.BlockSpec((n, 2 * bh), lambda i: (0, 0))
    out = jax.ShapeDtypeStruct((steps, n, bh), F32)
    return pl.pallas_call(
        functools.partial(_rwkv_scan_body, tb=tb, n=n, bh=bh),
        grid=(nblk,),
        in_specs=[fwd, rev, fwd, rev, fwd, rev, fwd, rev, fwd, rev, tab, tab],
        out_specs=[fwd, rev],
        out_shape=[out, out],
        scratch_shapes=[pltpu.VMEM((n, n, 2 * bh), F32), pltpu.VMEM((5, n, 2 * bh), F32)],
        compiler_params=_cparams(("arbitrary",)),
        name="rwkv_scan",
    )(r, r, k, k, v, v, w0, w1, a0, a1, k_k, k_a)


def _rwkv_post_body(of_ref, or_ref, r_ref, k_ref, v_ref, a0_ref, a1_ref, g_ref, ka_ref, rk_ref, lg_ref, lb_ref,
                    y_ref):
    o = of_ref[...] + or_ref[...]
    mean = jnp.mean(o, axis=1, keepdims=True)
    var = jnp.mean(jnp.square(o - mean), axis=1, keepdims=True)
    on = (o - mean) * lax.rsqrt(var + RW_GN_EPS) * lg_ref[...] + lb_ref[...]
    r = r_ref[...]
    k = k_ref[...]
    k_a = ka_ref[...]
    rk = rk_ref[...]
    kd0 = k * (1.0 + (a0_ref[...] - 1.0) * k_a)
    kd1 = k * (1.0 + (a1_ref[...] - 1.0) * k_a)
    bonus = (jnp.sum(r * kd0 * rk, axis=1, keepdims=True)
             + jnp.sum(r * kd1 * rk, axis=1, keepdims=True)) * v_ref[...]
    y = (on + bonus)
    y_ref[...] = (y * g_ref[...]).astype(y_ref.dtype)


def rwkv_post(o_f, o_r, r, k, v, a0, a1, gate, k_a, r_k, ln_g, ln_b, tb=32):
    steps, n, ch = o_f.shape
    while steps % tb:
        tb //= 2
    blk = pl.BlockSpec((tb, n, ch), lambda i: (i, 0, 0))
    tab = pl.BlockSpec((1, n, ch), lambda i: (0, 0, 0))
    return pl.pallas_call(
        _rwkv_post_body,
        grid=(steps // tb,),
        in_specs=[blk] * 8 + [tab] * 4,
        out_specs=blk,
        out_shape=jax.ShapeDtypeStruct((steps, n, ch), F32),
        compiler_params=_cparams(("parallel",)),
        name="rwkv_post",
    )(o_f, o_r, r, k, v, a0, a1, gate, k_a[None], r_k[None], ln_g[None], ln_b[None])


def _step_block_row(s, b, n_ctx_blk, n_lat_blk, n_batch):
    return jnp.where(s < n_ctx_blk, n_batch * n_lat_blk + b * n_ctx_blk + s, b * n_lat_blk + s - n_ctx_blk)


def _to_steps_body(*refs, n_heads):
    x_refs, o_ref = refs[:-1], refs[-1]
    parts = []
    for x_ref in x_refs:
        x = x_ref[...]
        t = x.shape[0]
        parts.append(jnp.swapaxes(x.reshape(t, n_heads, x.shape[1] // n_heads), 1, 2))
    o_ref[...] = jnp.concatenate(parts, axis=-1)


def tokens_to_steps(x, n_batch, seq, ctx_len, n_heads):
    m, d = x.shape
    n = d // n_heads
    t = RELAYOUT_ROWS
    n_ctx_blk, n_lat_blk = ctx_len // t, seq // t

    def spec(b):
        return pl.BlockSpec((t, d), lambda s: (_step_block_row(s, b, n_ctx_blk, n_lat_blk, n_batch), 0))

    return pl.pallas_call(
        functools.partial(_to_steps_body, n_heads=n_heads),
        grid=(n_ctx_blk + n_lat_blk,),
        in_specs=[spec(b) for b in range(n_batch)],
        out_specs=pl.BlockSpec((t, n, n_batch * n_heads), lambda s: (s, 0, 0)),
        out_shape=jax.ShapeDtypeStruct((ctx_len + seq, n, n_batch * n_heads), x.dtype),
        compiler_params=_cparams(("parallel",)),
        name="tokens_to_steps",
    )(*([x] * n_batch))


def _to_tokens_body(y_ref, o_ref, *, n_heads, n_batch):
    y = y_ref[...]
    t, n = y.shape[0], y.shape[1]
    for b in range(n_batch):
        @pl.when(pl.program_id(1) == b)
        def _():
            yb = y[:, :, b * n_heads:(b + 1) * n_heads]
            o_ref[...] = jnp.swapaxes(yb, 1, 2).reshape(t, n_heads * n)


def steps_to_tokens(y, n_batch, seq, ctx_len, n_heads):
    steps, n, _ = y.shape
    d = n_heads * n
    t = RELAYOUT_ROWS
    n_ctx_blk, n_lat_blk = ctx_len // t, seq // t
    return pl.pallas_call(
        functools.partial(_to_tokens_body, n_heads=n_heads, n_batch=n_batch),
        grid=(n_ctx_blk + n_lat_blk, n_batch),
        in_specs=[pl.BlockSpec((t, n, n_batch * n_heads), lambda s, b: (s, 0, 0))],
        out_specs=pl.BlockSpec((t, d), lambda s, b: (_step_block_row(s, b, n_ctx_blk, n_lat_blk, n_batch), 0)),
        out_shape=jax.ShapeDtypeStruct((n_batch * (seq + ctx_len), d), y.dtype),
        compiler_params=_cparams(("parallel", "arbitrary")),
        name="steps_to_tokens",
    )(y)


def _pad_cols(w, mult=128):
    n = w.shape[-1]
    p = -(-n // mult) * mult - n
    return jnp.pad(w, [(0, 0)] * (w.ndim - 1) + [(0, p)]) if p else w


def _pad_rows(w, mult=128):
    n = w.shape[-2]
    p = -(-n // mult) * mult - n
    return jnp.pad(w, [(0, 0)] * (w.ndim - 2) + [(0, p), (0, 0)]) if p else w


def rwkv_mixer_tokens(h, mu, w_rkv, w0, w1, w2, a0, a1, a2, k_k, k_a, r_k, ln_g, ln_b, g1, g2,
                      tm, n_batch, seq, ctx_len):
    m, d = h.shape
    n_heads, n = r_k.shape
    n_lat = n_batch * seq
    ts = _row_tile(seq, ctx_len, cap=256)
    x_r, x_k, x_v, x_w, x_a, x_g = token_shift_mix(h, mu, ts, seq, ctx_len, n_batch)
    mm = functools.partial(matmul, tm=tm)
    r = mm(x_r, w_rkv, w_index=0, full_rows=True)
    k = mm(x_k, w_rkv, w_index=1, full_rows=True)
    v = mm(x_v, w_rkv, w_index=2, full_rows=True)
    gate = mm(mm(x_g, _pad_cols(g1), act="sigmoid", out_dtype=BF16), _pad_rows(g2))
    dec, aa = [], []
    for dn in range(2):
        lw = mm(x_w, _pad_cols(w1[dn]), act="tanh", out_dtype=BF16)
        dec.append(mm(lw, _pad_rows(w2[dn]), bias=w0[dn], act="decay"))
        la = mm(x_a, _pad_cols(a1[dn]), out_dtype=BF16)
        aa.append(mm(la, _pad_rows(a2[dn]), bias=a0[dn], act="sigmoid"))

    to_seq = functools.partial(tokens_to_steps, n_batch=n_batch, seq=seq, ctx_len=ctx_len, n_heads=n_heads)
    r_t, k_t, v_t, g_t = to_seq(r), to_seq(k), to_seq(v), to_seq(gate)
    d0_t, d1_t, a0_t, a1_t = to_seq(dec[0]), to_seq(dec[1]), to_seq(aa[0]), to_seq(aa[1])

    def table(vec):
        return jnp.tile(vec.astype(F32).reshape(n_heads, n).T[:, None, :], (1, n_batch, 1)).reshape(n, n_batch * n_heads)

    kk_tab, ka_tab = table(k_k), table(k_a)
    o_f, o_r = rwkv_scan(r_t, k_t, v_t, d0_t, d1_t, a0_t, a1_t, jnp.concatenate([kk_tab, kk_tab], axis=-1),
                         jnp.concatenate([ka_tab, ka_tab], axis=-1), ctx_len)
    y_t = rwkv_post(o_f, o_r, r_t, k_t, v_t, a0_t, a1_t, g_t, ka_tab, table(r_k.reshape(-1)),
                    table(ln_g), table(ln_b))
    return steps_to_tokens(y_t, n_batch, seq, ctx_len, n_heads)


def kernel(x, c, ctx, c_ctx, mod_w, mod_b, norm_g, final_g, na_w_qkv, na_w_o, na_rpb, s5_lam_re, s5_lam_im, s5_log_dt, s5_b_re, s5_b_im, s5_c_re, s5_c_im, s5_d, s5_w_glu, rw_mu, rw_w_rkv, rw_w0, rw_w1, rw_w2, rw_a0, rw_a1, rw_a2, rw_k_k, rw_k_a, rw_r_k, rw_ln_g, rw_ln_b, rw_g1, rw_g2, rw_w_o, moe_wg, moe_bg, moe_we, moe_be, moe_w_gu, moe_w_down):
    n_batch, seq, d = x.shape
    ctx_len = ctx.shape[1]
    depth = mod_w.shape[0]
    n_lat = n_batch * seq
    n_ctx = n_batch * ctx_len
    tm = _row_tile(seq, n_ctx)
    tpb = seq // tm
    xs = jnp.concatenate([x.reshape(n_lat, d), ctx.reshape(n_ctx, d)], axis=0).astype(F32)

    cvecs = jnp.concatenate([c.astype(F32), c_ctx.astype(F32)[None]], axis=0)
    rows_pad = -(-(n_batch + 1) // 8) * 8
    cvecs = jnp.pad(cvecs, ((0, rows_pad - n_batch - 1), (0, 0)))
    mods = modulation(cvecs, mod_w, mod_b)[:, :n_batch + 1].reshape(depth, n_batch + 1, 6, 1, d)
    mods = mods.transpose(0, 2, 1, 3, 4)

    for i in range(depth):
        last = i == depth - 1
        sh1, sc1, g1, sh2, sc2, g2 = (mods[i, q] for q in range(6))
        pn = functools.partial(prenorm, tm=tm, tiles_per_batch=tpb, n_batch=n_batch)
        mm = functools.partial(matmul, tm=tm, tiles_per_batch=tpb, n_batch=n_batch)
        mix, j = i % N_MIXERS, i // N_MIXERS
        if mix == 0:
            (hb,) = pn(xs, norm_g[i, 0], 1.0 + sc1, sh1, out_dtypes=(BF16,))
            n_heads = na_rpb.shape[1]
            qscale = jnp.concatenate([jnp.full((d,), (d // n_heads) ** -0.5, F32), jnp.ones((2 * d,), F32)])
            qkv = mm(hb, na_w_qkv, w_index=j, out_dtype=BF16, colscale=qscale)
            kr = (na_rpb.shape[2] + 1) // 2
            o = na_attention(qkv, _na_bias_table(na_rpb[j], GRID_W, seq // GRID_W), kr, n_batch, seq, ctx_len,
                             not last)
            if last:
                xs = xs[:n_lat]
            xs = mm(o, na_w_o, w_index=j, mode="resid", resid=xs, gate=g1, full_rows=True)
        elif mix == 1:
            (hf,) = pn(xs, norm_g[i, 0], 1.0 + sc1, sh1, out_dtypes=(F32,))
            wts = _s5_weights(s5_lam_re[j], s5_lam_im[j], s5_log_dt[j], s5_b_re[j], s5_b_im[j],
                              s5_c_re[j], s5_c_im[j], S5_CHUNK)
            y = s5_core(hf, wts, s5_d[j], n_batch, seq, ctx_len)
            if last:
                xs, y = xs[:n_lat], y[:n_lat]
            xs = mm(y, s5_w_glu, w_index=j, mode="glu", resid=xs, gate=g1)
        else:
            (hf,) = pn(xs, norm_g[i, 0], 1.0 + sc1, sh1, out_dtypes=(F32,))
            y = rwkv_mixer_tokens(hf, rw_mu[j], rw_w_rkv[j], rw_w0[j], rw_w1[j], rw_w2[j], rw_a0[j], rw_a1[j],
                                  rw_a2[j], rw_k_k[j], rw_k_a[j], rw_r_k[j], rw_ln_g[j], rw_ln_b[j],
                                  rw_g1[j], rw_g2[j], tm, n_batch, seq, ctx_len)
            if last:
                xs, y = xs[:n_lat], y[:n_lat]
            xs = mm(y, rw_w_o, w_index=j, mode="resid", resid=xs, gate=g1, full_rows=True)
        h2, h2_slabs = pn(xs, norm_g[i, 1], 1.0 + sc2, sh2, out_dtypes=(F32, F32), lane_split=(False, True))
        xs = hier_moe(xs, h2, h2_slabs, moe_wg[i], moe_bg[i], moe_we[i], moe_be[i], moe_w_gu, moe_w_down, i, g2,
                      tm, n_batch, seq)
    ones = jnp.ones((n_batch + 1, 1, d), F32)
    (out,) = prenorm(xs[:n_lat], final_g, ones, jnp.zeros_like(ones), tm, tpb, n_batch, out_dtypes=(x.dtype,))
    return out.reshape(n_batch, seq, d)
```

```python
import functools
import math

import jax
import jax.numpy as jnp
from jax import lax
from jax.experimental import pallas as pl
from jax.experimental.pallas import tpu as pltpu

F32 = jnp.float32
BF16 = jnp.bfloat16

GRID_W = 64
N_MIXERS = 3
NORM_EPS = 1e-6
RW_GN_EPS = 64e-5
MOE_TOP_K = 2
MOE_BLOCK = 128
MOE_COMBINE_ROWS = 128
FULL_ROW_TILE = 256
S5_CHUNK = 16
SCAN_KEY_BLOCK = 16
LANES = 128
RELAYOUT_ROWS = 128
NA_Q_ROWS = 4
NA_KEY_ROWS = 12
NEG_BIG = -1e30
VMEM_LIMIT_BYTES = 48 * 1024 * 1024
VMEM_LIMIT_BIG_BYTES = 56 * 1024 * 1024


def _cparams(sem, limit=VMEM_LIMIT_BYTES):
    return pltpu.CompilerParams(dimension_semantics=sem, vmem_limit_bytes=limit)


def _row_tile(n_lat_per_batch, n_ctx_total, cap=512):
    t = cap
    while t > 8 and (n_lat_per_batch % t or (n_ctx_total and n_ctx_total % t)):
        t //= 2
    return t


def _seg_fn(tiles_per_batch, n_batch):
    def seg(i):
        return jnp.minimum(i // tiles_per_batch, n_batch)
    return seg


def _bdot(a, b):
    return jnp.dot(a.astype(BF16), b.astype(BF16), preferred_element_type=F32)


def _split3(x):
    hi = x.astype(BF16)
    r1 = x - hi.astype(F32)
    mid = r1.astype(BF16)
    lo = (r1 - mid.astype(F32)).astype(BF16)
    return hi, mid, lo


def _prenorm_body(x_ref, g_ref, sc_ref, sh_ref, *o_refs):
    x = x_ref[...]
    ms = jnp.mean(x * x, axis=-1, keepdims=True)
    h = (x * lax.rsqrt(ms + NORM_EPS)) * g_ref[...]
    h = h * sc_ref[0] + sh_ref[0]
    for o_ref in o_refs:
        o_ref[...] = h.astype(o_ref.dtype).reshape(o_ref.shape)


def prenorm(x, g, scale1p, shift, tm, tiles_per_batch, n_batch, out_dtypes, lane_split=None):
    m, d = x.shape
    seg = _seg_fn(tiles_per_batch, n_batch)
    row = pl.BlockSpec((tm, d), lambda i: (i, 0))
    row3 = pl.BlockSpec((tm, d // LANES, LANES), lambda i: (i, 0, 0))
    tab = pl.BlockSpec((1, 1, d), lambda i: (seg(i), 0, 0))
    lane_split = lane_split or (False,) * len(out_dtypes)
    outs = pl.pallas_call(
        _prenorm_body,
        grid=(m // tm,),
        in_specs=[row, pl.BlockSpec((1, d), lambda i: (0, 0)), tab, tab],
        out_specs=[row3 if sp else row for sp in lane_split],
        out_shape=[jax.ShapeDtypeStruct((m, d // LANES, LANES) if sp else (m, d), dt)
                   for dt, sp in zip(out_dtypes, lane_split)],
        compiler_params=_cparams(("parallel",)),
    )(x, g.reshape(1, d), scale1p, shift)
    return outs


def _softplus(z):
    return jnp.maximum(z, 0.0) + jnp.log(1.0 + jnp.exp(-jnp.abs(z)))


def _apply_act(y, act):
    if act is None:
        return y
    if act == "tanh":
        return jnp.tanh(y)
    if act == "sigmoid":
        return jax.nn.sigmoid(y)
    if act == "decay":
        return jnp.exp(-jnp.exp(-_softplus(-y) - 0.5))
    raise ValueError(act)


def _mm_body(*refs, mode, act, has_bias, has_scale, has_mask):
    it = iter(refs)
    x_ref = next(it)
    w_ref = next(it)
    w2_ref = next(it) if mode == "glu" else None
    b_ref = next(it) if has_bias else None
    s_ref = next(it) if has_scale else None
    m_ref = next(it) if has_mask else None
    r_ref = next(it) if mode in ("resid", "glu") else None
    g_ref = next(it) if mode in ("resid", "glu") else None
    o_ref = next(it)
    wb_ref = next(it)
    wb2_ref = next(it) if mode == "glu" else None

    @pl.when(pl.program_id(1) == 0)
    def _():
        wb_ref[...] = w_ref[...].astype(BF16)
        if mode == "glu":
            wb2_ref[...] = w2_ref[...].astype(BF16)

    x = x_ref[...].astype(BF16)
    acc = jnp.dot(x, wb_ref[...], preferred_element_type=F32)
    if mode == "glu":
        acc2 = jnp.dot(x, wb2_ref[...], preferred_element_type=F32)
        acc = acc * jax.nn.sigmoid(acc2)
    if has_bias:
        acc = acc + b_ref[...]
    acc = _apply_act(acc, act)
    if has_scale:
        acc = acc * s_ref[...]
    if has_mask:
        acc = acc * m_ref[...]
    if mode in ("resid", "glu"):
        acc = r_ref[...] + g_ref[0] * acc
    o_ref[...] = acc.astype(o_ref.dtype)


def matmul(x, w, *, tm, tn=512, out_dtype=F32, mode="plain", act=None, bias=None, colscale=None,
           resid=None, gate=None, tiles_per_batch=None, n_batch=None, w_index=0, tilemask=None,
           full_rows=False):
    m, k = x.shape
    if w.ndim == 2:
        w, w_index = w[None], 0
    n = w.shape[2] // 2 if mode == "glu" else w.shape[2]
    if full_rows:
        row_tile = FULL_ROW_TILE if full_rows is True else full_rows
        if tiles_per_batch is not None:
            tiles_per_batch = tiles_per_batch * (tm // row_tile)
        tm, tn = row_tile, n
    tn = min(tn, n)
    assert m % tm == 0 and n % tn == 0, (m, tm, n, tn)
    nj = n // tn
    in_specs = [pl.BlockSpec((tm, k), lambda j, i: (i, 0)),
                pl.BlockSpec((None, k, tn), lambda j, i: (w_index, 0, j))]
    args = [x, w]
    if mode == "glu":
        in_specs.append(pl.BlockSpec((None, k, tn), lambda j, i: (w_index, 0, j + nj)))
        args.append(w)
    col = pl.BlockSpec((1, tn), lambda j, i: (0, j))
    if bias is not None:
        in_specs.append(col)
        args.append(bias.reshape(1, n).astype(F32))
    if colscale is not None:
        in_specs.append(col)
        args.append(colscale.reshape(1, n).astype(F32))
    if tilemask is not None:
        assert tilemask.shape == (tm, tn), (tilemask.shape, tm, tn)
        in_specs.append(pl.BlockSpec((tm, tn), lambda j, i: (0, 0)))
        args.append(tilemask.astype(F32))
    if mode in ("resid", "glu"):
        seg = _seg_fn(tiles_per_batch, n_batch)
        in_specs.append(pl.BlockSpec((tm, tn), lambda j, i: (i, j)))
        in_specs.append(pl.BlockSpec((1, 1, tn), lambda j, i: (seg(i), 0, j)))
        args += [resid, gate]
    return pl.pallas_call(
        functools.partial(_mm_body, mode=mode, act=act, has_bias=bias is not None,
                          has_scale=colscale is not None, has_mask=tilemask is not None),
        grid=(nj, m // tm),
        in_specs=in_specs,
        out_specs=pl.BlockSpec((tm, tn), lambda j, i: (i, j)),
        out_shape=jax.ShapeDtypeStruct((m, n), out_dtype),
        scratch_shapes=[pltpu.VMEM((k, tn), BF16)] * (2 if mode == "glu" else 1),
        compiler_params=_cparams(("parallel", "arbitrary"),
                                 VMEM_LIMIT_BIG_BYTES if full_rows else VMEM_LIMIT_BYTES),
    )(*args)


def _router_body(x_ref, w_ref, b_ref, o_ref, *, n_grp, epg):
    xh, xm, xl = _split3(x_ref[...])
    wh, wm, wl = _split3(w_ref[...])
    dot = functools.partial(jnp.dot, preferred_element_type=F32)
    acc = dot(xh, wh) + (dot(xh, wm) + dot(xm, wh)) + (dot(xh, wl) + dot(xm, wm) + dot(xl, wh))
    acc = acc + b_ref[...]
    lane = lax.broadcasted_iota(jnp.int32, acc.shape, 1)
    lane_f = lane.astype(F32)
    far = 1e9

    def first_max(vals):
        m = jnp.max(vals, axis=-1, keepdims=True)
        return m, jnp.min(jnp.where(vals == m, lane_f, far), axis=-1, keepdims=True)

    gl = jnp.where(lane < n_grp, acc, NEG_BIG)
    gmax, grp = first_max(gl)
    p_grp = 1.0 / jnp.sum(jnp.exp(gl - gmax), axis=-1, keepdims=True)
    lo = n_grp + grp * epg
    el = jnp.where((lane_f >= lo) & (lane_f < lo + epg), acc, NEG_BIG)
    m1, i1 = first_max(el)
    m2, i2 = first_max(jnp.where(lane_f == i1, NEG_BIG, el))
    e21 = jnp.exp(m2 - m1)
    g1 = p_grp / (1.0 + e21)
    g2 = p_grp * e21 / (1.0 + e21)
    out = jnp.where(lane == 0, i1 - n_grp, jnp.where(lane == 1, i2 - n_grp,
                                                    jnp.where(lane == 2, g1, jnp.where(lane == 3, g2, 0.0))))
    o_ref[...] = out


def router(x, w, bias, tm, n_grp, epg):
    m, k = x.shape
    n = w.shape[1]
    return pl.pallas_call(
        functools.partial(_router_body, n_grp=n_grp, epg=epg),
        grid=(m // tm,),
        in_specs=[pl.BlockSpec((tm, k), lambda i: (i, 0)), pl.BlockSpec((k, n), lambda i: (0, 0)),
                  pl.BlockSpec((1, n), lambda i: (0, 0))],
        out_specs=pl.BlockSpec((tm, n), lambda i: (i, 0)),
        out_shape=jax.ShapeDtypeStruct((m, n), F32),
        compiler_params=_cparams(("parallel",)),
        name="moe_router",
    )(x, w, bias.reshape(1, n))


def _mod_body(s_ref, w_ref, b_ref, o_ref):
    s = s_ref[...]
    s = s * jax.nn.sigmoid(s)
    o_ref[0] = _bdot(s, w_ref[0]) + b_ref[0]


def modulation(cvecs, mod_w, mod_b, tn=768):
    depth, d, n = mod_w.shape
    rows = cvecs.shape[0]
    while n % tn:
        tn //= 2
    return pl.pallas_call(
        _mod_body,
        grid=(depth, n // tn),
        in_specs=[pl.BlockSpec((rows, d), lambda l, j: (0, 0)),
                  pl.BlockSpec((1, d, tn), lambda l, j: (l, 0, j)),
                  pl.BlockSpec((1, 1, tn), lambda l, j: (l, 0, j))],
        out_specs=pl.BlockSpec((1, rows, tn), lambda l, j: (l, 0, j)),
        out_shape=jax.ShapeDtypeStruct((depth, rows, n), F32),
        compiler_params=_cparams(("parallel", "parallel")),
    )(cvecs, mod_w, mod_b.reshape(depth, 1, n))


def _na_key_start(i, rows, kr):
    return jnp.clip(i * NA_Q_ROWS - kr // 2, 0, rows - NA_KEY_ROWS)


def _na_bias_table(rpb, grid_w, rows):
    n_heads = rpb.shape[0]
    kr = (rpb.shape[1] + 1) // 2
    kcw = (rpb.shape[2] + 1) // 2
    n_blk = rows // NA_Q_ROWS
    q = jnp.arange(grid_w)[:, None]
    kc = jnp.arange(grid_w)[None, :]
    win = jnp.clip(q - kcw // 2, 0, grid_w - kcw)
    col_ok = (kc >= win) & (kc < win + kcw)
    dc = jnp.clip(kc - q, 1 - kcw, kcw - 1) + kcw - 1
    blk = jnp.array([0, min(1, n_blk - 1), n_blk - 1])[:, None, None]
    r = blk * NA_Q_ROWS + jnp.arange(NA_Q_ROWS)[None, :, None]
    krow = _na_key_start(blk, rows, kr) + jnp.arange(NA_KEY_ROWS)[None, None, :]
    rs = jnp.clip(r - kr // 2, 0, rows - kr)
    row_ok = (krow >= rs) & (krow < rs + kr)
    dr = jnp.clip(krow - r + kr - 1, 0, 2 * kr - 2)
    hi = lax.Precision.HIGHEST
    sel_r = jax.nn.one_hot(dr.reshape(-1), rpb.shape[1], dtype=F32)
    sel_c = jax.nn.one_hot(dc.reshape(-1), rpb.shape[2], dtype=F32)
    tab = jnp.einsum("ra,has->hrs", sel_r, jnp.einsum("hab,sb->has", rpb.astype(F32), sel_c, precision=hi),
                     precision=hi)
    tab = tab.reshape(n_heads, 3, NA_Q_ROWS, NA_KEY_ROWS, grid_w, grid_w)
    ok = row_ok[None, :, :, :, None, None] & col_ok[None, None, None, None]
    tab = jnp.where(ok, tab, NEG_BIG)
    return tab.transpose(0, 1, 2, 4, 3, 5).reshape(n_heads, 3, NA_Q_ROWS * grid_w, NA_KEY_ROWS * grid_w)


def _na_body(q_ref, k_ref, v_ref, kc_ref, vc_ref, bias_ref, o_ref, *, hpb, dh, grid_w, kr, rows):
    ks = _na_key_start(pl.program_id(2), rows, kr)
    start = pl.multiple_of(ks * grid_w, grid_w)
    nt = (((1,), (1,)), ((), ()))
    for h in range(hpb):
        sl = slice(h * dh, (h + 1) * dh)
        q = q_ref[:, sl]
        k = k_ref[pl.ds(start, NA_KEY_ROWS * grid_w), sl]
        v = v_ref[pl.ds(start, NA_KEY_ROWS * grid_w), sl]
        s = lax.dot_general(q, k, nt, preferred_element_type=F32) + bias_ref[h, 0]
        sc = lax.dot_general(q, kc_ref[:, sl], nt, preferred_element_type=F32)
        m = jnp.maximum(jnp.max(s, axis=-1, keepdims=True), jnp.max(sc, axis=-1, keepdims=True))
        p = jnp.exp(s - m)
        pc = jnp.exp(sc - m)
        den = jnp.sum(p, axis=-1, keepdims=True) + jnp.sum(pc, axis=-1, keepdims=True)
        o = (jnp.dot(p.astype(BF16), v, preferred_element_type=F32)
             + jnp.dot(pc.astype(BF16), vc_ref[:, sl], preferred_element_type=F32))
        o_ref[:, sl] = (o / den).astype(o_ref.dtype)


def _ctx_attn_body(q_ref, k_ref, v_ref, o_ref, *, hpb, dh):
    nt = (((1,), (1,)), ((), ()))
    for h in range(hpb):
        sl = slice(h * dh, (h + 1) * dh)
        s = lax.dot_general(q_ref[:, sl], k_ref[:, sl], nt, preferred_element_type=F32)
        m = jnp.max(s, axis=-1, keepdims=True)
        p = jnp.exp(s - m)
        den = jnp.sum(p, axis=-1, keepdims=True)
        o = jnp.dot(p.astype(BF16), v_ref[:, sl], preferred_element_type=F32)
        o_ref[:, sl] = (o / den).astype(o_ref.dtype)


def na_attention(qkv, bias_tab, kr, n_batch, seq, ctx_len, need_ctx):
    d = qkv.shape[1] // 3
    n_heads = bias_tab.shape[0]
    dh = d // n_heads
    hpb = max(1, min(n_heads, 256 // dh))
    bw = hpb * dh
    n_d = d // bw
    rows = seq // GRID_W
    n_blk = rows // NA_Q_ROWS
    qb = NA_Q_ROWS * GRID_W
    n_lat = n_batch * seq
    ctx_blk0 = n_lat // ctx_len

    def cls_of(i):
        return jnp.where(i == 0, 0, jnp.where(i == n_blk - 1, 2, 1))

    o_lat = pl.pallas_call(
        functools.partial(_na_body, hpb=hpb, dh=dh, grid_w=GRID_W, kr=kr, rows=rows),
        grid=(n_batch, n_d, n_blk),
        in_specs=[
            pl.BlockSpec((qb, bw), lambda b, g, i: (b * n_blk + i, g)),
            pl.BlockSpec((seq, bw), lambda b, g, i: (b, n_d + g)),
            pl.BlockSpec((seq, bw), lambda b, g, i: (b, 2 * n_d + g)),
            pl.BlockSpec((ctx_len, bw), lambda b, g, i: (ctx_blk0 + b, n_d + g)),
            pl.BlockSpec((ctx_len, bw), lambda b, g, i: (ctx_blk0 + b, 2 * n_d + g)),
            pl.BlockSpec((hpb, 1, qb, NA_KEY_ROWS * GRID_W), lambda b, g, i: (g, cls_of(i), 0, 0)),
        ],
        out_specs=pl.BlockSpec((qb, bw), lambda b, g, i: (b * n_blk + i, g)),
        out_shape=jax.ShapeDtypeStruct((n_lat, d), BF16),
        compiler_params=_cparams(("parallel", "parallel", "arbitrary")),
        name="na_attention",
    )(qkv, qkv, qkv, qkv, qkv, bias_tab)
    if not need_ctx:
        return o_lat
    o_ctx = pl.pallas_call(
        functools.partial(_ctx_attn_body, hpb=hpb, dh=dh),
        grid=(n_batch, n_d),
        in_specs=[
            pl.BlockSpec((ctx_len, bw), lambda b, g: (ctx_blk0 + b, g)),
            pl.BlockSpec((ctx_len, bw), lambda b, g: (ctx_blk0 + b, n_d + g)),
            pl.BlockSpec((ctx_len, bw), lambda b, g: (ctx_blk0 + b, 2 * n_d + g)),
        ],
        out_specs=pl.BlockSpec((ctx_len, bw), lambda b, g: (b, g)),
        out_shape=jax.ShapeDtypeStruct((n_batch * ctx_len, d), BF16),
        compiler_params=_cparams(("parallel", "parallel")),
    )(qkv, qkv, qkv)
    return jnp.concatenate([o_lat, o_ctx], axis=0)


def _expert_body(be_ref, idx_ref, idx_next_ref, x_hbm, wgu_ref, wd_ref, o_ref, xbuf, sem, wgu_s, wd_s, *,
                 hidden, n_blocks, n_slab):
    j = pl.program_id(0)
    cur = j % 2

    def row_copy(idx, buf, r):
        return pltpu.make_async_copy(x_hbm.at[pl.ds(idx[0, 0, r], 1)], xbuf.at[buf, pl.ds(r, 1)], sem.at[buf])

    @pl.when(j == 0)
    def _():
        for r in range(MOE_BLOCK):
            row_copy(idx_ref, 0, r).start(priority=r % 2)

    prev = be_ref[jnp.maximum(j - 1, 0)]
    changed = jnp.logical_or(j == 0, be_ref[j] != prev)

    @pl.when(changed)
    def _():
        wgu_s[...] = wgu_ref[...].astype(BF16)
        wd_s[...] = wd_ref[...].astype(BF16)

    for r in range(MOE_BLOCK):
        row_copy(idx_ref, cur, r).wait()

    for r in range(MOE_BLOCK):
        row_copy(idx_next_ref, 1 - cur, r).start(priority=r % 2)
    xb = xbuf.at[cur]
    gu = None
    for c in range(0, n_slab, 2):
        xa = jnp.concatenate([xb[:, c, :], xb[:, c + 1, :]], axis=-1).astype(BF16)
        part = jnp.dot(xa, wgu_s[c * LANES:(c + 2) * LANES, :], preferred_element_type=F32)
        gu = part if gu is None else gu + part
    g = gu[:, :hidden]
    u = gu[:, hidden:]
    a = (g * jax.nn.sigmoid(g) * u).astype(BF16)
    o_ref[...] = jnp.dot(a, wd_s[...], preferred_element_type=F32)

    @pl.when(j == n_blocks - 1)
    def _():
        for r in range(MOE_BLOCK):
            row_copy(idx_next_ref, 1 - cur, r).wait()


def expert_blocks(x, slot_tok, blk_e, w_gu, w_down, layer):
    n_slab = x.shape[1]
    d = n_slab * LANES
    hidden = w_down.shape[2]
    n_blocks = blk_e.shape[0]
    idx = slot_tok.reshape(n_blocks, 1, MOE_BLOCK)
    return pl.pallas_call(
        functools.partial(_expert_body, hidden=hidden, n_blocks=n_blocks, n_slab=n_slab),
        grid_spec=pltpu.PrefetchScalarGridSpec(
            num_scalar_prefetch=1,
            grid=(n_blocks,),
            in_specs=[pl.BlockSpec((1, 1, MOE_BLOCK), lambda j, be: (j, 0, 0), memory_space=pltpu.SMEM),
                      pl.BlockSpec((1, 1, MOE_BLOCK), lambda j, be: (jnp.minimum(j + 1, n_blocks - 1), 0, 0),
                                   memory_space=pltpu.SMEM),
                      pl.BlockSpec(memory_space=pl.ANY),
                      pl.BlockSpec((None, None, d, 2 * hidden), lambda j, be: (layer, be[j], 0, 0)),
                      pl.BlockSpec((None, None, hidden, d), lambda j, be: (layer, be[j], 0, 0))],
            out_specs=pl.BlockSpec((MOE_BLOCK, d), lambda j, be: (j, 0)),
            scratch_shapes=[pltpu.VMEM((2, MOE_BLOCK, n_slab, LANES), F32), pltpu.SemaphoreType.DMA((2,)),
                            pltpu.VMEM((d, 2 * hidden), BF16), pltpu.VMEM((hidden, d), BF16)]),
        out_shape=jax.ShapeDtypeStruct((n_blocks * MOE_BLOCK, d), F32),
        compiler_params=_cparams(("arbitrary",), VMEM_LIMIT_BIG_BYTES),
        name="moe_experts",
    )(blk_e, idx, idx, x, w_gu, w_down)


def _combine_body(idx_ref, idx_next_ref, route_ref, x_ref, g_ref, yb_hbm, o_ref, ybuf, sem, *, tmc, n_tiles):
    i = pl.program_id(0)
    cur = i % 2

    def row_copy(idx, buf, k, r):
        return pltpu.make_async_copy(yb_hbm.at[pl.ds(idx[0, 0, k * tmc + r], 1)],
                                     ybuf.at[buf, k, pl.ds(r, 1)], sem.at[buf])

    def start_all(idx, buf):
        for k in range(MOE_TOP_K):
            for r in range(tmc):
                row_copy(idx, buf, k, r).start(priority=r % 2)

    @pl.when(i == 0)
    def _():
        start_all(idx_ref, 0)

    @pl.when(i + 1 < n_tiles)
    def _():
        start_all(idx_next_ref, 1 - cur)

    for k in range(MOE_TOP_K):
        for r in range(tmc):
            row_copy(idx_ref, cur, k, r).wait()
    route = route_ref[...]
    f = ybuf[cur, 0] * route[:, MOE_TOP_K:MOE_TOP_K + 1]
    for k in range(1, MOE_TOP_K):
        f = f + ybuf[cur, k] * route[:, MOE_TOP_K + k:MOE_TOP_K + k + 1]
    o_ref[...] = x_ref[...] + g_ref[0] * f


def moe_combine(xs, yb, slot_of_asg, route, gate_tab, n_batch, seq):
    n_tok, d = xs.shape
    tmc = MOE_COMBINE_ROWS
    n_tiles = n_tok // tmc
    idx = slot_of_asg.reshape(n_tiles, tmc, MOE_TOP_K).transpose(0, 2, 1).reshape(n_tiles, 1, MOE_TOP_K * tmc)
    seg = _seg_fn(seq // tmc, n_batch)
    return pl.pallas_call(
        functools.partial(_combine_body, tmc=tmc, n_tiles=n_tiles),
        grid=(n_tiles,),
        in_specs=[pl.BlockSpec((1, 1, MOE_TOP_K * tmc), lambda i: (i, 0, 0), memory_space=pltpu.SMEM),
                  pl.BlockSpec((1, 1, MOE_TOP_K * tmc), lambda i: (jnp.minimum(i + 1, n_tiles - 1), 0, 0),
                               memory_space=pltpu.SMEM),
                  pl.BlockSpec((tmc, route.shape[1]), lambda i: (i, 0)),
                  pl.BlockSpec((tmc, d), lambda i: (i, 0)),
                  pl.BlockSpec((1, 1, d), lambda i: (seg(i), 0, 0)),
                  pl.BlockSpec(memory_space=pl.ANY)],
        out_specs=pl.BlockSpec((tmc, d), lambda i: (i, 0)),
        out_shape=jax.ShapeDtypeStruct((n_tok, d), F32),
        scratch_shapes=[pltpu.VMEM((2, MOE_TOP_K, tmc, d), F32), pltpu.SemaphoreType.DMA((2,))],
        compiler_params=_cparams(("arbitrary",)),
        name="moe_combine",
    )(idx, idx, route, xs, gate_tab, yb)


def hier_moe(xs, h, h_slabs, wg, bg, we, be, w_gu, w_down, layer, gate_tab, tm, n_batch, seq):
    n_tok, d = h.shape
    n_grp = wg.shape[1]
    n_exp = we.shape[1]
    epg = n_exp // n_grp
    n_logit = n_grp + n_exp
    n_pad = -(-n_logit // 128) * 128
    wcat = jnp.pad(jnp.concatenate([wg, we], axis=1), ((0, 0), (0, n_pad - n_logit)))
    bcat = jnp.pad(jnp.concatenate([bg, be], axis=0), (0, n_pad - n_logit))
    route = router(h, wcat, bcat, tm, n_grp, epg)

    flat_e = route[:, :MOE_TOP_K].astype(jnp.int32).reshape(-1)
    n_asg = flat_e.shape[0]
    onehot = (flat_e[:, None] == jnp.arange(n_exp, dtype=jnp.int32)[None, :]).astype(F32)
    ck = 128
    oh3 = onehot.reshape(n_asg // ck, ck, n_exp)
    tri = jnp.tril(jnp.ones((ck, ck), F32))
    within = jnp.einsum("ij,cjk->cik", tri, oh3, precision=lax.Precision.HIGHEST)
    tot = within[:, -1, :]
    before = jnp.cumsum(tot, axis=0) - tot
    rank = jnp.sum(oh3 * (within + before[:, None, :]), axis=-1).reshape(n_asg).astype(jnp.int32) - 1
    counts = jnp.sum(tot, axis=0).astype(jnp.int32)
    padded = (counts + MOE_BLOCK - 1) // MOE_BLOCK * MOE_BLOCK
    pad_end = jnp.cumsum(padded)
    pad_start = pad_end - padded
    slot = jnp.sum(onehot * pad_start.astype(F32)[None, :], axis=-1).astype(jnp.int32) + rank
    n_blocks = -(-n_asg // MOE_BLOCK) + n_exp
    slot_tok = jnp.zeros((n_blocks * MOE_BLOCK,), jnp.int32).at[slot].set(
        jnp.arange(n_asg, dtype=jnp.int32) // MOE_TOP_K)
    blk_start = jnp.arange(n_blocks, dtype=jnp.int32) * MOE_BLOCK
    blk_e = jnp.minimum(jnp.sum((pad_end[None, :] <= blk_start[:, None]).astype(jnp.int32), axis=1), n_exp - 1)
    yb = expert_blocks(h_slabs, slot_tok, blk_e, w_gu, w_down, layer)
    return moe_combine(xs, yb, slot.reshape(n_tok, MOE_TOP_K), route, gate_tab, n_batch, seq)


def _s5_weights(lam_re, lam_im, log_dt, b_re, b_im, c_re, c_im, t_chunk):
    n_grp, n_state = lam_re.shape[1], lam_re.shape[2]
    cg = b_re.shape[-1]
    lam = lax.complex(lam_re.astype(F32), lam_im.astype(F32))
    dt = jnp.exp(log_dt.astype(F32))[..., None]
    lam_bar = jnp.exp(lam * dt)
    b_bar = ((lam_bar - 1.0) / lam)[..., None] * lax.complex(b_re.astype(F32), b_im.astype(F32))
    c_mat = lax.complex(c_re.astype(F32), c_im.astype(F32))
    ks = jnp.arange(t_chunk + 1, dtype=F32)
    pw = jnp.exp((lam * dt)[..., None] * ks)
    hi = lax.Precision.HIGHEST
    kern = jnp.einsum("dgcp,dgpk,dgpi->dgkci", c_mat, pw[..., :t_chunk], b_bar, precision=hi).real
    t_idx = jnp.arange(t_chunk)
    lag = t_idx[None, :] - t_idx[:, None]
    k_f = kern[0][:, jnp.clip(lag, 0, t_chunk - 1)]
    k_r = kern[1][:, jnp.clip(-lag, 0, t_chunk - 1)]
    toep = (jnp.where((lag >= 0)[None, :, :, None, None], k_f, 0.0)
            + jnp.where((lag <= 0)[None, :, :, None, None], k_r, 0.0))
    toep = toep.transpose(0, 1, 4, 2, 3).reshape(n_grp, t_chunk * cg, t_chunk * cg)
    pf = pw[0][..., :t_chunk][..., ::-1]
    pr = pw[1][..., :t_chunk]
    bend_f = jnp.einsum("gps,gpc->gscp", pf, b_bar[0]).reshape(n_grp, t_chunk * cg, n_state)
    bend_r = jnp.einsum("gps,gpc->gscp", pr, b_bar[1]).reshape(n_grp, t_chunk * cg, n_state)
    bend = jnp.stack([bend_f.real, bend_f.imag, bend_r.real, bend_r.imag], axis=1)
    of = jnp.einsum("gcp,gpt->gptc", c_mat[0], pw[0][..., 1:]).reshape(n_grp, n_state, t_chunk * cg)
    orv = jnp.einsum("gcp,gpt->gptc", c_mat[1], pw[1][..., 1:][..., ::-1]).reshape(n_grp, n_state, t_chunk * cg)
    cout = jnp.stack([of.real, -of.imag, orv.real, -orv.imag], axis=1)

    gsz = LANES // cg
    nb = n_grp // gsz
    tile = 512

    def expand(n_outer, n_inner):
        dst = jnp.arange(n_outer * gsz * n_inner)
        src = (dst // (gsz * n_inner)) * n_inner + dst % n_inner
        return (jnp.arange(n_outer * n_inner)[:, None] == src[None, :]).astype(F32)

    def grp(size, period, width):
        return (jnp.arange(size) % period) // width

    def own_group(row_grp, col_grp):
        return (row_grp[:, None] == col_grp[None, :]).astype(F32)

    e_tc, e_xp = expand(t_chunk, cg), expand(4, n_state)
    wide_ch, wide_st = e_tc.shape[1], e_xp.shape[1]
    kin = t_chunk * cg
    expand_mm = functools.partial(matmul, tm=tile, out_dtype=BF16, full_rows=tile)
    xt = toep.reshape(nb, gsz, t_chunk, cg, kin).transpose(0, 2, 1, 3, 4).reshape(n_grp * kin, kin)
    wt = expand_mm(xt, e_tc, tilemask=own_group(grp(tile, LANES, cg), grp(wide_ch, LANES, cg)))
    wt = wt.reshape(nb, t_chunk, LANES, t_chunk * LANES)
    xb = bend.reshape(nb, gsz, 4, t_chunk, cg, n_state).transpose(0, 3, 1, 4, 2, 5).reshape(n_grp * kin, 4 * n_state)
    wb = expand_mm(xb, e_xp, tilemask=own_group(grp(tile, LANES, cg), grp(wide_st, gsz * n_state, n_state)))
    wb = wb.reshape(nb, t_chunk, LANES, 4 * gsz * n_state)
    xc = cout.reshape(nb, gsz, 4, n_state, kin).transpose(0, 2, 1, 3, 4).reshape(4 * n_grp * n_state, kin)
    wc = expand_mm(xc, e_tc, tilemask=own_group(grp(tile, gsz * n_state, n_state), grp(wide_ch, LANES, cg)))
    wc = wc.reshape(nb, 4 * gsz * n_state, t_chunk * LANES)
    lam_t = pw[..., t_chunk]
    lam_t4 = jnp.stack([lam_t[0].real, lam_t[0].imag, lam_t[1].real, lam_t[1].imag]).reshape(4, n_grp * n_state)
    return wt, wb, wc, lam_t4


def _s5_a_body(h_ref, wb_ref, e_ref, *, t_chunk):
    acc = jnp.dot(h_ref[:, 0, :].astype(BF16), wb_ref[0], preferred_element_type=F32)
    for s in range(1, t_chunk):
        acc = acc + jnp.dot(h_ref[:, s, :].astype(BF16), wb_ref[s], preferred_element_type=F32)
    e_ref[...] = acc


def _s5_b_body(e_ref, lam_ref, x_ref, *, n_batch, n_lat_chunk, n_ctx_chunk, width):
    ctx0 = n_batch * n_lat_chunk
    n_chunk = n_lat_chunk + n_ctx_chunk

    def comp(x):
        return slice(x * width, (x + 1) * width)

    lfr, lfi, lrr, lri = (lam_ref[:, comp(x)] for x in range(4))

    for b in range(n_batch):
        def fwd(n, carry):
            sr, si = carry
            row = jnp.where(n < n_ctx_chunk, ctx0 + b * n_ctx_chunk + n, b * n_lat_chunk + n - n_ctx_chunk)
            x_ref[pl.ds(row, 1), comp(0)] = sr
            x_ref[pl.ds(row, 1), comp(1)] = si
            er = e_ref[pl.ds(row, 1), comp(0)]
            ei = e_ref[pl.ds(row, 1), comp(1)]
            return lfr * sr - lfi * si + er, lfr * si + lfi * sr + ei

        def rev(n, carry):
            sr, si = carry
            row = jnp.where(n < n_ctx_chunk, ctx0 + b * n_ctx_chunk + n_ctx_chunk - 1 - n,
                            b * n_lat_chunk + n_chunk - 1 - n)
            x_ref[pl.ds(row, 1), comp(2)] = sr
            x_ref[pl.ds(row, 1), comp(3)] = si
            er = e_ref[pl.ds(row, 1), comp(2)]
            ei = e_ref[pl.ds(row, 1), comp(3)]
            return lrr * sr - lri * si + er, lrr * si + lri * sr + ei

        z = jnp.zeros((1, width), F32)
        lax.fori_loop(0, n_chunk, fwd, (z, z))
        lax.fori_loop(0, n_chunk, rev, (z, z))


def _gelu(y):
    return 0.5 * y * (1.0 + jnp.tanh(0.7978845608028654 * (y + 0.044715 * (y * y * y))))


def _s5_c_body(h_ref, x_ref, wt_ref, wc_ref, dsk_ref, o_ref, *, t_chunk):
    acc = jnp.dot(x_ref[...].astype(BF16), wc_ref[...], preferred_element_type=F32)
    for s in range(t_chunk):
        acc = acc + jnp.dot(h_ref[:, s, :].astype(BF16), wt_ref[s], preferred_element_type=F32)
    dsk = dsk_ref[...]
    for t in range(t_chunk):
        o_ref[:, t, :] = _gelu(acc[:, t * LANES:(t + 1) * LANES] + h_ref[:, t, :] * dsk)


def s5_core(h_f32, weights, d_skip, n_batch, seq, ctx_len):
    wt, wb, wc, lam_t4 = weights
    t = S5_CHUNK
    m, d = h_f32.shape
    nb = wt.shape[0]
    sw = wb.shape[-1]
    width = sw // 4
    rows = m // t
    rbs = rows // 2 if rows % 16 == 0 else rows
    h3 = h_f32.reshape(rows, t, d)
    hblk = pl.BlockSpec((rbs, t, LANES), lambda g, i: (i, 0, g))
    sblk = pl.BlockSpec((rbs, sw), lambda g, i: (i, g))
    e = pl.pallas_call(
        functools.partial(_s5_a_body, t_chunk=t),
        grid=(nb, rows // rbs),
        in_specs=[hblk, pl.BlockSpec((None, t, LANES, sw), lambda g, i: (g, 0, 0, 0))],
        out_specs=sblk,
        out_shape=jax.ShapeDtypeStruct((rows, nb * sw), F32),
        compiler_params=_cparams(("parallel", "arbitrary")),
        name="s5_chunk_states",
    )(h3, wb)

    lam_blk = lam_t4.reshape(4, nb, width).transpose(1, 0, 2).reshape(nb, 1, sw)
    x_in = pl.pallas_call(
        functools.partial(_s5_b_body, n_batch=n_batch, n_lat_chunk=seq // t, n_ctx_chunk=ctx_len // t,
                          width=width),
        grid=(nb,),
        in_specs=[pl.BlockSpec((rows, sw), lambda g: (0, g)), pl.BlockSpec((None, 1, sw), lambda g: (g, 0, 0))],
        out_specs=pl.BlockSpec((rows, sw), lambda g: (0, g)),
        out_shape=jax.ShapeDtypeStruct((rows, nb * sw), F32),
        compiler_params=_cparams(("parallel",)),
        name="s5_chunk_recurrence",
    )(e, lam_blk)

    y3 = pl.pallas_call(
        functools.partial(_s5_c_body, t_chunk=t),
        grid=(nb, rows // rbs),
        in_specs=[hblk, sblk,
                  pl.BlockSpec((None, t, LANES, t * LANES), lambda g, i: (g, 0, 0, 0)),
                  pl.BlockSpec((None, sw, t * LANES), lambda g, i: (g, 0, 0)),
                  pl.BlockSpec((1, LANES), lambda g, i: (0, g))],
        out_specs=hblk,
        out_shape=jax.ShapeDtypeStruct((rows, t, d), F32),
        compiler_params=_cparams(("parallel", "arbitrary"), VMEM_LIMIT_BIG_BYTES),
        name="s5_outputs",
    )(h3, x_in, wt, wc, d_skip.astype(F32).reshape(1, d))
    return y3.reshape(m, d)


def _shift_body(h_ref, hp_ref, hn_ref, mu_ref, *o_refs, tm, tiles_lat, tiles_ctx, n_lat_tiles):
    i = pl.program_id(0)
    h = h_ref[...]
    first = jnp.where(i < n_lat_tiles, i % tiles_lat == 0, (i - n_lat_tiles) % tiles_ctx == 0)
    last = jnp.where(i < n_lat_tiles, i % tiles_lat == tiles_lat - 1,
                     (i - n_lat_tiles) % tiles_ctx == tiles_ctx - 1)
    rid = lax.broadcasted_iota(jnp.int32, h.shape, 0)
    prev_edge = jnp.where(first, 0.0, hp_ref[7:8, :])
    next_edge = jnp.where(last, 0.0, hn_ref[0:1, :])
    prev = jnp.where(rid == 0, prev_edge, pltpu.roll(h, 1, 0))
    nxt = jnp.where(rid == tm - 1, next_edge, pltpu.roll(h, tm - 1, 0))
    xx = 0.5 * (prev + nxt) - h
    for m, o_ref in enumerate(o_refs):
        o_ref[...] = (h + xx * mu_ref[m:m + 1, :]).astype(o_ref.dtype)


def token_shift_mix(h, mu, tm, seq, ctx_len, n_batch):
    m, d = h.shape
    n_mix = mu.shape[0]
    n_lat_tiles = n_batch * seq // tm
    nb8 = m // 8
    per = tm // 8
    row = pl.BlockSpec((tm, d), lambda i: (i, 0))
    return pl.pallas_call(
        functools.partial(_shift_body, tm=tm, tiles_lat=seq // tm, tiles_ctx=max(ctx_len // tm, 1),
                          n_lat_tiles=n_lat_tiles),
        grid=(m // tm,),
        in_specs=[row,
                  pl.BlockSpec((8, d), lambda i: (jnp.maximum(i * per - 1, 0), 0)),
                  pl.BlockSpec((8, d), lambda i: (jnp.minimum((i + 1) * per, nb8 - 1), 0)),
                  pl.BlockSpec((n_mix, d), lambda i: (0, 0))],
        out_specs=[row] * n_mix,
        out_shape=[jax.ShapeDtypeStruct((m, d), BF16)] * n_mix,
        compiler_params=_cparams(("parallel",)),
    )(h, h, h, mu.astype(F32))


def _rwkv_scan_body(rf_ref, rr_ref, kf_ref, kr_ref, vf_ref, vr_ref, wf_ref, wr_ref, af_ref, ar_ref,
                    kk_ref, ka_ref, of_ref, or_ref, s_ref, op_ref, *, tb, n, bh):
    @pl.when(pl.program_id(0) == 0)
    def _():
        s_ref[...] = jnp.zeros_like(s_ref)

    k_k = kk_ref[...]
    k_a = ka_ref[...]

    def step(t, carry):
        tr = tb - 1 - t
        r = jnp.concatenate([rf_ref[t], rr_ref[tr]], axis=-1)
        k = jnp.concatenate([kf_ref[t], kr_ref[tr]], axis=-1)
        v = jnp.concatenate([vf_ref[t], vr_ref[tr]], axis=-1)
        w = jnp.concatenate([wf_ref[t], wr_ref[tr]], axis=-1)
        a = jnp.concatenate([af_ref[t], ar_ref[tr]], axis=-1)
        kk = k * k_k
        nrm = jnp.sqrt(jnp.sum(kk * kk, axis=0, keepdims=True))
        kk = kk / jnp.maximum(nrm, 1e-12)
        nkk = -kk
        akk = kk * a
        kd = k * (1.0 + (a - 1.0) * k_a)
        op_ref[0] = nkk
        op_ref[1] = w * r
        op_ref[2] = w
        op_ref[3] = akk
        op_ref[4] = kd
        c_a = jnp.sum(akk * r, axis=0, keepdims=True)
        c_k = jnp.sum(kd * r, axis=0, keepdims=True)
        def reduce_keys(jb, acc):
            sa, so = acc
            for jj in range(SCAN_KEY_BLOCK):
                j = jb * SCAN_KEY_BLOCK + jj
                s_j = s_ref[j]
                sa = sa + s_j * op_ref[0, pl.ds(j, 1), :]
                so = so + s_j * op_ref[1, pl.ds(j, 1), :]
            return sa, so

        zero = jnp.zeros_like(v)
        sa, so = lax.fori_loop(0, n // SCAN_KEY_BLOCK, reduce_keys, (zero, zero))
        o = so + sa * c_a + v * c_k
        of_ref[t] = o[:, :bh]
        or_ref[tr] = o[:, bh:]

        def update_keys(jb, c):
            for jj in range(SCAN_KEY_BLOCK):
                j = jb * SCAN_KEY_BLOCK + jj
                s_ref[j] = (s_ref[j] * op_ref[2, pl.ds(j, 1), :] + sa * op_ref[3, pl.ds(j, 1), :]
                            + v * op_ref[4, pl.ds(j, 1), :])
            return c

        lax.fori_loop(0, n // SCAN_KEY_BLOCK, update_keys, 0)
        return carry

    lax.fori_loop(0, tb, step, 0)


def rwkv_scan(r, k, v, w0, w1, a0, a1, k_k, k_a, ctx_len, tb=32):
    steps, n, bh = r.shape
    while steps % tb or ctx_len % tb:
        tb //= 2
    nblk, nblk_ctx = steps // tb, ctx_len // tb

    def rev_blk(i):
        return jnp.where(i < nblk_ctx, nblk_ctx - 1 - i, nblk + nblk_ctx - 1 - i)

    fwd = pl.BlockSpec((tb, n, bh), lambda i: (i, 0, 0))
    rev = pl.BlockSpec((tb, n, bh), lambda i: (rev_blk(i), 0, 0))
    tab = pl.BlockSpec((n, 2 * bh), lambda i: (0, 0))
    out = jax.ShapeDtypeStruct((steps, n, bh), F32)
    return pl.pallas_call(
        functools.partial(_rwkv_scan_body, tb=tb, n=n, bh=bh),
        grid=(nblk,),
        in_specs=[fwd, rev, fwd, rev, fwd, rev, fwd, rev, fwd, rev, tab, tab],
        out_specs=[fwd, rev],
        out_shape=[out, out],
        scratch_shapes=[pltpu.VMEM((n, n, 2 * bh), F32), pltpu.VMEM((5, n, 2 * bh), F32)],
        compiler_params=_cparams(("arbitrary",)),
        name="rwkv_scan",
    )(r, r, k, k, v, v, w0, w1, a0, a1, k_k, k_a)


def _rwkv_post_body(of_ref, or_ref, r_ref, k_ref, v_ref, a0_ref, a1_ref, g_ref, ka_ref, rk_ref, lg_ref, lb_ref,
                    y_ref):
    o = of_ref[...] + or_ref[...]
    mean = jnp.mean(o, axis=1, keepdims=True)
    var = jnp.mean(jnp.square(o - mean), axis=1, keepdims=True)
    on = (o - mean) * lax.rsqrt(var + RW_GN_EPS) * lg_ref[...] + lb_ref[...]
    r = r_ref[...]
    k = k_ref[...]
    k_a = ka_ref[...]
    rk = rk_ref[...]
    kd0 = k * (1.0 + (a0_ref[...] - 1.0) * k_a)
    kd1 = k * (1.0 + (a1_ref[...] - 1.0) * k_a)
    bonus = (jnp.sum(r * kd0 * rk, axis=1, keepdims=True)
             + jnp.sum(r * kd1 * rk, axis=1, keepdims=True)) * v_ref[...]
    y = (on + bonus)
    y_ref[...] = (y * g_ref[...]).astype(y_ref.dtype)


def rwkv_post(o_f, o_r, r, k, v, a0, a1, gate, k_a, r_k, ln_g, ln_b, tb=32):
    steps, n, ch = o_f.shape
    while steps % tb:
        tb //= 2
    blk = pl.BlockSpec((tb, n, ch), lambda i: (i, 0, 0))
    tab = pl.BlockSpec((1, n, ch), lambda i: (0, 0, 0))
    return pl.pallas_call(
        _rwkv_post_body,
        grid=(steps // tb,),
        in_specs=[blk] * 8 + [tab] * 4,
        out_specs=blk,
        out_shape=jax.ShapeDtypeStruct((steps, n, ch), F32),
        compiler_params=_cparams(("parallel",)),
        name="rwkv_post",
    )(o_f, o_r, r, k, v, a0, a1, gate, k_a[None], r_k[None], ln_g[None], ln_b[None])


def _step_block_row(s, b, n_ctx_blk, n_lat_blk, n_batch):
    return jnp.where(s < n_ctx_blk, n_batch * n_lat_blk + b * n_ctx_blk + s, b * n_lat_blk + s - n_ctx_blk)


def _to_steps_body(*refs, n_heads):
    x_refs, o_ref = refs[:-1], refs[-1]
    parts = []
    for x_ref in x_refs:
        x = x_ref[...]
        t = x.shape[0]
        parts.append(jnp.swapaxes(x.reshape(t, n_heads, x.shape[1] // n_heads), 1, 2))
    o_ref[...] = jnp.concatenate(parts, axis=-1)


def tokens_to_steps(x, n_batch, seq, ctx_len, n_heads):
    m, d = x.shape
    n = d // n_heads
    t = RELAYOUT_ROWS
    n_ctx_blk, n_lat_blk = ctx_len // t, seq // t

    def spec(b):
        return pl.BlockSpec((t, d), lambda s: (_step_block_row(s, b, n_ctx_blk, n_lat_blk, n_batch), 0))

    return pl.pallas_call(
        functools.partial(_to_steps_body, n_heads=n_heads),
        grid=(n_ctx_blk + n_lat_blk,),
        in_specs=[spec(b) for b in range(n_batch)],
        out_specs=pl.BlockSpec((t, n, n_batch * n_heads), lambda s: (s, 0, 0)),
        out_shape=jax.ShapeDtypeStruct((ctx_len + seq, n, n_batch * n_heads), x.dtype),
        compiler_params=_cparams(("parallel",)),
        name="tokens_to_steps",
    )(*([x] * n_batch))


def _to_tokens_body(y_ref, o_ref, *, n_heads, n_batch):
    y = y_ref[...]
    t, n = y.shape[0], y.shape[1]
    for b in range(n_batch):
        @pl.when(pl.program_id(1) == b)
        def _():
            yb = y[:, :, b * n_heads:(b + 1) * n_heads]
            o_ref[...] = jnp.swapaxes(yb, 1, 2).reshape(t, n_heads * n)


def steps_to_tokens(y, n_batch, seq, ctx_len, n_heads):
    steps, n, _ = y.shape
    d = n_heads * n
    t = RELAYOUT_ROWS
    n_ctx_blk, n_lat_blk = ctx_len // t, seq // t
    return pl.pallas_call(
        functools.partial(_to_tokens_body, n_heads=n_heads, n_batch=n_batch),
        grid=(n_ctx_blk + n_lat_blk, n_batch),
        in_specs=[pl.BlockSpec((t, n, n_batch * n_heads), lambda s, b: (s, 0, 0))],
        out_specs=pl.BlockSpec((t, d), lambda s, b: (_step_block_row(s, b, n_ctx_blk, n_lat_blk, n_batch), 0)),
        out_shape=jax.ShapeDtypeStruct((n_batch * (seq + ctx_len), d), y.dtype),
        compiler_params=_cparams(("parallel", "arbitrary")),
        name="steps_to_tokens",
    )(y)


def _pad_cols(w, mult=128):
    n = w.shape[-1]
    p = -(-n // mult) * mult - n
    return jnp.pad(w, [(0, 0)] * (w.ndim - 1) + [(0, p)]) if p else w


def _pad_rows(w, mult=128):
    n = w.shape[-2]
    p = -(-n // mult) * mult - n
    return jnp.pad(w, [(0, 0)] * (w.ndim - 2) + [(0, p), (0, 0)]) if p else w


def rwkv_mixer_tokens(h, mu, w_rkv, w0, w1, w2, a0, a1, a2, k_k, k_a, r_k, ln_g, ln_b, g1, g2,
                      tm, n_batch, seq, ctx_len):
    m, d = h.shape
    n_heads, n = r_k.shape
    n_lat = n_batch * seq
    ts = _row_tile(seq, ctx_len, cap=256)
    x_r, x_k, x_v, x_w, x_a, x_g = token_shift_mix(h, mu, ts, seq, ctx_len, n_batch)
    mm = functools.partial(matmul, tm=tm)
    r = mm(x_r, w_rkv, w_index=0, full_rows=True)
    k = mm(x_k, w_rkv, w_index=1, full_rows=True)
    v = mm(x_v, w_rkv, w_index=2, full_rows=True)
    gate = mm(mm(x_g, _pad_cols(g1), act="sigmoid", out_dtype=BF16), _pad_rows(g2))
    dec, aa = [], []
    for dn in range(2):
        lw = mm(x_w, _pad_cols(w1[dn]), act="tanh", out_dtype=BF16)
        dec.append(mm(lw, _pad_rows(w2[dn]), bias=w0[dn], act="decay"))
        la = mm(x_a, _pad_cols(a1[dn]), out_dtype=BF16)
        aa.append(mm(la, _pad_rows(a2[dn]), bias=a0[dn], act="sigmoid"))

    to_seq = functools.partial(tokens_to_steps, n_batch=n_batch, seq=seq, ctx_len=ctx_len, n_heads=n_heads)
    r_t, k_t, v_t, g_t = to_seq(r), to_seq(k), to_seq(v), to_seq(gate)
    d0_t, d1_t, a0_t, a1_t = to_seq(dec[0]), to_seq(dec[1]), to_seq(aa[0]), to_seq(aa[1])

    def table(vec):
        return jnp.tile(vec.astype(F32).reshape(n_heads, n).T[:, None, :], (1, n_batch, 1)).reshape(n, n_batch * n_heads)

    kk_tab, ka_tab = table(k_k), table(k_a)
    o_f, o_r = rwkv_scan(r_t, k_t, v_t, d0_t, d1_t, a0_t, a1_t, jnp.concatenate([kk_tab, kk_tab], axis=-1),
                         jnp.concatenate([ka_tab, ka_tab], axis=-1), ctx_len)
    y_t = rwkv_post(o_f, o_r, r_t, k_t, v_t, a0_t, a1_t, g_t, ka_tab, table(r_k.reshape(-1)),
                    table(ln_g), table(ln_b))
    return steps_to_tokens(y_t, n_batch, seq, ctx_len, n_heads)


def kernel(x, c, ctx, c_ctx, mod_w, mod_b, norm_g, final_g, na_w_qkv, na_w_o, na_rpb, s5_lam_re, s5_lam_im, s5_log_dt, s5_b_re, s5_b_im, s5_c_re, s5_c_im, s5_d, s5_w_glu, rw_mu, rw_w_rkv, rw_w0, rw_w1, rw_w2, rw_a0, rw_a1, rw_a2, rw_k_k, rw_k_a, rw_r_k, rw_ln_g, rw_ln_b, rw_g1, rw_g2, rw_w_o, moe_wg, moe_bg, moe_we, moe_be, moe_w_gu, moe_w_down):
    n_batch, seq, d = x.shape
    ctx_len = ctx.shape[1]
    depth = mod_w.shape[0]
    n_lat = n_batch * seq
    n_ctx = n_batch * ctx_len
    tm = _row_tile(seq, n_ctx)
    tpb = seq // tm
    xs = jnp.concatenate([x.reshape(n_lat, d), ctx.reshape(n_ctx, d)], axis=0).astype(F32)

    cvecs = jnp.concatenate([c.astype(F32), c_ctx.astype(F32)[None]], axis=0)
    rows_pad = -(-(n_batch + 1) // 8) * 8
    cvecs = jnp.pad(cvecs, ((0, rows_pad - n_batch - 1), (0, 0)))
    mods = modulation(cvecs, mod_w, mod_b)[:, :n_batch + 1].reshape(depth, n_batch + 1, 6, 1, d)
    mods = mods.transpose(0, 2, 1, 3, 4)

    for i in range(depth):
        last = i == depth - 1
        sh1, sc1, g1, sh2, sc2, g2 = (mods[i, q] for q in range(6))
        pn = functools.partial(prenorm, tm=tm, tiles_per_batch=tpb, n_batch=n_batch)
        mm = functools.partial(matmul, tm=tm, tiles_per_batch=tpb, n_batch=n_batch)
        mix, j = i % N_MIXERS, i // N_MIXERS
        if mix == 0:
            (hb,) = pn(xs, norm_g[i, 0], 1.0 + sc1, sh1, out_dtypes=(BF16,))
            n_heads = na_rpb.shape[1]
            qscale = jnp.concatenate([jnp.full((d,), (d // n_heads) ** -0.5, F32), jnp.ones((2 * d,), F32)])
            qkv = mm(hb, na_w_qkv, w_index=j, out_dtype=BF16, colscale=qscale)
            kr = (na_rpb.shape[2] + 1) // 2
            o = na_attention(qkv, _na_bias_table(na_rpb[j], GRID_W, seq // GRID_W), kr, n_batch, seq, ctx_len,
                             not last)
            if last:
                xs = xs[:n_lat]
            xs = mm(o, na_w_o, w_index=j, mode="resid", resid=xs, gate=g1, full_rows=True)
        elif mix == 1:
            (hf,) = pn(xs, norm_g[i, 0], 1.0 + sc1, sh1, out_dtypes=(F32,))
            wts = _s5_weights(s5_lam_re[j], s5_lam_im[j], s5_log_dt[j], s5_b_re[j], s5_b_im[j],
                              s5_c_re[j], s5_c_im[j], S5_CHUNK)
            y = s5_core(hf, wts, s5_d[j], n_batch, seq, ctx_len)
            if last:
                xs, y = xs[:n_lat], y[:n_lat]
            xs = mm(y, s5_w_glu, w_index=j, mode="glu", resid=xs, gate=g1)
        else:
            (hf,) = pn(xs, norm_g[i, 0], 1.0 + sc1, sh1, out_dtypes=(F32,))
            y = rwkv_mixer_tokens(hf, rw_mu[j], rw_w_rkv[j], rw_w0[j], rw_w1[j], rw_w2[j], rw_a0[j], rw_a1[j],
                                  rw_a2[j], rw_k_k[j], rw_k_a[j], rw_r_k[j], rw_ln_g[j], rw_ln_b[j],
                                  rw_g1[j], rw_g2[j], tm, n_batch, seq, ctx_len)
            if last:
                xs, y = xs[:n_lat], y[:n_lat]
            xs = mm(y, rw_w_o, w_index=j, mode="resid", resid=xs, gate=g1, full_rows=True)
        h2, h2_slabs = pn(xs, norm_g[i, 1], 1.0 + sc2, sh2, out_dtypes=(F32, F32), lane_split=(False, True))
        xs = hier_moe(xs, h2, h2_slabs, moe_wg[i], moe_bg[i], moe_we[i], moe_be[i], moe_w_gu, moe_w_down, i, g2,
                      tm, n_batch, seq)
    ones = jnp.ones((n_batch + 1, 1, d), F32)
    (out,) = prenorm(xs[:n_lat], final_g, ones, jnp.zeros_like(ones), tm, tpb, n_batch, out_dtypes=(x.dtype,))
    return out.reshape(n_batch, seq, d)
```

```python
import functools
import math

import jax
import jax.numpy as jnp
from jax import lax
from jax.experimental import pallas as pl
from jax.experimental.pallas import tpu as pltpu

F32 = jnp.float32
BF16 = jnp.bfloat16

GRID_W = 64
N_MIXERS = 3
NORM_EPS = 1e-6
RW_GN_EPS = 64e-5
MOE_TOP_K = 2
MOE_BLOCK = 128
MOE_COMBINE_ROWS = 128
FULL_ROW_TILE = 256
S5_CHUNK = 16
SCAN_KEY_BLOCK = 16
LANES = 128
RELAYOUT_ROWS = 128
NA_Q_ROWS = 4
NA_KEY_ROWS = 12
NEG_BIG = -1e30
VMEM_LIMIT_BYTES = 48 * 1024 * 1024
VMEM_LIMIT_BIG_BYTES = 56 * 1024 * 1024


def _cparams(sem, limit=VMEM_LIMIT_BYTES):
    return pltpu.CompilerParams(dimension_semantics=sem, vmem_limit_bytes=limit)


def _row_tile(n_lat_per_batch, n_ctx_total, cap=512):
    t = cap
    while t > 8 and (n_lat_per_batch % t or (n_ctx_total and n_ctx_total % t)):
        t //= 2
    return t


def _seg_fn(tiles_per_batch, n_batch):
    def seg(i):
        return jnp.minimum(i // tiles_per_batch, n_batch)
    return seg


def _bdot(a, b):
    return jnp.dot(a.astype(BF16), b.astype(BF16), preferred_element_type=F32)


def _split3(x):
    hi = x.astype(BF16)
    r1 = x - hi.astype(F32)
    mid = r1.astype(BF16)
    lo = (r1 - mid.astype(F32)).astype(BF16)
    return hi, mid, lo


def _prenorm_body(x_ref, g_ref, sc_ref, sh_ref, *o_refs):
    x = x_ref[...]
    ms = jnp.mean(x * x, axis=-1, keepdims=True)
    h = (x * lax.rsqrt(ms + NORM_EPS)) * g_ref[...]
    h = h * sc_ref[0] + sh_ref[0]
    for o_ref in o_refs:
        o_ref[...] = h.astype(o_ref.dtype).reshape(o_ref.shape)


def prenorm(x, g, scale1p, shift, tm, tiles_per_batch, n_batch, out_dtypes, lane_split=None):
    m, d = x.shape
    seg = _seg_fn(tiles_per_batch, n_batch)
    row = pl.BlockSpec((tm, d), lambda i: (i, 0))
    row3 = pl.BlockSpec((tm, d // LANES, LANES), lambda i: (i, 0, 0))
    tab = pl.BlockSpec((1, 1, d), lambda i: (seg(i), 0, 0))
    lane_split = lane_split or (False,) * len(out_dtypes)
    outs = pl.pallas_call(
        _prenorm_body,
        grid=(m // tm,),
        in_specs=[row, pl.BlockSpec((1, d), lambda i: (0, 0)), tab, tab],
        out_specs=[row3 if sp else row for sp in lane_split],
        out_shape=[jax.ShapeDtypeStruct((m, d // LANES, LANES) if sp else (m, d), dt)
                   for dt, sp in zip(out_dtypes, lane_split)],
        compiler_params=_cparams(("parallel",)),
    )(x, g.reshape(1, d), scale1p, shift)
    return outs


def _softplus(z):
    return jnp.maximum(z, 0.0) + jnp.log(1.0 + jnp.exp(-jnp.abs(z)))


def _apply_act(y, act):
    if act is None:
        return y
    if act == "tanh":
        return jnp.tanh(y)
    if act == "sigmoid":
        return jax.nn.sigmoid(y)
    if act == "decay":
        return jnp.exp(-jnp.exp(-_softplus(-y) - 0.5))
    raise ValueError(act)


def _mm_body(*refs, mode, act, has_bias, has_scale, has_mask):
    it = iter(refs)
    x_ref = next(it)
    w_ref = next(it)
    w2_ref = next(it) if mode == "glu" else None
    b_ref = next(it) if has_bias else None
    s_ref = next(it) if has_scale else None
    m_ref = next(it) if has_mask else None
    r_ref = next(it) if mode in ("resid", "glu") else None
    g_ref = next(it) if mode in ("resid", "glu") else None
    o_ref = next(it)
    wb_ref = next(it)
    wb2_ref = next(it) if mode == "glu" else None

    @pl.when(pl.program_id(1) == 0)
    def _():
        wb_ref[...] = w_ref[...].astype(BF16)
        if mode == "glu":
            wb2_ref[...] = w2_ref[...].astype(BF16)

    x = x_ref[...].astype(BF16)
    acc = jnp.dot(x, wb_ref[...], preferred_element_type=F32)
    if mode == "glu":
        acc2 = jnp.dot(x, wb2_ref[...], preferred_element_type=F32)
        acc = acc * jax.nn.sigmoid(acc2)
    if has_bias:
        acc = acc + b_ref[...]
    acc = _apply_act(acc, act)
    if has_scale:
        acc = acc * s_ref[...]
    if has_mask:
        acc = acc * m_ref[...]
    if mode in ("resid", "glu"):
        acc = r_ref[...] + g_ref[0] * acc
    o_ref[...] = acc.astype(o_ref.dtype)


def matmul(x, w, *, tm, tn=512, out_dtype=F32, mode="plain", act=None, bias=None, colscale=None,
           resid=None, gate=None, tiles_per_batch=None, n_batch=None, w_index=0, tilemask=None,
           full_rows=False):
    m, k = x.shape
    if w.ndim == 2:
        w, w_index = w[None], 0
    n = w.shape[2] // 2 if mode == "glu" else w.shape[2]
    if full_rows:
        row_tile = FULL_ROW_TILE if full_rows is True else full_rows
        if tiles_per_batch is not None:
            tiles_per_batch = tiles_per_batch * (tm // row_tile)
        tm, tn = row_tile, n
    tn = min(tn, n)
    assert m % tm == 0 and n % tn == 0, (m, tm, n, tn)
    nj = n // tn
    in_specs = [pl.BlockSpec((tm, k), lambda j, i: (i, 0)),
                pl.BlockSpec((None, k, tn), lambda j, i: (w_index, 0, j))]
    args = [x, w]
    if mode == "glu":
        in_specs.append(pl.BlockSpec((None, k, tn), lambda j, i: (w_index, 0, j + nj)))
        args.append(w)
    col = pl.BlockSpec((1, tn), lambda j, i: (0, j))
    if bias is not None:
        in_specs.append(col)
        args.append(bias.reshape(1, n).astype(F32))
    if colscale is not None:
        in_specs.append(col)
        args.append(colscale.reshape(1, n).astype(F32))
    if tilemask is not None:
        assert tilemask.shape == (tm, tn), (tilemask.shape, tm, tn)
        in_specs.append(pl.BlockSpec((tm, tn), lambda j, i: (0, 0)))
        args.append(tilemask.astype(F32))
    if mode in ("resid", "glu"):
        seg = _seg_fn(tiles_per_batch, n_batch)
        in_specs.append(pl.BlockSpec((tm, tn), lambda j, i: (i, j)))
        in_specs.append(pl.BlockSpec((1, 1, tn), lambda j, i: (seg(i), 0, j)))
        args += [resid, gate]
    return pl.pallas_call(
        functools.partial(_mm_body, mode=mode, act=act, has_bias=bias is not None,
                          has_scale=colscale is not None, has_mask=tilemask is not None),
        grid=(nj, m // tm),
        in_specs=in_specs,
        out_specs=pl.BlockSpec((tm, tn), lambda j, i: (i, j)),
        out_shape=jax.ShapeDtypeStruct((m, n), out_dtype),
        scratch_shapes=[pltpu.VMEM((k, tn), BF16)] * (2 if mode == "glu" else 1),
        compiler_params=_cparams(("parallel", "arbitrary"),
                                 VMEM_LIMIT_BIG_BYTES if full_rows else VMEM_LIMIT_BYTES),
    )(*args)


def _router_body(x_ref, w_ref, b_ref, o_ref, *, n_grp, epg):
    xh, xm, xl = _split3(x_ref[...])
    wh, wm, wl = _split3(w_ref[...])
    dot = functools.partial(jnp.dot, preferred_element_type=F32)
    acc = dot(xh, wh) + (dot(xh, wm) + dot(xm, wh)) + (dot(xh, wl) + dot(xm, wm) + dot(xl, wh))
    acc = acc + b_ref[...]
    lane = lax.broadcasted_iota(jnp.int32, acc.shape, 1)
    lane_f = lane.astype(F32)
    far = 1e9

    def first_max(vals):
        m = jnp.max(vals, axis=-1, keepdims=True)
        return m, jnp.min(jnp.where(vals == m, lane_f, far), axis=-1, keepdims=True)

    gl = jnp.where(lane < n_grp, acc, NEG_BIG)
    gmax, grp = first_max(gl)
    p_grp = 1.0 / jnp.sum(jnp.exp(gl - gmax), axis=-1, keepdims=True)
    lo = n_grp + grp * epg
    el = jnp.where((lane_f >= lo) & (lane_f < lo + epg), acc, NEG_BIG)
    m1, i1 = first_max(el)
    m2, i2 = first_max(jnp.where(lane_f == i1, NEG_BIG, el))
    e21 = jnp.exp(m2 - m1)
    g1 = p_grp / (1.0 + e21)
    g2 = p_grp * e21 / (1.0 + e21)
    out = jnp.where(lane == 0, i1 - n_grp, jnp.where(lane == 1, i2 - n_grp,
                                                    jnp.where(lane == 2, g1, jnp.where(lane == 3, g2, 0.0))))
    o_ref[...] = out


def router(x, w, bias, tm, n_grp, epg):
    m, k = x.shape
    n = w.shape[1]
    return pl.pallas_call(
        functools.partial(_router_body, n_grp=n_grp, epg=epg),
        grid=(m // tm,),
        in_specs=[pl.BlockSpec((tm, k), lambda i: (i, 0)), pl.BlockSpec((k, n), lambda i: (0, 0)),
                  pl.BlockSpec((1, n), lambda i: (0, 0))],
        out_specs=pl.BlockSpec((tm, n), lambda i: (i, 0)),
        out_shape=jax.ShapeDtypeStruct((m, n), F32),
        compiler_params=_cparams(("parallel",)),
        name="moe_router",
    )(x, w, bias.reshape(1, n))


def _mod_body(s_ref, w_ref, b_ref, o_ref):
    s = s_ref[...]
    s = s * jax.nn.sigmoid(s)
    o_ref[0] = _bdot(s, w_ref[0]) + b_ref[0]


def modulation(cvecs, mod_w, mod_b, tn=768):
    depth, d, n = mod_w.shape
    rows = cvecs.shape[0]
    while n % tn:
        tn //= 2
    return pl.pallas_call(
        _mod_body,
        grid=(depth, n // tn),
        in_specs=[pl.BlockSpec((rows, d), lambda l, j: (0, 0)),
                  pl.BlockSpec((1, d, tn), lambda l, j: (l, 0, j)),
                  pl.BlockSpec((1, 1, tn), lambda l, j: (l, 0, j))],
        out_specs=pl.BlockSpec((1, rows, tn), lambda l, j: (l, 0, j)),
        out_shape=jax.ShapeDtypeStruct((depth, rows, n), F32),
        compiler_params=_cparams(("parallel", "parallel")),
    )(cvecs, mod_w, mod_b.reshape(depth, 1, n))


def _na_key_start(i, rows, kr):
    return jnp.clip(i * NA_Q_ROWS - kr // 2, 0, rows - NA_KEY_ROWS)


def _na_bias_table(rpb, grid_w, rows):
    n_heads = rpb.shape[0]
    kr = (rpb.shape[1] + 1) // 2
    kcw = (rpb.shape[2] + 1) // 2
    n_blk = rows // NA_Q_ROWS
    q = jnp.arange(grid_w)[:, None]
    kc = jnp.arange(grid_w)[None, :]
    win = jnp.clip(q - kcw // 2, 0, grid_w - kcw)
    col_ok = (kc >= win) & (kc < win + kcw)
    dc = jnp.clip(kc - q, 1 - kcw, kcw - 1) + kcw - 1
    blk = jnp.array([0, min(1, n_blk - 1), n_blk - 1])[:, None, None]
    r = blk * NA_Q_ROWS + jnp.arange(NA_Q_ROWS)[None, :, None]
    krow = _na_key_start(blk, rows, kr) + jnp.arange(NA_KEY_ROWS)[None, None, :]
    rs = jnp.clip(r - kr // 2, 0, rows - kr)
    row_ok = (krow >= rs) & (krow < rs + kr)
    dr = jnp.clip(krow - r + kr - 1, 0, 2 * kr - 2)
    hi = lax.Precision.HIGHEST
    sel_r = jax.nn.one_hot(dr.reshape(-1), rpb.shape[1], dtype=F32)
    sel_c = jax.nn.one_hot(dc.reshape(-1), rpb.shape[2], dtype=F32)
    tab = jnp.einsum("ra,has->hrs", sel_r, jnp.einsum("hab,sb->has", rpb.astype(F32), sel_c, precision=hi),
                     precision=hi)
    tab = tab.reshape(n_heads, 3, NA_Q_ROWS, NA_KEY_ROWS, grid_w, grid_w)
    ok = row_ok[None, :, :, :, None, None] & col_ok[None, None, None, None]
    tab = jnp.where(ok, tab, NEG_BIG)
    return tab.transpose(0, 1, 2, 4, 3, 5).reshape(n_heads, 3, NA_Q_ROWS * grid_w, NA_KEY_ROWS * grid_w)


def _na_body(q_ref, k_ref, v_ref, kc_ref, vc_ref, bias_ref, o_ref, *, hpb, dh, grid_w, kr, rows):
    ks = _na_key_start(pl.program_id(2), rows, kr)
    start = pl.multiple_of(ks * grid_w, grid_w)
    nt = (((1,), (1,)), ((), ()))
    for h in range(hpb):
        sl = slice(h * dh, (h + 1) * dh)
        q = q_ref[:, sl]
        k = k_ref[pl.ds(start, NA_KEY_ROWS * grid_w), sl]
        v = v_ref[pl.ds(start, NA_KEY_ROWS * grid_w), sl]
        s = lax.dot_general(q, k, nt, preferred_element_type=F32) + bias_ref[h, 0]
        sc = lax.dot_general(q, kc_ref[:, sl], nt, preferred_element_type=F32)
        m = jnp.maximum(jnp.max(s, axis=-1, keepdims=True), jnp.max(sc, axis=-1, keepdims=True))
        p = jnp.exp(s - m)
        pc = jnp.exp(sc - m)
        den = jnp.sum(p, axis=-1, keepdims=True) + jnp.sum(pc, axis=-1, keepdims=True)
        o = (jnp.dot(p.astype(BF16), v, preferred_element_type=F32)
             + jnp.dot(pc.astype(BF16), vc_ref[:, sl], preferred_element_type=F32))
        o_ref[:, sl] = (o / den).astype(o_ref.dtype)


def _ctx_attn_body(q_ref, k_ref, v_ref, o_ref, *, hpb, dh):
    nt = (((1,), (1,)), ((), ()))
    for h in range(hpb):
        sl = slice(h * dh, (h + 1) * dh)
        s = lax.dot_general(q_ref[:, sl], k_ref[:, sl], nt, preferred_element_type=F32)
        m = jnp.max(s, axis=-1, keepdims=True)
        p = jnp.exp(s - m)
        den = jnp.sum(p, axis=-1, keepdims=True)
        o = jnp.dot(p.astype(BF16), v_ref[:, sl], preferred_element_type=F32)
        o_ref[:, sl] = (o / den).astype(o_ref.dtype)


def na_attention(qkv, bias_tab, kr, n_batch, seq, ctx_len, need_ctx):
    d = qkv.shape[1] // 3
    n_heads = bias_tab.shape[0]
    dh = d // n_heads
    hpb = max(1, min(n_heads, 256 // dh))
    bw = hpb * dh
    n_d = d // bw
    rows = seq // GRID_W
    n_blk = rows // NA_Q_ROWS
    qb = NA_Q_ROWS * GRID_W
    n_lat = n_batch * seq
    ctx_blk0 = n_lat // ctx_len

    def cls_of(i):
        return jnp.where(i == 0, 0, jnp.where(i == n_blk - 1, 2, 1))

    o_lat = pl.pallas_call(
        functools.partial(_na_body, hpb=hpb, dh=dh, grid_w=GRID_W, kr=kr, rows=rows),
        grid=(n_batch, n_d, n_blk),
        in_specs=[
            pl.BlockSpec((qb, bw), lambda b, g, i: (b * n_blk + i, g)),
            pl.BlockSpec((seq, bw), lambda b, g, i: (b, n_d + g)),
            pl.BlockSpec((seq, bw), lambda b, g, i: (b, 2 * n_d + g)),
            pl.BlockSpec((ctx_len, bw), lambda b, g, i: (ctx_blk0 + b, n_d + g)),
            pl.BlockSpec((ctx_len, bw), lambda b, g, i: (ctx_blk0 + b, 2 * n_d + g)),
            pl.BlockSpec((hpb, 1, qb, NA_KEY_ROWS * GRID_W), lambda b, g, i: (g, cls_of(i), 0, 0)),
        ],
        out_specs=pl.BlockSpec((qb, bw), lambda b, g, i: (b * n_blk + i, g)),
        out_shape=jax.ShapeDtypeStruct((n_lat, d), BF16),
        compiler_params=_cparams(("parallel", "parallel", "arbitrary")),
        name="na_attention",
    )(qkv, qkv, qkv, qkv, qkv, bias_tab)
    if not need_ctx:
        return o_lat
    o_ctx = pl.pallas_call(
        functools.partial(_ctx_attn_body, hpb=hpb, dh=dh),
        grid=(n_batch, n_d),
        in_specs=[
            pl.BlockSpec((ctx_len, bw), lambda b, g: (ctx_blk0 + b, g)),
            pl.BlockSpec((ctx_len, bw), lambda b, g: (ctx_blk0 + b, n_d + g)),
            pl.BlockSpec((ctx_len, bw), lambda b, g: (ctx_blk0 + b, 2 * n_d + g)),
        ],
        out_specs=pl.BlockSpec((ctx_len, bw), lambda b, g: (b, g)),
        out_shape=jax.ShapeDtypeStruct((n_batch * ctx_len, d), BF16),
        compiler_params=_cparams(("parallel", "parallel")),
    )(qkv, qkv, qkv)
    return jnp.concatenate([o_lat, o_ctx], axis=0)


def _expert_body(be_ref, idx_ref, idx_next_ref, x_hbm, wgu_ref, wd_ref, o_ref, xbuf, sem, wgu_s, wd_s, *,
                 hidden, n_blocks, n_slab):
    j = pl.program_id(0)
    cur = j % 2

    def row_copy(idx, buf, r):
        return pltpu.make_async_copy(x_hbm.at[pl.ds(idx[0, 0, r], 1)], xbuf.at[buf, pl.ds(r, 1)], sem.at[buf])

    @pl.when(j == 0)
    def _():
        for r in range(MOE_BLOCK):
            row_copy(idx_ref, 0, r).start(priority=r % 2)

    prev = be_ref[jnp.maximum(j - 1, 0)]
    changed = jnp.logical_or(j == 0, be_ref[j] != prev)

    @pl.when(changed)
    def _():
        wgu_s[...] = wgu_ref[...].astype(BF16)
        wd_s[...] = wd_ref[...].astype(BF16)

    for r in range(MOE_BLOCK):
        row_copy(idx_ref, cur, r).wait()

    for r in range(MOE_BLOCK):
        row_copy(idx_next_ref, 1 - cur, r).start(priority=r % 2)
    xb = xbuf.at[cur]
    gu = None
    for c in range(0, n_slab, 2):
        xa = jnp.concatenate([xb[:, c, :], xb[:, c + 1, :]], axis=-1).astype(BF16)
        part = jnp.dot(xa, wgu_s[c * LANES:(c + 2) * LANES, :], preferred_element_type=F32)
        gu = part if gu is None else gu + part
    g = gu[:, :hidden]
    u = gu[:, hidden:]
    a = (g * jax.nn.sigmoid(g) * u).astype(BF16)
    o_ref[...] = jnp.dot(a, wd_s[...], preferred_element_type=F32)

    @pl.when(j == n_blocks - 1)
    def _():
        for r in range(MOE_BLOCK):
            row_copy(idx_next_ref, 1 - cur, r).wait()


def expert_blocks(x, slot_tok, blk_e, w_gu, w_down, layer):
    n_slab = x.shape[1]
    d = n_slab * LANES
    hidden = w_down.shape[2]
    n_blocks = blk_e.shape[0]
    idx = slot_tok.reshape(n_blocks, 1, MOE_BLOCK)
    return pl.pallas_call(
        functools.partial(_expert_body, hidden=hidden, n_blocks=n_blocks, n_slab=n_slab),
        grid_spec=pltpu.PrefetchScalarGridSpec(
            num_scalar_prefetch=1,
            grid=(n_blocks,),
            in_specs=[pl.BlockSpec((1, 1, MOE_BLOCK), lambda j, be: (j, 0, 0), memory_space=pltpu.SMEM),
                      pl.BlockSpec((1, 1, MOE_BLOCK), lambda j, be: (jnp.minimum(j + 1, n_blocks - 1), 0, 0),
                                   memory_space=pltpu.SMEM),
                      pl.BlockSpec(memory_space=pl.ANY),
                      pl.BlockSpec((None, None, d, 2 * hidden), lambda j, be: (layer, be[j], 0, 0)),
                      pl.BlockSpec((None, None, hidden, d), lambda j, be: (layer, be[j], 0, 0))],
            out_specs=pl.BlockSpec((MOE_BLOCK, d), lambda j, be: (j, 0)),
            scratch_shapes=[pltpu.VMEM((2, MOE_BLOCK, n_slab, LANES), F32), pltpu.SemaphoreType.DMA((2,)),
                            pltpu.VMEM((d, 2 * hidden), BF16), pltpu.VMEM((hidden, d), BF16)]),
        out_shape=jax.ShapeDtypeStruct((n_blocks * MOE_BLOCK, d), F32),
        compiler_params=_cparams(("arbitrary",), VMEM_LIMIT_BIG_BYTES),
        name="moe_experts",
    )(blk_e, idx, idx, x, w_gu, w_down)


def _combine_body(idx_ref, idx_next_ref, route_ref, x_ref, g_ref, yb_hbm, o_ref, ybuf, sem, *, tmc, n_tiles):
    i = pl.program_id(0)
    cur = i % 2

    def row_copy(idx, buf, k, r):
        return pltpu.make_async_copy(yb_hbm.at[pl.ds(idx[0, 0, k * tmc + r], 1)],
                                     ybuf.at[buf, k, pl.ds(r, 1)], sem.at[buf])

    def start_all(idx, buf):
        for k in range(MOE_TOP_K):
            for r in range(tmc):
                row_copy(idx, buf, k, r).start(priority=r % 2)

    @pl.when(i == 0)
    def _():
        start_all(idx_ref, 0)

    @pl.when(i + 1 < n_tiles)
    def _():
        start_all(idx_next_ref, 1 - cur)

    for k in range(MOE_TOP_K):
        for r in range(tmc):
            row_copy(idx_ref, cur, k, r).wait()
    route = route_ref[...]
    f = ybuf[cur, 0] * route[:, MOE_TOP_K:MOE_TOP_K + 1]
    for k in range(1, MOE_TOP_K):
        f = f + ybuf[cur, k] * route[:, MOE_TOP_K + k:MOE_TOP_K + k + 1]
    o_ref[...] = x_ref[...] + g_ref[0] * f


def moe_combine(xs, yb, slot_of_asg, route, gate_tab, n_batch, seq):
    n_tok, d = xs.shape
    tmc = MOE_COMBINE_ROWS
    n_tiles = n_tok // tmc
    idx = slot_of_asg.reshape(n_tiles, tmc, MOE_TOP_K).transpose(0, 2, 1).reshape(n_tiles, 1, MOE_TOP_K * tmc)
    seg = _seg_fn(seq // tmc, n_batch)
    return pl.pallas_call(
        functools.partial(_combine_body, tmc=tmc, n_tiles=n_tiles),
        grid=(n_tiles,),
        in_specs=[pl.BlockSpec((1, 1, MOE_TOP_K * tmc), lambda i: (i, 0, 0), memory_space=pltpu.SMEM),
                  pl.BlockSpec((1, 1, MOE_TOP_K * tmc), lambda i: (jnp.minimum(i + 1, n_tiles - 1), 0, 0),
                               memory_space=pltpu.SMEM),
                  pl.BlockSpec((tmc, route.shape[1]), lambda i: (i, 0)),
                  pl.BlockSpec((tmc, d), lambda i: (i, 0)),
                  pl.BlockSpec((1, 1, d), lambda i: (seg(i), 0, 0)),
                  pl.BlockSpec(memory_space=pl.ANY)],
        out_specs=pl.BlockSpec((tmc, d), lambda i: (i, 0)),
        out_shape=jax.ShapeDtypeStruct((n_tok, d), F32),
        scratch_shapes=[pltpu.VMEM((2, MOE_TOP_K, tmc, d), F32), pltpu.SemaphoreType.DMA((2,))],
        compiler_params=_cparams(("arbitrary",)),
        name="moe_combine",
    )(idx, idx, route, xs, gate_tab, yb)


def hier_moe(xs, h, h_slabs, wg, bg, we, be, w_gu, w_down, layer, gate_tab, tm, n_batch, seq):
    n_tok, d = h.shape
    n_grp = wg.shape[1]
    n_exp = we.shape[1]
    epg = n_exp // n_grp
    n_logit = n_grp + n_exp
    n_pad = -(-n_logit // 128) * 128
    wcat = jnp.pad(jnp.concatenate([wg, we], axis=1), ((0, 0), (0, n_pad - n_logit)))
    bcat = jnp.pad(jnp.concatenate([bg, be], axis=0), (0, n_pad - n_logit))
    route = router(h, wcat, bcat, tm, n_grp, epg)

    flat_e = route[:, :MOE_TOP_K].astype(jnp.int32).reshape(-1)
    n_asg = flat_e.shape[0]
    onehot = (flat_e[:, None] == jnp.arange(n_exp, dtype=jnp.int32)[None, :]).astype(F32)
    ck = 128
    oh3 = onehot.reshape(n_asg // ck, ck, n_exp)
    tri = jnp.tril(jnp.ones((ck, ck), F32))
    within = jnp.einsum("ij,cjk->cik", tri, oh3, precision=lax.Precision.HIGHEST)
    tot = within[:, -1, :]
    before = jnp.cumsum(tot, axis=0) - tot
    rank = jnp.sum(oh3 * (within + before[:, None, :]), axis=-1).reshape(n_asg).astype(jnp.int32) - 1
    counts = jnp.sum(tot, axis=0).astype(jnp.int32)
    padded = (counts + MOE_BLOCK - 1) // MOE_BLOCK * MOE_BLOCK
    pad_end = jnp.cumsum(padded)
    pad_start = pad_end - padded
    slot = jnp.sum(onehot * pad_start.astype(F32)[None, :], axis=-1).astype(jnp.int32) + rank
    n_blocks = -(-n_asg // MOE_BLOCK) + n_exp
    slot_tok = jnp.zeros((n_blocks * MOE_BLOCK,), jnp.int32).at[slot].set(
        jnp.arange(n_asg, dtype=jnp.int32) // MOE_TOP_K)
    blk_start = jnp.arange(n_blocks, dtype=jnp.int32) * MOE_BLOCK
    blk_e = jnp.minimum(jnp.sum((pad_end[None, :] <= blk_start[:, None]).astype(jnp.int32), axis=1), n_exp - 1)
    yb = expert_blocks(h_slabs, slot_tok, blk_e, w_gu, w_down, layer)
    return moe_combine(xs, yb, slot.reshape(n_tok, MOE_TOP_K), route, gate_tab, n_batch, seq)


def _s5_weights(lam_re, lam_im, log_dt, b_re, b_im, c_re, c_im, t_chunk):
    n_grp, n_state = lam_re.shape[1], lam_re.shape[2]
    cg = b_re.shape[-1]
    lam = lax.complex(lam_re.astype(F32), lam_im.astype(F32))
    dt = jnp.exp(log_dt.astype(F32))[..., None]
    lam_bar = jnp.exp(lam * dt)
    b_bar = ((lam_bar - 1.0) / lam)[..., None] * lax.complex(b_re.astype(F32), b_im.astype(F32))
    c_mat = lax.complex(c_re.astype(F32), c_im.astype(F32))
    ks = jnp.arange(t_chunk + 1, dtype=F32)
    pw = jnp.exp((lam * dt)[..., None] * ks)
    hi = lax.Precision.HIGHEST
    kern = jnp.einsum("dgcp,dgpk,dgpi->dgkci", c_mat, pw[..., :t_chunk], b_bar, precision=hi).real
    t_idx = jnp.arange(t_chunk)
    lag = t_idx[None, :] - t_idx[:, None]
    k_f = kern[0][:, jnp.clip(lag, 0, t_chunk - 1)]
    k_r = kern[1][:, jnp.clip(-lag, 0, t_chunk - 1)]
    toep = (jnp.where((lag >= 0)[None, :, :, None, None], k_f, 0.0)
            + jnp.where((lag <= 0)[None, :, :, None, None], k_r, 0.0))
    toep = toep.transpose(0, 1, 4, 2, 3).reshape(n_grp, t_chunk * cg, t_chunk * cg)
    pf = pw[0][..., :t_chunk][..., ::-1]
    pr = pw[1][..., :t_chunk]
    bend_f = jnp.einsum("gps,gpc->gscp", pf, b_bar[0]).reshape(n_grp, t_chunk * cg, n_state)
    bend_r = jnp.einsum("gps,gpc->gscp", pr, b_bar[1]).reshape(n_grp, t_chunk * cg, n_state)
    bend = jnp.stack([bend_f.real, bend_f.imag, bend_r.real, bend_r.imag], axis=1)
    of = jnp.einsum("gcp,gpt->gptc", c_mat[0], pw[0][..., 1:]).reshape(n_grp, n_state, t_chunk * cg)
    orv = jnp.einsum("gcp,gpt->gptc", c_mat[1], pw[1][..., 1:][..., ::-1]).reshape(n_grp, n_state, t_chunk * cg)
    cout = jnp.stack([of.real, -of.imag, orv.real, -orv.imag], axis=1)

    gsz = LANES // cg
    nb = n_grp // gsz
    tile = 512

    def expand(n_outer, n_inner):
        dst = jnp.arange(n_outer * gsz * n_inner)
        src = (dst // (gsz * n_inner)) * n_inner + dst % n_inner
        return (jnp.arange(n_outer * n_inner)[:, None] == src[None, :]).astype(F32)

    def grp(size, period, width):
        return (jnp.arange(size) % period) // width

    def own_group(row_grp, col_grp):
        return (row_grp[:, None] == col_grp[None, :]).astype(F32)

    e_tc, e_xp = expand(t_chunk, cg), expand(4, n_state)
    wide_ch, wide_st = e_tc.shape[1], e_xp.shape[1]
    kin = t_chunk * cg
    expand_mm = functools.partial(matmul, tm=tile, out_dtype=BF16, full_rows=tile)
    xt = toep.reshape(nb, gsz, t_chunk, cg, kin).transpose(0, 2, 1, 3, 4).reshape(n_grp * kin, kin)
    wt = expand_mm(xt, e_tc, tilemask=own_group(grp(tile, LANES, cg), grp(wide_ch, LANES, cg)))
    wt = wt.reshape(nb, t_chunk, LANES, t_chunk * LANES)
    xb = bend.reshape(nb, gsz, 4, t_chunk, cg, n_state).transpose(0, 3, 1, 4, 2, 5).reshape(n_grp * kin, 4 * n_state)
    wb = expand_mm(xb, e_xp, tilemask=own_group(grp(tile, LANES, cg), grp(wide_st, gsz * n_state, n_state)))
    wb = wb.reshape(nb, t_chunk, LANES, 4 * gsz * n_state)
    xc = cout.reshape(nb, gsz, 4, n_state, kin).transpose(0, 2, 1, 3, 4).reshape(4 * n_grp * n_state, kin)
    wc = expand_mm(xc, e_tc, tilemask=own_group(grp(tile, gsz * n_state, n_state), grp(wide_ch, LANES, cg)))
    wc = wc.reshape(nb, 4 * gsz * n_state, t_chunk * LANES)
    lam_t = pw[..., t_chunk]
    lam_t4 = jnp.stack([lam_t[0].real, lam_t[0].imag, lam_t[1].real, lam_t[1].imag]).reshape(4, n_grp * n_state)
    return wt, wb, wc, lam_t4


def _s5_a_body(h_ref, wb_ref, e_ref, *, t_chunk):
    acc = jnp.dot(h_ref[:, 0, :].astype(BF16), wb_ref[0], preferred_element_type=F32)
    for s in range(1, t_chunk):
        acc = acc + jnp.dot(h_ref[:, s, :].astype(BF16), wb_ref[s], preferred_element_type=F32)
    e_ref[...] = acc


def _s5_b_body(e_ref, lam_ref, x_ref, *, n_batch, n_lat_chunk, n_ctx_chunk, width):
    ctx0 = n_batch * n_lat_chunk
    n_chunk = n_lat_chunk + n_ctx_chunk

    def comp(x):
        return slice(x * width, (x + 1) * width)

    lfr, lfi, lrr, lri = (lam_ref[:, comp(x)] for x in range(4))

    for b in range(n_batch):
        def fwd(n, carry):
            sr, si = carry
            row = jnp.where(n < n_ctx_chunk, ctx0 + b * n_ctx_chunk + n, b * n_lat_chunk + n - n_ctx_chunk)
            x_ref[pl.ds(row, 1), comp(0)] = sr
            x_ref[pl.ds(row, 1), comp(1)] = si
            er = e_ref[pl.ds(row, 1), comp(0)]
            ei = e_ref[pl.ds(row, 1), comp(1)]
            return lfr * sr - lfi * si + er, lfr * si + lfi * sr + ei

        def rev(n, carry):
            sr, si = carry
            row = jnp.where(n < n_ctx_chunk, ctx0 + b * n_ctx_chunk + n_ctx_chunk - 1 - n,
                            b * n_lat_chunk + n_chunk - 1 - n)
            x_ref[pl.ds(row, 1), comp(2)] = sr
            x_ref[pl.ds(row, 1), comp(3)] = si
            er = e_ref[pl.ds(row, 1), comp(2)]
            ei = e_ref[pl.ds(row, 1), comp(3)]
            return lrr * sr - lri * si + er, lrr * si + lri * sr + ei

        z = jnp.zeros((1, width), F32)
        lax.fori_loop(0, n_chunk, fwd, (z, z))
        lax.fori_loop(0, n_chunk, rev, (z, z))


def _gelu(y):
    return 0.5 * y * (1.0 + jnp.tanh(0.7978845608028654 * (y + 0.044715 * (y * y * y))))


def _s5_c_body(h_ref, x_ref, wt_ref, wc_ref, dsk_ref, o_ref, *, t_chunk):
    acc = jnp.dot(x_ref[...].astype(BF16), wc_ref[...], preferred_element_type=F32)
    for s in range(t_chunk):
        acc = acc + jnp.dot(h_ref[:, s, :].astype(BF16), wt_ref[s], preferred_element_type=F32)
    dsk = dsk_ref[...]
    for t in range(t_chunk):
        o_ref[:, t, :] = _gelu(acc[:, t * LANES:(t + 1) * LANES] + h_ref[:, t, :] * dsk)


def s5_core(h_f32, weights, d_skip, n_batch, seq, ctx_len):
    wt, wb, wc, lam_t4 = weights
    t = S5_CHUNK
    m, d = h_f32.shape
    nb = wt.shape[0]
    sw = wb.shape[-1]
    width = sw // 4
    rows = m // t
    rbs = rows // 2 if rows % 16 == 0 else rows
    h3 = h_f32.reshape(rows, t, d)
    hblk = pl.BlockSpec((rbs, t, LANES), lambda g, i: (i, 0, g))
    sblk = pl.BlockSpec((rbs, sw), lambda g, i: (i, g))
    e = pl.pallas_call(
        functools.partial(_s5_a_body, t_chunk=t),
        grid=(nb, rows // rbs),
        in_specs=[hblk, pl.BlockSpec((None, t, LANES, sw), lambda g, i: (g, 0, 0, 0))],
        out_specs=sblk,
        out_shape=jax.ShapeDtypeStruct((rows, nb * sw), F32),
        compiler_params=_cparams(("parallel", "arbitrary")),
        name="s5_chunk_states",
    )(h3, wb)

    lam_blk = lam_t4.reshape(4, nb, width).transpose(1, 0, 2).reshape(nb, 1, sw)
    x_in = pl.pallas_call(
        functools.partial(_s5_b_body, n_batch=n_batch, n_lat_chunk=seq // t, n_ctx_chunk=ctx_len // t,
                          width=width),
        grid=(nb,),
        in_specs=[pl.BlockSpec((rows, sw), lambda g: (0, g)), pl.BlockSpec((None, 1, sw), lambda g: (g, 0, 0))],
        out_specs=pl.BlockSpec((rows, sw), lambda g: (0, g)),
        out_shape=jax.ShapeDtypeStruct((rows, nb * sw), F32),
        compiler_params=_cparams(("parallel",)),
        name="s5_chunk_recurrence",
    )(e, lam_blk)

    y3 = pl.pallas_call(
        functools.partial(_s5_c_body, t_chunk=t),
        grid=(nb, rows // rbs),
        in_specs=[hblk, sblk,
                  pl.BlockSpec((None, t, LANES, t * LANES), lambda g, i: (g, 0, 0, 0)),
                  pl.BlockSpec((None, sw, t * LANES), lambda g, i: (g, 0, 0)),
                  pl.BlockSpec((1, LANES), lambda g, i: (0, g))],
        out_specs=hblk,
        out_shape=jax.ShapeDtypeStruct((rows, t, d), F32),
        compiler_params=_cparams(("parallel", "arbitrary"), VMEM_LIMIT_BIG_BYTES),
        name="s5_outputs",
    )(h3, x_in, wt, wc, d_skip.astype(F32).reshape(1, d))
    return y3.reshape(m, d)


def _shift_body(h_ref, hp_ref, hn_ref, mu_ref, *o_refs, tm, tiles_lat, tiles_ctx, n_lat_tiles):
    i = pl.program_id(0)
    h = h_ref[...]
    first = jnp.where(i < n_lat_tiles, i % tiles_lat == 0, (i - n_lat_tiles) % tiles_ctx == 0)
    last = jnp.where(i < n_lat_tiles, i % tiles_lat == tiles_lat - 1,
                     (i - n_lat_tiles) % tiles_ctx == tiles_ctx - 1)
    rid = lax.broadcasted_iota(jnp.int32, h.shape, 0)
    prev_edge = jnp.where(first, 0.0, hp_ref[7:8, :])
    next_edge = jnp.where(last, 0.0, hn_ref[0:1, :])
    prev = jnp.where(rid == 0, prev_edge, pltpu.roll(h, 1, 0))
    nxt = jnp.where(rid == tm - 1, next_edge, pltpu.roll(h, tm - 1, 0))
    xx = 0.5 * (prev + nxt) - h
    for m, o_ref in enumerate(o_refs):
        o_ref[...] = (h + xx * mu_ref[m:m + 1, :]).astype(o_ref.dtype)


def token_shift_mix(h, mu, tm, seq, ctx_len, n_batch):
    m, d = h.shape
    n_mix = mu.shape[0]
    n_lat_tiles = n_batch * seq // tm
    nb8 = m // 8
    per = tm // 8
    row = pl.BlockSpec((tm, d), lambda i: (i, 0))
    return pl.pallas_call(
        functools.partial(_shift_body, tm=tm, tiles_lat=seq // tm, tiles_ctx=max(ctx_len // tm, 1),
                          n_lat_tiles=n_lat_tiles),
        grid=(m // tm,),
        in_specs=[row,
                  pl.BlockSpec((8, d), lambda i: (jnp.maximum(i * per - 1, 0), 0)),
                  pl.BlockSpec((8, d), lambda i: (jnp.minimum((i + 1) * per, nb8 - 1), 0)),
                  pl.BlockSpec((n_mix, d), lambda i: (0, 0))],
        out_specs=[row] * n_mix,
        out_shape=[jax.ShapeDtypeStruct((m, d), BF16)] * n_mix,
        compiler_params=_cparams(("parallel",)),
    )(h, h, h, mu.astype(F32))


def _rwkv_scan_body(rf_ref, rr_ref, kf_ref, kr_ref, vf_ref, vr_ref, wf_ref, wr_ref, af_ref, ar_ref,
                    kk_ref, ka_ref, of_ref, or_ref, s_ref, op_ref, *, tb, n, bh):
    @pl.when(pl.program_id(0) == 0)
    def _():
        s_ref[...] = jnp.zeros_like(s_ref)

    k_k = kk_ref[...]
    k_a = ka_ref[...]

    def step(t, carry):
        tr = tb - 1 - t
        r = jnp.concatenate([rf_ref[t], rr_ref[tr]], axis=-1)
        k = jnp.concatenate([kf_ref[t], kr_ref[tr]], axis=-1)
        v = jnp.concatenate([vf_ref[t], vr_ref[tr]], axis=-1)
        w = jnp.concatenate([wf_ref[t], wr_ref[tr]], axis=-1)
        a = jnp.concatenate([af_ref[t], ar_ref[tr]], axis=-1)
        kk = k * k_k
        nrm = jnp.sqrt(jnp.sum(kk * kk, axis=0, keepdims=True))
        kk = kk / jnp.maximum(nrm, 1e-12)
        nkk = -kk
        akk = kk * a
        kd = k * (1.0 + (a - 1.0) * k_a)
        op_ref[0] = nkk
        op_ref[1] = w * r
        op_ref[2] = w
        op_ref[3] = akk
        op_ref[4] = kd
        c_a = jnp.sum(akk * r, axis=0, keepdims=True)
        c_k = jnp.sum(kd * r, axis=0, keepdims=True)
        def reduce_keys(jb, acc):
            sa, so = acc
            for jj in range(SCAN_KEY_BLOCK):
                j = jb * SCAN_KEY_BLOCK + jj
                s_j = s_ref[j]
                sa = sa + s_j * op_ref[0, pl.ds(j, 1), :]
                so = so + s_j * op_ref[1, pl.ds(j, 1), :]
            return sa, so

        zero = jnp.zeros_like(v)
        sa, so = lax.fori_loop(0, n // SCAN_KEY_BLOCK, reduce_keys, (zero, zero))
        o = so + sa * c_a + v * c_k
        of_ref[t] = o[:, :bh]
        or_ref[tr] = o[:, bh:]

        def update_keys(jb, c):
            for jj in range(SCAN_KEY_BLOCK):
                j = jb * SCAN_KEY_BLOCK + jj
                s_ref[j] = (s_ref[j] * op_ref[2, pl.ds(j, 1), :] + sa * op_ref[3, pl.ds(j, 1), :]
                            + v * op_ref[4, pl.ds(j, 1), :])
            return c

        lax.fori_loop(0, n // SCAN_KEY_BLOCK, update_keys, 0)
        return carry

    lax.fori_loop(0, tb, step, 0)


def rwkv_scan(r, k, v, w0, w1, a0, a1, k_k, k_a, ctx_len, tb=32):
    steps, n, bh = r.shape
    while steps % tb or ctx_len % tb:
        tb //= 2
    nblk, nblk_ctx = steps // tb, ctx_len // tb

    def rev_blk(i):
        return jnp.where(i < nblk_ctx, nblk_ctx - 1 - i, nblk + nblk_ctx - 1 - i)

    fwd = pl.BlockSpec((tb, n, bh), lambda i: (i, 0, 0))
    rev = pl.BlockSpec((tb, n, bh), lambda i: (rev_blk(i), 0, 0))
    tab = pl.BlockSpec((n, 2 * bh), lambda i: (0, 0))
    out = jax.ShapeDtypeStruct((steps, n, bh), F32)
    return pl.pallas_call(
        functools.partial(_rwkv_scan_body, tb=tb, n=n, bh=bh),
        grid=(nblk,),
        in_specs=[fwd, rev, fwd, rev, fwd, rev, fwd, rev, fwd, rev, tab, tab],
        out_specs=[fwd, rev],
        out_shape=[out, out],
        scratch_shapes=[pltpu.VMEM((n, n, 2 * bh), F32), pltpu.VMEM((5, n, 2 * bh), F32)],
        compiler_params=_cparams(("arbitrary",)),
        name="rwkv_scan",
    )(r, r, k, k, v, v, w0, w1, a0, a1, k_k, k_a)


def _rwkv_post_body(of_ref, or_ref, r_ref, k_ref, v_ref, a0_ref, a1_ref, g_ref, ka_ref, rk_ref, lg_ref, lb_ref,
                    y_ref):
    o = of_ref[...] + or_ref[...]
    mean = jnp.mean(o, axis=1, keepdims=True)
    var = jnp.mean(jnp.square(o - mean), axis=1, keepdims=True)
    on = (o - mean) * lax.rsqrt(var + RW_GN_EPS) * lg_ref[...] + lb_ref[...]
    r = r_ref[...]
    k = k_ref[...]
    k_a = ka_ref[...]
    rk = rk_ref[...]
    kd0 = k * (1.0 + (a0_ref[...] - 1.0) * k_a)
    kd1 = k * (1.0 + (a1_ref[...] - 1.0) * k_a)
    bonus = (jnp.sum(r * kd0 * rk, axis=1, keepdims=True)
             + jnp.sum(r * kd1 * rk, axis=1, keepdims=True)) * v_ref[...]
    y = (on + bonus)
    y_ref[...] = (y * g_ref[...]).astype(y_ref.dtype)


def rwkv_post(o_f, o_r, r, k, v, a0, a1, gate, k_a, r_k, ln_g, ln_b, tb=32):
    steps, n, ch = o_f.shape
    while steps % tb:
        tb //= 2
    blk = pl.BlockSpec((tb, n, ch), lambda i: (i, 0, 0))
    tab = pl.BlockSpec((1, n, ch), lambda i: (0, 0, 0))
    return pl.pallas_call(
        _rwkv_post_body,
        grid=(steps // tb,),
        in_specs=[blk] * 8 + [tab] * 4,
        out_specs=blk,
        out_shape=jax.ShapeDtypeStruct((steps, n, ch), F32),
        compiler_params=_cparams(("parallel",)),
        name="rwkv_post",
    )(o_f, o_r, r, k, v, a0, a1, gate, k_a[None], r_k[None], ln_g[None], ln_b[None])


def _step_block_row(s, b, n_ctx_blk, n_lat_blk, n_batch):
    return jnp.where(s < n_ctx_blk, n_batch * n_lat_blk + b * n_ctx_blk + s, b * n_lat_blk + s - n_ctx_blk)


def _to_steps_body(*refs, n_heads):
    x_refs, o_ref = refs[:-1], refs[-1]
    parts = []
    for x_ref in x_refs:
        x = x_ref[...]
        t = x.shape[0]
        parts.append(jnp.swapaxes(x.reshape(t, n_heads, x.shape[1] // n_heads), 1, 2))
    o_ref[...] = jnp.concatenate(parts, axis=-1)


def tokens_to_steps(x, n_batch, seq, ctx_len, n_heads):
    m, d = x.shape
    n = d // n_heads
    t = RELAYOUT_ROWS
    n_ctx_blk, n_lat_blk = ctx_len // t, seq // t

    def spec(b):
        return pl.BlockSpec((t, d), lambda s: (_step_block_row(s, b, n_ctx_blk, n_lat_blk, n_batch), 0))

    return pl.pallas_call(
        functools.partial(_to_steps_body, n_heads=n_heads),
        grid=(n_ctx_blk + n_lat_blk,),
        in_specs=[spec(b) for b in range(n_batch)],
        out_specs=pl.BlockSpec((t, n, n_batch * n_heads), lambda s: (s, 0, 0)),
        out_shape=jax.ShapeDtypeStruct((ctx_len + seq, n, n_batch * n_heads), x.dtype),
        compiler_params=_cparams(("parallel",)),
        name="tokens_to_steps",
    )(*([x] * n_batch))


def _to_tokens_body(y_ref, o_ref, *, n_heads, n_batch):
    y = y_ref[...]
    t, n = y.shape[0], y.shape[1]
    for b in range(n_batch):
        @pl.when(pl.program_id(1) == b)
        def _():
            yb = y[:, :, b * n_heads:(b + 1) * n_heads]
            o_ref[...] = jnp.swapaxes(yb, 1, 2).reshape(t, n_heads * n)


def steps_to_tokens(y, n_batch, seq, ctx_len, n_heads):
    steps, n, _ = y.shape
    d = n_heads * n
    t = RELAYOUT_ROWS
    n_ctx_blk, n_lat_blk = ctx_len // t, seq // t
    return pl.pallas_call(
        functools.partial(_to_tokens_body, n_heads=n_heads, n_batch=n_batch),
        grid=(n_ctx_blk + n_lat_blk, n_batch),
        in_specs=[pl.BlockSpec((t, n, n_batch * n_heads), lambda s, b: (s, 0, 0))],
        out_specs=pl.BlockSpec((t, d), lambda s, b: (_step_block_row(s, b, n_ctx_blk, n_lat_blk, n_batch), 0)),
        out_shape=jax.ShapeDtypeStruct((n_batch * (seq + ctx_len), d), y.dtype),
        compiler_params=_cparams(("parallel", "arbitrary")),
        name="steps_to_tokens",
    )(y)


def _pad_cols(w, mult=128):
    n = w.shape[-1]
    p = -(-n // mult) * mult - n
    return jnp.pad(w, [(0, 0)] * (w.ndim - 1) + [(0, p)]) if p else w


def _pad_rows(w, mult=128):
    n = w.shape[-2]
    p = -(-n // mult) * mult - n
    return jnp.pad(w, [(0, 0)] * (w.ndim - 2) + [(0, p), (0, 0)]) if p else w


def rwkv_mixer_tokens(h, mu, w_rkv, w0, w1, w2, a0, a1, a2, k_k, k_a, r_k, ln_g, ln_b, g1, g2,
                      tm, n_batch, seq, ctx_len):
    m, d = h.shape
    n_heads, n = r_k.shape
    n_lat = n_batch * seq
    ts = _row_tile(seq, ctx_len, cap=256)
    x_r, x_k, x_v, x_w, x_a, x_g = token_shift_mix(h, mu, ts, seq, ctx_len, n_batch)
    mm = functools.partial(matmul, tm=tm)
    r = mm(x_r, w_rkv, w_index=0, full_rows=True)
    k = mm(x_k, w_rkv, w_index=1, full_rows=True)
    v = mm(x_v, w_rkv, w_index=2, full_rows=True)
    gate = mm(mm(x_g, _pad_cols(g1), act="sigmoid", out_dtype=BF16), _pad_rows(g2), full_rows=True)
    dec, aa = [], []
    for dn in range(2):
        lw = mm(x_w, _pad_cols(w1[dn]), act="tanh", out_dtype=BF16)
        dec.append(mm(lw, _pad_rows(w2[dn]), bias=w0[dn], act="decay", full_rows=True))
        la = mm(x_a, _pad_cols(a1[dn]), out_dtype=BF16)
        aa.append(mm(la, _pad_rows(a2[dn]), bias=a0[dn], act="sigmoid", full_rows=True))

    to_seq = functools.partial(tokens_to_steps, n_batch=n_batch, seq=seq, ctx_len=ctx_len, n_heads=n_heads)
    r_t, k_t, v_t, g_t = to_seq(r), to_seq(k), to_seq(v), to_seq(gate)
    d0_t, d1_t, a0_t, a1_t = to_seq(dec[0]), to_seq(dec[1]), to_seq(aa[0]), to_seq(aa[1])

    def table(vec):
        return jnp.tile(vec.astype(F32).reshape(n_heads, n).T[:, None, :], (1, n_batch, 1)).reshape(n, n_batch * n_heads)

    kk_tab, ka_tab = table(k_k), table(k_a)
    o_f, o_r = rwkv_scan(r_t, k_t, v_t, d0_t, d1_t, a0_t, a1_t, jnp.concatenate([kk_tab, kk_tab], axis=-1),
                         jnp.concatenate([ka_tab, ka_tab], axis=-1), ctx_len)
    y_t = rwkv_post(o_f, o_r, r_t, k_t, v_t, a0_t, a1_t, g_t, ka_tab, table(r_k.reshape(-1)),
                    table(ln_g), table(ln_b))
    return steps_to_tokens(y_t, n_batch, seq, ctx_len, n_heads)


def kernel(x, c, ctx, c_ctx, mod_w, mod_b, norm_g, final_g, na_w_qkv, na_w_o, na_rpb, s5_lam_re, s5_lam_im, s5_log_dt, s5_b_re, s5_b_im, s5_c_re, s5_c_im, s5_d, s5_w_glu, rw_mu, rw_w_rkv, rw_w0, rw_w1, rw_w2, rw_a0, rw_a1, rw_a2, rw_k_k, rw_k_a, rw_r_k, rw_ln_g, rw_ln_b, rw_g1, rw_g2, rw_w_o, moe_wg, moe_bg, moe_we, moe_be, moe_w_gu, moe_w_down):
    n_batch, seq, d = x.shape
    ctx_len = ctx.shape[1]
    depth = mod_w.shape[0]
    n_lat = n_batch * seq
    n_ctx = n_batch * ctx_len
    tm = _row_tile(seq, n_ctx)
    tpb = seq // tm
    xs = jnp.concatenate([x.reshape(n_lat, d), ctx.reshape(n_ctx, d)], axis=0).astype(F32)

    cvecs = jnp.concatenate([c.astype(F32), c_ctx.astype(F32)[None]], axis=0)
    rows_pad = -(-(n_batch + 1) // 8) * 8
    cvecs = jnp.pad(cvecs, ((0, rows_pad - n_batch - 1), (0, 0)))
    mods = modulation(cvecs, mod_w, mod_b)[:, :n_batch + 1].reshape(depth, n_batch + 1, 6, 1, d)
    mods = mods.transpose(0, 2, 1, 3, 4)

    for i in range(depth):
        last = i == depth - 1
        sh1, sc1, g1, sh2, sc2, g2 = (mods[i, q] for q in range(6))
        pn = functools.partial(prenorm, tm=tm, tiles_per_batch=tpb, n_batch=n_batch)
        mm = functools.partial(matmul, tm=tm, tiles_per_batch=tpb, n_batch=n_batch)
        mix, j = i % N_MIXERS, i // N_MIXERS
        if mix == 0:
            (hb,) = pn(xs, norm_g[i, 0], 1.0 + sc1, sh1, out_dtypes=(BF16,))
            n_heads = na_rpb.shape[1]
            qscale = jnp.concatenate([jnp.full((d,), (d // n_heads) ** -0.5, F32), jnp.ones((2 * d,), F32)])
            qkv = mm(hb, na_w_qkv, w_index=j, out_dtype=BF16, colscale=qscale)
            kr = (na_rpb.shape[2] + 1) // 2
            o = na_attention(qkv, _na_bias_table(na_rpb[j], GRID_W, seq // GRID_W), kr, n_batch, seq, ctx_len,
                             not last)
            if last:
                xs = xs[:n_lat]
            xs = mm(o, na_w_o, w_index=j, mode="resid", resid=xs, gate=g1, full_rows=True)
        elif mix == 1:
            (hf,) = pn(xs, norm_g[i, 0], 1.0 + sc1, sh1, out_dtypes=(F32,))
            wts = _s5_weights(s5_lam_re[j], s5_lam_im[j], s5_log_dt[j], s5_b_re[j], s5_b_im[j],
                              s5_c_re[j], s5_c_im[j], S5_CHUNK)
            y = s5_core(hf, wts, s5_d[j], n_batch, seq, ctx_len)
            if last:
                xs, y = xs[:n_lat], y[:n_lat]
            xs = mm(y, s5_w_glu, w_index=j, mode="glu", resid=xs, gate=g1)
        else:
            (hf,) = pn(xs, norm_g[i, 0], 1.0 + sc1, sh1, out_dtypes=(F32,))
            y = rwkv_mixer_tokens(hf, rw_mu[j], rw_w_rkv[j], rw_w0[j], rw_w1[j], rw_w2[j], rw_a0[j], rw_a1[j],
                                  rw_a2[j], rw_k_k[j], rw_k_a[j], rw_r_k[j], rw_ln_g[j], rw_ln_b[j],
                                  rw_g1[j], rw_g2[j], tm, n_batch, seq, ctx_len)
            if last:
                xs, y = xs[:n_lat], y[:n_lat]
            xs = mm(y, rw_w_o, w_index=j, mode="resid", resid=xs, gate=g1, full_rows=True)
        h2, h2_slabs = pn(xs, norm_g[i, 1], 1.0 + sc2, sh2, out_dtypes=(F32, F32), lane_split=(False, True))
        xs = hier_moe(xs, h2, h2_slabs, moe_wg[i], moe_bg[i], moe_we[i], moe_be[i], moe_w_gu, moe_w_down, i, g2,
                      tm, n_batch, seq)
    ones = jnp.ones((n_batch + 1, 1, d), F32)
    (out,) = prenorm(xs[:n_lat], final_g, ones, jnp.zeros_like(ones), tm, tpb, n_batch, out_dtypes=(x.dtype,))
    return out.reshape(n_batch, seq, d)
```
